```python
import math
import jax, jax.numpy as jnp
from jax import lax
import numpy as np

D_MODEL = 2048
BATCH = 4
SEQ = 2048
DEPTH = 1

MIX_WIDTH = D_MODEL
LRU_WIDTH = MIX_WIDTH // 2
LRU_BLOCKS = 8
LRU_BLOCK_DIM = LRU_WIDTH // LRU_BLOCKS
CONV_WIDTH = 4
LRU_C = 8.0
ATT_WIDTH = MIX_WIDTH - LRU_WIDTH
ATT_HEADS = 8
HEAD_DIM = ATT_WIDTH // ATT_HEADS
KV_LATENT = 256
IDX_HEADS = 16
IDX_DIM = 64
INDEX_TOPK = 256
Q_BLOCK = 128
REL_BUCKETS = 32
REL_MAX_DIST = 128
EPS = 1e-6

IN_SIZES = (
    LRU_WIDTH,
    LRU_WIDTH,
    ATT_HEADS * HEAD_DIM,
    KV_LATENT,
    ATT_WIDTH,
    IDX_HEADS * IDX_DIM,
    IDX_DIM,
    IDX_HEADS,
)
IN_COLS = sum(IN_SIZES)
IN_SPLITS = tuple(int(v) for v in np.cumsum(IN_SIZES)[:-1])

kernel_name = "hymba_rglru_dsa_hybrid"


def _rmsnorm(x, g):
    x32 = x.astype(jnp.float32)
    y = x32 * lax.rsqrt(jnp.mean(x32 * x32, axis=-1, keepdims=True) + EPS)
    return (y * g.astype(jnp.float32)).astype(x.dtype)


def _layernorm(x, g, b):
    x32 = x.astype(jnp.float32)
    mu = jnp.mean(x32, axis=-1, keepdims=True)
    var = jnp.mean(jnp.square(x32 - mu), axis=-1, keepdims=True)
    y = (x32 - mu) * lax.rsqrt(var + EPS)
    return (y * g.astype(jnp.float32) + b.astype(jnp.float32)).astype(x.dtype)


def _t5_bucket(dist):
    n = jnp.maximum(dist, 0)
    max_exact = REL_BUCKETS // 2
    nf = jnp.maximum(n, 1).astype(jnp.float32)
    large = max_exact + (jnp.log(nf / max_exact) / math.log(REL_MAX_DIST / max_exact)
                         * (REL_BUCKETS - max_exact)).astype(jnp.int32)
    large = jnp.minimum(large, REL_BUCKETS - 1)
    return jnp.where(n < max_exact, n, large)


def _lru_combine(left, right):
    a1, b1 = left
    a2, b2 = right
    return a1 * a2, a2 * b1 + b2


def _rglru_branch(xa, conv_w, conv_b, wa, ba, wx, bx, lam):
    B, S, W = xa.shape
    xp = jnp.pad(xa, ((0, 0), (CONV_WIDTH - 1, 0), (0, 0)))
    xc = conv_b + sum(xp[:, j:j + S] * conv_w[j] for j in range(CONV_WIDTH))
    xb = xc.reshape(B, S, LRU_BLOCKS, LRU_BLOCK_DIM)
    r = jax.nn.sigmoid(jnp.einsum('bsgi,gij->bsgj', xb, wa).reshape(B, S, W) + ba)
    i = jax.nn.sigmoid(jnp.einsum('bsgi,gij->bsgj', xb, wx).reshape(B, S, W) + bx)
    log_a = -LRU_C * r.astype(jnp.float32) * jax.nn.softplus(-lam.astype(jnp.float32))
    a = jnp.exp(log_a)
    mult = jnp.sqrt(-jnp.expm1(2.0 * log_a))
    mult = jnp.where((jnp.arange(S) == 0)[None, :, None], 1.0, mult)
    bvals = mult * (i * xc).astype(jnp.float32)
    _, h = lax.associative_scan(_lru_combine, (a, bvals), axis=1)
    return h.astype(xa.dtype)


def _dsa_branch(q, ckv, qi, ki, wi, ckv_norm_g, ik_g, ik_b, w_uk, w_uv, rel_bias):
    B, S, _ = q.shape
    topk = min(INDEX_TOPK, S // 4)
    nblk = S // Q_BLOCK
    neg = jnp.finfo(jnp.float32).min
    q = q.reshape(B, S, ATT_HEADS, HEAD_DIM)
    q_abs = jnp.einsum('bshd,hdc->bshc', q, w_uk)
    c = _rmsnorm(ckv, ckv_norm_g)
    qi = qi.reshape(B, S, IDX_HEADS, IDX_DIM)
    ki = _layernorm(ki, ik_g, ik_b)
    wi = wi * (IDX_HEADS ** -0.5 * IDX_DIM ** -0.5)
    pos = jnp.arange(S, dtype=jnp.int32)

    def to_blocks(t):
        return jnp.moveaxis(t.reshape((B, nblk, Q_BLOCK) + t.shape[2:]), 1, 0)

    def block_fn(args):
        qa_b, qi_b, wi_b, pos_b = args
        s_idx = jnp.einsum('bqhd,bsd->bqhs', qi_b, ki).astype(jnp.float32)
        score = jnp.einsum('bqhs,bqh->bqs', jax.nn.relu(s_idx), wi_b.astype(jnp.float32))
        causal = pos[None, None, :] <= pos_b[None, :, None]
        score = jnp.where(causal, score, neg)
        _, sel = lax.top_k(score, topk)
        c_sel = jax.vmap(lambda cb, ib: cb[ib])(c, sel)
        logits = jnp.einsum('bqhc,bqkc->bqhk', qa_b, c_sel).astype(jnp.float32) * (HEAD_DIM ** -0.5)
        dist = pos_b[None, :, None] - sel
        bias = rel_bias[_t5_bucket(dist)].astype(jnp.float32)
        logits = logits + jnp.moveaxis(bias, -1, 2)
        logits = jnp.where((dist >= 0)[:, :, None, :], logits, neg)
        p = jax.nn.softmax(logits, axis=-1).astype(c.dtype)
        return jnp.einsum('bqhk,bqkc->bqhc', p, c_sel)

    o = lax.map(block_fn, (to_blocks(q_abs), to_blocks(qi), to_blocks(wi), pos.reshape(nblk, Q_BLOCK)))
    o = jnp.moveaxis(o, 0, 1).reshape(B, S, ATT_HEADS, KV_LATENT)
    out = jnp.einsum('bshc,hcd->bshd', o, w_uv)
    return out.reshape(B, S, ATT_WIDTH)


def setup_inputs(seed: int = 0) -> dict:
    key = jax.random.key(seed)
    ks = jax.random.split(key, 20)
    f32 = jnp.float32
    nrm = lambda k, shape, scale: jax.random.normal(k, shape, f32) * scale
    a0 = jax.random.uniform(ks[9], (DEPTH, LRU_WIDTH), f32, 0.9, 0.999)
    s0 = a0 ** (1.0 / LRU_C)
    lam = jnp.log(s0) - jnp.log1p(-s0)
    return {
        "x": nrm(ks[0], (BATCH, SEQ, D_MODEL), 1.0),
        "norm_g": 1.0 + nrm(ks[1], (DEPTH, D_MODEL), 0.02),
        "w_in": nrm(ks[2], (DEPTH, D_MODEL, IN_COLS), D_MODEL ** -0.5),
        "conv_w": nrm(ks[3], (DEPTH, CONV_WIDTH, LRU_WIDTH), CONV_WIDTH ** -0.5),
        "conv_b": nrm(ks[4], (DEPTH, LRU_WIDTH), 0.01),
        "lru_wa": nrm(ks[5], (DEPTH, LRU_BLOCKS, LRU_BLOCK_DIM, LRU_BLOCK_DIM), LRU_BLOCK_DIM ** -0.5),
        "lru_ba": nrm(ks[6], (DEPTH, LRU_WIDTH), 0.01),
        "lru_wx": nrm(ks[7], (DEPTH, LRU_BLOCKS, LRU_BLOCK_DIM, LRU_BLOCK_DIM), LRU_BLOCK_DIM ** -0.5),
        "lru_bx": nrm(ks[8], (DEPTH, LRU_WIDTH), 0.01),
        "lru_lambda": lam,
        "ckv_norm_g": 1.0 + nrm(ks[10], (DEPTH, KV_LATENT), 0.02),
        "idx_k_norm_g": 1.0 + nrm(ks[11], (DEPTH, IDX_DIM), 0.02),
        "idx_k_norm_b": nrm(ks[12], (DEPTH, IDX_DIM), 0.01),
        "w_uk": nrm(ks[13], (DEPTH, ATT_HEADS, HEAD_DIM, KV_LATENT), HEAD_DIM ** -0.5),
        "w_uv": nrm(ks[14], (DEPTH, ATT_HEADS, KV_LATENT, HEAD_DIM), KV_LATENT ** -0.5),
        "w_out": nrm(ks[15], (DEPTH, MIX_WIDTH, D_MODEL), MIX_WIDTH ** -0.5),
        "rel_bias": nrm(ks[16], (REL_BUCKETS, ATT_HEADS), 0.5),
        "final_norm_g": 1.0 + nrm(ks[17], (D_MODEL,), 0.02),
    }


def reference(x, norm_g, w_in, conv_w, conv_b, lru_wa, lru_ba, lru_wx, lru_bx, lru_lambda,
              ckv_norm_g, idx_k_norm_g, idx_k_norm_b, w_uk, w_uv, w_out, rel_bias, final_norm_g):
    for l in range(DEPTH):
        h = _rmsnorm(x, norm_g[l])
        proj = jnp.einsum('bsd,de->bse', h, w_in[l])
        xa, ga, q, ckv, gb, qi, ki, wi = jnp.split(proj, IN_SPLITS, axis=-1)
        ya = _rglru_branch(xa, conv_w[l], conv_b[l], lru_wa[l], lru_ba[l],
                           lru_wx[l], lru_bx[l], lru_lambda[l]) * jax.nn.silu(ga)
        yb = _dsa_branch(q, ckv, qi, ki, wi, ckv_norm_g[l], idx_k_norm_g[l], idx_k_norm_b[l],
                         w_uk[l], w_uv[l], rel_bias) * jax.nn.silu(gb)
        y = jnp.concatenate([ya, yb], axis=-1)
        x = x + jnp.einsum('bse,ed->bsd', y, w_out[l])
    return _rmsnorm(x, final_norm_g)
```

```python
import functools

import numpy as np
import jax
import jax.numpy as jnp
from jax import lax
from jax.experimental import pallas as pl
from jax.experimental.pallas import tpu as pltpu

F32 = jnp.float32
BF16 = jnp.bfloat16
I32 = jnp.int32

LRU_BLOCKS = 8
CONV_WIDTH = 4
LRU_C = 8.0
ATT_HEADS = 8
HEAD_DIM = 128
KV_LATENT = 256
IDX_HEADS = 16
IDX_DIM = 64
INDEX_TOPK = 256
REL_BUCKETS = 32
REL_MAX_DIST = 128
EPS = 1e-6

Q_TILE = 128
K_CHUNK = 128
COUNT_CHUNK = 512
NEG = float(np.finfo(np.float32).min)
INT_MIN = -(2 ** 31)
VMEM_LIMIT = 56 * 1024 * 1024


def _cparams(sem):
    return pltpu.CompilerParams(dimension_semantics=sem, vmem_limit_bytes=VMEM_LIMIT)


def _proj_kernel(x_ref, g_ref, w_ref, ws_ref, o_ref, os_ref, h_ref):
    @pl.when(pl.program_id(1) == 0)
    def _():
        x = x_ref[...]
        y = x * lax.rsqrt(jnp.mean(x * x, axis=-1, keepdims=True) + EPS)
        hb = (y * g_ref[...]).astype(BF16)
        h_ref[...] = hb
        os_ref[...] = jnp.dot(hb, ws_ref[...], preferred_element_type=F32)

    o_ref[...] = jnp.dot(h_ref[...], w_ref[...], preferred_element_type=F32)


def _proj(x2, g, w_main, w_small, tm=512, tn=768):
    m, d = x2.shape
    n = w_main.shape[1]
    ns = w_small.shape[1]
    return pl.pallas_call(
        _proj_kernel,
        grid=(m // tm, n // tn),
        in_specs=[
            pl.BlockSpec((tm, d), lambda i, j: (i, 0)),
            pl.BlockSpec((1, d), lambda i, j: (0, 0)),
            pl.BlockSpec((d, tn), lambda i, j: (0, j)),
            pl.BlockSpec((d, ns), lambda i, j: (0, 0)),
        ],
        out_specs=[
            pl.BlockSpec((tm, tn), lambda i, j: (i, j)),
            pl.BlockSpec((tm, ns), lambda i, j: (i, 0)),
        ],
        out_shape=[
            jax.ShapeDtypeStruct((m, n), F32),
            jax.ShapeDtypeStruct((m, ns), F32),
        ],
        scratch_shapes=[pltpu.VMEM((tm, d), BF16)],
        compiler_params=_cparams(("parallel", "arbitrary")),
        name="proj",
    )(x2, g, w_main, w_small)


def _sigmoid(v):
    return 1.0 / (1.0 + jnp.exp(-v))


def _rglru_kernel(xa_ref, ga_ref, cw_ref, cb_ref, wa_ref, ba_ref, wx_ref, bx_ref, lam_ref,
                  o_ref, a_s, b_s):
    s, w = xa_ref.shape
    xa = xa_ref[...]
    row = lax.broadcasted_iota(I32, (s, w), 0)

    def delayed(v, k):
        return jnp.where(row >= k, pltpu.roll(v, k, axis=0), 0.0)

    acc = delayed(xa, CONV_WIDTH - 1) * cw_ref[0:1, :]
    for j in range(1, CONV_WIDTH - 1):
        acc = acc + delayed(xa, CONV_WIDTH - 1 - j) * cw_ref[j:j + 1, :]
    acc = acc + xa * cw_ref[CONV_WIDTH - 1:CONV_WIDTH, :]
    xc = cb_ref[...] + acc

    xcb = xc.astype(BF16)
    r = _sigmoid(jnp.dot(xcb, wa_ref[...], preferred_element_type=F32) + ba_ref[...])
    i = _sigmoid(jnp.dot(xcb, wx_ref[...], preferred_element_type=F32) + bx_ref[...])
    z = -lam_ref[...]
    softplus = jnp.maximum(z, 0.0) + jnp.log1p(jnp.exp(-jnp.abs(z)))
    log_a = (-LRU_C) * r * softplus
    a = jnp.exp(log_a)
    y2 = 2.0 * log_a
    u = a * a
    em1 = jnp.where(u == 1.0, y2, (u - 1.0) * y2 / jnp.log(u))
    mult = jnp.sqrt(-em1)
    mult = jnp.where(row == 0, 1.0, mult)
    b = mult * (i * xc)

    row8 = row & 7
    for k in (1, 2, 4):
        keep = row8 >= k
        a_prev = jnp.where(keep, pltpu.roll(a, k, axis=0), 1.0)
        b_prev = jnp.where(keep, pltpu.roll(b, k, axis=0), 0.0)
        b = a * b_prev + b
        a = a * a_prev
    a_s[...] = a
    b_s[...] = b

    def carry(t, h_last):
        r0 = pl.multiple_of(t * 8, 8)
        h = a_s[pl.ds(r0, 8), :] * h_last + b_s[pl.ds(r0, 8), :]
        b_s[pl.ds(r0, 8), :] = h
        return h[7:8, :]

    lax.fori_loop(0, s // 8, carry, jnp.zeros((1, w), F32), unroll=8)

    ga = ga_ref[...]
    o_ref[...] = (b_s[...] * (ga * _sigmoid(ga))).astype(o_ref.dtype)


def _rglru(proj3, cols, conv_w, conv_b, wa, ba, wx, bx, lam):
    bsz, s, _ = proj3.shape
    g, w = wa.shape[0], wa.shape[-1]
    xa_blk = _col_block(cols["xa"], w)
    ga_blk = _col_block(cols["ga"], w)
    vec = lambda: pl.BlockSpec((1, w), lambda b, j: (0, j))
    return pl.pallas_call(
        _rglru_kernel,
        grid=(bsz, g),
        in_specs=[
            pl.BlockSpec((None, s, w), lambda b, j: (b, 0, xa_blk + j)),
            pl.BlockSpec((None, s, w), lambda b, j: (b, 0, ga_blk + j)),
            pl.BlockSpec((CONV_WIDTH, w), lambda b, j: (0, j)),
            vec(),
            pl.BlockSpec((None, w, w), lambda b, j: (j, 0, 0)),
            vec(),
            pl.BlockSpec((None, w, w), lambda b, j: (j, 0, 0)),
            vec(),
            vec(),
        ],
        out_specs=pl.BlockSpec((None, s, w), lambda b, j: (b, 0, j)),
        out_shape=jax.ShapeDtypeStruct((bsz, s, g * w), BF16),
        scratch_shapes=[pltpu.VMEM((s, w), F32), pltpu.VMEM((s, w), F32)],
        compiler_params=_cparams(("parallel", "parallel")),
        name="rglru",
    )(proj3, proj3, conv_w, conv_b, wa, ba, wx, bx, lam)


def _prep_kernel(q_ref, qi_ref, ckv_ref, sm_ref, wuk_ref, cg_ref, kg_ref, kb_ref,
                 qa_ref, qih_ref, c_ref, ct_ref, kn_ref, wit_ref):
    q = q_ref[...].astype(BF16)
    for h in range(ATT_HEADS):
        qa = jnp.dot(q[:, h * HEAD_DIM:(h + 1) * HEAD_DIM], wuk_ref[h], preferred_element_type=F32)
        qa_ref[h] = qa.astype(BF16)
    qi = qi_ref[...]
    for h in range(IDX_HEADS):
        qih_ref[h] = qi[:, h * IDX_DIM:(h + 1) * IDX_DIM].astype(BF16)

    ckv = ckv_ref[...]
    c = ckv * lax.rsqrt(jnp.mean(ckv * ckv, axis=-1, keepdims=True) + EPS) * cg_ref[...]
    c_ref[...] = c.astype(BF16)
    ct_ref[...] = c.T.astype(BF16)

    sm = sm_ref[...]
    ki = sm[:, :IDX_DIM]
    mu = jnp.mean(ki, axis=-1, keepdims=True)
    var = jnp.mean(jnp.square(ki - mu), axis=-1, keepdims=True)
    kn = (ki - mu) * lax.rsqrt(var + EPS) * kg_ref[...] + kb_ref[...]
    kn_ref[...] = kn.astype(BF16)
    wit_ref[...] = sm.T[IDX_DIM:IDX_DIM + IDX_HEADS, :] * (IDX_HEADS ** -0.5 * IDX_DIM ** -0.5)


def _col_block(offset, width):
    assert offset % width == 0
    return offset // width


def _prep(proj3, small3, cols, w_uk, ckv_g, k_g, k_b, tm=256):
    bsz, s, _ = proj3.shape
    att_w = ATT_HEADS * HEAD_DIM
    idx_w = IDX_HEADS * IDX_DIM
    q_blk = _col_block(cols["q"], att_w)
    qi_blk = _col_block(cols["qi"], idx_w)
    ckv_blk = _col_block(cols["ckv"], KV_LATENT)
    const = lambda shape: pl.BlockSpec(shape, lambda b, i: (0,) * len(shape))
    return pl.pallas_call(
        _prep_kernel,
        grid=(bsz, s // tm),
        in_specs=[
            pl.BlockSpec((None, tm, att_w), lambda b, i: (b, i, q_blk)),
            pl.BlockSpec((None, tm, idx_w), lambda b, i: (b, i, qi_blk)),
            pl.BlockSpec((None, tm, KV_LATENT), lambda b, i: (b, i, ckv_blk)),
            pl.BlockSpec((None, tm, small3.shape[-1]), lambda b, i: (b, i, 0)),
            const(w_uk.shape),
            const((1, KV_LATENT)),
            const((1, IDX_DIM)),
            const((1, IDX_DIM)),
        ],
        out_specs=[
            pl.BlockSpec((None, ATT_HEADS, tm, KV_LATENT), lambda b, i: (b, 0, i, 0)),
            pl.BlockSpec((None, IDX_HEADS, tm, IDX_DIM), lambda b, i: (b, 0, i, 0)),
            pl.BlockSpec((None, tm, KV_LATENT), lambda b, i: (b, i, 0)),
            pl.BlockSpec((None, KV_LATENT, tm), lambda b, i: (b, 0, i)),
            pl.BlockSpec((None, tm, IDX_DIM), lambda b, i: (b, i, 0)),
            pl.BlockSpec((None, IDX_HEADS, tm), lambda b, i: (b, 0, i)),
        ],
        out_shape=[
            jax.ShapeDtypeStruct((bsz, ATT_HEADS, s, KV_LATENT), BF16),
            jax.ShapeDtypeStruct((bsz, IDX_HEADS, s, IDX_DIM), BF16),
            jax.ShapeDtypeStruct((bsz, s, KV_LATENT), BF16),
            jax.ShapeDtypeStruct((bsz, KV_LATENT, s), BF16),
            jax.ShapeDtypeStruct((bsz, s, IDX_DIM), BF16),
            jax.ShapeDtypeStruct((bsz, IDX_HEADS, s), F32),
        ],
        compiler_params=_cparams(("parallel", "parallel")),
        name="prep",
    )(proj3, proj3, proj3, small3, w_uk, ckv_g, k_g, k_b)


def _dsa_kernel(qih_ref, wit_ref, kn_ref, c_ref, ct_ref, qa_ref, gb_ref, bias_ref, wuvt_ref,
                o_ref, key_s, mb_s, lg_s, topk):
    qb = pl.program_id(1)
    nkc = qb + 1
    ncc = (nkc * K_CHUNK + COUNT_CHUNK - 1) // COUNT_CHUNK
    per = COUNT_CHUNK // K_CHUNK
    dn_t = (((1,), (1,)), ((), ()))

    kiota = lax.broadcasted_iota(I32, (K_CHUNK, Q_TILE), 0)
    qpos = qb * Q_TILE + lax.broadcasted_iota(I32, (K_CHUNK, Q_TILE), 1)

    def score_chunk(kc, carry):
        k0 = pl.multiple_of(kc * K_CHUNK, K_CHUNK)
        kn = kn_ref[pl.ds(k0, K_CHUNK), :]
        acc = jnp.zeros((K_CHUNK, Q_TILE), F32)
        for h in range(IDX_HEADS):
            sc = lax.dot_general(kn, qih_ref[h], dn_t, preferred_element_type=F32)
            acc = acc + jnp.maximum(sc, 0.0) * wit_ref[h:h + 1, :]
        bits = lax.bitcast_convert_type(acc, I32)
        key = bits ^ ((bits >> 31) & 0x7FFFFFFF)
        key_s[pl.ds(k0, K_CHUNK), :] = jnp.where(kiota + k0 <= qpos, key, INT_MIN)
        return carry

    lax.fori_loop(0, nkc, score_chunk, 0)

    def pad_chunk(kc, carry):
        k0 = pl.multiple_of(kc * K_CHUNK, K_CHUNK)
        key_s[pl.ds(k0, K_CHUNK), :] = jnp.full((K_CHUNK, Q_TILE), INT_MIN, I32)
        return carry

    lax.fori_loop(nkc, ncc * per, pad_chunk, 0)

    def bit_step(i, thr):
        cand = thr ^ (jnp.int32(1) << (31 - i))

        def count_chunk(cc, cnt):
            r0 = pl.multiple_of(cc * COUNT_CHUNK, COUNT_CHUNK)
            hit = (key_s[pl.ds(r0, COUNT_CHUNK), :] >= cand).astype(I32)
            return cnt + jnp.sum(hit.reshape(COUNT_CHUNK // 8, 8, Q_TILE), axis=0)

        cnt = lax.fori_loop(0, ncc, count_chunk, jnp.zeros((8, Q_TILE), I32))
        total = jnp.sum(cnt, axis=0, keepdims=True)
        return jnp.where(total >= topk, cand, thr)

    thr = lax.fori_loop(0, 32, bit_step, jnp.full((1, Q_TILE), INT_MIN, I32))
    thr = jnp.maximum(thr, INT_MIN + 1)

    def mask_chunk(kc, carry):
        k0 = pl.multiple_of(kc * K_CHUNK, K_CHUNK)
        mb_s[pl.ds(k0, K_CHUNK), :] = jnp.where(key_s[pl.ds(k0, K_CHUNK), :] >= thr, 0.0, NEG)
        return carry

    lax.fori_loop(0, nkc, mask_chunk, 0)

    scale = HEAD_DIM ** -0.5
    for h in range(ATT_HEADS):
        qa = qa_ref[h]

        def logit_chunk(kc, m8):
            k0 = pl.multiple_of(kc * K_CHUNK, K_CHUNK)
            lg = lax.dot_general(c_ref[pl.ds(k0, K_CHUNK), :], qa, dn_t, preferred_element_type=F32)
            near = jnp.maximum(kc - qb + 2, 0)
            lg = lg * scale + bias_ref[h, near] + mb_s[pl.ds(k0, K_CHUNK), :]
            lg_s[pl.ds(k0, K_CHUNK), :] = lg
            return jnp.maximum(m8, jnp.max(lg.reshape(K_CHUNK // 8, 8, Q_TILE), axis=0))

        m8 = lax.fori_loop(0, nkc, logit_chunk, jnp.full((8, Q_TILE), NEG, F32))
        m = jnp.max(m8, axis=0, keepdims=True)

        def pv_chunk(kc, carry):
            l8, acc = carry
            k0 = pl.multiple_of(kc * K_CHUNK, K_CHUNK)
            p = jnp.exp(lg_s[pl.ds(k0, K_CHUNK), :] - m)
            l8 = l8 + jnp.sum(p.reshape(K_CHUNK // 8, 8, Q_TILE), axis=0)
            acc = acc + jnp.dot(ct_ref[:, pl.ds(k0, K_CHUNK)], p.astype(BF16), preferred_element_type=F32)
            return l8, acc

        l8, acc = lax.fori_loop(0, nkc, pv_chunk,
                                (jnp.zeros((8, Q_TILE), F32), jnp.zeros((KV_LATENT, Q_TILE), F32)))
        o_t = acc * (1.0 / jnp.sum(l8, axis=0, keepdims=True))
        y_t = jnp.dot(wuvt_ref[h], o_t.astype(BF16), preferred_element_type=F32)
        gb = gb_ref[:, h * HEAD_DIM:(h + 1) * HEAD_DIM]
        o_ref[:, h * HEAD_DIM:(h + 1) * HEAD_DIM] = (y_t.T * (gb * _sigmoid(gb))).astype(o_ref.dtype)


def _dsa(qih, wit, kn, c, ct, qa, proj3, gb_blk, bias_ext, wuvt, topk):
    bsz, s, _ = c.shape
    att_w = ATT_HEADS * HEAD_DIM
    const = lambda shape: pl.BlockSpec(shape, lambda b, i: (0,) * len(shape))
    s_pad = ((s + COUNT_CHUNK - 1) // COUNT_CHUNK) * COUNT_CHUNK
    return pl.pallas_call(
        functools.partial(_dsa_kernel, topk=topk),
        grid=(bsz, s // Q_TILE),
        in_specs=[
            pl.BlockSpec((None, IDX_HEADS, Q_TILE, IDX_DIM), lambda b, i: (b, 0, i, 0)),
            pl.BlockSpec((None, IDX_HEADS, Q_TILE), lambda b, i: (b, 0, i)),
            pl.BlockSpec((None, s, IDX_DIM), lambda b, i: (b, 0, 0)),
            pl.BlockSpec((None, s, KV_LATENT), lambda b, i: (b, 0, 0)),
            pl.BlockSpec((None, KV_LATENT, s), lambda b, i: (b, 0, 0)),
            pl.BlockSpec((None, ATT_HEADS, Q_TILE, KV_LATENT), lambda b, i: (b, 0, i, 0)),
            pl.BlockSpec((None, Q_TILE, att_w), lambda b, i: (b, i, gb_blk)),
            const(bias_ext.shape),
            const(wuvt.shape),
        ],
        out_specs=pl.BlockSpec((None, Q_TILE, att_w), lambda b, i: (b, i, 0)),
        out_shape=jax.ShapeDtypeStruct((bsz, s, att_w), BF16),
        scratch_shapes=[
            pltpu.VMEM((s_pad, Q_TILE), I32),
            pltpu.VMEM((s, Q_TILE), F32),
            pltpu.VMEM((s, Q_TILE), F32),
        ],
        compiler_params=_cparams(("parallel", "arbitrary")),
        name="dsa",
    )(qih, wit, kn, c, ct, qa, proj3, bias_ext, wuvt)


def _outp_kernel(ya_ref, yb_ref, wa_ref, wb_ref, x_ref, g_ref, o_ref, *, final_norm):
    acc = jnp.dot(ya_ref[...], wa_ref[...], preferred_element_type=F32)
    acc = acc + jnp.dot(yb_ref[...], wb_ref[...], preferred_element_type=F32)
    x = x_ref[...] + acc
    if final_norm:
        x = x * lax.rsqrt(jnp.mean(x * x, axis=-1, keepdims=True) + EPS) * g_ref[...]
    o_ref[...] = x


def _outp(ya, yb, w_a, w_b, x2, g, final_norm, tm=256):
    m, d = x2.shape
    ka, kb = ya.shape[1], yb.shape[1]
    return pl.pallas_call(
        functools.partial(_outp_kernel, final_norm=final_norm),
        grid=(m // tm,),
        in_specs=[
            pl.BlockSpec((tm, ka), lambda i: (i, 0)),
            pl.BlockSpec((tm, kb), lambda i: (i, 0)),
            pl.BlockSpec((ka, d), lambda i: (0, 0)),
            pl.BlockSpec((kb, d), lambda i: (0, 0)),
            pl.BlockSpec((tm, d), lambda i: (i, 0)),
            pl.BlockSpec((1, d), lambda i: (0, 0)),
        ],
        out_specs=pl.BlockSpec((tm, d), lambda i: (i, 0)),
        out_shape=jax.ShapeDtypeStruct((m, d), F32),
        compiler_params=_cparams(("parallel",)),
        name="outp",
    )(ya, yb, w_a, w_b, x2, g)


def _t5_bucket(dist):
    n = jnp.maximum(dist, 0)
    max_exact = REL_BUCKETS // 2
    nf = jnp.maximum(n, 1).astype(F32)
    large = max_exact + (jnp.log(nf / max_exact) / np.log(REL_MAX_DIST / max_exact)
                         * (REL_BUCKETS - max_exact)).astype(I32)
    large = jnp.minimum(large, REL_BUCKETS - 1)
    return jnp.where(n < max_exact, n, large)


def _bias_tiles(rel_bias):
    span = K_CHUNK + Q_TILE
    table = rel_bias[_t5_bucket(jnp.arange(span + Q_TILE, dtype=I32))].astype(F32)
    table = table - table[-1:]
    ki = np.arange(span)[:, None]
    qi = np.arange(Q_TILE)[None, :]
    dist = qi - ki + K_CHUNK
    tiles = jnp.where((dist >= 0)[..., None], table[np.clip(dist, 0, None)], 0.0)
    tiles = jnp.transpose(tiles, (2, 0, 1)).reshape(ATT_HEADS, 2, K_CHUNK, Q_TILE)
    return jnp.concatenate([jnp.zeros_like(tiles[:, :1]), tiles], axis=1)


def kernel(x, norm_g, w_in, conv_w, conv_b, lru_wa, lru_ba, lru_wx, lru_bx, lru_lambda, ckv_norm_g, idx_k_norm_g, idx_k_norm_b, w_uk, w_uv, w_out, rel_bias, final_norm_g):
    bsz, s, d = x.shape
    depth = w_in.shape[0]
    lru_w = lru_wa.shape[1] * lru_wa.shape[2]
    att_w = ATT_HEADS * HEAD_DIM
    idx_w = IDX_HEADS * IDX_DIM
    assert K_CHUNK == Q_TILE and REL_MAX_DIST <= K_CHUNK and s % COUNT_CHUNK == 0
    assert lru_w == att_w == idx_w and att_w % KV_LATENT == 0
    topk = min(INDEX_TOPK, s // 4)

    o_q = 2 * lru_w
    o_ckv = o_q + att_w
    o_gb = o_ckv + KV_LATENT
    o_qi = o_gb + att_w
    o_ki = o_qi + idx_w
    n_small = IDX_DIM + IDX_HEADS
    small_pad = 128 - n_small
    cols = {"xa": 0, "ga": lru_w, "q": o_q, "gb": o_q + att_w, "qi": o_q + 2 * att_w,
            "ckv": o_q + 2 * att_w + idx_w}

    bias_ext = _bias_tiles(rel_bias)
    x2 = x.reshape(bsz * s, d)
    for l in range(depth):
        wl = w_in[l]
        w_main = jnp.concatenate(
            [wl[:, :o_ckv], wl[:, o_gb:o_qi], wl[:, o_qi:o_ki], wl[:, o_ckv:o_gb]], axis=1).astype(BF16)
        w_small = jnp.pad(wl[:, o_ki:], ((0, 0), (0, small_pad))).astype(BF16)
        proj, small = _proj(x2, norm_g[l][None, :], w_main, w_small)
        proj3 = proj.reshape(bsz, s, -1)
        small3 = small.reshape(bsz, s, -1)

        ya = _rglru(proj3, cols, conv_w[l], conv_b[l][None, :], lru_wa[l].astype(BF16), lru_ba[l][None, :],
                    lru_wx[l].astype(BF16), lru_bx[l][None, :], lru_lambda[l][None, :])

        qa, qih, c, ct, kn, wit = _prep(proj3, small3, cols, w_uk[l].astype(BF16), ckv_norm_g[l][None, :],
                                        idx_k_norm_g[l][None, :], idx_k_norm_b[l][None, :])
        wuvt = jnp.transpose(w_uv[l], (0, 2, 1)).astype(BF16)
        yb = _dsa(qih, wit, kn, c, ct, qa, proj3, _col_block(cols["gb"], att_w), bias_ext, wuvt, topk)

        wo = w_out[l].astype(BF16)
        x2 = _outp(ya.reshape(bsz * s, lru_w), yb.reshape(bsz * s, att_w), wo[:lru_w], wo[lru_w:],
                   x2, final_norm_g[None, :], final_norm=(l == depth - 1))
    return x2.reshape(bsz, s, d)
```

```python
import functools

import numpy as np
import jax
import jax.numpy as jnp
from jax import lax
from jax.experimental import pallas as pl
from jax.experimental.pallas import tpu as pltpu

F32 = jnp.float32
BF16 = jnp.bfloat16
I32 = jnp.int32

LRU_BLOCKS = 8
CONV_WIDTH = 4
LRU_C = 8.0
ATT_HEADS = 8
HEAD_DIM = 128
KV_LATENT = 256
IDX_HEADS = 16
IDX_DIM = 64
INDEX_TOPK = 256
REL_BUCKETS = 32
REL_MAX_DIST = 128
EPS = 1e-6

Q_TILE = 128
K_CHUNK = 128
COUNT_CHUNK = 512
ATT_CHUNK = 256
NEG = float(np.finfo(np.float32).min)
INT_MIN = -(2 ** 31)
VMEM_LIMIT = 56 * 1024 * 1024


def _cparams(sem):
    return pltpu.CompilerParams(dimension_semantics=sem, vmem_limit_bytes=VMEM_LIMIT)


def _proj_kernel(x_ref, g_ref, w_ref, ws_ref, o_ref, os_ref, h_ref):
    @pl.when(pl.program_id(1) == 0)
    def _():
        x = x_ref[...]
        y = x * lax.rsqrt(jnp.mean(x * x, axis=-1, keepdims=True) + EPS)
        hb = (y * g_ref[...]).astype(BF16)
        h_ref[...] = hb
        os_ref[...] = jnp.dot(hb, ws_ref[...], preferred_element_type=F32)

    o_ref[...] = jnp.dot(h_ref[...], w_ref[...], preferred_element_type=F32)


def _proj(x2, g, w_main, w_small, tm=512, tn=768):
    m, d = x2.shape
    n = w_main.shape[1]
    ns = w_small.shape[1]
    return pl.pallas_call(
        _proj_kernel,
        grid=(m // tm, n // tn),
        in_specs=[
            pl.BlockSpec((tm, d), lambda i, j: (i, 0)),
            pl.BlockSpec((1, d), lambda i, j: (0, 0)),
            pl.BlockSpec((d, tn), lambda i, j: (0, j)),
            pl.BlockSpec((d, ns), lambda i, j: (0, 0)),
        ],
        out_specs=[
            pl.BlockSpec((tm, tn), lambda i, j: (i, j)),
            pl.BlockSpec((tm, ns), lambda i, j: (i, 0)),
        ],
        out_shape=[
            jax.ShapeDtypeStruct((m, n), F32),
            jax.ShapeDtypeStruct((m, ns), F32),
        ],
        scratch_shapes=[pltpu.VMEM((tm, d), BF16)],
        compiler_params=_cparams(("parallel", "arbitrary")),
        name="proj",
    )(x2, g, w_main, w_small)


def _sigmoid(v):
    return 1.0 / (1.0 + jnp.exp(-v))


def _rglru_kernel(xa_ref, ga_ref, cw_ref, cb_ref, wa_ref, ba_ref, wx_ref, bx_ref, lam_ref,
                  o_ref, a_s, b_s):
    s, w = xa_ref.shape
    xa = xa_ref[...]
    row = lax.broadcasted_iota(I32, (s, w), 0)

    def delayed(v, k):
        return jnp.where(row >= k, pltpu.roll(v, k, axis=0), 0.0)

    acc = delayed(xa, CONV_WIDTH - 1) * cw_ref[0:1, :]
    for j in range(1, CONV_WIDTH - 1):
        acc = acc + delayed(xa, CONV_WIDTH - 1 - j) * cw_ref[j:j + 1, :]
    acc = acc + xa * cw_ref[CONV_WIDTH - 1:CONV_WIDTH, :]
    xc = cb_ref[...] + acc

    xcb = xc.astype(BF16)
    r = _sigmoid(jnp.dot(xcb, wa_ref[...], preferred_element_type=F32) + ba_ref[...])
    i = _sigmoid(jnp.dot(xcb, wx_ref[...], preferred_element_type=F32) + bx_ref[...])
    z = -lam_ref[...]
    softplus = jnp.maximum(z, 0.0) + jnp.log1p(jnp.exp(-jnp.abs(z)))
    log_a = (-LRU_C) * r * softplus
    a = jnp.exp(log_a)
    y2 = 2.0 * log_a
    u = a * a
    em1 = jnp.where(u == 1.0, y2, (u - 1.0) * y2 / jnp.log(u))
    mult = jnp.sqrt(-em1)
    mult = jnp.where(row == 0, 1.0, mult)
    b = mult * (i * xc)

    row8 = row & 7
    for k in (1, 2, 4):
        keep = row8 >= k
        a_prev = jnp.where(keep, pltpu.roll(a, k, axis=0), 1.0)
        b_prev = jnp.where(keep, pltpu.roll(b, k, axis=0), 0.0)
        b = a * b_prev + b
        a = a * a_prev
    a_s[...] = a
    b_s[...] = b

    def carry(t, h_last):
        r0 = pl.multiple_of(t * 8, 8)
        h = a_s[pl.ds(r0, 8), :] * h_last + b_s[pl.ds(r0, 8), :]
        b_s[pl.ds(r0, 8), :] = h
        return h[7:8, :]

    lax.fori_loop(0, s // 8, carry, jnp.zeros((1, w), F32), unroll=8)

    ga = ga_ref[...]
    o_ref[...] = (b_s[...] * (ga * _sigmoid(ga))).astype(o_ref.dtype)


def _rglru(proj3, cols, conv_w, conv_b, wa, ba, wx, bx, lam):
    bsz, s, _ = proj3.shape
    g, w = wa.shape[0], wa.shape[-1]
    xa_blk = _col_block(cols["xa"], w)
    ga_blk = _col_block(cols["ga"], w)
    vec = lambda: pl.BlockSpec((1, w), lambda b, j: (0, j))
    return pl.pallas_call(
        _rglru_kernel,
        grid=(bsz, g),
        in_specs=[
            pl.BlockSpec((None, s, w), lambda b, j: (b, 0, xa_blk + j)),
            pl.BlockSpec((None, s, w), lambda b, j: (b, 0, ga_blk + j)),
            pl.BlockSpec((CONV_WIDTH, w), lambda b, j: (0, j)),
            vec(),
            pl.BlockSpec((None, w, w), lambda b, j: (j, 0, 0)),
            vec(),
            pl.BlockSpec((None, w, w), lambda b, j: (j, 0, 0)),
            vec(),
            vec(),
        ],
        out_specs=pl.BlockSpec((None, s, w), lambda b, j: (b, 0, j)),
        out_shape=jax.ShapeDtypeStruct((bsz, s, g * w), BF16),
        scratch_shapes=[pltpu.VMEM((s, w), F32), pltpu.VMEM((s, w), F32)],
        compiler_params=_cparams(("parallel", "parallel")),
        name="rglru",
    )(proj3, proj3, conv_w, conv_b, wa, ba, wx, bx, lam)


def _prep_kernel(q_ref, qi_ref, ckv_ref, sm_ref, wuk_ref, cg_ref, kg_ref, kb_ref,
                 qa_ref, qih_ref, c_ref, ct_ref, kn_ref, wit_ref):
    q = q_ref[...].astype(BF16)
    for h in range(ATT_HEADS):
        qa = jnp.dot(q[:, h * HEAD_DIM:(h + 1) * HEAD_DIM], wuk_ref[h], preferred_element_type=F32)
        qa_ref[h] = qa.astype(BF16)
    qi = qi_ref[...]
    for h in range(IDX_HEADS):
        qih_ref[h] = qi[:, h * IDX_DIM:(h + 1) * IDX_DIM].astype(BF16)

    ckv = ckv_ref[...]
    c = ckv * lax.rsqrt(jnp.mean(ckv * ckv, axis=-1, keepdims=True) + EPS) * cg_ref[...]
    c_ref[...] = c.astype(BF16)
    ct_ref[...] = c.T.astype(BF16)

    sm = sm_ref[...]
    ki = sm[:, :IDX_DIM]
    mu = jnp.mean(ki, axis=-1, keepdims=True)
    var = jnp.mean(jnp.square(ki - mu), axis=-1, keepdims=True)
    kn = (ki - mu) * lax.rsqrt(var + EPS) * kg_ref[...] + kb_ref[...]
    kn_ref[...] = kn.astype(BF16)
    wit_ref[...] = sm.T[IDX_DIM:IDX_DIM + IDX_HEADS, :] * (IDX_HEADS ** -0.5 * IDX_DIM ** -0.5)


def _col_block(offset, width):
    assert offset % width == 0
    return offset // width


def _prep(proj3, small3, cols, w_uk, ckv_g, k_g, k_b, tm=256):
    bsz, s, _ = proj3.shape
    att_w = ATT_HEADS * HEAD_DIM
    idx_w = IDX_HEADS * IDX_DIM
    q_blk = _col_block(cols["q"], att_w)
    qi_blk = _col_block(cols["qi"], idx_w)
    ckv_blk = _col_block(cols["ckv"], KV_LATENT)
    const = lambda shape: pl.BlockSpec(shape, lambda b, i: (0,) * len(shape))
    return pl.pallas_call(
        _prep_kernel,
        grid=(bsz, s // tm),
        in_specs=[
            pl.BlockSpec((None, tm, att_w), lambda b, i: (b, i, q_blk)),
            pl.BlockSpec((None, tm, idx_w), lambda b, i: (b, i, qi_blk)),
            pl.BlockSpec((None, tm, KV_LATENT), lambda b, i: (b, i, ckv_blk)),
            pl.BlockSpec((None, tm, small3.shape[-1]), lambda b, i: (b, i, 0)),
            const(w_uk.shape),
            const((1, KV_LATENT)),
            const((1, IDX_DIM)),
            const((1, IDX_DIM)),
        ],
        out_specs=[
            pl.BlockSpec((None, ATT_HEADS, tm, KV_LATENT), lambda b, i: (b, 0, i, 0)),
            pl.BlockSpec((None, IDX_HEADS, tm, IDX_DIM), lambda b, i: (b, 0, i, 0)),
            pl.BlockSpec((None, tm, KV_LATENT), lambda b, i: (b, i, 0)),
            pl.BlockSpec((None, KV_LATENT, tm), lambda b, i: (b, 0, i)),
            pl.BlockSpec((None, tm, IDX_DIM), lambda b, i: (b, i, 0)),
            pl.BlockSpec((None, IDX_HEADS, tm), lambda b, i: (b, 0, i)),
        ],
        out_shape=[
            jax.ShapeDtypeStruct((bsz, ATT_HEADS, s, KV_LATENT), BF16),
            jax.ShapeDtypeStruct((bsz, IDX_HEADS, s, IDX_DIM), BF16),
            jax.ShapeDtypeStruct((bsz, s, KV_LATENT), BF16),
            jax.ShapeDtypeStruct((bsz, KV_LATENT, s), BF16),
            jax.ShapeDtypeStruct((bsz, s, IDX_DIM), BF16),
            jax.ShapeDtypeStruct((bsz, IDX_HEADS, s), F32),
        ],
        compiler_params=_cparams(("parallel", "parallel")),
        name="prep",
    )(proj3, proj3, proj3, small3, w_uk, ckv_g, k_g, k_b)


def _dsa_kernel(qih_ref, wit_ref, kn_ref, c_ref, ct_ref, qa_ref, gb_ref, bias_ref, wuvt_ref,
                o_ref, key_s, lg_s, acc_s, topk):
    qb = pl.program_id(1)
    nkc = qb + 1
    ncc = (nkc * K_CHUNK + COUNT_CHUNK - 1) // COUNT_CHUNK
    nac = (nkc * K_CHUNK + ATT_CHUNK - 1) // ATT_CHUNK
    dn_t = (((1,), (1,)), ((), ()))

    kiota = lax.broadcasted_iota(I32, (K_CHUNK, Q_TILE), 0)
    qpos = qb * Q_TILE + lax.broadcasted_iota(I32, (K_CHUNK, Q_TILE), 1)

    def score_chunk(cc, carry):
        for t in range(COUNT_CHUNK // K_CHUNK):
            k0 = pl.multiple_of(cc * COUNT_CHUNK + t * K_CHUNK, K_CHUNK)
            kn = kn_ref[pl.ds(k0, K_CHUNK), :]
            acc = jnp.zeros((K_CHUNK, Q_TILE), F32)
            for h in range(IDX_HEADS):
                sc = lax.dot_general(kn, qih_ref[h], dn_t, preferred_element_type=F32)
                acc = acc + jnp.maximum(sc, 0.0) * wit_ref[h:h + 1, :]
            bits = lax.bitcast_convert_type(acc, I32)
            key = bits ^ ((bits >> 31) & 0x7FFFFFFF)
            key_s[pl.ds(k0, K_CHUNK), :] = jnp.where(kiota + k0 <= qpos, key, INT_MIN)
        return carry

    lax.fori_loop(0, ncc, score_chunk, 0)

    def bit_step(i, thr):
        cand = thr ^ (jnp.int32(1) << (31 - i))

        def count_chunk(cc, cnt):
            r0 = pl.multiple_of(cc * COUNT_CHUNK, COUNT_CHUNK)
            hit = (key_s[pl.ds(r0, COUNT_CHUNK), :] >= cand).astype(I32)
            return cnt + jnp.sum(hit.reshape(COUNT_CHUNK // 8, 8, Q_TILE), axis=0)

        cnt = lax.fori_loop(0, ncc, count_chunk, jnp.zeros((8, Q_TILE), I32))
        total = jnp.sum(cnt, axis=0, keepdims=True)
        return jnp.where(total >= topk, cand, thr)

    thr = lax.fori_loop(0, 32, bit_step, jnp.full((1, Q_TILE), INT_MIN, I32))
    thr = jnp.maximum(thr, INT_MIN + 1)

    scale = HEAD_DIM ** -0.5
    tiles = ATT_CHUNK // K_CHUNK

    def logit_chunk(ac, m8s):
        r0 = pl.multiple_of(ac * ATT_CHUNK, ATT_CHUNK)
        c_chunk = c_ref[pl.ds(r0, ATT_CHUNK), :]
        mbias = jnp.where(key_s[pl.ds(r0, ATT_CHUNK), :] >= thr, 0.0, NEG)
        near = [jnp.clip(ac * tiles + t - qb + 2, 0, 2) for t in range(tiles)]
        out = []
        for h in range(ATT_HEADS):
            lg = lax.dot_general(c_chunk, qa_ref[h], dn_t, preferred_element_type=F32) * scale + mbias
            lg = jnp.concatenate(
                [lg[t * K_CHUNK:(t + 1) * K_CHUNK] + bias_ref[h, near[t]] for t in range(tiles)], axis=0)
            lg_s[h, pl.ds(r0, ATT_CHUNK), :] = lg
            out.append(jnp.maximum(m8s[h], jnp.max(lg.reshape(ATT_CHUNK // 8, 8, Q_TILE), axis=0)))
        return tuple(out)

    m8s = lax.fori_loop(0, nac, logit_chunk,
                        tuple(jnp.full((8, Q_TILE), NEG, F32) for _ in range(ATT_HEADS)))
    ms = [jnp.max(m8, axis=0, keepdims=True) for m8 in m8s]

    acc_s[...] = jnp.zeros(acc_s.shape, F32)

    def pv_chunk(ac, l8s):
        r0 = pl.multiple_of(ac * ATT_CHUNK, ATT_CHUNK)
        ct_chunk = ct_ref[:, pl.ds(r0, ATT_CHUNK)]
        out = []
        for h in range(ATT_HEADS):
            p = jnp.exp(lg_s[h, pl.ds(r0, ATT_CHUNK), :] - ms[h])
            out.append(l8s[h] + jnp.sum(p.reshape(ATT_CHUNK // 8, 8, Q_TILE), axis=0))
            acc_s[h] += jnp.dot(ct_chunk, p.astype(BF16), preferred_element_type=F32)
        return tuple(out)

    l8s = lax.fori_loop(0, nac, pv_chunk, tuple(jnp.zeros((8, Q_TILE), F32) for _ in range(ATT_HEADS)))

    for h in range(ATT_HEADS):
        o_t = acc_s[h] * (1.0 / jnp.sum(l8s[h], axis=0, keepdims=True))
        y_t = jnp.dot(wuvt_ref[h], o_t.astype(BF16), preferred_element_type=F32)
        gb = gb_ref[:, h * HEAD_DIM:(h + 1) * HEAD_DIM]
        o_ref[:, h * HEAD_DIM:(h + 1) * HEAD_DIM] = (y_t.T * (gb * _sigmoid(gb))).astype(o_ref.dtype)


def _dsa(qih, wit, kn, c, ct, qa, proj3, gb_blk, bias_ext, wuvt, topk):
    bsz, s, _ = c.shape
    att_w = ATT_HEADS * HEAD_DIM
    const = lambda shape: pl.BlockSpec(shape, lambda b, i: (0,) * len(shape))
    s_pad = ((s + COUNT_CHUNK - 1) // COUNT_CHUNK) * COUNT_CHUNK
    return pl.pallas_call(
        functools.partial(_dsa_kernel, topk=topk),
        grid=(bsz, s // Q_TILE),
        in_specs=[
            pl.BlockSpec((None, IDX_HEADS, Q_TILE, IDX_DIM), lambda b, i: (b, 0, i, 0)),
            pl.BlockSpec((None, IDX_HEADS, Q_TILE), lambda b, i: (b, 0, i)),
            pl.BlockSpec((None, s, IDX_DIM), lambda b, i: (b, 0, 0)),
            pl.BlockSpec((None, s, KV_LATENT), lambda b, i: (b, 0, 0)),
            pl.BlockSpec((None, KV_LATENT, s), lambda b, i: (b, 0, 0)),
            pl.BlockSpec((None, ATT_HEADS, Q_TILE, KV_LATENT), lambda b, i: (b, 0, i, 0)),
            pl.BlockSpec((None, Q_TILE, att_w), lambda b, i: (b, i, gb_blk)),
            const(bias_ext.shape),
            const(wuvt.shape),
        ],
        out_specs=pl.BlockSpec((None, Q_TILE, att_w), lambda b, i: (b, i, 0)),
        out_shape=jax.ShapeDtypeStruct((bsz, s, att_w), BF16),
        scratch_shapes=[
            pltpu.VMEM((s_pad, Q_TILE), I32),
            pltpu.VMEM((ATT_HEADS, s, Q_TILE), F32),
            pltpu.VMEM((ATT_HEADS, KV_LATENT, Q_TILE), F32),
        ],
        compiler_params=_cparams(("parallel", "arbitrary")),
        name="dsa",
    )(qih, wit, kn, c, ct, qa, proj3, bias_ext, wuvt)


def _outp_kernel(ya_ref, yb_ref, wa_ref, wb_ref, x_ref, g_ref, o_ref, *, final_norm):
    acc = jnp.dot(ya_ref[...], wa_ref[...], preferred_element_type=F32)
    acc = acc + jnp.dot(yb_ref[...], wb_ref[...], preferred_element_type=F32)
    x = x_ref[...] + acc
    if final_norm:
        x = x * lax.rsqrt(jnp.mean(x * x, axis=-1, keepdims=True) + EPS) * g_ref[...]
    o_ref[...] = x


def _outp(ya, yb, w_a, w_b, x2, g, final_norm, tm=256):
    m, d = x2.shape
    ka, kb = ya.shape[1], yb.shape[1]
    return pl.pallas_call(
        functools.partial(_outp_kernel, final_norm=final_norm),
        grid=(m // tm,),
        in_specs=[
            pl.BlockSpec((tm, ka), lambda i: (i, 0)),
            pl.BlockSpec((tm, kb), lambda i: (i, 0)),
            pl.BlockSpec((ka, d), lambda i: (0, 0)),
            pl.BlockSpec((kb, d), lambda i: (0, 0)),
            pl.BlockSpec((tm, d), lambda i: (i, 0)),
            pl.BlockSpec((1, d), lambda i: (0, 0)),
        ],
        out_specs=pl.BlockSpec((tm, d), lambda i: (i, 0)),
        out_shape=jax.ShapeDtypeStruct((m, d), F32),
        compiler_params=_cparams(("parallel",)),
        name="outp",
    )(ya, yb, w_a, w_b, x2, g)


def _t5_bucket(dist):
    n = jnp.maximum(dist, 0)
    max_exact = REL_BUCKETS // 2
    nf = jnp.maximum(n, 1).astype(F32)
    large = max_exact + (jnp.log(nf / max_exact) / np.log(REL_MAX_DIST / max_exact)
                         * (REL_BUCKETS - max_exact)).astype(I32)
    large = jnp.minimum(large, REL_BUCKETS - 1)
    return jnp.where(n < max_exact, n, large)


def _bias_tiles(rel_bias):
    span = K_CHUNK + Q_TILE
    table = rel_bias[_t5_bucket(jnp.arange(span + 1, dtype=I32))].astype(F32)
    table = (table[:span] - table[span:]).T
    n = span + Q_TILE - 1
    a = jnp.concatenate([jnp.zeros((ATT_HEADS, Q_TILE - 1), F32), table], axis=1)
    shifted = jnp.tile(a, (1, span + 1))[:, :span * (n + 1)].reshape(ATT_HEADS, span, n + 1)
    tiles = shifted[:, ::-1, :Q_TILE].reshape(ATT_HEADS, 2, K_CHUNK, Q_TILE)
    return jnp.concatenate([jnp.zeros_like(tiles[:, :1]), tiles], axis=1)


def kernel(x, norm_g, w_in, conv_w, conv_b, lru_wa, lru_ba, lru_wx, lru_bx, lru_lambda, ckv_norm_g, idx_k_norm_g, idx_k_norm_b, w_uk, w_uv, w_out, rel_bias, final_norm_g):
    bsz, s, d = x.shape
    depth = w_in.shape[0]
    lru_w = lru_wa.shape[1] * lru_wa.shape[2]
    att_w = ATT_HEADS * HEAD_DIM
    idx_w = IDX_HEADS * IDX_DIM
    assert K_CHUNK == Q_TILE and REL_MAX_DIST <= K_CHUNK and s % COUNT_CHUNK == 0
    assert lru_w == att_w == idx_w and att_w % KV_LATENT == 0
    topk = min(INDEX_TOPK, s // 4)

    o_q = 2 * lru_w
    o_ckv = o_q + att_w
    o_gb = o_ckv + KV_LATENT
    o_qi = o_gb + att_w
    o_ki = o_qi + idx_w
    n_small = IDX_DIM + IDX_HEADS
    small_pad = 128 - n_small
    cols = {"xa": 0, "ga": lru_w, "q": o_q, "gb": o_q + att_w, "qi": o_q + 2 * att_w,
            "ckv": o_q + 2 * att_w + idx_w}

    bias_ext = _bias_tiles(rel_bias)
    x2 = x.reshape(bsz * s, d)
    for l in range(depth):
        wl = w_in[l]
        w_main = jnp.concatenate(
            [wl[:, :o_ckv], wl[:, o_gb:o_qi], wl[:, o_qi:o_ki], wl[:, o_ckv:o_gb]], axis=1).astype(BF16)
        w_small = jnp.pad(wl[:, o_ki:], ((0, 0), (0, small_pad))).astype(BF16)
        proj, small = _proj(x2, norm_g[l][None, :], w_main, w_small)
        proj3 = proj.reshape(bsz, s, -1)
        small3 = small.reshape(bsz, s, -1)

        ya = _rglru(proj3, cols, conv_w[l], conv_b[l][None, :], lru_wa[l].astype(BF16), lru_ba[l][None, :],
                    lru_wx[l].astype(BF16), lru_bx[l][None, :], lru_lambda[l][None, :])

        qa, qih, c, ct, kn, wit = _prep(proj3, small3, cols, w_uk[l].astype(BF16), ckv_norm_g[l][None, :],
                                        idx_k_norm_g[l][None, :], idx_k_norm_b[l][None, :])
        wuvt = jnp.transpose(w_uv[l], (0, 2, 1)).astype(BF16)
        yb = _dsa(qih, wit, kn, c, ct, qa, proj3, _col_block(cols["gb"], att_w), bias_ext, wuvt, topk)

        wo = w_out[l].astype(BF16)
        x2 = _outp(ya.reshape(bsz * s, lru_w), yb.reshape(bsz * s, att_w), wo[:lru_w], wo[lru_w:],
                   x2, final_norm_g[None, :], final_norm=(l == depth - 1))
    return x2.reshape(bsz, s, d)
```

```python
import functools

import numpy as np
import jax
import jax.numpy as jnp
from jax import lax
from jax.experimental import pallas as pl
from jax.experimental.pallas import tpu as pltpu

F32 = jnp.float32
BF16 = jnp.bfloat16
I32 = jnp.int32
I16 = jnp.int16

LRU_BLOCKS = 8
CONV_WIDTH = 4
LRU_C = 8.0
ATT_HEADS = 8
HEAD_DIM = 128
KV_LATENT = 256
IDX_HEADS = 16
IDX_DIM = 64
INDEX_TOPK = 256
REL_BUCKETS = 32
REL_MAX_DIST = 128
EPS = 1e-6

Q_TILE = 128
K_CHUNK = 128
ATT_CHUNK = 256
NEG = float(np.finfo(np.float32).min)
INT_MIN = -(2 ** 31)
I16_MIN = -(2 ** 15)
VMEM_LIMIT = 56 * 1024 * 1024


def _cparams(sem):
    return pltpu.CompilerParams(dimension_semantics=sem, vmem_limit_bytes=VMEM_LIMIT)


def _col_block(offset, width):
    assert offset % width == 0
    return offset // width


def _proj_kernel(x_ref, g_ref, w_ref, ws_ref, o_ref, os_ref, h_ref):
    @pl.when(pl.program_id(1) == 0)
    def _():
        x = x_ref[...]
        y = x * lax.rsqrt(jnp.mean(x * x, axis=-1, keepdims=True) + EPS)
        hb = (y * g_ref[...]).astype(BF16)
        h_ref[...] = hb
        os_ref[...] = jnp.dot(hb, ws_ref[...], preferred_element_type=F32)

    o_ref[...] = jnp.dot(h_ref[...], w_ref[...], preferred_element_type=F32)


def _proj(x2, g, w_main, w_small, tm=512, tn=768):
    m, d = x2.shape
    n = w_main.shape[1]
    ns = w_small.shape[1]
    return pl.pallas_call(
        _proj_kernel,
        grid=(m // tm, n // tn),
        in_specs=[
            pl.BlockSpec((tm, d), lambda i, j: (i, 0)),
            pl.BlockSpec((1, d), lambda i, j: (0, 0)),
            pl.BlockSpec((d, tn), lambda i, j: (0, j)),
            pl.BlockSpec((d, ns), lambda i, j: (0, 0)),
        ],
        out_specs=[
            pl.BlockSpec((tm, tn), lambda i, j: (i, j)),
            pl.BlockSpec((tm, ns), lambda i, j: (i, 0)),
        ],
        out_shape=[
            jax.ShapeDtypeStruct((m, n), F32),
            jax.ShapeDtypeStruct((m, ns), F32),
        ],
        scratch_shapes=[pltpu.VMEM((tm, d), BF16)],
        compiler_params=_cparams(("parallel", "arbitrary")),
        name="proj",
    )(x2, g, w_main, w_small)


def _sigmoid(v):
    return 1.0 / (1.0 + jnp.exp(-v))


def _rglru_kernel(xa_ref, ga_ref, cw_ref, cb_ref, wa_ref, ba_ref, wx_ref, bx_ref, lam_ref,
                  o_ref, a_s, b_s):
    s, w = xa_ref.shape
    xa = xa_ref[...]
    row = lax.broadcasted_iota(I32, (s, w), 0)

    def delayed(v, k):
        return jnp.where(row >= k, pltpu.roll(v, k, axis=0), 0.0)

    acc = delayed(xa, CONV_WIDTH - 1) * cw_ref[0:1, :]
    for j in range(1, CONV_WIDTH - 1):
        acc = acc + delayed(xa, CONV_WIDTH - 1 - j) * cw_ref[j:j + 1, :]
    acc = acc + xa * cw_ref[CONV_WIDTH - 1:CONV_WIDTH, :]
    xc = cb_ref[...] + acc

    xcb = xc.astype(BF16)
    r = _sigmoid(jnp.dot(xcb, wa_ref[...], preferred_element_type=F32) + ba_ref[...])
    i = _sigmoid(jnp.dot(xcb, wx_ref[...], preferred_element_type=F32) + bx_ref[...])
    z = -lam_ref[...]
    softplus = jnp.maximum(z, 0.0) + jnp.log1p(jnp.exp(-jnp.abs(z)))
    log_a = (-LRU_C) * r * softplus
    a = jnp.exp(log_a)
    y2 = 2.0 * log_a
    u = a * a
    em1 = jnp.where(u == 1.0, y2, (u - 1.0) * y2 / jnp.log(u))
    mult = jnp.sqrt(-em1)
    mult = jnp.where(row == 0, 1.0, mult)
    b = mult * (i * xc)

    row8 = row & 7
    for k in (1, 2, 4):
        keep = row8 >= k
        a_prev = jnp.where(keep, pltpu.roll(a, k, axis=0), 1.0)
        b_prev = jnp.where(keep, pltpu.roll(b, k, axis=0), 0.0)
        b = a * b_prev + b
        a = a * a_prev
    a_s[...] = a
    b_s[...] = b

    def carry(t, h_last):
        r0 = pl.multiple_of(t * 8, 8)
        h = a_s[pl.ds(r0, 8), :] * h_last + b_s[pl.ds(r0, 8), :]
        b_s[pl.ds(r0, 8), :] = h
        return h[7:8, :]

    lax.fori_loop(0, s // 8, carry, jnp.zeros((1, w), F32), unroll=8)

    ga = ga_ref[...]
    o_ref[...] = (b_s[...] * (ga * _sigmoid(ga))).astype(o_ref.dtype)


def _rglru(proj3, cols, conv_w, conv_b, wa, ba, wx, bx, lam):
    bsz, s, _ = proj3.shape
    g, w = wa.shape[0], wa.shape[-1]
    xa_blk = _col_block(cols["xa"], w)
    ga_blk = _col_block(cols["ga"], w)
    vec = lambda: pl.BlockSpec((1, w), lambda b, j: (0, j))
    return pl.pallas_call(
        _rglru_kernel,
        grid=(bsz, g),
        in_specs=[
            pl.BlockSpec((None, s, w), lambda b, j: (b, 0, xa_blk + j)),
            pl.BlockSpec((None, s, w), lambda b, j: (b, 0, ga_blk + j)),
            pl.BlockSpec((CONV_WIDTH, w), lambda b, j: (0, j)),
            vec(),
            pl.BlockSpec((None, w, w), lambda b, j: (j, 0, 0)),
            vec(),
            pl.BlockSpec((None, w, w), lambda b, j: (j, 0, 0)),
            vec(),
            vec(),
        ],
        out_specs=pl.BlockSpec((None, s, w), lambda b, j: (b, 0, j)),
        out_shape=jax.ShapeDtypeStruct((bsz, s, g * w), BF16),
        scratch_shapes=[pltpu.VMEM((s, w), F32), pltpu.VMEM((s, w), F32)],
        compiler_params=_cparams(("parallel", "parallel")),
        name="rglru",
    )(proj3, proj3, conv_w, conv_b, wa, ba, wx, bx, lam)


def _prep_kernel(q_ref, qi_ref, ckv_ref, sm_ref, wukt_ref, cg_ref, kg_ref, kb_ref,
                 qat_ref, qit_ref, c_ref, ct_ref, kn_ref, wit_ref):
    tm = q_ref.shape[0]
    qt = q_ref[...].T.astype(BF16)
    scale = HEAD_DIM ** -0.5
    qat = [(jnp.dot(wukt_ref[h], qt[h * HEAD_DIM:(h + 1) * HEAD_DIM], preferred_element_type=F32)
            * scale).astype(BF16) for h in range(ATT_HEADS)]
    qit = qi_ref[...].T.astype(BF16)
    for j in range(tm // Q_TILE):
        cols = slice(j * Q_TILE, (j + 1) * Q_TILE)
        for p in range(ATT_HEADS // 2):
            qat_ref[j, p] = jnp.concatenate([qat[2 * p][:, cols], qat[2 * p + 1][:, cols]], axis=1)
        for p in range(IDX_HEADS // 2):
            lo, hi = 2 * p * IDX_DIM, (2 * p + 1) * IDX_DIM
            qit_ref[j, p] = jnp.concatenate([qit[lo:lo + IDX_DIM, cols], qit[hi:hi + IDX_DIM, cols]], axis=1)

    ckv = ckv_ref[...]
    c = ckv * lax.rsqrt(jnp.mean(ckv * ckv, axis=-1, keepdims=True) + EPS) * cg_ref[...]
    c_ref[...] = c.astype(BF16)
    ct_ref[...] = c.T.astype(BF16)

    sm = sm_ref[...]
    ki = sm[:, :IDX_DIM]
    mu = jnp.mean(ki, axis=-1, keepdims=True)
    var = jnp.mean(jnp.square(ki - mu), axis=-1, keepdims=True)
    kn = (ki - mu) * lax.rsqrt(var + EPS) * kg_ref[...] + kb_ref[...]
    kn_ref[...] = kn.astype(BF16)
    wit_ref[...] = sm.T[IDX_DIM:IDX_DIM + IDX_HEADS, :] * (IDX_HEADS ** -0.5 * IDX_DIM ** -0.5)


def _prep(proj3, small3, cols, w_ukt, ckv_g, k_g, k_b, tm=256):
    bsz, s, _ = proj3.shape
    att_w = ATT_HEADS * HEAD_DIM
    idx_w = IDX_HEADS * IDX_DIM
    q_blk = _col_block(cols["q"], att_w)
    qi_blk = _col_block(cols["qi"], idx_w)
    ckv_blk = _col_block(cols["ckv"], KV_LATENT)
    tq = tm // Q_TILE
    const = lambda shape: pl.BlockSpec(shape, lambda b, i: (0,) * len(shape))
    return pl.pallas_call(
        _prep_kernel,
        grid=(bsz, s // tm),
        in_specs=[
            pl.BlockSpec((None, tm, att_w), lambda b, i: (b, i, q_blk)),
            pl.BlockSpec((None, tm, idx_w), lambda b, i: (b, i, qi_blk)),
            pl.BlockSpec((None, tm, KV_LATENT), lambda b, i: (b, i, ckv_blk)),
            pl.BlockSpec((None, tm, small3.shape[-1]), lambda b, i: (b, i, 0)),
            const(w_ukt.shape),
            const((1, KV_LATENT)),
            const((1, IDX_DIM)),
            const((1, IDX_DIM)),
        ],
        out_specs=[
            pl.BlockSpec((None, tq, ATT_HEADS // 2, KV_LATENT, 2 * Q_TILE), lambda b, i: (b, i, 0, 0, 0)),
            pl.BlockSpec((None, tq, IDX_HEADS // 2, IDX_DIM, 2 * Q_TILE), lambda b, i: (b, i, 0, 0, 0)),
            pl.BlockSpec((None, tm, KV_LATENT), lambda b, i: (b, i, 0)),
            pl.BlockSpec((None, KV_LATENT, tm), lambda b, i: (b, 0, i)),
            pl.BlockSpec((None, tm, IDX_DIM), lambda b, i: (b, i, 0)),
            pl.BlockSpec((None, IDX_HEADS, tm), lambda b, i: (b, 0, i)),
        ],
        out_shape=[
            jax.ShapeDtypeStruct((bsz, s // Q_TILE, ATT_HEADS // 2, KV_LATENT, 2 * Q_TILE), BF16),
            jax.ShapeDtypeStruct((bsz, s // Q_TILE, IDX_HEADS // 2, IDX_DIM, 2 * Q_TILE), BF16),
            jax.ShapeDtypeStruct((bsz, s, KV_LATENT), BF16),
            jax.ShapeDtypeStruct((bsz, KV_LATENT, s), BF16),
            jax.ShapeDtypeStruct((bsz, s, IDX_DIM), BF16),
            jax.ShapeDtypeStruct((bsz, IDX_HEADS, s), F32),
        ],
        compiler_params=_cparams(("parallel", "parallel")),
        name="prep",
    )(proj3, proj3, proj3, small3, w_ukt, ckv_g, k_g, k_b)


def _tree_sum(parts):
    while len(parts) > 1:
        paired = [parts[i] + parts[i + 1] for i in range(0, len(parts) - 1, 2)]
        parts = paired + ([parts[-1]] if len(parts) % 2 else [])
    return parts[0]


def _count(ref, rows, pred):
    lanes = 4
    accs = [None] * lanes
    for r in range(rows // 16):
        hit = jnp.where(pred(ref[r * 16:(r + 1) * 16, :]), jnp.ones((), BF16), jnp.zeros((), BF16))
        accs[r % lanes] = hit if accs[r % lanes] is None else accs[r % lanes] + hit
    cnt = _tree_sum([a for a in accs if a is not None])
    return jnp.sum(cnt.astype(F32), axis=0, keepdims=True)


def _search16(ref, rows, need):
    def step(i, u):
        cand = u | (jnp.int32(1) << (15 - i))
        c16 = (cand + I16_MIN).astype(I16)
        return jnp.where(_count(ref, rows, lambda v: v >= c16) >= need, cand, u)

    return lax.fori_loop(0, 16, step, jnp.zeros((1, Q_TILE), I32)) + I16_MIN


def _dsa_kernel(qit_ref, wit_ref, kn_ref, c_ref, ct_ref, qat_ref, gb_ref, bias_ref, wuvt_ref,
                o_ref, key_s, hi_s, lo_s, thr_s, lg_s, acc_s, topk):
    qb = pl.program_id(1)
    nkc = qb + 1
    nac = (nkc * K_CHUNK + ATT_CHUNK - 1) // ATT_CHUNK
    tiles = ATT_CHUNK // K_CHUNK
    pair_w = 2 * Q_TILE

    kiota = lax.broadcasted_iota(I32, (K_CHUNK, Q_TILE), 0)
    qpos = qb * Q_TILE + lax.broadcasted_iota(I32, (K_CHUNK, Q_TILE), 1)

    def score_chunk(ac, carry):
        for t in range(tiles):
            k0 = pl.multiple_of(ac * ATT_CHUNK + t * K_CHUNK, K_CHUNK)
            kn = kn_ref[pl.ds(k0, K_CHUNK), :]
            acc = jnp.zeros((K_CHUNK, Q_TILE), F32)
            for p in range(IDX_HEADS // 2):
                sc = jnp.dot(kn, qit_ref[p], preferred_element_type=F32)
                acc = acc + jnp.maximum(sc[:, :Q_TILE], 0.0) * wit_ref[2 * p:2 * p + 1, :]
                acc = acc + jnp.maximum(sc[:, Q_TILE:], 0.0) * wit_ref[2 * p + 1:2 * p + 2, :]
            bits = lax.bitcast_convert_type(acc, I32)
            key = bits ^ ((bits >> 31) & 0x7FFFFFFF)
            key = jnp.where(kiota + k0 <= qpos, key, INT_MIN)
            key_s[pl.ds(k0, K_CHUNK), :] = key
            hi_s[pl.ds(k0, K_CHUNK), :] = (key >> 16).astype(I16)
            lo_s[pl.ds(k0, K_CHUNK), :] = ((key & 0xFFFF) + I16_MIN).astype(I16)
        return carry

    lax.fori_loop(0, nac, score_chunk, 0)

    for v in range(1, key_s.shape[0] // ATT_CHUNK + 1):
        @pl.when(nac == v)
        def _(rows=v * ATT_CHUNK):
            t_hi = _search16(hi_s, rows, float(topk))
            t16 = t_hi.astype(I16)
            above = _count(hi_s, rows, lambda hv: hv > t16)
            lo_s[0:rows, :] = jnp.where(hi_s[0:rows, :] == t16, lo_s[0:rows, :], I16_MIN)
            t_lo = _search16(lo_s, rows, float(topk) - above)
            thr = (t_hi << 16) + (t_lo - I16_MIN)
            thr = jnp.maximum(thr, INT_MIN + 1)
            thr_s[...] = jnp.broadcast_to(thr, thr_s.shape)

    thr = thr_s[0:1, :]

    def logit_chunk(ac, m8s):
        r0 = pl.multiple_of(ac * ATT_CHUNK, ATT_CHUNK)
        c_chunk = c_ref[pl.ds(r0, ATT_CHUNK), :]
        mbias = jnp.where(key_s[pl.ds(r0, ATT_CHUNK), :] >= thr, 0.0, NEG)
        mbias = jnp.concatenate([mbias, mbias], axis=1)
        near = [jnp.clip(ac * tiles + t - qb + 2, 0, 2) for t in range(tiles)]
        out = []
        for p in range(ATT_HEADS // 2):
            lg = jnp.dot(c_chunk, qat_ref[p], preferred_element_type=F32) + mbias
            lg = jnp.concatenate(
                [lg[t * K_CHUNK:(t + 1) * K_CHUNK] + bias_ref[p, near[t]] for t in range(tiles)], axis=0)
            lg_s[p, pl.ds(r0, ATT_CHUNK), :] = lg
            out.append(jnp.maximum(m8s[p], jnp.max(lg.reshape(ATT_CHUNK // 8, 8, pair_w), axis=0)))
        return tuple(out)

    m8s = lax.fori_loop(0, nac, logit_chunk,
                        tuple(jnp.full((8, pair_w), NEG, F32) for _ in range(ATT_HEADS // 2)))
    ms = [jnp.max(m8, axis=0, keepdims=True) for m8 in m8s]

    acc_s[...] = jnp.zeros(acc_s.shape, F32)

    def pv_chunk(ac, l8s):
        r0 = pl.multiple_of(ac * ATT_CHUNK, ATT_CHUNK)
        ct_chunk = ct_ref[:, pl.ds(r0, ATT_CHUNK)]
        out = []
        for p in range(ATT_HEADS // 2):
            pr = jnp.exp(lg_s[p, pl.ds(r0, ATT_CHUNK), :] - ms[p])
            out.append(l8s[p] + jnp.sum(pr.reshape(ATT_CHUNK // 8, 8, pair_w), axis=0))
            acc_s[p] += jnp.dot(ct_chunk, pr.astype(BF16), preferred_element_type=F32)
        return tuple(out)

    l8s = lax.fori_loop(0, nac, pv_chunk,
                        tuple(jnp.zeros((8, pair_w), F32) for _ in range(ATT_HEADS // 2)))

    for h in range(ATT_HEADS):
        p, half = divmod(h, 2)
        lanes = slice(half * Q_TILE, (half + 1) * Q_TILE)
        denom = jnp.sum(l8s[p][:, lanes], axis=0, keepdims=True)
        o_t = acc_s[p, :, lanes] * (1.0 / denom)
        y_t = jnp.dot(wuvt_ref[h], o_t.astype(BF16), preferred_element_type=F32)
        gb = gb_ref[:, h * HEAD_DIM:(h + 1) * HEAD_DIM]
        o_ref[:, h * HEAD_DIM:(h + 1) * HEAD_DIM] = (y_t.T * (gb * _sigmoid(gb))).astype(o_ref.dtype)


def _dsa(qit, wit, kn, c, ct, qat, proj3, gb_blk, bias_pairs, wuvt, topk):
    bsz, s, _ = c.shape
    att_w = ATT_HEADS * HEAD_DIM
    assert s % ATT_CHUNK == 0 and s // 16 <= 256
    const = lambda shape: pl.BlockSpec(shape, lambda b, i: (0,) * len(shape))
    per_tile = lambda shape: pl.BlockSpec((None, None) + shape, lambda b, i: (b, i) + (0,) * len(shape))
    return pl.pallas_call(
        functools.partial(_dsa_kernel, topk=topk),
        grid=(bsz, s // Q_TILE),
        in_specs=[
            per_tile(qit.shape[2:]),
            pl.BlockSpec((None, IDX_HEADS, Q_TILE), lambda b, i: (b, 0, i)),
            pl.BlockSpec((None, s, IDX_DIM), lambda b, i: (b, 0, 0)),
            pl.BlockSpec((None, s, KV_LATENT), lambda b, i: (b, 0, 0)),
            pl.BlockSpec((None, KV_LATENT, s), lambda b, i: (b, 0, 0)),
            per_tile(qat.shape[2:]),
            pl.BlockSpec((None, Q_TILE, att_w), lambda b, i: (b, i, gb_blk)),
            const(bias_pairs.shape),
            const(wuvt.shape),
        ],
        out_specs=pl.BlockSpec((None, Q_TILE, att_w), lambda b, i: (b, i, 0)),
        out_shape=jax.ShapeDtypeStruct((bsz, s, att_w), BF16),
        scratch_shapes=[
            pltpu.VMEM((s, Q_TILE), I32),
            pltpu.VMEM((s, Q_TILE), I16),
            pltpu.VMEM((s, Q_TILE), I16),
            pltpu.VMEM((8, Q_TILE), I32),
            pltpu.VMEM((ATT_HEADS // 2, s, 2 * Q_TILE), F32),
            pltpu.VMEM((ATT_HEADS // 2, KV_LATENT, 2 * Q_TILE), F32),
        ],
        compiler_params=_cparams(("parallel", "arbitrary")),
        name="dsa",
    )(qit, wit, kn, c, ct, qat, proj3, bias_pairs, wuvt)


def _outp_kernel(ya_ref, yb_ref, wa_ref, wb_ref, x_ref, g_ref, o_ref, *, final_norm):
    acc = jnp.dot(ya_ref[...], wa_ref[...], preferred_element_type=F32)
    acc = acc + jnp.dot(yb_ref[...], wb_ref[...], preferred_element_type=F32)
    x = x_ref[...] + acc
    if final_norm:
        x = x * lax.rsqrt(jnp.mean(x * x, axis=-1, keepdims=True) + EPS) * g_ref[...]
    o_ref[...] = x


def _outp(ya, yb, w_a, w_b, x2, g, final_norm, tm=256):
    m, d = x2.shape
    ka, kb = ya.shape[1], yb.shape[1]
    return pl.pallas_call(
        functools.partial(_outp_kernel, final_norm=final_norm),
        grid=(m // tm,),
        in_specs=[
            pl.BlockSpec((tm, ka), lambda i: (i, 0)),
            pl.BlockSpec((tm, kb), lambda i: (i, 0)),
            pl.BlockSpec((ka, d), lambda i: (0, 0)),
            pl.BlockSpec((kb, d), lambda i: (0, 0)),
            pl.BlockSpec((tm, d), lambda i: (i, 0)),
            pl.BlockSpec((1, d), lambda i: (0, 0)),
        ],
        out_specs=pl.BlockSpec((tm, d), lambda i: (i, 0)),
        out_shape=jax.ShapeDtypeStruct((m, d), F32),
        compiler_params=_cparams(("parallel",)),
        name="outp",
    )(ya, yb, w_a, w_b, x2, g)


def _t5_bucket(dist):
    n = jnp.maximum(dist, 0)
    max_exact = REL_BUCKETS // 2
    nf = jnp.maximum(n, 1).astype(F32)
    large = max_exact + (jnp.log(nf / max_exact) / np.log(REL_MAX_DIST / max_exact)
                         * (REL_BUCKETS - max_exact)).astype(I32)
    large = jnp.minimum(large, REL_BUCKETS - 1)
    return jnp.where(n < max_exact, n, large)


def _bias_tiles(rel_bias):
    span = K_CHUNK + Q_TILE
    table = rel_bias[_t5_bucket(jnp.arange(span + 1, dtype=I32))].astype(F32)
    table = (table[:span] - table[span:]).T
    n = span + Q_TILE - 1
    a = jnp.concatenate([jnp.zeros((ATT_HEADS, Q_TILE - 1), F32), table], axis=1)
    shifted = jnp.tile(a, (1, span + 1))[:, :span * (n + 1)].reshape(ATT_HEADS, span, n + 1)
    tiles = shifted[:, ::-1, :Q_TILE].reshape(ATT_HEADS, 2, K_CHUNK, Q_TILE)
    tiles = jnp.concatenate([jnp.zeros_like(tiles[:, :1]), tiles], axis=1)
    pairs = tiles.reshape(ATT_HEADS // 2, 2, 3, K_CHUNK, Q_TILE)
    return jnp.transpose(pairs, (0, 2, 3, 1, 4)).reshape(ATT_HEADS // 2, 3, K_CHUNK, 2 * Q_TILE)


def kernel(x, norm_g, w_in, conv_w, conv_b, lru_wa, lru_ba, lru_wx, lru_bx, lru_lambda, ckv_norm_g, idx_k_norm_g, idx_k_norm_b, w_uk, w_uv, w_out, rel_bias, final_norm_g):
    bsz, s, d = x.shape
    depth = w_in.shape[0]
    lru_w = lru_wa.shape[1] * lru_wa.shape[2]
    att_w = ATT_HEADS * HEAD_DIM
    idx_w = IDX_HEADS * IDX_DIM
    assert K_CHUNK == Q_TILE and REL_MAX_DIST <= K_CHUNK
    assert lru_w == att_w == idx_w and att_w % KV_LATENT == 0
    topk = min(INDEX_TOPK, s // 4)

    o_q = 2 * lru_w
    o_ckv = o_q + att_w
    o_gb = o_ckv + KV_LATENT
    o_qi = o_gb + att_w
    o_ki = o_qi + idx_w
    n_small = IDX_DIM + IDX_HEADS
    small_pad = 128 - n_small
    cols = {"xa": 0, "ga": lru_w, "q": o_q, "gb": o_q + att_w, "qi": o_q + 2 * att_w,
            "ckv": o_q + 2 * att_w + idx_w}

    bias_pairs = _bias_tiles(rel_bias)
    x2 = x.reshape(bsz * s, d)
    for l in range(depth):
        wl = w_in[l]
        w_main = jnp.concatenate(
            [wl[:, :o_ckv], wl[:, o_gb:o_qi], wl[:, o_qi:o_ki], wl[:, o_ckv:o_gb]], axis=1).astype(BF16)
        w_small = jnp.pad(wl[:, o_ki:], ((0, 0), (0, small_pad))).astype(BF16)
        proj, small = _proj(x2, norm_g[l][None, :], w_main, w_small)
        proj3 = proj.reshape(bsz, s, -1)
        small3 = small.reshape(bsz, s, -1)

        ya = _rglru(proj3, cols, conv_w[l], conv_b[l][None, :], lru_wa[l].astype(BF16), lru_ba[l][None, :],
                    lru_wx[l].astype(BF16), lru_bx[l][None, :], lru_lambda[l][None, :])

        wukt = jnp.transpose(w_uk[l], (0, 2, 1)).astype(BF16)
        qat, qit, c, ct, kn, wit = _prep(proj3, small3, cols, wukt, ckv_norm_g[l][None, :],
                                          idx_k_norm_g[l][None, :], idx_k_norm_b[l][None, :])
        wuvt = jnp.transpose(w_uv[l], (0, 2, 1)).astype(BF16)
        yb = _dsa(qit, wit, kn, c, ct, qat, proj3, _col_block(cols["gb"], att_w), bias_pairs, wuvt, topk)

        wo = w_out[l].astype(BF16)
        x2 = _outp(ya.reshape(bsz * s, lru_w), yb.reshape(bsz * s, att_w), wo[:lru_w], wo[lru_w:],
                   x2, final_norm_g[None, :], final_norm=(l == depth - 1))
    return x2.reshape(bsz, s, d)
```

```python
import functools

import numpy as np
import jax
import jax.numpy as jnp
from jax import lax
from jax.experimental import pallas as pl
from jax.experimental.pallas import tpu as pltpu

F32 = jnp.float32
BF16 = jnp.bfloat16
I32 = jnp.int32

LRU_BLOCKS = 8
CONV_WIDTH = 4
LRU_C = 8.0
ATT_HEADS = 8
HEAD_DIM = 128
KV_LATENT = 256
IDX_HEADS = 16
IDX_DIM = 64
INDEX_TOPK = 256
REL_BUCKETS = 32
REL_MAX_DIST = 128
EPS = 1e-6

Q_TILE = 128
K_CHUNK = 128
ATT_CHUNK = 256
NEG = float(np.finfo(np.float32).min)
INT_MIN = -(2 ** 31)
VMEM_LIMIT = 56 * 1024 * 1024


def _cparams(sem):
    return pltpu.CompilerParams(dimension_semantics=sem, vmem_limit_bytes=VMEM_LIMIT)


def _col_block(offset, width):
    assert offset % width == 0
    return offset // width


def _proj_kernel(x_ref, g_ref, w_ref, oa_ref, ob_ref, oc_ref, h_ref, *, na, nb):
    j = pl.program_id(1)

    @pl.when(j == 0)
    def _():
        x = x_ref[...]
        y = x * lax.rsqrt(jnp.mean(x * x, axis=-1, keepdims=True) + EPS)
        h_ref[...] = (y * g_ref[...]).astype(BF16)

    @pl.when(j < na)
    def _():
        oa_ref[...] = jnp.dot(h_ref[...], w_ref[...], preferred_element_type=F32)

    @pl.when((j >= na) & (j < na + nb))
    def _():
        ob_ref[...] = jnp.dot(h_ref[...], w_ref[...], preferred_element_type=F32).astype(BF16)

    @pl.when(j >= na + nb)
    def _():
        oc_ref[...] = jnp.dot(h_ref[...], w_ref[...], preferred_element_type=F32)


def _proj(x2, g, w_all, n_f32, n_bf16, tm=1024, tn=512):
    m, d = x2.shape
    na, nb = n_f32 // tn, n_bf16 // tn
    assert n_f32 % tn == 0 and n_bf16 % tn == 0 and w_all.shape[1] == n_f32 + n_bf16 + tn
    return pl.pallas_call(
        functools.partial(_proj_kernel, na=na, nb=nb),
        grid=(m // tm, na + nb + 1),
        in_specs=[
            pl.BlockSpec((tm, d), lambda i, j: (i, 0)),
            pl.BlockSpec((1, d), lambda i, j: (0, 0)),
            pl.BlockSpec((d, tn), lambda i, j: (0, j)),
        ],
        out_specs=[
            pl.BlockSpec((tm, tn), lambda i, j: (i, jnp.minimum(j, na - 1))),
            pl.BlockSpec((tm, tn), lambda i, j: (i, jnp.clip(j - na, 0, nb - 1))),
            pl.BlockSpec((tm, tn), lambda i, j: (i, 0)),
        ],
        out_shape=[
            jax.ShapeDtypeStruct((m, n_f32), F32),
            jax.ShapeDtypeStruct((m, n_bf16), BF16),
            jax.ShapeDtypeStruct((m, tn), F32),
        ],
        scratch_shapes=[pltpu.VMEM((tm, d), BF16)],
        compiler_params=_cparams(("parallel", "arbitrary")),
        name="proj",
    )(x2, g, w_all)


def _sigmoid(v):
    return 1.0 / (1.0 + jnp.exp(-v))


def _rglru_kernel(xa_ref, ga_ref, cw_ref, cb_ref, wa_ref, ba_ref, wx_ref, bx_ref, lam_ref,
                  o_ref, a_s, b_s):
    s, w = xa_ref.shape
    xa = xa_ref[...]
    row = lax.broadcasted_iota(I32, (s, w), 0)

    def delayed(v, k):
        return jnp.where(row >= k, pltpu.roll(v, k, axis=0), 0.0)

    acc = delayed(xa, CONV_WIDTH - 1) * cw_ref[0:1, :]
    for j in range(1, CONV_WIDTH - 1):
        acc = acc + delayed(xa, CONV_WIDTH - 1 - j) * cw_ref[j:j + 1, :]
    acc = acc + xa * cw_ref[CONV_WIDTH - 1:CONV_WIDTH, :]
    xc = cb_ref[...] + acc

    xcb = xc.astype(BF16)
    r = _sigmoid(jnp.dot(xcb, wa_ref[...], preferred_element_type=F32) + ba_ref[...])
    i = _sigmoid(jnp.dot(xcb, wx_ref[...], preferred_element_type=F32) + bx_ref[...])
    z = -lam_ref[...]
    softplus = jnp.maximum(z, 0.0) + jnp.log1p(jnp.exp(-jnp.abs(z)))
    log_a = (-LRU_C) * r * softplus
    a = jnp.exp(log_a)
    y2 = 2.0 * log_a
    u = a * a
    em1 = jnp.where(u == 1.0, y2, (u - 1.0) * y2 / jnp.log(u))
    mult = jnp.sqrt(-em1)
    mult = jnp.where(row == 0, 1.0, mult)
    b = mult * (i * xc)

    row8 = row & 7
    for k in (1, 2, 4):
        keep = row8 >= k
        a_prev = jnp.where(keep, pltpu.roll(a, k, axis=0), 1.0)
        b_prev = jnp.where(keep, pltpu.roll(b, k, axis=0), 0.0)
        b = a * b_prev + b
        a = a * a_prev
    a_s[...] = a
    b_s[...] = b

    def carry(t, h_last):
        r0 = pl.multiple_of(t * 8, 8)
        h = a_s[pl.ds(r0, 8), :] * h_last + b_s[pl.ds(r0, 8), :]
        b_s[pl.ds(r0, 8), :] = h
        return h[7:8, :]

    lax.fori_loop(0, s // 8, carry, jnp.zeros((1, w), F32), unroll=8)

    ga = ga_ref[...]
    o_ref[...] = (b_s[...] * (ga * _sigmoid(ga))).astype(o_ref.dtype)


def _rglru(pa3, cols, conv_w, conv_b, wa, ba, wx, bx, lam):
    bsz, s, _ = pa3.shape
    g, w = wa.shape[0], wa.shape[-1]
    xa_blk = _col_block(cols["xa"], w)
    ga_blk = _col_block(cols["ga"], w)
    vec = lambda: pl.BlockSpec((1, w), lambda b, j: (0, j))
    return pl.pallas_call(
        _rglru_kernel,
        grid=(bsz, g),
        in_specs=[
            pl.BlockSpec((None, s, w), lambda b, j: (b, 0, xa_blk + j)),
            pl.BlockSpec((None, s, w), lambda b, j: (b, 0, ga_blk + j)),
            pl.BlockSpec((CONV_WIDTH, w), lambda b, j: (0, j)),
            vec(),
            pl.BlockSpec((None, w, w), lambda b, j: (j, 0, 0)),
            vec(),
            pl.BlockSpec((None, w, w), lambda b, j: (j, 0, 0)),
            vec(),
            vec(),
        ],
        out_specs=pl.BlockSpec((None, s, w), lambda b, j: (b, 0, j)),
        out_shape=jax.ShapeDtypeStruct((bsz, s, g * w), BF16),
        scratch_shapes=[pltpu.VMEM((s, w), F32), pltpu.VMEM((s, w), F32)],
        compiler_params=_cparams(("parallel", "parallel")),
        name="rglru",
    )(pa3, pa3, conv_w, conv_b, wa, ba, wx, bx, lam)


def _prep_kernel(q_ref, qi_ref, ckv_ref, sm_ref, wukt_ref, cg_ref, kg_ref, kb_ref,
                 qat_ref, qit_ref, c_ref, ct_ref, kn_ref, wit_ref):
    tm = q_ref.shape[0]
    qt = q_ref[...].T
    scale = HEAD_DIM ** -0.5
    qat = [(jnp.dot(wukt_ref[h], qt[h * HEAD_DIM:(h + 1) * HEAD_DIM], preferred_element_type=F32)
            * scale).astype(BF16) for h in range(ATT_HEADS)]
    qit = qi_ref[...].T
    for j in range(tm // Q_TILE):
        cols = slice(j * Q_TILE, (j + 1) * Q_TILE)
        for p in range(ATT_HEADS // 2):
            qat_ref[j, p] = jnp.concatenate([qat[2 * p][:, cols], qat[2 * p + 1][:, cols]], axis=1)
        for p in range(IDX_HEADS // 2):
            lo, hi = 2 * p * IDX_DIM, (2 * p + 1) * IDX_DIM
            qit_ref[j, p] = jnp.concatenate([qit[lo:lo + IDX_DIM, cols], qit[hi:hi + IDX_DIM, cols]], axis=1)

    ckv = ckv_ref[...]
    c = ckv * lax.rsqrt(jnp.mean(ckv * ckv, axis=-1, keepdims=True) + EPS) * cg_ref[...]
    c_ref[...] = c.astype(BF16)
    ct_ref[...] = c.T.astype(BF16)

    sm = sm_ref[...]
    ki = sm[:, :IDX_DIM]
    mu = jnp.mean(ki, axis=-1, keepdims=True)
    var = jnp.mean(jnp.square(ki - mu), axis=-1, keepdims=True)
    kn = (ki - mu) * lax.rsqrt(var + EPS) * kg_ref[...] + kb_ref[...]
    kn_ref[...] = kn.astype(BF16)
    wit_ref[...] = sm.T[IDX_DIM:IDX_DIM + IDX_HEADS, :] * (IDX_HEADS ** -0.5 * IDX_DIM ** -0.5)


def _prep(pb3, pc3, cols, w_ukt, ckv_g, k_g, k_b, tm=256):
    bsz, s, _ = pb3.shape
    att_w = ATT_HEADS * HEAD_DIM
    idx_w = IDX_HEADS * IDX_DIM
    q_blk = _col_block(cols["q"], att_w)
    qi_blk = _col_block(cols["qi"], idx_w)
    ckv_blk = _col_block(cols["ckv"], KV_LATENT)
    small_w = 128
    small_blk = _col_block(cols["small"], small_w)
    tq = tm // Q_TILE
    const = lambda shape: pl.BlockSpec(shape, lambda b, i: (0,) * len(shape))
    return pl.pallas_call(
        _prep_kernel,
        grid=(bsz, s // tm),
        in_specs=[
            pl.BlockSpec((None, tm, att_w), lambda b, i: (b, i, q_blk)),
            pl.BlockSpec((None, tm, idx_w), lambda b, i: (b, i, qi_blk)),
            pl.BlockSpec((None, tm, KV_LATENT), lambda b, i: (b, i, ckv_blk)),
            pl.BlockSpec((None, tm, small_w), lambda b, i: (b, i, small_blk)),
            const(w_ukt.shape),
            const((1, KV_LATENT)),
            const((1, IDX_DIM)),
            const((1, IDX_DIM)),
        ],
        out_specs=[
            pl.BlockSpec((None, tq, ATT_HEADS // 2, KV_LATENT, 2 * Q_TILE), lambda b, i: (b, i, 0, 0, 0)),
            pl.BlockSpec((None, tq, IDX_HEADS // 2, IDX_DIM, 2 * Q_TILE), lambda b, i: (b, i, 0, 0, 0)),
            pl.BlockSpec((None, tm, KV_LATENT), lambda b, i: (b, i, 0)),
            pl.BlockSpec((None, KV_LATENT, tm), lambda b, i: (b, 0, i)),
            pl.BlockSpec((None, tm, IDX_DIM), lambda b, i: (b, i, 0)),
            pl.BlockSpec((None, IDX_HEADS, tm), lambda b, i: (b, 0, i)),
        ],
        out_shape=[
            jax.ShapeDtypeStruct((bsz, s // Q_TILE, ATT_HEADS // 2, KV_LATENT, 2 * Q_TILE), BF16),
            jax.ShapeDtypeStruct((bsz, s // Q_TILE, IDX_HEADS // 2, IDX_DIM, 2 * Q_TILE), BF16),
            jax.ShapeDtypeStruct((bsz, s, KV_LATENT), BF16),
            jax.ShapeDtypeStruct((bsz, KV_LATENT, s), BF16),
            jax.ShapeDtypeStruct((bsz, s, IDX_DIM), BF16),
            jax.ShapeDtypeStruct((bsz, IDX_HEADS, s), F32),
        ],
        compiler_params=_cparams(("parallel", "parallel")),
        name="prep",
    )(pb3, pb3, pc3, pc3, w_ukt, ckv_g, k_g, k_b)


def _tree_sum(parts):
    while len(parts) > 1:
        paired = [parts[i] + parts[i + 1] for i in range(0, len(parts) - 1, 2)]
        parts = paired + ([parts[-1]] if len(parts) % 2 else [])
    return parts[0]


def _sortable_to_f32(u):
    key = u ^ INT_MIN
    return lax.bitcast_convert_type(key ^ ((key >> 31) & 0x7FFFFFFF), F32)


def _kth_largest(score_ref, rows, k):
    chains = 4

    def step(i, u):
        cand = u | (jnp.int32(1) << (31 - i))
        cand_f = _sortable_to_f32(cand)
        accs = [None] * chains
        for r in range(rows // 8):
            hit = jnp.where(score_ref[r * 8:(r + 1) * 8, :] >= cand_f, 1.0, 0.0)
            accs[r % chains] = hit if accs[r % chains] is None else accs[r % chains] + hit
        cnt = jnp.sum(_tree_sum([a for a in accs if a is not None]), axis=0, keepdims=True)
        return jnp.where(cnt >= k, cand, u)

    return _sortable_to_f32(lax.fori_loop(0, 32, step, jnp.zeros((1, Q_TILE), I32)))


def _dsa_kernel(qit_ref, wit_ref, kn_ref, c_ref, ct_ref, qat_ref, gb_ref, bias_ref, wuvt_ref,
                o_ref, score_s, thr_s, lg_s, acc_s, topk):
    qb = pl.program_id(1)
    nkc = qb + 1
    nac = (nkc * K_CHUNK + ATT_CHUNK - 1) // ATT_CHUNK
    tiles = ATT_CHUNK // K_CHUNK
    pair_w = 2 * Q_TILE

    kiota = lax.broadcasted_iota(I32, (K_CHUNK, Q_TILE), 0)
    qpos = qb * Q_TILE + lax.broadcasted_iota(I32, (K_CHUNK, Q_TILE), 1)

    def score_chunk(ac, carry):
        for t in range(tiles):
            k0 = pl.multiple_of(ac * ATT_CHUNK + t * K_CHUNK, K_CHUNK)
            kn = kn_ref[pl.ds(k0, K_CHUNK), :]
            acc = jnp.zeros((K_CHUNK, Q_TILE), F32)
            for p in range(IDX_HEADS // 2):
                sc = jnp.dot(kn, qit_ref[p], preferred_element_type=F32)
                acc = acc + jnp.maximum(sc[:, :Q_TILE], 0.0) * wit_ref[2 * p:2 * p + 1, :]
                acc = acc + jnp.maximum(sc[:, Q_TILE:], 0.0) * wit_ref[2 * p + 1:2 * p + 2, :]
            score_s[pl.ds(k0, K_CHUNK), :] = jnp.where(kiota + k0 <= qpos, acc, -jnp.inf)
        return carry

    lax.fori_loop(0, nac, score_chunk, 0)

    for v in range(1, score_s.shape[0] // ATT_CHUNK + 1):
        @pl.when(nac == v)
        def _(rows=v * ATT_CHUNK):
            thr = _kth_largest(score_s, rows, float(topk))
            thr = jnp.where(thr >= NEG, thr, NEG)
            thr_s[...] = jnp.broadcast_to(thr, thr_s.shape)

    thr = thr_s[0:1, :]

    def logit_chunk(ac, m8s):
        r0 = pl.multiple_of(ac * ATT_CHUNK, ATT_CHUNK)
        c_chunk = c_ref[pl.ds(r0, ATT_CHUNK), :]
        mbias = jnp.where(score_s[pl.ds(r0, ATT_CHUNK), :] >= thr, 0.0, NEG)
        mbias = jnp.concatenate([mbias, mbias], axis=1)
        near = [jnp.clip(ac * tiles + t - qb + 2, 0, 2) for t in range(tiles)]
        out = []
        for p in range(ATT_HEADS // 2):
            lg = jnp.dot(c_chunk, qat_ref[p], preferred_element_type=F32) + mbias
            lg = jnp.concatenate(
                [lg[t * K_CHUNK:(t + 1) * K_CHUNK] + bias_ref[p, near[t]] for t in range(tiles)], axis=0)
            lg_s[p, pl.ds(r0, ATT_CHUNK), :] = lg
            out.append(jnp.maximum(m8s[p], jnp.max(lg.reshape(ATT_CHUNK // 8, 8, pair_w), axis=0)))
        return tuple(out)

    m8s = lax.fori_loop(0, nac, logit_chunk,
                        tuple(jnp.full((8, pair_w), NEG, F32) for _ in range(ATT_HEADS // 2)))
    ms = [jnp.max(m8, axis=0, keepdims=True) for m8 in m8s]

    acc_s[...] = jnp.zeros(acc_s.shape, F32)

    def pv_chunk(ac, l8s):
        r0 = pl.multiple_of(ac * ATT_CHUNK, ATT_CHUNK)
        ct_chunk = ct_ref[:, pl.ds(r0, ATT_CHUNK)]
        out = []
        for p in range(ATT_HEADS // 2):
            pr = jnp.exp(lg_s[p, pl.ds(r0, ATT_CHUNK), :] - ms[p])
            out.append(l8s[p] + jnp.sum(pr.reshape(ATT_CHUNK // 8, 8, pair_w), axis=0))
            acc_s[p] += jnp.dot(ct_chunk, pr.astype(BF16), preferred_element_type=F32)
        return tuple(out)

    l8s = lax.fori_loop(0, nac, pv_chunk,
                        tuple(jnp.zeros((8, pair_w), F32) for _ in range(ATT_HEADS // 2)))

    for h in range(ATT_HEADS):
        p, half = divmod(h, 2)
        lanes = slice(half * Q_TILE, (half + 1) * Q_TILE)
        denom = jnp.sum(l8s[p][:, lanes], axis=0, keepdims=True)
        o_t = acc_s[p, :, lanes] * (1.0 / denom)
        y_t = jnp.dot(wuvt_ref[h], o_t.astype(BF16), preferred_element_type=F32)
        gb = gb_ref[:, h * HEAD_DIM:(h + 1) * HEAD_DIM]
        o_ref[:, h * HEAD_DIM:(h + 1) * HEAD_DIM] = (y_t.T * (gb * _sigmoid(gb))).astype(o_ref.dtype)


def _dsa(qit, wit, kn, c, ct, qat, pa3, gb_blk, bias_pairs, wuvt, topk):
    bsz, s, _ = c.shape
    att_w = ATT_HEADS * HEAD_DIM
    assert s % ATT_CHUNK == 0
    const = lambda shape: pl.BlockSpec(shape, lambda b, i: (0,) * len(shape))
    per_tile = lambda shape: pl.BlockSpec((None, None) + shape, lambda b, i: (b, i) + (0,) * len(shape))
    return pl.pallas_call(
        functools.partial(_dsa_kernel, topk=topk),
        grid=(bsz, s // Q_TILE),
        in_specs=[
            per_tile(qit.shape[2:]),
            pl.BlockSpec((None, IDX_HEADS, Q_TILE), lambda b, i: (b, 0, i)),
            pl.BlockSpec((None, s, IDX_DIM), lambda b, i: (b, 0, 0)),
            pl.BlockSpec((None, s, KV_LATENT), lambda b, i: (b, 0, 0)),
            pl.BlockSpec((None, KV_LATENT, s), lambda b, i: (b, 0, 0)),
            per_tile(qat.shape[2:]),
            pl.BlockSpec((None, Q_TILE, att_w), lambda b, i: (b, i, gb_blk)),
            const(bias_pairs.shape),
            const(wuvt.shape),
        ],
        out_specs=pl.BlockSpec((None, Q_TILE, att_w), lambda b, i: (b, i, 0)),
        out_shape=jax.ShapeDtypeStruct((bsz, s, att_w), BF16),
        scratch_shapes=[
            pltpu.VMEM((s, Q_TILE), F32),
            pltpu.VMEM((8, Q_TILE), F32),
            pltpu.VMEM((ATT_HEADS // 2, s, 2 * Q_TILE), F32),
            pltpu.VMEM((ATT_HEADS // 2, KV_LATENT, 2 * Q_TILE), F32),
        ],
        compiler_params=_cparams(("parallel", "arbitrary")),
        name="dsa",
    )(qit, wit, kn, c, ct, qat, pa3, bias_pairs, wuvt)


def _outp_kernel(ya_ref, yb_ref, wa_ref, wb_ref, x_ref, g_ref, o_ref, *, final_norm):
    acc = jnp.dot(ya_ref[...], wa_ref[...], preferred_element_type=F32)
    acc = acc + jnp.dot(yb_ref[...], wb_ref[...], preferred_element_type=F32)
    x = x_ref[...] + acc
    if final_norm:
        x = x * lax.rsqrt(jnp.mean(x * x, axis=-1, keepdims=True) + EPS) * g_ref[...]
    o_ref[...] = x


def _outp(ya, yb, w_out, x2, g, final_norm, tm=256):
    m, d = x2.shape
    ka, kb = ya.shape[1], yb.shape[1]
    assert ka == kb and w_out.shape[0] == ka + kb
    return pl.pallas_call(
        functools.partial(_outp_kernel, final_norm=final_norm),
        grid=(m // tm,),
        in_specs=[
            pl.BlockSpec((tm, ka), lambda i: (i, 0)),
            pl.BlockSpec((tm, kb), lambda i: (i, 0)),
            pl.BlockSpec((ka, d), lambda i: (0, 0)),
            pl.BlockSpec((kb, d), lambda i: (1, 0)),
            pl.BlockSpec((tm, d), lambda i: (i, 0)),
            pl.BlockSpec((1, d), lambda i: (0, 0)),
        ],
        out_specs=pl.BlockSpec((tm, d), lambda i: (i, 0)),
        out_shape=jax.ShapeDtypeStruct((m, d), F32),
        compiler_params=_cparams(("parallel",)),
        name="outp",
    )(ya, yb, w_out, w_out, x2, g)


def _t5_bucket(dist):
    n = jnp.maximum(dist, 0)
    max_exact = REL_BUCKETS // 2
    nf = jnp.maximum(n, 1).astype(F32)
    large = max_exact + (jnp.log(nf / max_exact) / np.log(REL_MAX_DIST / max_exact)
                         * (REL_BUCKETS - max_exact)).astype(I32)
    large = jnp.minimum(large, REL_BUCKETS - 1)
    return jnp.where(n < max_exact, n, large)


def _bias_tiles(rel_bias):
    span = K_CHUNK + Q_TILE
    table = rel_bias[_t5_bucket(jnp.arange(span + 1, dtype=I32))].astype(F32)
    table = (table[:span] - table[span:]).T
    n = span + Q_TILE - 1
    a = jnp.concatenate([jnp.zeros((ATT_HEADS, Q_TILE - 1), F32), table], axis=1)
    shifted = jnp.tile(a, (1, span + 1))[:, :span * (n + 1)].reshape(ATT_HEADS, span, n + 1)
    tiles = shifted[:, ::-1, :Q_TILE].reshape(ATT_HEADS, 2, K_CHUNK, Q_TILE)
    tiles = jnp.concatenate([jnp.zeros_like(tiles[:, :1]), tiles], axis=1)
    pairs = tiles.reshape(ATT_HEADS // 2, 2, 3, K_CHUNK, Q_TILE)
    return jnp.transpose(pairs, (0, 2, 3, 1, 4)).reshape(ATT_HEADS // 2, 3, K_CHUNK, 2 * Q_TILE)


def kernel(x, norm_g, w_in, conv_w, conv_b, lru_wa, lru_ba, lru_wx, lru_bx, lru_lambda, ckv_norm_g, idx_k_norm_g, idx_k_norm_b, w_uk, w_uv, w_out, rel_bias, final_norm_g):
    bsz, s, d = x.shape
    depth = w_in.shape[0]
    lru_w = lru_wa.shape[1] * lru_wa.shape[2]
    att_w = ATT_HEADS * HEAD_DIM
    idx_w = IDX_HEADS * IDX_DIM
    assert K_CHUNK == Q_TILE and REL_MAX_DIST <= K_CHUNK
    assert lru_w == att_w == idx_w and att_w % KV_LATENT == 0
    topk = min(INDEX_TOPK, s // 4)

    o_q = 2 * lru_w
    o_ckv = o_q + att_w
    o_gb = o_ckv + KV_LATENT
    o_qi = o_gb + att_w
    o_ki = o_qi + idx_w
    tn = 512
    cols_a = {"xa": 0, "ga": lru_w, "gb": 2 * lru_w}
    cols_b = {"q": 0, "qi": att_w}
    cols_c = {"ckv": 0, "small": KV_LATENT}
    n_f32, n_bf16 = 3 * lru_w, att_w + idx_w
    tail_pad = tn - KV_LATENT - (IDX_DIM + IDX_HEADS)

    bias_pairs = _bias_tiles(rel_bias)
    x2 = x.reshape(bsz * s, d)
    for l in range(depth):
        wl = w_in[l]
        w_all = jnp.concatenate(
            [wl[:, :o_q], wl[:, o_gb:o_qi], wl[:, o_q:o_ckv], wl[:, o_qi:o_ki], wl[:, o_ckv:o_gb], wl[:, o_ki:],
             jnp.zeros((d, tail_pad), wl.dtype)], axis=1).astype(BF16)
        pa, pb, pc = _proj(x2, norm_g[l][None, :], w_all, n_f32, n_bf16, tn=tn)
        pa3 = pa.reshape(bsz, s, -1)
        pb3 = pb.reshape(bsz, s, -1)
        pc3 = pc.reshape(bsz, s, -1)

        ya = _rglru(pa3, cols_a, conv_w[l], conv_b[l][None, :], lru_wa[l].astype(BF16), lru_ba[l][None, :],
                    lru_wx[l].astype(BF16), lru_bx[l][None, :], lru_lambda[l][None, :])

        wukt = jnp.transpose(w_uk[l], (0, 2, 1)).astype(BF16)
        qat, qit, c, ct, kn, wit = _prep(pb3, pc3, {**cols_b, **cols_c}, wukt, ckv_norm_g[l][None, :],
                                          idx_k_norm_g[l][None, :], idx_k_norm_b[l][None, :])
        wuvt = jnp.transpose(w_uv[l], (0, 2, 1)).astype(BF16)
        yb = _dsa(qit, wit, kn, c, ct, qat, pa3, _col_block(cols_a["gb"], att_w), bias_pairs, wuvt, topk)

        x2 = _outp(ya.reshape(bsz * s, lru_w), yb.reshape(bsz * s, att_w), w_out[l].astype(BF16), x2,
                   final_norm_g[None, :], final_norm=(l == depth - 1))
    return x2.reshape(bsz, s, d)
```

```python
import functools

import numpy as np
import jax
import jax.numpy as jnp
from jax import lax
from jax.experimental import pallas as pl
from jax.experimental.pallas import tpu as pltpu

F32 = jnp.float32
BF16 = jnp.bfloat16
I32 = jnp.int32

LRU_BLOCKS = 8
CONV_WIDTH = 4
LRU_C = 8.0
ATT_HEADS = 8
HEAD_DIM = 128
KV_LATENT = 256
IDX_HEADS = 16
IDX_DIM = 64
INDEX_TOPK = 256
REL_BUCKETS = 32
REL_MAX_DIST = 128
EPS = 1e-6

Q_TILE = 128
K_CHUNK = 128
ATT_CHUNK = 256
NEG = float(np.finfo(np.float32).min)
INT_MIN = -(2 ** 31)
VMEM_LIMIT = 56 * 1024 * 1024


def _cparams(sem):
    return pltpu.CompilerParams(dimension_semantics=sem, vmem_limit_bytes=VMEM_LIMIT)


def _col_block(offset, width):
    assert offset % width == 0
    return offset // width


def _proj_kernel(x_ref, g_ref, w_ref, oa_ref, ob_ref, oc_ref, h_ref, *, na, nb):
    j = pl.program_id(1)

    @pl.when(j == 0)
    def _():
        x = x_ref[...]
        y = x * lax.rsqrt(jnp.mean(x * x, axis=-1, keepdims=True) + EPS)
        h_ref[...] = (y * g_ref[...]).astype(BF16)

    @pl.when(j < na)
    def _():
        oa_ref[...] = jnp.dot(h_ref[...], w_ref[...], preferred_element_type=F32)

    @pl.when((j >= na) & (j < na + nb))
    def _():
        ob_ref[...] = jnp.dot(h_ref[...], w_ref[...], preferred_element_type=F32).astype(BF16)

    @pl.when(j >= na + nb)
    def _():
        oc_ref[...] = jnp.dot(h_ref[...], w_ref[...], preferred_element_type=F32)


def _proj(x2, g, w_all, n_f32, n_bf16, tm=1024, tn=512):
    m, d = x2.shape
    na, nb = n_f32 // tn, n_bf16 // tn
    assert n_f32 % tn == 0 and n_bf16 % tn == 0 and w_all.shape[1] == n_f32 + n_bf16 + tn
    return pl.pallas_call(
        functools.partial(_proj_kernel, na=na, nb=nb),
        grid=(m // tm, na + nb + 1),
        in_specs=[
            pl.BlockSpec((tm, d), lambda i, j: (i, 0)),
            pl.BlockSpec((1, d), lambda i, j: (0, 0)),
            pl.BlockSpec((d, tn), lambda i, j: (0, j)),
        ],
        out_specs=[
            pl.BlockSpec((tm, tn), lambda i, j: (i, jnp.minimum(j, na - 1))),
            pl.BlockSpec((tm, tn), lambda i, j: (i, jnp.clip(j - na, 0, nb - 1))),
            pl.BlockSpec((tm, tn), lambda i, j: (i, 0)),
        ],
        out_shape=[
            jax.ShapeDtypeStruct((m, n_f32), F32),
            jax.ShapeDtypeStruct((m, n_bf16), BF16),
            jax.ShapeDtypeStruct((m, tn), F32),
        ],
        scratch_shapes=[pltpu.VMEM((tm, d), BF16)],
        compiler_params=_cparams(("parallel", "arbitrary")),
        name="proj",
    )(x2, g, w_all)


def _sigmoid(v):
    return 0.5 * jnp.tanh(0.5 * v) + 0.5


def _scan_step(a, b, k, axis, idx):
    keep = idx >= k
    a_prev = jnp.where(keep, pltpu.roll(a, k, axis=axis), 1.0)
    b_prev = jnp.where(keep, pltpu.roll(b, k, axis=axis), 0.0)
    return a * a_prev, a * b_prev + b


def _rglru_kernel(xa_ref, ga_ref, cw_ref, cb_ref, wa_ref, ba_ref, wx_ref, bx_ref, lam_ref,
                  o_ref, pad_s, a_s, b_s, c_s):
    s, w = xa_ref.shape
    tile = 8
    n_tiles = s // tile

    pad_s[0:tile, :] = jnp.zeros((tile, w), F32)
    pad_s[tile:tile + s, :] = xa_ref[...]
    acc = pad_s[tile:tile + s, :] * cw_ref[CONV_WIDTH - 1:CONV_WIDTH, :]
    for j in range(CONV_WIDTH - 1):
        back = CONV_WIDTH - 1 - j
        acc = acc + pad_s[tile - back:tile - back + s, :] * cw_ref[j:j + 1, :]
    xc = cb_ref[...] + acc

    xcb = xc.astype(BF16)
    r = _sigmoid(jnp.dot(xcb, wa_ref[...], preferred_element_type=F32) + ba_ref[...])
    i = _sigmoid(jnp.dot(xcb, wx_ref[...], preferred_element_type=F32) + bx_ref[...])
    z = -lam_ref[...]
    softplus = jnp.maximum(z, 0.0) + jnp.log1p(jnp.exp(-jnp.abs(z)))
    log_a = (-LRU_C) * r * softplus
    a = jnp.exp(log_a)
    m2 = (1.0 + a * a) * jnp.tanh(-log_a)
    mult = jnp.where(m2 > 0.0, m2 * lax.rsqrt(m2), 0.0)
    gated = i * xc
    b_s[...] = mult * gated
    b_s[0:1, :] = gated[0:1, :]

    a3 = a.reshape(n_tiles, tile, w)
    b3 = b_s[...].reshape(n_tiles, tile, w)
    sub = lax.broadcasted_iota(I32, (n_tiles, tile, w), 1)
    for k in (1, 2, 4):
        a3, b3 = _scan_step(a3, b3, k, 1, sub)
    a_s[...] = a3.reshape(s, w)
    b_s[...] = b3.reshape(s, w)

    at = a_s[pl.ds(tile - 1, n_tiles, stride=tile), :]
    bt = b_s[pl.ds(tile - 1, n_tiles, stride=tile), :]
    trow = lax.broadcasted_iota(I32, (n_tiles, w), 0)
    k = 1
    while k < n_tiles:
        at, bt = _scan_step(at, bt, k, 0, trow)
        k *= 2
    c_s[0:tile, :] = jnp.zeros((tile, w), F32)
    c_s[tile:tile + n_tiles, :] = bt

    def apply(t, carry):
        r0 = pl.multiple_of(t * tile, tile)
        before = c_s[pl.ds(tile - 1 + t, tile, stride=0), :]
        h = a_s[pl.ds(r0, tile), :] * before + b_s[pl.ds(r0, tile), :]
        ga = ga_ref[pl.ds(r0, tile), :]
        o_ref[pl.ds(r0, tile), :] = (h * (ga * _sigmoid(ga))).astype(o_ref.dtype)
        return carry

    lax.fori_loop(0, n_tiles, apply, 0, unroll=8)


def _rglru(pa3, cols, conv_w, conv_b, wa, ba, wx, bx, lam):
    bsz, s, _ = pa3.shape
    g, w = wa.shape[0], wa.shape[-1]
    xa_blk = _col_block(cols["xa"], w)
    ga_blk = _col_block(cols["ga"], w)
    vec = lambda: pl.BlockSpec((1, w), lambda b, j: (0, j))
    return pl.pallas_call(
        _rglru_kernel,
        grid=(bsz, g),
        in_specs=[
            pl.BlockSpec((None, s, w), lambda b, j: (b, 0, xa_blk + j)),
            pl.BlockSpec((None, s, w), lambda b, j: (b, 0, ga_blk + j)),
            pl.BlockSpec((CONV_WIDTH, w), lambda b, j: (0, j)),
            vec(),
            pl.BlockSpec((None, w, w), lambda b, j: (j, 0, 0)),
            vec(),
            pl.BlockSpec((None, w, w), lambda b, j: (j, 0, 0)),
            vec(),
            vec(),
        ],
        out_specs=pl.BlockSpec((None, s, w), lambda b, j: (b, 0, j)),
        out_shape=jax.ShapeDtypeStruct((bsz, s, g * w), BF16),
        scratch_shapes=[pltpu.VMEM((s + 8, w), F32), pltpu.VMEM((s, w), F32), pltpu.VMEM((s, w), F32),
                        pltpu.VMEM((s // 8 + 8, w), F32)],
        compiler_params=_cparams(("parallel", "parallel")),
        name="rglru",
    )(pa3, pa3, conv_w, conv_b, wa, ba, wx, bx, lam)


def _prep_kernel(q_ref, qi_ref, ckv_ref, sm_ref, wukt_ref, cg_ref, kg_ref, kb_ref,
                 qat_ref, qit_ref, c_ref, ct_ref, kn_ref, wit_ref):
    tm = q_ref.shape[0]
    qt = q_ref[...].T
    scale = HEAD_DIM ** -0.5
    qat = [(jnp.dot(wukt_ref[h], qt[h * HEAD_DIM:(h + 1) * HEAD_DIM], preferred_element_type=F32)
            * scale).astype(BF16) for h in range(ATT_HEADS)]
    qit = qi_ref[...].T
    for j in range(tm // Q_TILE):
        cols = slice(j * Q_TILE, (j + 1) * Q_TILE)
        for p in range(ATT_HEADS // 2):
            qat_ref[j, p] = jnp.concatenate([qat[2 * p][:, cols], qat[2 * p + 1][:, cols]], axis=1)
        for p in range(IDX_HEADS // 2):
            lo, hi = 2 * p * IDX_DIM, (2 * p + 1) * IDX_DIM
            qit_ref[j, p] = jnp.concatenate([qit[lo:lo + IDX_DIM, cols], qit[hi:hi + IDX_DIM, cols]], axis=1)

    ckv = ckv_ref[...]
    c = ckv * lax.rsqrt(jnp.mean(ckv * ckv, axis=-1, keepdims=True) + EPS) * cg_ref[...]
    c_ref[...] = c.astype(BF16)
    ct_ref[...] = c.T.astype(BF16)

    sm = sm_ref[...]
    ki = sm[:, :IDX_DIM]
    mu = jnp.mean(ki, axis=-1, keepdims=True)
    var = jnp.mean(jnp.square(ki - mu), axis=-1, keepdims=True)
    kn = (ki - mu) * lax.rsqrt(var + EPS) * kg_ref[...] + kb_ref[...]
    kn_ref[...] = kn.astype(BF16)
    wit_ref[...] = sm.T[IDX_DIM:IDX_DIM + IDX_HEADS, :] * (IDX_HEADS ** -0.5 * IDX_DIM ** -0.5)


def _prep(pb3, pc3, cols, w_ukt, ckv_g, k_g, k_b, tm=256):
    bsz, s, _ = pb3.shape
    att_w = ATT_HEADS * HEAD_DIM
    idx_w = IDX_HEADS * IDX_DIM
    q_blk = _col_block(cols["q"], att_w)
    qi_blk = _col_block(cols["qi"], idx_w)
    ckv_blk = _col_block(cols["ckv"], KV_LATENT)
    small_w = 128
    small_blk = _col_block(cols["small"], small_w)
    tq = tm // Q_TILE
    const = lambda shape: pl.BlockSpec(shape, lambda b, i: (0,) * len(shape))
    return pl.pallas_call(
        _prep_kernel,
        grid=(bsz, s // tm),
        in_specs=[
            pl.BlockSpec((None, tm, att_w), lambda b, i: (b, i, q_blk)),
            pl.BlockSpec((None, tm, idx_w), lambda b, i: (b, i, qi_blk)),
            pl.BlockSpec((None, tm, KV_LATENT), lambda b, i: (b, i, ckv_blk)),
            pl.BlockSpec((None, tm, small_w), lambda b, i: (b, i, small_blk)),
            const(w_ukt.shape),
            const((1, KV_LATENT)),
            const((1, IDX_DIM)),
            const((1, IDX_DIM)),
        ],
        out_specs=[
            pl.BlockSpec((None, tq, ATT_HEADS // 2, KV_LATENT, 2 * Q_TILE), lambda b, i: (b, i, 0, 0, 0)),
            pl.BlockSpec((None, tq, IDX_HEADS // 2, IDX_DIM, 2 * Q_TILE), lambda b, i: (b, i, 0, 0, 0)),
            pl.BlockSpec((None, tm, KV_LATENT), lambda b, i: (b, i, 0)),
            pl.BlockSpec((None, KV_LATENT, tm), lambda b, i: (b, 0, i)),
            pl.BlockSpec((None, tm, IDX_DIM), lambda b, i: (b, i, 0)),
            pl.BlockSpec((None, IDX_HEADS, tm), lambda b, i: (b, 0, i)),
        ],
        out_shape=[
            jax.ShapeDtypeStruct((bsz, s // Q_TILE, ATT_HEADS // 2, KV_LATENT, 2 * Q_TILE), BF16),
            jax.ShapeDtypeStruct((bsz, s // Q_TILE, IDX_HEADS // 2, IDX_DIM, 2 * Q_TILE), BF16),
            jax.ShapeDtypeStruct((bsz, s, KV_LATENT), BF16),
            jax.ShapeDtypeStruct((bsz, KV_LATENT, s), BF16),
            jax.ShapeDtypeStruct((bsz, s, IDX_DIM), BF16),
            jax.ShapeDtypeStruct((bsz, IDX_HEADS, s), F32),
        ],
        compiler_params=_cparams(("parallel", "parallel")),
        name="prep",
    )(pb3, pb3, pc3, pc3, w_ukt, ckv_g, k_g, k_b)


def _tree_sum(parts):
    while len(parts) > 1:
        paired = [parts[i] + parts[i + 1] for i in range(0, len(parts) - 1, 2)]
        parts = paired + ([parts[-1]] if len(parts) % 2 else [])
    return parts[0]


def _sortable_to_f32(u):
    key = u ^ INT_MIN
    return lax.bitcast_convert_type(key ^ ((key >> 31) & 0x7FFFFFFF), F32)


def _kth_largest(score_ref, rows, k):
    chains = 4

    def step(i, u):
        cand = u | (jnp.int32(1) << (31 - i))
        cand_f = _sortable_to_f32(cand)
        accs = [None] * chains
        for r in range(rows // 8):
            hit = jnp.where(score_ref[r * 8:(r + 1) * 8, :] >= cand_f, 1.0, 0.0)
            accs[r % chains] = hit if accs[r % chains] is None else accs[r % chains] + hit
        cnt = jnp.sum(_tree_sum([a for a in accs if a is not None]), axis=0, keepdims=True)
        return jnp.where(cnt >= k, cand, u)

    return _sortable_to_f32(lax.fori_loop(0, 32, step, jnp.zeros((1, Q_TILE), I32)))


def _dsa_kernel(qit_ref, wit_ref, kn_ref, c_ref, ct_ref, qat_ref, gb_ref, bias_ref, wuvt_ref,
                o_ref, score_s, thr_s, lg_s, acc_s, topk):
    qb = pl.program_id(1)
    nkc = qb + 1
    nac = (nkc * K_CHUNK + ATT_CHUNK - 1) // ATT_CHUNK
    tiles = ATT_CHUNK // K_CHUNK
    pair_w = 2 * Q_TILE

    kiota = lax.broadcasted_iota(I32, (K_CHUNK, Q_TILE), 0)
    qpos = qb * Q_TILE + lax.broadcasted_iota(I32, (K_CHUNK, Q_TILE), 1)

    def score_chunk(ac, carry):
        for t in range(tiles):
            k0 = pl.multiple_of(ac * ATT_CHUNK + t * K_CHUNK, K_CHUNK)
            kn = kn_ref[pl.ds(k0, K_CHUNK), :]
            acc = jnp.zeros((K_CHUNK, Q_TILE), F32)
            for p in range(IDX_HEADS // 2):
                sc = jnp.dot(kn, qit_ref[p], preferred_element_type=F32)
                acc = acc + jnp.maximum(sc[:, :Q_TILE], 0.0) * wit_ref[2 * p:2 * p + 1, :]
                acc = acc + jnp.maximum(sc[:, Q_TILE:], 0.0) * wit_ref[2 * p + 1:2 * p + 2, :]
            score_s[pl.ds(k0, K_CHUNK), :] = jnp.where(kiota + k0 <= qpos, acc, -jnp.inf)
        return carry

    lax.fori_loop(0, nac, score_chunk, 0)

    for v in range(1, score_s.shape[0] // ATT_CHUNK + 1):
        @pl.when(nac == v)
        def _(rows=v * ATT_CHUNK):
            thr = _kth_largest(score_s, rows, float(topk))
            thr = jnp.where(thr >= NEG, thr, NEG)
            thr_s[...] = jnp.broadcast_to(thr, thr_s.shape)

    thr = thr_s[0:1, :]

    def logit_chunk(ac, m8s):
        r0 = pl.multiple_of(ac * ATT_CHUNK, ATT_CHUNK)
        c_chunk = c_ref[pl.ds(r0, ATT_CHUNK), :]
        mbias = jnp.where(score_s[pl.ds(r0, ATT_CHUNK), :] >= thr, 0.0, NEG)
        mbias = jnp.concatenate([mbias, mbias], axis=1)
        near = [jnp.clip(ac * tiles + t - qb + 2, 0, 2) for t in range(tiles)]
        out = []
        for p in range(ATT_HEADS // 2):
            lg = jnp.dot(c_chunk, qat_ref[p], preferred_element_type=F32) + mbias
            lg = jnp.concatenate(
                [lg[t * K_CHUNK:(t + 1) * K_CHUNK] + bias_ref[p, near[t]] for t in range(tiles)], axis=0)
            lg_s[p, pl.ds(r0, ATT_CHUNK), :] = lg
            out.append(jnp.maximum(m8s[p], jnp.max(lg.reshape(ATT_CHUNK // 8, 8, pair_w), axis=0)))
        return tuple(out)

    m8s = lax.fori_loop(0, nac, logit_chunk,
                        tuple(jnp.full((8, pair_w), NEG, F32) for _ in range(ATT_HEADS // 2)))
    ms = [jnp.max(m8, axis=0, keepdims=True) for m8 in m8s]

    acc_s[...] = jnp.zeros(acc_s.shape, F32)

    def pv_chunk(ac, l8s):
        r0 = pl.multiple_of(ac * ATT_CHUNK, ATT_CHUNK)
        ct_chunk = ct_ref[:, pl.ds(r0, ATT_CHUNK)]
        out = []
        for p in range(ATT_HEADS // 2):
            pr = jnp.exp(lg_s[p, pl.ds(r0, ATT_CHUNK), :] - ms[p])
            out.append(l8s[p] + jnp.sum(pr.reshape(ATT_CHUNK // 8, 8, pair_w), axis=0))
            acc_s[p] += jnp.dot(ct_chunk, pr.astype(BF16), preferred_element_type=F32)
        return tuple(out)

    l8s = lax.fori_loop(0, nac, pv_chunk,
                        tuple(jnp.zeros((8, pair_w), F32) for _ in range(ATT_HEADS // 2)))

    for h in range(ATT_HEADS):
        p, half = divmod(h, 2)
        lanes = slice(half * Q_TILE, (half + 1) * Q_TILE)
        denom = jnp.sum(l8s[p][:, lanes], axis=0, keepdims=True)
        o_t = acc_s[p, :, lanes] * (1.0 / denom)
        y_t = jnp.dot(wuvt_ref[h], o_t.astype(BF16), preferred_element_type=F32)
        gb = gb_ref[:, h * HEAD_DIM:(h + 1) * HEAD_DIM]
        o_ref[:, h * HEAD_DIM:(h + 1) * HEAD_DIM] = (y_t.T * (gb * _sigmoid(gb))).astype(o_ref.dtype)


def _dsa(qit, wit, kn, c, ct, qat, pa3, gb_blk, bias_pairs, wuvt, topk):
    bsz, s, _ = c.shape
    att_w = ATT_HEADS * HEAD_DIM
    assert s % ATT_CHUNK == 0
    const = lambda shape: pl.BlockSpec(shape, lambda b, i: (0,) * len(shape))
    per_tile = lambda shape: pl.BlockSpec((None, None) + shape, lambda b, i: (b, i) + (0,) * len(shape))
    return pl.pallas_call(
        functools.partial(_dsa_kernel, topk=topk),
        grid=(bsz, s // Q_TILE),
        in_specs=[
            per_tile(qit.shape[2:]),
            pl.BlockSpec((None, IDX_HEADS, Q_TILE), lambda b, i: (b, 0, i)),
            pl.BlockSpec((None, s, IDX_DIM), lambda b, i: (b, 0, 0)),
            pl.BlockSpec((None, s, KV_LATENT), lambda b, i: (b, 0, 0)),
            pl.BlockSpec((None, KV_LATENT, s), lambda b, i: (b, 0, 0)),
            per_tile(qat.shape[2:]),
            pl.BlockSpec((None, Q_TILE, att_w), lambda b, i: (b, i, gb_blk)),
            const(bias_pairs.shape),
            const(wuvt.shape),
        ],
        out_specs=pl.BlockSpec((None, Q_TILE, att_w), lambda b, i: (b, i, 0)),
        out_shape=jax.ShapeDtypeStruct((bsz, s, att_w), BF16),
        scratch_shapes=[
            pltpu.VMEM((s, Q_TILE), F32),
            pltpu.VMEM((8, Q_TILE), F32),
            pltpu.VMEM((ATT_HEADS // 2, s, 2 * Q_TILE), F32),
            pltpu.VMEM((ATT_HEADS // 2, KV_LATENT, 2 * Q_TILE), F32),
        ],
        compiler_params=_cparams(("parallel", "arbitrary")),
        name="dsa",
    )(qit, wit, kn, c, ct, qat, pa3, bias_pairs, wuvt)


def _outp_kernel(ya_ref, yb_ref, wa_ref, wb_ref, x_ref, g_ref, o_ref, *, final_norm):
    acc = jnp.dot(ya_ref[...], wa_ref[...], preferred_element_type=F32)
    acc = acc + jnp.dot(yb_ref[...], wb_ref[...], preferred_element_type=F32)
    x = x_ref[...] + acc
    if final_norm:
        x = x * lax.rsqrt(jnp.mean(x * x, axis=-1, keepdims=True) + EPS) * g_ref[...]
    o_ref[...] = x


def _outp(ya, yb, w_out, x2, g, final_norm, tm=256):
    m, d = x2.shape
    ka, kb = ya.shape[1], yb.shape[1]
    assert ka == kb and w_out.shape[0] == ka + kb
    return pl.pallas_call(
        functools.partial(_outp_kernel, final_norm=final_norm),
        grid=(m // tm,),
        in_specs=[
            pl.BlockSpec((tm, ka), lambda i: (i, 0)),
            pl.BlockSpec((tm, kb), lambda i: (i, 0)),
            pl.BlockSpec((ka, d), lambda i: (0, 0)),
            pl.BlockSpec((kb, d), lambda i: (1, 0)),
            pl.BlockSpec((tm, d), lambda i: (i, 0)),
            pl.BlockSpec((1, d), lambda i: (0, 0)),
        ],
        out_specs=pl.BlockSpec((tm, d), lambda i: (i, 0)),
        out_shape=jax.ShapeDtypeStruct((m, d), F32),
        compiler_params=_cparams(("parallel",)),
        name="outp",
    )(ya, yb, w_out, w_out, x2, g)


def _t5_bucket(dist):
    n = jnp.maximum(dist, 0)
    max_exact = REL_BUCKETS // 2
    nf = jnp.maximum(n, 1).astype(F32)
    large = max_exact + (jnp.log(nf / max_exact) / np.log(REL_MAX_DIST / max_exact)
                         * (REL_BUCKETS - max_exact)).astype(I32)
    large = jnp.minimum(large, REL_BUCKETS - 1)
    return jnp.where(n < max_exact, n, large)


def _bias_tiles(rel_bias):
    span = K_CHUNK + Q_TILE
    table = rel_bias[_t5_bucket(jnp.arange(span + 1, dtype=I32))].astype(F32)
    table = (table[:span] - table[span:]).T
    n = span + Q_TILE - 1
    a = jnp.concatenate([jnp.zeros((ATT_HEADS, Q_TILE - 1), F32), table], axis=1)
    shifted = jnp.tile(a, (1, span + 1))[:, :span * (n + 1)].reshape(ATT_HEADS, span, n + 1)
    tiles = shifted[:, ::-1, :Q_TILE].reshape(ATT_HEADS, 2, K_CHUNK, Q_TILE)
    tiles = jnp.concatenate([jnp.zeros_like(tiles[:, :1]), tiles], axis=1)
    pairs = tiles.reshape(ATT_HEADS // 2, 2, 3, K_CHUNK, Q_TILE)
    return jnp.transpose(pairs, (0, 2, 3, 1, 4)).reshape(ATT_HEADS // 2, 3, K_CHUNK, 2 * Q_TILE)


def kernel(x, norm_g, w_in, conv_w, conv_b, lru_wa, lru_ba, lru_wx, lru_bx, lru_lambda, ckv_norm_g, idx_k_norm_g, idx_k_norm_b, w_uk, w_uv, w_out, rel_bias, final_norm_g):
    bsz, s, d = x.shape
    depth = w_in.shape[0]
    lru_w = lru_wa.shape[1] * lru_wa.shape[2]
    att_w = ATT_HEADS * HEAD_DIM
    idx_w = IDX_HEADS * IDX_DIM
    assert K_CHUNK == Q_TILE and REL_MAX_DIST <= K_CHUNK
    assert lru_w == att_w == idx_w and att_w % KV_LATENT == 0
    topk = min(INDEX_TOPK, s // 4)

    o_q = 2 * lru_w
    o_ckv = o_q + att_w
    o_gb = o_ckv + KV_LATENT
    o_qi = o_gb + att_w
    o_ki = o_qi + idx_w
    tn = 512
    cols_a = {"xa": 0, "ga": lru_w, "gb": 2 * lru_w}
    cols_b = {"q": 0, "qi": att_w}
    cols_c = {"ckv": 0, "small": KV_LATENT}
    n_f32, n_bf16 = 3 * lru_w, att_w + idx_w
    tail_pad = tn - KV_LATENT - (IDX_DIM + IDX_HEADS)

    bias_pairs = _bias_tiles(rel_bias)
    x2 = x.reshape(bsz * s, d)
    for l in range(depth):
        wl = w_in[l]
        pieces = [wl[:, :o_q], wl[:, o_gb:o_qi], wl[:, o_q:o_ckv], wl[:, o_qi:o_ki], wl[:, o_ckv:o_gb], wl[:, o_ki:]]
        w_all = jnp.concatenate([p.astype(BF16) for p in pieces] + [jnp.zeros((d, tail_pad), BF16)], axis=1)
        pa, pb, pc = _proj(x2, norm_g[l][None, :], w_all, n_f32, n_bf16, tn=tn)
        pa3 = pa.reshape(bsz, s, -1)
        pb3 = pb.reshape(bsz, s, -1)
        pc3 = pc.reshape(bsz, s, -1)

        ya = _rglru(pa3, cols_a, conv_w[l], conv_b[l][None, :], lru_wa[l].astype(BF16), lru_ba[l][None, :],
                    lru_wx[l].astype(BF16), lru_bx[l][None, :], lru_lambda[l][None, :])

        wukt = jnp.transpose(w_uk[l], (0, 2, 1)).astype(BF16)
        qat, qit, c, ct, kn, wit = _prep(pb3, pc3, {**cols_b, **cols_c}, wukt, ckv_norm_g[l][None, :],
                                          idx_k_norm_g[l][None, :], idx_k_norm_b[l][None, :])
        wuvt = jnp.transpose(w_uv[l], (0, 2, 1)).astype(BF16)
        yb = _dsa(qit, wit, kn, c, ct, qat, pa3, _col_block(cols_a["gb"], att_w), bias_pairs, wuvt, topk)

        x2 = _outp(ya.reshape(bsz * s, lru_w), yb.reshape(bsz * s, att_w), w_out[l].astype(BF16), x2,
                   final_norm_g[None, :], final_norm=(l == depth - 1))
    return x2.reshape(bsz, s, d)
```

```python
import functools

import numpy as np
import jax
import jax.numpy as jnp
from jax import lax
from jax.experimental import pallas as pl
from jax.experimental.pallas import tpu as pltpu

F32 = jnp.float32
BF16 = jnp.bfloat16
I32 = jnp.int32

LRU_BLOCKS = 8
CONV_WIDTH = 4
LRU_C = 8.0
ATT_HEADS = 8
HEAD_DIM = 128
KV_LATENT = 256
IDX_HEADS = 16
IDX_DIM = 64
INDEX_TOPK = 256
REL_BUCKETS = 32
REL_MAX_DIST = 128
EPS = 1e-6

Q_TILE = 128
K_CHUNK = 128
ATT_CHUNK = 256
NEG = float(np.finfo(np.float32).min)
INT_MIN = -(2 ** 31)
VMEM_LIMIT = 56 * 1024 * 1024


def _cparams(sem):
    return pltpu.CompilerParams(dimension_semantics=sem, vmem_limit_bytes=VMEM_LIMIT)


def _col_block(offset, width):
    assert offset % width == 0
    return offset // width


def _wcast(w, order, tn=256):
    d, n = w.shape
    src_blocks = []
    for start, stop in order:
        assert start % tn == 0 and (stop % tn == 0 or stop == n)
        src_blocks += list(range(start // tn, -(-stop // tn)))
    table = jnp.asarray(src_blocks, I32)
    grid_spec = pltpu.PrefetchScalarGridSpec(
        num_scalar_prefetch=1,
        grid=(len(src_blocks),),
        in_specs=[pl.BlockSpec((d, tn), lambda j, t: (0, t[j]))],
        out_specs=pl.BlockSpec((d, tn), lambda j, t: (0, j)),
    )

    def body(t_ref, w_ref, o_ref):
        first = t_ref[pl.program_id(0)] * tn
        col = first + lax.broadcasted_iota(I32, w_ref.shape, 1)
        o_ref[...] = jnp.where(col < n, w_ref[...], 0.0).astype(o_ref.dtype)

    return pl.pallas_call(
        body,
        grid_spec=grid_spec,
        out_shape=jax.ShapeDtypeStruct((d, len(src_blocks) * tn), BF16),
        compiler_params=_cparams(("arbitrary",)),
        name="wcast",
    )(table, w)


def _proj_kernel(x_ref, g_ref, w_ref, oa_ref, ob_ref, oc_ref, h_ref, *, na, nb):
    j = pl.program_id(1)

    @pl.when(j == 0)
    def _():
        x = x_ref[...]
        y = x * lax.rsqrt(jnp.mean(x * x, axis=-1, keepdims=True) + EPS)
        h_ref[...] = (y * g_ref[...]).astype(BF16)

    @pl.when(j < na)
    def _():
        oa_ref[...] = jnp.dot(h_ref[...], w_ref[...], preferred_element_type=F32)

    @pl.when((j >= na) & (j < na + nb))
    def _():
        ob_ref[...] = jnp.dot(h_ref[...], w_ref[...], preferred_element_type=F32).astype(BF16)

    @pl.when(j >= na + nb)
    def _():
        oc_ref[...] = jnp.dot(h_ref[...], w_ref[...], preferred_element_type=F32)


def _proj(x2, g, w_all, n_f32, n_bf16, tm=1024, tn=512):
    m, d = x2.shape
    na, nb = n_f32 // tn, n_bf16 // tn
    assert n_f32 % tn == 0 and n_bf16 % tn == 0 and w_all.shape[1] == n_f32 + n_bf16 + tn
    return pl.pallas_call(
        functools.partial(_proj_kernel, na=na, nb=nb),
        grid=(m // tm, na + nb + 1),
        in_specs=[
            pl.BlockSpec((tm, d), lambda i, j: (i, 0)),
            pl.BlockSpec((1, d), lambda i, j: (0, 0)),
            pl.BlockSpec((d, tn), lambda i, j: (0, j)),
        ],
        out_specs=[
            pl.BlockSpec((tm, tn), lambda i, j: (i, jnp.minimum(j, na - 1))),
            pl.BlockSpec((tm, tn), lambda i, j: (i, jnp.clip(j - na, 0, nb - 1))),
            pl.BlockSpec((tm, tn), lambda i, j: (i, 0)),
        ],
        out_shape=[
            jax.ShapeDtypeStruct((m, n_f32), F32),
            jax.ShapeDtypeStruct((m, n_bf16), BF16),
            jax.ShapeDtypeStruct((m, tn), F32),
        ],
        scratch_shapes=[pltpu.VMEM((tm, d), BF16)],
        compiler_params=_cparams(("parallel", "arbitrary")),
        name="proj",
    )(x2, g, w_all)


def _sigmoid(v):
    return 0.5 * jnp.tanh(0.5 * v) + 0.5


def _scan_step(a, b, k, axis, idx):
    keep = idx >= k
    a_prev = jnp.where(keep, pltpu.roll(a, k, axis=axis), 1.0)
    b_prev = jnp.where(keep, pltpu.roll(b, k, axis=axis), 0.0)
    return a * a_prev, a * b_prev + b


def _rglru_kernel(xa_ref, ga_ref, cw_ref, cb_ref, wa_ref, ba_ref, wx_ref, bx_ref, lam_ref,
                  o_ref, pad_s, a_s, b_s, c_s):
    s, w = xa_ref.shape
    tile = 8
    n_tiles = s // tile

    pad_s[0:tile, :] = jnp.zeros((tile, w), F32)
    pad_s[tile:tile + s, :] = xa_ref[...]
    acc = pad_s[tile:tile + s, :] * cw_ref[CONV_WIDTH - 1:CONV_WIDTH, :]
    for j in range(CONV_WIDTH - 1):
        back = CONV_WIDTH - 1 - j
        acc = acc + pad_s[tile - back:tile - back + s, :] * cw_ref[j:j + 1, :]
    xc = cb_ref[...] + acc

    xcb = xc.astype(BF16)
    r = _sigmoid(jnp.dot(xcb, wa_ref[...], preferred_element_type=F32) + ba_ref[...])
    i = _sigmoid(jnp.dot(xcb, wx_ref[...], preferred_element_type=F32) + bx_ref[...])
    z = -lam_ref[...]
    softplus = jnp.maximum(z, 0.0) + jnp.log1p(jnp.exp(-jnp.abs(z)))
    log_a = (-LRU_C) * r * softplus
    a = jnp.exp(log_a)
    m2 = (1.0 + a * a) * jnp.tanh(-log_a)
    mult = jnp.where(m2 > 0.0, m2 * lax.rsqrt(m2), 0.0)
    gated = i * xc
    b_s[...] = mult * gated
    b_s[0:1, :] = gated[0:1, :]

    a3 = a.reshape(n_tiles, tile, w)
    b3 = b_s[...].reshape(n_tiles, tile, w)
    sub = lax.broadcasted_iota(I32, (n_tiles, tile, w), 1)
    for k in (1, 2, 4):
        a3, b3 = _scan_step(a3, b3, k, 1, sub)
    a_s[...] = a3.reshape(s, w)
    b_s[...] = b3.reshape(s, w)

    at = a_s[pl.ds(tile - 1, n_tiles, stride=tile), :]
    bt = b_s[pl.ds(tile - 1, n_tiles, stride=tile), :]
    trow = lax.broadcasted_iota(I32, (n_tiles, w), 0)
    k = 1
    while k < n_tiles:
        at, bt = _scan_step(at, bt, k, 0, trow)
        k *= 2
    c_s[0:tile, :] = jnp.zeros((tile, w), F32)
    c_s[tile:tile + n_tiles, :] = bt

    def apply(t, carry):
        r0 = pl.multiple_of(t * tile, tile)
        before = c_s[pl.ds(tile - 1 + t, tile, stride=0), :]
        h = a_s[pl.ds(r0, tile), :] * before + b_s[pl.ds(r0, tile), :]
        ga = ga_ref[pl.ds(r0, tile), :]
        o_ref[pl.ds(r0, tile), :] = (h * (ga * _sigmoid(ga))).astype(o_ref.dtype)
        return carry

    lax.fori_loop(0, n_tiles, apply, 0, unroll=8)


def _rglru(pa3, cols, conv_w, conv_b, wa, ba, wx, bx, lam):
    bsz, s, _ = pa3.shape
    g, w = wa.shape[0], wa.shape[-1]
    xa_blk = _col_block(cols["xa"], w)
    ga_blk = _col_block(cols["ga"], w)
    vec = lambda: pl.BlockSpec((1, w), lambda b, j: (0, j))
    return pl.pallas_call(
        _rglru_kernel,
        grid=(bsz, g),
        in_specs=[
            pl.BlockSpec((None, s, w), lambda b, j: (b, 0, xa_blk + j)),
            pl.BlockSpec((None, s, w), lambda b, j: (b, 0, ga_blk + j)),
            pl.BlockSpec((CONV_WIDTH, w), lambda b, j: (0, j)),
            vec(),
            pl.BlockSpec((None, w, w), lambda b, j: (j, 0, 0)),
            vec(),
            pl.BlockSpec((None, w, w), lambda b, j: (j, 0, 0)),
            vec(),
            vec(),
        ],
        out_specs=pl.BlockSpec((None, s, w), lambda b, j: (b, 0, j)),
        out_shape=jax.ShapeDtypeStruct((bsz, s, g * w), BF16),
        scratch_shapes=[pltpu.VMEM((s + 8, w), F32), pltpu.VMEM((s, w), F32), pltpu.VMEM((s, w), F32),
                        pltpu.VMEM((s // 8 + 8, w), F32)],
        compiler_params=_cparams(("parallel", "parallel")),
        name="rglru",
    )(pa3, pa3, conv_w, conv_b, wa, ba, wx, bx, lam)


def _prep_kernel(q_ref, qi_ref, ckv_ref, sm_ref, wukt_ref, cg_ref, kg_ref, kb_ref,
                 qat_ref, qit_ref, c_ref, ct_ref, kn_ref, wit_ref):
    tm = q_ref.shape[0]
    qt = q_ref[...].T
    scale = HEAD_DIM ** -0.5
    qat = [(jnp.dot(wukt_ref[h], qt[h * HEAD_DIM:(h + 1) * HEAD_DIM], preferred_element_type=F32)
            * scale).astype(BF16) for h in range(ATT_HEADS)]
    qit = qi_ref[...].T
    for j in range(tm // Q_TILE):
        cols = slice(j * Q_TILE, (j + 1) * Q_TILE)
        for p in range(ATT_HEADS // 2):
            qat_ref[j, p] = jnp.concatenate([qat[2 * p][:, cols], qat[2 * p + 1][:, cols]], axis=1)
        for p in range(IDX_HEADS // 2):
            lo, hi = 2 * p * IDX_DIM, (2 * p + 1) * IDX_DIM
            qit_ref[j, p] = jnp.concatenate([qit[lo:lo + IDX_DIM, cols], qit[hi:hi + IDX_DIM, cols]], axis=1)

    ckv = ckv_ref[...]
    c = ckv * lax.rsqrt(jnp.mean(ckv * ckv, axis=-1, keepdims=True) + EPS) * cg_ref[...]
    c_ref[...] = c.astype(BF16)
    ct_ref[...] = c.T.astype(BF16)

    sm = sm_ref[...]
    ki = sm[:, :IDX_DIM]
    mu = jnp.mean(ki, axis=-1, keepdims=True)
    var = jnp.mean(jnp.square(ki - mu), axis=-1, keepdims=True)
    kn = (ki - mu) * lax.rsqrt(var + EPS) * kg_ref[...] + kb_ref[...]
    kn_ref[...] = kn.astype(BF16)
    wit_ref[...] = sm.T[IDX_DIM:IDX_DIM + IDX_HEADS, :] * (IDX_HEADS ** -0.5 * IDX_DIM ** -0.5)


def _prep(pb3, pc3, cols, w_ukt, ckv_g, k_g, k_b, tm=256):
    bsz, s, _ = pb3.shape
    att_w = ATT_HEADS * HEAD_DIM
    idx_w = IDX_HEADS * IDX_DIM
    q_blk = _col_block(cols["q"], att_w)
    qi_blk = _col_block(cols["qi"], idx_w)
    ckv_blk = _col_block(cols["ckv"], KV_LATENT)
    small_w = 128
    small_blk = _col_block(cols["small"], small_w)
    tq = tm // Q_TILE
    const = lambda shape: pl.BlockSpec(shape, lambda b, i: (0,) * len(shape))
    return pl.pallas_call(
        _prep_kernel,
        grid=(bsz, s // tm),
        in_specs=[
            pl.BlockSpec((None, tm, att_w), lambda b, i: (b, i, q_blk)),
            pl.BlockSpec((None, tm, idx_w), lambda b, i: (b, i, qi_blk)),
            pl.BlockSpec((None, tm, KV_LATENT), lambda b, i: (b, i, ckv_blk)),
            pl.BlockSpec((None, tm, small_w), lambda b, i: (b, i, small_blk)),
            const(w_ukt.shape),
            const((1, KV_LATENT)),
            const((1, IDX_DIM)),
            const((1, IDX_DIM)),
        ],
        out_specs=[
            pl.BlockSpec((None, tq, ATT_HEADS // 2, KV_LATENT, 2 * Q_TILE), lambda b, i: (b, i, 0, 0, 0)),
            pl.BlockSpec((None, tq, IDX_HEADS // 2, IDX_DIM, 2 * Q_TILE), lambda b, i: (b, i, 0, 0, 0)),
            pl.BlockSpec((None, tm, KV_LATENT), lambda b, i: (b, i, 0)),
            pl.BlockSpec((None, KV_LATENT, tm), lambda b, i: (b, 0, i)),
            pl.BlockSpec((None, tm, IDX_DIM), lambda b, i: (b, i, 0)),
            pl.BlockSpec((None, IDX_HEADS, tm), lambda b, i: (b, 0, i)),
        ],
        out_shape=[
            jax.ShapeDtypeStruct((bsz, s // Q_TILE, ATT_HEADS // 2, KV_LATENT, 2 * Q_TILE), BF16),
            jax.ShapeDtypeStruct((bsz, s // Q_TILE, IDX_HEADS // 2, IDX_DIM, 2 * Q_TILE), BF16),
            jax.ShapeDtypeStruct((bsz, s, KV_LATENT), BF16),
            jax.ShapeDtypeStruct((bsz, KV_LATENT, s), BF16),
            jax.ShapeDtypeStruct((bsz, s, IDX_DIM), BF16),
            jax.ShapeDtypeStruct((bsz, IDX_HEADS, s), F32),
        ],
        compiler_params=_cparams(("parallel", "parallel")),
        name="prep",
    )(pb3, pb3, pc3, pc3, w_ukt, ckv_g, k_g, k_b)


def _tree_sum(parts):
    while len(parts) > 1:
        paired = [parts[i] + parts[i + 1] for i in range(0, len(parts) - 1, 2)]
        parts = paired + ([parts[-1]] if len(parts) % 2 else [])
    return parts[0]


def _sortable_to_f32(u):
    key = u ^ INT_MIN
    return lax.bitcast_convert_type(key ^ ((key >> 31) & 0x7FFFFFFF), F32)


def _kth_largest(score_ref, rows, k):
    chains = 4

    def step(i, u):
        cand = u | (jnp.int32(1) << (31 - i))
        cand_f = _sortable_to_f32(cand)
        accs = [None] * chains
        for r in range(rows // 8):
            hit = jnp.where(score_ref[r * 8:(r + 1) * 8, :] >= cand_f, 1.0, 0.0)
            accs[r % chains] = hit if accs[r % chains] is None else accs[r % chains] + hit
        cnt = jnp.sum(_tree_sum([a for a in accs if a is not None]), axis=0, keepdims=True)
        return jnp.where(cnt >= k, cand, u)

    return _sortable_to_f32(lax.fori_loop(0, 32, step, jnp.zeros((1, Q_TILE), I32)))


def _dsa_kernel(qit_ref, wit_ref, kn_ref, c_ref, ct_ref, qat_ref, gb_ref, bias_ref, wuvt_ref,
                o_ref, score_s, thr_s, lg_s, acc_s, topk):
    qb = pl.program_id(1)
    nkc = qb + 1
    nac = (nkc * K_CHUNK + ATT_CHUNK - 1) // ATT_CHUNK
    tiles = ATT_CHUNK // K_CHUNK
    pair_w = 2 * Q_TILE

    kiota = lax.broadcasted_iota(I32, (K_CHUNK, Q_TILE), 0)
    qpos = qb * Q_TILE + lax.broadcasted_iota(I32, (K_CHUNK, Q_TILE), 1)

    def score_chunk(ac, carry):
        for t in range(tiles):
            k0 = pl.multiple_of(ac * ATT_CHUNK + t * K_CHUNK, K_CHUNK)
            kn = kn_ref[pl.ds(k0, K_CHUNK), :]
            acc = jnp.zeros((K_CHUNK, Q_TILE), F32)
            for p in range(IDX_HEADS // 2):
                sc = jnp.dot(kn, qit_ref[p], preferred_element_type=F32)
                acc = acc + jnp.maximum(sc[:, :Q_TILE], 0.0) * wit_ref[2 * p:2 * p + 1, :]
                acc = acc + jnp.maximum(sc[:, Q_TILE:], 0.0) * wit_ref[2 * p + 1:2 * p + 2, :]
            score_s[pl.ds(k0, K_CHUNK), :] = jnp.where(kiota + k0 <= qpos, acc, -jnp.inf)
        return carry

    lax.fori_loop(0, nac, score_chunk, 0)

    for v in range(1, score_s.shape[0] // ATT_CHUNK + 1):
        @pl.when(nac == v)
        def _(rows=v * ATT_CHUNK):
            thr = _kth_largest(score_s, rows, float(topk))
            thr = jnp.where(thr >= NEG, thr, NEG)
            thr_s[...] = jnp.broadcast_to(thr, thr_s.shape)

    thr = thr_s[0:1, :]

    def logit_chunk(ac, m8s):
        r0 = pl.multiple_of(ac * ATT_CHUNK, ATT_CHUNK)
        c_chunk = c_ref[pl.ds(r0, ATT_CHUNK), :]
        mbias = jnp.where(score_s[pl.ds(r0, ATT_CHUNK), :] >= thr, 0.0, NEG)
        mbias = jnp.concatenate([mbias, mbias], axis=1)
        near = [jnp.clip(ac * tiles + t - qb + 2, 0, 2) for t in range(tiles)]
        out = []
        for p in range(ATT_HEADS // 2):
            lg = jnp.dot(c_chunk, qat_ref[p], preferred_element_type=F32) + mbias
            lg = jnp.concatenate(
                [lg[t * K_CHUNK:(t + 1) * K_CHUNK] + bias_ref[p, near[t]] for t in range(tiles)], axis=0)
            lg_s[p, pl.ds(r0, ATT_CHUNK), :] = lg
            out.append(jnp.maximum(m8s[p], jnp.max(lg.reshape(ATT_CHUNK // 8, 8, pair_w), axis=0)))
        return tuple(out)

    m8s = lax.fori_loop(0, nac, logit_chunk,
                        tuple(jnp.full((8, pair_w), NEG, F32) for _ in range(ATT_HEADS // 2)))
    ms = [jnp.max(m8, axis=0, keepdims=True) for m8 in m8s]

    acc_s[...] = jnp.zeros(acc_s.shape, F32)

    def pv_chunk(ac, l8s):
        r0 = pl.multiple_of(ac * ATT_CHUNK, ATT_CHUNK)
        ct_chunk = ct_ref[:, pl.ds(r0, ATT_CHUNK)]
        out = []
        for p in range(ATT_HEADS // 2):
            pr = jnp.exp(lg_s[p, pl.ds(r0, ATT_CHUNK), :] - ms[p])
            out.append(l8s[p] + jnp.sum(pr.reshape(ATT_CHUNK // 8, 8, pair_w), axis=0))
            acc_s[p] += jnp.dot(ct_chunk, pr.astype(BF16), preferred_element_type=F32)
        return tuple(out)

    l8s = lax.fori_loop(0, nac, pv_chunk,
                        tuple(jnp.zeros((8, pair_w), F32) for _ in range(ATT_HEADS // 2)))

    for h in range(ATT_HEADS):
        p, half = divmod(h, 2)
        lanes = slice(half * Q_TILE, (half + 1) * Q_TILE)
        denom = jnp.sum(l8s[p][:, lanes], axis=0, keepdims=True)
        o_t = acc_s[p, :, lanes] * (1.0 / denom)
        y_t = jnp.dot(wuvt_ref[h], o_t.astype(BF16), preferred_element_type=F32)
        gb = gb_ref[:, h * HEAD_DIM:(h + 1) * HEAD_DIM]
        o_ref[:, h * HEAD_DIM:(h + 1) * HEAD_DIM] = (y_t.T * (gb * _sigmoid(gb))).astype(o_ref.dtype)


def _dsa(qit, wit, kn, c, ct, qat, pa3, gb_blk, bias_pairs, wuvt, topk):
    bsz, s, _ = c.shape
    att_w = ATT_HEADS * HEAD_DIM
    assert s % ATT_CHUNK == 0
    const = lambda shape: pl.BlockSpec(shape, lambda b, i: (0,) * len(shape))
    per_tile = lambda shape: pl.BlockSpec((None, None) + shape, lambda b, i: (b, i) + (0,) * len(shape))
    return pl.pallas_call(
        functools.partial(_dsa_kernel, topk=topk),
        grid=(bsz, s // Q_TILE),
        in_specs=[
            per_tile(qit.shape[2:]),
            pl.BlockSpec((None, IDX_HEADS, Q_TILE), lambda b, i: (b, 0, i)),
            pl.BlockSpec((None, s, IDX_DIM), lambda b, i: (b, 0, 0)),
            pl.BlockSpec((None, s, KV_LATENT), lambda b, i: (b, 0, 0)),
            pl.BlockSpec((None, KV_LATENT, s), lambda b, i: (b, 0, 0)),
            per_tile(qat.shape[2:]),
            pl.BlockSpec((None, Q_TILE, att_w), lambda b, i: (b, i, gb_blk)),
            const(bias_pairs.shape),
            const(wuvt.shape),
        ],
        out_specs=pl.BlockSpec((None, Q_TILE, att_w), lambda b, i: (b, i, 0)),
        out_shape=jax.ShapeDtypeStruct((bsz, s, att_w), BF16),
        scratch_shapes=[
            pltpu.VMEM((s, Q_TILE), F32),
            pltpu.VMEM((8, Q_TILE), F32),
            pltpu.VMEM((ATT_HEADS // 2, s, 2 * Q_TILE), F32),
            pltpu.VMEM((ATT_HEADS // 2, KV_LATENT, 2 * Q_TILE), F32),
        ],
        compiler_params=_cparams(("parallel", "arbitrary")),
        name="dsa",
    )(qit, wit, kn, c, ct, qat, pa3, bias_pairs, wuvt)


def _outp_kernel(ya_ref, yb_ref, wa_ref, wb_ref, x_ref, g_ref, o_ref, *, final_norm):
    acc = jnp.dot(ya_ref[...], wa_ref[...], preferred_element_type=F32)
    acc = acc + jnp.dot(yb_ref[...], wb_ref[...], preferred_element_type=F32)
    x = x_ref[...] + acc
    if final_norm:
        x = x * lax.rsqrt(jnp.mean(x * x, axis=-1, keepdims=True) + EPS) * g_ref[...]
    o_ref[...] = x


def _outp(ya, yb, w_out, x2, g, final_norm, tm=256):
    m, d = x2.shape
    ka, kb = ya.shape[1], yb.shape[1]
    assert ka == kb and w_out.shape[0] == ka + kb
    return pl.pallas_call(
        functools.partial(_outp_kernel, final_norm=final_norm),
        grid=(m // tm,),
        in_specs=[
            pl.BlockSpec((tm, ka), lambda i: (i, 0)),
            pl.BlockSpec((tm, kb), lambda i: (i, 0)),
            pl.BlockSpec((ka, d), lambda i: (0, 0)),
            pl.BlockSpec((kb, d), lambda i: (1, 0)),
            pl.BlockSpec((tm, d), lambda i: (i, 0)),
            pl.BlockSpec((1, d), lambda i: (0, 0)),
        ],
        out_specs=pl.BlockSpec((tm, d), lambda i: (i, 0)),
        out_shape=jax.ShapeDtypeStruct((m, d), F32),
        compiler_params=_cparams(("parallel",)),
        name="outp",
    )(ya, yb, w_out, w_out, x2, g)


def _t5_bucket(dist):
    n = jnp.maximum(dist, 0)
    max_exact = REL_BUCKETS // 2
    nf = jnp.maximum(n, 1).astype(F32)
    large = max_exact + (jnp.log(nf / max_exact) / np.log(REL_MAX_DIST / max_exact)
                         * (REL_BUCKETS - max_exact)).astype(I32)
    large = jnp.minimum(large, REL_BUCKETS - 1)
    return jnp.where(n < max_exact, n, large)


def _bias_tiles(rel_bias):
    span = K_CHUNK + Q_TILE
    table = rel_bias[_t5_bucket(jnp.arange(span + 1, dtype=I32))].astype(F32)
    table = (table[:span] - table[span:]).T
    n = span + Q_TILE - 1
    a = jnp.concatenate([jnp.zeros((ATT_HEADS, Q_TILE - 1), F32), table], axis=1)
    shifted = jnp.tile(a, (1, span + 1))[:, :span * (n + 1)].reshape(ATT_HEADS, span, n + 1)
    tiles = shifted[:, ::-1, :Q_TILE].reshape(ATT_HEADS, 2, K_CHUNK, Q_TILE)
    tiles = jnp.concatenate([jnp.zeros_like(tiles[:, :1]), tiles], axis=1)
    pairs = tiles.reshape(ATT_HEADS // 2, 2, 3, K_CHUNK, Q_TILE)
    return jnp.transpose(pairs, (0, 2, 3, 1, 4)).reshape(ATT_HEADS // 2, 3, K_CHUNK, 2 * Q_TILE)


def kernel(x, norm_g, w_in, conv_w, conv_b, lru_wa, lru_ba, lru_wx, lru_bx, lru_lambda, ckv_norm_g, idx_k_norm_g, idx_k_norm_b, w_uk, w_uv, w_out, rel_bias, final_norm_g):
    bsz, s, d = x.shape
    depth = w_in.shape[0]
    lru_w = lru_wa.shape[1] * lru_wa.shape[2]
    att_w = ATT_HEADS * HEAD_DIM
    idx_w = IDX_HEADS * IDX_DIM
    assert K_CHUNK == Q_TILE and REL_MAX_DIST <= K_CHUNK
    assert lru_w == att_w == idx_w and att_w % KV_LATENT == 0
    topk = min(INDEX_TOPK, s // 4)

    o_q = 2 * lru_w
    o_ckv = o_q + att_w
    o_gb = o_ckv + KV_LATENT
    o_qi = o_gb + att_w
    o_ki = o_qi + idx_w
    tn = 512
    cols_a = {"xa": 0, "ga": lru_w, "gb": 2 * lru_w}
    cols_b = {"q": 0, "qi": att_w}
    cols_c = {"ckv": 0, "small": KV_LATENT}
    n_f32, n_bf16 = 3 * lru_w, att_w + idx_w

    bias_pairs = _bias_tiles(rel_bias)
    x2 = x.reshape(bsz * s, d)
    for l in range(depth):
        wl = w_in[l]
        w_all = _wcast(wl, [(0, o_q), (o_gb, o_qi), (o_q, o_ckv), (o_qi, o_ki), (o_ckv, o_gb), (o_ki, wl.shape[1])])
        pa, pb, pc = _proj(x2, norm_g[l][None, :], w_all, n_f32, n_bf16, tn=tn)
        pa3 = pa.reshape(bsz, s, -1)
        pb3 = pb.reshape(bsz, s, -1)
        pc3 = pc.reshape(bsz, s, -1)

        ya = _rglru(pa3, cols_a, conv_w[l], conv_b[l][None, :], lru_wa[l].astype(BF16), lru_ba[l][None, :],
                    lru_wx[l].astype(BF16), lru_bx[l][None, :], lru_lambda[l][None, :])

        wukt = jnp.transpose(w_uk[l], (0, 2, 1)).astype(BF16)
        qat, qit, c, ct, kn, wit = _prep(pb3, pc3, {**cols_b, **cols_c}, wukt, ckv_norm_g[l][None, :],
                                          idx_k_norm_g[l][None, :], idx_k_norm_b[l][None, :])
        wuvt = jnp.transpose(w_uv[l], (0, 2, 1)).astype(BF16)
        yb = _dsa(qit, wit, kn, c, ct, qat, pa3, _col_block(cols_a["gb"], att_w), bias_pairs, wuvt, topk)

        x2 = _outp(ya.reshape(bsz * s, lru_w), yb.reshape(bsz * s, att_w), w_out[l].astype(BF16), x2,
                   final_norm_g[None, :], final_norm=(l == depth - 1))
    return x2.reshape(bsz, s, d)
```

```python
import functools

import numpy as np
import jax
import jax.numpy as jnp
from jax import lax
from jax.experimental import pallas as pl
from jax.experimental.pallas import tpu as pltpu

F32 = jnp.float32
BF16 = jnp.bfloat16
I32 = jnp.int32

LRU_BLOCKS = 8
CONV_WIDTH = 4
LRU_C = 8.0
ATT_HEADS = 8
HEAD_DIM = 128
KV_LATENT = 256
IDX_HEADS = 16
IDX_DIM = 64
INDEX_TOPK = 256
REL_BUCKETS = 32
REL_MAX_DIST = 128
EPS = 1e-6

Q_TILE = 128
K_CHUNK = 128
ATT_CHUNK = 256
NEG = float(np.finfo(np.float32).min)
INT_MIN = -(2 ** 31)
VMEM_LIMIT = 56 * 1024 * 1024


def _cparams(sem):
    return pltpu.CompilerParams(dimension_semantics=sem, vmem_limit_bytes=VMEM_LIMIT)


def _col_block(offset, width):
    assert offset % width == 0
    return offset // width


def _wcast(wt, order, tn=256):
    n, d = wt.shape
    src_blocks = []
    for start, stop in order:
        assert start % tn == 0 and (stop % tn == 0 or stop == n)
        src_blocks += list(range(start // tn, -(-stop // tn)))
    table = jnp.asarray(src_blocks, I32)
    grid_spec = pltpu.PrefetchScalarGridSpec(
        num_scalar_prefetch=1,
        grid=(len(src_blocks),),
        in_specs=[pl.BlockSpec((tn, d), lambda j, t: (t[j], 0))],
        out_specs=pl.BlockSpec((tn, d), lambda j, t: (j, 0)),
    )

    def body(t_ref, w_ref, o_ref):
        first = t_ref[pl.program_id(0)] * tn
        row = first + lax.broadcasted_iota(I32, w_ref.shape, 0)
        o_ref[...] = jnp.where(row < n, w_ref[...], 0.0).astype(o_ref.dtype)

    return pl.pallas_call(
        body,
        grid_spec=grid_spec,
        out_shape=jax.ShapeDtypeStruct((len(src_blocks) * tn, d), BF16),
        compiler_params=_cparams(("arbitrary",)),
        name="wcast",
    )(table, wt)


def _proj_kernel(x_ref, g_ref, w_ref, oa_ref, ob_ref, oc_ref, h_ref, *, na, nb):
    j = pl.program_id(1)
    nt = (((1,), (1,)), ((), ()))

    @pl.when(j == 0)
    def _():
        x = x_ref[...]
        y = x * lax.rsqrt(jnp.mean(x * x, axis=-1, keepdims=True) + EPS)
        h_ref[...] = (y * g_ref[...]).astype(BF16)

    @pl.when(j < na)
    def _():
        oa_ref[...] = lax.dot_general(h_ref[...], w_ref[...], nt, preferred_element_type=F32)

    @pl.when((j >= na) & (j < na + nb))
    def _():
        ob_ref[...] = lax.dot_general(h_ref[...], w_ref[...], nt, preferred_element_type=F32).astype(BF16)

    @pl.when(j >= na + nb)
    def _():
        oc_ref[...] = lax.dot_general(h_ref[...], w_ref[...], nt, preferred_element_type=F32)


def _proj(x2, g, w_all_t, n_f32, n_bf16, tm=1024, tn=512):
    m, d = x2.shape
    na, nb = n_f32 // tn, n_bf16 // tn
    assert n_f32 % tn == 0 and n_bf16 % tn == 0 and w_all_t.shape[0] == n_f32 + n_bf16 + tn
    return pl.pallas_call(
        functools.partial(_proj_kernel, na=na, nb=nb),
        grid=(m // tm, na + nb + 1),
        in_specs=[
            pl.BlockSpec((tm, d), lambda i, j: (i, 0)),
            pl.BlockSpec((1, d), lambda i, j: (0, 0)),
            pl.BlockSpec((tn, d), lambda i, j: (j, 0)),
        ],
        out_specs=[
            pl.BlockSpec((tm, tn), lambda i, j: (i, jnp.minimum(j, na - 1))),
            pl.BlockSpec((tm, tn), lambda i, j: (i, jnp.clip(j - na, 0, nb - 1))),
            pl.BlockSpec((tm, tn), lambda i, j: (i, 0)),
        ],
        out_shape=[
            jax.ShapeDtypeStruct((m, n_f32), F32),
            jax.ShapeDtypeStruct((m, n_bf16), BF16),
            jax.ShapeDtypeStruct((m, tn), F32),
        ],
        scratch_shapes=[pltpu.VMEM((tm, d), BF16)],
        compiler_params=_cparams(("parallel", "arbitrary")),
        name="proj",
    )(x2, g, w_all_t)


def _sigmoid(v):
    return 0.5 * jnp.tanh(0.5 * v) + 0.5


def _scan_step(a, b, k, axis, idx):
    keep = idx >= k
    a_prev = jnp.where(keep, pltpu.roll(a, k, axis=axis), 1.0)
    b_prev = jnp.where(keep, pltpu.roll(b, k, axis=axis), 0.0)
    return a * a_prev, a * b_prev + b


def _rglru_kernel(xa_ref, ga_ref, cw_ref, cb_ref, wa_ref, ba_ref, wx_ref, bx_ref, lam_ref,
                  o_ref, pad_s, a_s, b_s, c_s):
    s, w = xa_ref.shape
    tile = 8
    n_tiles = s // tile

    pad_s[0:tile, :] = jnp.zeros((tile, w), F32)
    pad_s[tile:tile + s, :] = xa_ref[...]
    acc = pad_s[tile:tile + s, :] * cw_ref[CONV_WIDTH - 1:CONV_WIDTH, :]
    for j in range(CONV_WIDTH - 1):
        back = CONV_WIDTH - 1 - j
        acc = acc + pad_s[tile - back:tile - back + s, :] * cw_ref[j:j + 1, :]
    xc = cb_ref[...] + acc

    xcb = xc.astype(BF16)
    r = _sigmoid(jnp.dot(xcb, wa_ref[...], preferred_element_type=F32) + ba_ref[...])
    i = _sigmoid(jnp.dot(xcb, wx_ref[...], preferred_element_type=F32) + bx_ref[...])
    z = -lam_ref[...]
    softplus = jnp.maximum(z, 0.0) + jnp.log1p(jnp.exp(-jnp.abs(z)))
    log_a = (-LRU_C) * r * softplus
    a = jnp.exp(log_a)
    m2 = (1.0 + a * a) * jnp.tanh(-log_a)
    mult = jnp.where(m2 > 0.0, m2 * lax.rsqrt(m2), 0.0)
    gated = i * xc
    b_s[...] = mult * gated
    b_s[0:1, :] = gated[0:1, :]

    a3 = a.reshape(n_tiles, tile, w)
    b3 = b_s[...].reshape(n_tiles, tile, w)
    sub = lax.broadcasted_iota(I32, (n_tiles, tile, w), 1)
    for k in (1, 2, 4):
        a3, b3 = _scan_step(a3, b3, k, 1, sub)
    a_s[...] = a3.reshape(s, w)
    b_s[...] = b3.reshape(s, w)

    at = a_s[pl.ds(tile - 1, n_tiles, stride=tile), :]
    bt = b_s[pl.ds(tile - 1, n_tiles, stride=tile), :]
    trow = lax.broadcasted_iota(I32, (n_tiles, w), 0)
    k = 1
    while k < n_tiles:
        at, bt = _scan_step(at, bt, k, 0, trow)
        k *= 2
    c_s[0:tile, :] = jnp.zeros((tile, w), F32)
    c_s[tile:tile + n_tiles, :] = bt

    def apply(t, carry):
        r0 = pl.multiple_of(t * tile, tile)
        before = c_s[pl.ds(tile - 1 + t, tile, stride=0), :]
        h = a_s[pl.ds(r0, tile), :] * before + b_s[pl.ds(r0, tile), :]
        ga = ga_ref[pl.ds(r0, tile), :]
        o_ref[pl.ds(r0, tile), :] = (h * (ga * _sigmoid(ga))).astype(o_ref.dtype)
        return carry

    lax.fori_loop(0, n_tiles, apply, 0, unroll=8)


def _rglru(pa3, cols, conv_w, conv_b, wa, ba, wx, bx, lam):
    bsz, s, _ = pa3.shape
    g, w = wa.shape[0], wa.shape[-1]
    xa_blk = _col_block(cols["xa"], w)
    ga_blk = _col_block(cols["ga"], w)
    vec = lambda: pl.BlockSpec((1, w), lambda b, j: (0, j))
    return pl.pallas_call(
        _rglru_kernel,
        grid=(bsz, g),
        in_specs=[
            pl.BlockSpec((None, s, w), lambda b, j: (b, 0, xa_blk + j)),
            pl.BlockSpec((None, s, w), lambda b, j: (b, 0, ga_blk + j)),
            pl.BlockSpec((CONV_WIDTH, w), lambda b, j: (0, j)),
            vec(),
            pl.BlockSpec((None, w, w), lambda b, j: (j, 0, 0)),
            vec(),
            pl.BlockSpec((None, w, w), lambda b, j: (j, 0, 0)),
            vec(),
            vec(),
        ],
        out_specs=pl.BlockSpec((None, s, w), lambda b, j: (b, 0, j)),
        out_shape=jax.ShapeDtypeStruct((bsz, s, g * w), BF16),
        scratch_shapes=[pltpu.VMEM((s + 8, w), F32), pltpu.VMEM((s, w), F32), pltpu.VMEM((s, w), F32),
                        pltpu.VMEM((s // 8 + 8, w), F32)],
        compiler_params=_cparams(("parallel", "parallel")),
        name="rglru",
    )(pa3, pa3, conv_w, conv_b, wa, ba, wx, bx, lam)


def _prep_kernel(q_ref, qi_ref, ckv_ref, sm_ref, wukt_ref, cg_ref, kg_ref, kb_ref,
                 qat_ref, qit_ref, c_ref, ct_ref, kn_ref, wit_ref):
    tm = q_ref.shape[0]
    qt = q_ref[...].T
    scale = HEAD_DIM ** -0.5
    qat = [(jnp.dot(wukt_ref[h], qt[h * HEAD_DIM:(h + 1) * HEAD_DIM], preferred_element_type=F32)
            * scale).astype(BF16) for h in range(ATT_HEADS)]
    qit = qi_ref[...].T
    for j in range(tm // Q_TILE):
        cols = slice(j * Q_TILE, (j + 1) * Q_TILE)
        for p in range(ATT_HEADS // 2):
            qat_ref[j, p] = jnp.concatenate([qat[2 * p][:, cols], qat[2 * p + 1][:, cols]], axis=1)
        for p in range(IDX_HEADS // 2):
            lo, hi = 2 * p * IDX_DIM, (2 * p + 1) * IDX_DIM
            qit_ref[j, p] = jnp.concatenate([qit[lo:lo + IDX_DIM, cols], qit[hi:hi + IDX_DIM, cols]], axis=1)

    ckv = ckv_ref[...]
    c = ckv * lax.rsqrt(jnp.mean(ckv * ckv, axis=-1, keepdims=True) + EPS) * cg_ref[...]
    c_ref[...] = c.astype(BF16)
    ct_ref[...] = c.T.astype(BF16)

    sm = sm_ref[...]
    ki = sm[:, :IDX_DIM]
    mu = jnp.mean(ki, axis=-1, keepdims=True)
    var = jnp.mean(jnp.square(ki - mu), axis=-1, keepdims=True)
    kn = (ki - mu) * lax.rsqrt(var + EPS) * kg_ref[...] + kb_ref[...]
    kn_ref[...] = kn.astype(BF16)
    wit_ref[...] = sm.T[IDX_DIM:IDX_DIM + IDX_HEADS, :] * (IDX_HEADS ** -0.5 * IDX_DIM ** -0.5)


def _prep(pb3, pc3, cols, w_ukt, ckv_g, k_g, k_b, tm=256):
    bsz, s, _ = pb3.shape
    att_w = ATT_HEADS * HEAD_DIM
    idx_w = IDX_HEADS * IDX_DIM
    q_blk = _col_block(cols["q"], att_w)
    qi_blk = _col_block(cols["qi"], idx_w)
    ckv_blk = _col_block(cols["ckv"], KV_LATENT)
    small_w = 128
    small_blk = _col_block(cols["small"], small_w)
    tq = tm // Q_TILE
    const = lambda shape: pl.BlockSpec(shape, lambda b, i: (0,) * len(shape))
    return pl.pallas_call(
        _prep_kernel,
        grid=(bsz, s // tm),
        in_specs=[
            pl.BlockSpec((None, tm, att_w), lambda b, i: (b, i, q_blk)),
            pl.BlockSpec((None, tm, idx_w), lambda b, i: (b, i, qi_blk)),
            pl.BlockSpec((None, tm, KV_LATENT), lambda b, i: (b, i, ckv_blk)),
            pl.BlockSpec((None, tm, small_w), lambda b, i: (b, i, small_blk)),
            const(w_ukt.shape),
            const((1, KV_LATENT)),
            const((1, IDX_DIM)),
            const((1, IDX_DIM)),
        ],
        out_specs=[
            pl.BlockSpec((None, tq, ATT_HEADS // 2, KV_LATENT, 2 * Q_TILE), lambda b, i: (b, i, 0, 0, 0)),
            pl.BlockSpec((None, tq, IDX_HEADS // 2, IDX_DIM, 2 * Q_TILE), lambda b, i: (b, i, 0, 0, 0)),
            pl.BlockSpec((None, tm, KV_LATENT), lambda b, i: (b, i, 0)),
            pl.BlockSpec((None, KV_LATENT, tm), lambda b, i: (b, 0, i)),
            pl.BlockSpec((None, tm, IDX_DIM), lambda b, i: (b, i, 0)),
            pl.BlockSpec((None, IDX_HEADS, tm), lambda b, i: (b, 0, i)),
        ],
        out_shape=[
            jax.ShapeDtypeStruct((bsz, s // Q_TILE, ATT_HEADS // 2, KV_LATENT, 2 * Q_TILE), BF16),
            jax.ShapeDtypeStruct((bsz, s // Q_TILE, IDX_HEADS // 2, IDX_DIM, 2 * Q_TILE), BF16),
            jax.ShapeDtypeStruct((bsz, s, KV_LATENT), BF16),
            jax.ShapeDtypeStruct((bsz, KV_LATENT, s), BF16),
            jax.ShapeDtypeStruct((bsz, s, IDX_DIM), BF16),
            jax.ShapeDtypeStruct((bsz, IDX_HEADS, s), F32),
        ],
        compiler_params=_cparams(("parallel", "parallel")),
        name="prep",
    )(pb3, pb3, pc3, pc3, w_ukt, ckv_g, k_g, k_b)


def _tree_sum(parts):
    while len(parts) > 1:
        paired = [parts[i] + parts[i + 1] for i in range(0, len(parts) - 1, 2)]
        parts = paired + ([parts[-1]] if len(parts) % 2 else [])
    return parts[0]


def _sortable_to_f32(u):
    key = u ^ INT_MIN
    return lax.bitcast_convert_type(key ^ ((key >> 31) & 0x7FFFFFFF), F32)


def _kth_largest(score_ref, rows, k):
    chains = 4

    def step(i, u):
        cand = u | (jnp.int32(1) << (31 - i))
        cand_f = _sortable_to_f32(cand)
        accs = [None] * chains
        for r in range(rows // 8):
            hit = jnp.where(score_ref[r * 8:(r + 1) * 8, :] >= cand_f, 1.0, 0.0)
            accs[r % chains] = hit if accs[r % chains] is None else accs[r % chains] + hit
        cnt = jnp.sum(_tree_sum([a for a in accs if a is not None]), axis=0, keepdims=True)
        return jnp.where(cnt >= k, cand, u)

    return _sortable_to_f32(lax.fori_loop(0, 32, step, jnp.zeros((1, Q_TILE), I32)))


def _dsa_kernel(qit_ref, wit_ref, kn_ref, c_ref, ct_ref, qat_ref, gb_ref, bias_ref, wuvt_ref,
                o_ref, score_s, thr_s, lg_s, acc_s, topk):
    qb = pl.program_id(1)
    nkc = qb + 1
    nac = (nkc * K_CHUNK + ATT_CHUNK - 1) // ATT_CHUNK
    tiles = ATT_CHUNK // K_CHUNK
    pair_w = 2 * Q_TILE

    kiota = lax.broadcasted_iota(I32, (K_CHUNK, Q_TILE), 0)
    qpos = qb * Q_TILE + lax.broadcasted_iota(I32, (K_CHUNK, Q_TILE), 1)

    def score_chunk(ac, carry):
        for t in range(tiles):
            k0 = pl.multiple_of(ac * ATT_CHUNK + t * K_CHUNK, K_CHUNK)
            kn = kn_ref[pl.ds(k0, K_CHUNK), :]
            acc = jnp.zeros((K_CHUNK, Q_TILE), F32)
            for p in range(IDX_HEADS // 2):
                sc = jnp.dot(kn, qit_ref[p], preferred_element_type=F32)
                acc = acc + jnp.maximum(sc[:, :Q_TILE], 0.0) * wit_ref[2 * p:2 * p + 1, :]
                acc = acc + jnp.maximum(sc[:, Q_TILE:], 0.0) * wit_ref[2 * p + 1:2 * p + 2, :]
            score_s[pl.ds(k0, K_CHUNK), :] = jnp.where(kiota + k0 <= qpos, acc, -jnp.inf)
        return carry

    lax.fori_loop(0, nac, score_chunk, 0)

    for v in range(1, score_s.shape[0] // ATT_CHUNK + 1):
        @pl.when(nac == v)
        def _(rows=v * ATT_CHUNK):
            thr = _kth_largest(score_s, rows, float(topk))
            thr = jnp.where(thr >= NEG, thr, NEG)
            thr_s[...] = jnp.broadcast_to(thr, thr_s.shape)

    thr = thr_s[0:1, :]

    def logit_chunk(ac, m8s):
        r0 = pl.multiple_of(ac * ATT_CHUNK, ATT_CHUNK)
        c_chunk = c_ref[pl.ds(r0, ATT_CHUNK), :]
        mbias = jnp.where(score_s[pl.ds(r0, ATT_CHUNK), :] >= thr, 0.0, NEG)
        mbias = jnp.concatenate([mbias, mbias], axis=1)
        near = [jnp.clip(ac * tiles + t - qb + 2, 0, 2) for t in range(tiles)]
        out = []
        for p in range(ATT_HEADS // 2):
            lg = jnp.dot(c_chunk, qat_ref[p], preferred_element_type=F32) + mbias
            lg = jnp.concatenate(
                [lg[t * K_CHUNK:(t + 1) * K_CHUNK] + bias_ref[p, near[t]] for t in range(tiles)], axis=0)
            lg_s[p, pl.ds(r0, ATT_CHUNK), :] = lg
            out.append(jnp.maximum(m8s[p], jnp.max(lg.reshape(ATT_CHUNK // 8, 8, pair_w), axis=0)))
        return tuple(out)

    m8s = lax.fori_loop(0, nac, logit_chunk,
                        tuple(jnp.full((8, pair_w), NEG, F32) for _ in range(ATT_HEADS // 2)))
    ms = [jnp.max(m8, axis=0, keepdims=True) for m8 in m8s]

    acc_s[...] = jnp.zeros(acc_s.shape, F32)

    def pv_chunk(ac, l8s):
        r0 = pl.multiple_of(ac * ATT_CHUNK, ATT_CHUNK)
        ct_chunk = ct_ref[:, pl.ds(r0, ATT_CHUNK)]
        out = []
        for p in range(ATT_HEADS // 2):
            pr = jnp.exp(lg_s[p, pl.ds(r0, ATT_CHUNK), :] - ms[p])
            out.append(l8s[p] + jnp.sum(pr.reshape(ATT_CHUNK // 8, 8, pair_w), axis=0))
            acc_s[p] += jnp.dot(ct_chunk, pr.astype(BF16), preferred_element_type=F32)
        return tuple(out)

    l8s = lax.fori_loop(0, nac, pv_chunk,
                        tuple(jnp.zeros((8, pair_w), F32) for _ in range(ATT_HEADS // 2)))

    for h in range(ATT_HEADS):
        p, half = divmod(h, 2)
        lanes = slice(half * Q_TILE, (half + 1) * Q_TILE)
        denom = jnp.sum(l8s[p][:, lanes], axis=0, keepdims=True)
        o_t = acc_s[p, :, lanes] * (1.0 / denom)
        y_t = jnp.dot(wuvt_ref[h], o_t.astype(BF16), preferred_element_type=F32)
        gb = gb_ref[:, h * HEAD_DIM:(h + 1) * HEAD_DIM]
        o_ref[:, h * HEAD_DIM:(h + 1) * HEAD_DIM] = (y_t.T * (gb * _sigmoid(gb))).astype(o_ref.dtype)


def _dsa(qit, wit, kn, c, ct, qat, pa3, gb_blk, bias_pairs, wuvt, topk):
    bsz, s, _ = c.shape
    att_w = ATT_HEADS * HEAD_DIM
    assert s % ATT_CHUNK == 0
    const = lambda shape: pl.BlockSpec(shape, lambda b, i: (0,) * len(shape))
    per_tile = lambda shape: pl.BlockSpec((None, None) + shape, lambda b, i: (b, i) + (0,) * len(shape))
    return pl.pallas_call(
        functools.partial(_dsa_kernel, topk=topk),
        grid=(bsz, s // Q_TILE),
        in_specs=[
            per_tile(qit.shape[2:]),
            pl.BlockSpec((None, IDX_HEADS, Q_TILE), lambda b, i: (b, 0, i)),
            pl.BlockSpec((None, s, IDX_DIM), lambda b, i: (b, 0, 0)),
            pl.BlockSpec((None, s, KV_LATENT), lambda b, i: (b, 0, 0)),
            pl.BlockSpec((None, KV_LATENT, s), lambda b, i: (b, 0, 0)),
            per_tile(qat.shape[2:]),
            pl.BlockSpec((None, Q_TILE, att_w), lambda b, i: (b, i, gb_blk)),
            const(bias_pairs.shape),
            const(wuvt.shape),
        ],
        out_specs=pl.BlockSpec((None, Q_TILE, att_w), lambda b, i: (b, i, 0)),
        out_shape=jax.ShapeDtypeStruct((bsz, s, att_w), BF16),
        scratch_shapes=[
            pltpu.VMEM((s, Q_TILE), F32),
            pltpu.VMEM((8, Q_TILE), F32),
            pltpu.VMEM((ATT_HEADS // 2, s, 2 * Q_TILE), F32),
            pltpu.VMEM((ATT_HEADS // 2, KV_LATENT, 2 * Q_TILE), F32),
        ],
        compiler_params=_cparams(("parallel", "arbitrary")),
        name="dsa",
    )(qit, wit, kn, c, ct, qat, pa3, bias_pairs, wuvt)


def _outp_kernel(ya_ref, yb_ref, wa_ref, wb_ref, x_ref, g_ref, o_ref, *, final_norm):
    acc = jnp.dot(ya_ref[...], wa_ref[...], preferred_element_type=F32)
    acc = acc + jnp.dot(yb_ref[...], wb_ref[...], preferred_element_type=F32)
    x = x_ref[...] + acc
    if final_norm:
        x = x * lax.rsqrt(jnp.mean(x * x, axis=-1, keepdims=True) + EPS) * g_ref[...]
    o_ref[...] = x


def _outp(ya, yb, w_out, x2, g, final_norm, tm=256):
    m, d = x2.shape
    ka, kb = ya.shape[1], yb.shape[1]
    assert ka == kb and w_out.shape[0] == ka + kb
    return pl.pallas_call(
        functools.partial(_outp_kernel, final_norm=final_norm),
        grid=(m // tm,),
        in_specs=[
            pl.BlockSpec((tm, ka), lambda i: (i, 0)),
            pl.BlockSpec((tm, kb), lambda i: (i, 0)),
            pl.BlockSpec((ka, d), lambda i: (0, 0)),
            pl.BlockSpec((kb, d), lambda i: (1, 0)),
            pl.BlockSpec((tm, d), lambda i: (i, 0)),
            pl.BlockSpec((1, d), lambda i: (0, 0)),
        ],
        out_specs=pl.BlockSpec((tm, d), lambda i: (i, 0)),
        out_shape=jax.ShapeDtypeStruct((m, d), F32),
        compiler_params=_cparams(("parallel",)),
        name="outp",
    )(ya, yb, w_out, w_out, x2, g)


def _t5_bucket(dist):
    n = jnp.maximum(dist, 0)
    max_exact = REL_BUCKETS // 2
    nf = jnp.maximum(n, 1).astype(F32)
    large = max_exact + (jnp.log(nf / max_exact) / np.log(REL_MAX_DIST / max_exact)
                         * (REL_BUCKETS - max_exact)).astype(I32)
    large = jnp.minimum(large, REL_BUCKETS - 1)
    return jnp.where(n < max_exact, n, large)


def _bias_tiles(rel_bias):
    span = K_CHUNK + Q_TILE
    table = rel_bias[_t5_bucket(jnp.arange(span + 1, dtype=I32))].astype(F32)
    table = (table[:span] - table[span:]).T
    n = span + Q_TILE - 1
    a = jnp.concatenate([jnp.zeros((ATT_HEADS, Q_TILE - 1), F32), table], axis=1)
    shifted = jnp.tile(a, (1, span + 1))[:, :span * (n + 1)].reshape(ATT_HEADS, span, n + 1)
    tiles = shifted[:, ::-1, :Q_TILE].reshape(ATT_HEADS, 2, K_CHUNK, Q_TILE)
    tiles = jnp.concatenate([jnp.zeros_like(tiles[:, :1]), tiles], axis=1)
    pairs = tiles.reshape(ATT_HEADS // 2, 2, 3, K_CHUNK, Q_TILE)
    return jnp.transpose(pairs, (0, 2, 3, 1, 4)).reshape(ATT_HEADS // 2, 3, K_CHUNK, 2 * Q_TILE)


def kernel(x, norm_g, w_in, conv_w, conv_b, lru_wa, lru_ba, lru_wx, lru_bx, lru_lambda, ckv_norm_g, idx_k_norm_g, idx_k_norm_b, w_uk, w_uv, w_out, rel_bias, final_norm_g):
    bsz, s, d = x.shape
    depth = w_in.shape[0]
    lru_w = lru_wa.shape[1] * lru_wa.shape[2]
    att_w = ATT_HEADS * HEAD_DIM
    idx_w = IDX_HEADS * IDX_DIM
    assert K_CHUNK == Q_TILE and REL_MAX_DIST <= K_CHUNK
    assert lru_w == att_w == idx_w and att_w % KV_LATENT == 0
    topk = min(INDEX_TOPK, s // 4)

    o_q = 2 * lru_w
    o_ckv = o_q + att_w
    o_gb = o_ckv + KV_LATENT
    o_qi = o_gb + att_w
    o_ki = o_qi + idx_w
    tn = 512
    cols_a = {"xa": 0, "ga": lru_w, "gb": 2 * lru_w}
    cols_b = {"q": 0, "qi": att_w}
    cols_c = {"ckv": 0, "small": KV_LATENT}
    n_f32, n_bf16 = 3 * lru_w, att_w + idx_w

    bias_pairs = _bias_tiles(rel_bias)
    x2 = x.reshape(bsz * s, d)
    for l in range(depth):
        w_all_t = _wcast(w_in[l].T, [(0, o_q), (o_gb, o_qi), (o_q, o_ckv), (o_qi, o_ki), (o_ckv, o_gb),
                                     (o_ki, w_in.shape[2])])
        pa, pb, pc = _proj(x2, norm_g[l][None, :], w_all_t, n_f32, n_bf16, tn=tn)
        pa3 = pa.reshape(bsz, s, -1)
        pb3 = pb.reshape(bsz, s, -1)
        pc3 = pc.reshape(bsz, s, -1)

        ya = _rglru(pa3, cols_a, conv_w[l], conv_b[l][None, :], lru_wa[l].astype(BF16), lru_ba[l][None, :],
                    lru_wx[l].astype(BF16), lru_bx[l][None, :], lru_lambda[l][None, :])

        wukt = jnp.transpose(w_uk[l], (0, 2, 1)).astype(BF16)
        qat, qit, c, ct, kn, wit = _prep(pb3, pc3, {**cols_b, **cols_c}, wukt, ckv_norm_g[l][None, :],
                                          idx_k_norm_g[l][None, :], idx_k_norm_b[l][None, :])
        wuvt = jnp.transpose(w_uv[l], (0, 2, 1)).astype(BF16)
        yb = _dsa(qit, wit, kn, c, ct, qat, pa3, _col_block(cols_a["gb"], att_w), bias_pairs, wuvt, topk)

        x2 = _outp(ya.reshape(bsz * s, lru_w), yb.reshape(bsz * s, att_w), w_out[l].astype(BF16), x2,
                   final_norm_g[None, :], final_norm=(l == depth - 1))
    return x2.reshape(bsz, s, d)
```

```python
import functools

import numpy as np
import jax
import jax.numpy as jnp
from jax import lax
from jax.experimental import pallas as pl
from jax.experimental.pallas import tpu as pltpu

F32 = jnp.float32
BF16 = jnp.bfloat16
I32 = jnp.int32

LRU_BLOCKS = 8
CONV_WIDTH = 4
LRU_C = 8.0
ATT_HEADS = 8
HEAD_DIM = 128
KV_LATENT = 256
IDX_HEADS = 16
IDX_DIM = 64
INDEX_TOPK = 256
REL_BUCKETS = 32
REL_MAX_DIST = 128
EPS = 1e-6

Q_TILE = 256
K_CHUNK = 128
ATT_CHUNK = 256
NEG = float(np.finfo(np.float32).min)
INT_MIN = -(2 ** 31)
VMEM_LIMIT = 56 * 1024 * 1024


def _cparams(sem):
    return pltpu.CompilerParams(dimension_semantics=sem, vmem_limit_bytes=VMEM_LIMIT)


def _col_block(offset, width):
    assert offset % width == 0
    return offset // width


def _wcast(wt, order, tn=256):
    n, d = wt.shape
    src_blocks = []
    for start, stop in order:
        assert start % tn == 0 and (stop % tn == 0 or stop == n)
        src_blocks += list(range(start // tn, -(-stop // tn)))
    table = jnp.asarray(src_blocks, I32)
    grid_spec = pltpu.PrefetchScalarGridSpec(
        num_scalar_prefetch=1,
        grid=(len(src_blocks),),
        in_specs=[pl.BlockSpec((tn, d), lambda j, t: (t[j], 0))],
        out_specs=pl.BlockSpec((tn, d), lambda j, t: (j, 0)),
    )

    def body(t_ref, w_ref, o_ref):
        first = t_ref[pl.program_id(0)] * tn
        row = first + lax.broadcasted_iota(I32, w_ref.shape, 0)
        o_ref[...] = jnp.where(row < n, w_ref[...], 0.0).astype(o_ref.dtype)

    return pl.pallas_call(
        body,
        grid_spec=grid_spec,
        out_shape=jax.ShapeDtypeStruct((len(src_blocks) * tn, d), BF16),
        compiler_params=_cparams(("arbitrary",)),
        name="wcast",
    )(table, wt)


def _proj_kernel(x_ref, g_ref, w_ref, oa_ref, ob_ref, oc_ref, h_ref, *, na, nb):
    j = pl.program_id(1)
    nt = (((1,), (1,)), ((), ()))

    @pl.when(j == 0)
    def _():
        x = x_ref[...]
        y = x * lax.rsqrt(jnp.mean(x * x, axis=-1, keepdims=True) + EPS)
        h_ref[...] = (y * g_ref[...]).astype(BF16)

    @pl.when(j < na)
    def _():
        oa_ref[...] = lax.dot_general(h_ref[...], w_ref[...], nt, preferred_element_type=F32)

    @pl.when((j >= na) & (j < na + nb))
    def _():
        ob_ref[...] = lax.dot_general(h_ref[...], w_ref[...], nt, preferred_element_type=F32).astype(BF16)

    @pl.when(j >= na + nb)
    def _():
        oc_ref[...] = lax.dot_general(h_ref[...], w_ref[...], nt, preferred_element_type=F32)


def _proj(x2, g, w_all_t, n_f32, n_bf16, tm=1024, tn=512):
    m, d = x2.shape
    na, nb = n_f32 // tn, n_bf16 // tn
    assert n_f32 % tn == 0 and n_bf16 % tn == 0 and w_all_t.shape[0] == n_f32 + n_bf16 + tn
    return pl.pallas_call(
        functools.partial(_proj_kernel, na=na, nb=nb),
        grid=(m // tm, na + nb + 1),
        in_specs=[
            pl.BlockSpec((tm, d), lambda i, j: (i, 0)),
            pl.BlockSpec((1, d), lambda i, j: (0, 0)),
            pl.BlockSpec((tn, d), lambda i, j: (j, 0)),
        ],
        out_specs=[
            pl.BlockSpec((tm, tn), lambda i, j: (i, jnp.minimum(j, na - 1))),
            pl.BlockSpec((tm, tn), lambda i, j: (i, jnp.clip(j - na, 0, nb - 1))),
            pl.BlockSpec((tm, tn), lambda i, j: (i, 0)),
        ],
        out_shape=[
            jax.ShapeDtypeStruct((m, n_f32), F32),
            jax.ShapeDtypeStruct((m, n_bf16), BF16),
            jax.ShapeDtypeStruct((m, tn), F32),
        ],
        scratch_shapes=[pltpu.VMEM((tm, d), BF16)],
        compiler_params=_cparams(("parallel", "arbitrary")),
        name="proj",
    )(x2, g, w_all_t)


def _sigmoid(v):
    return 0.5 * jnp.tanh(0.5 * v) + 0.5


def _scan_step(a, b, k, axis, idx):
    keep = idx >= k
    a_prev = jnp.where(keep, pltpu.roll(a, k, axis=axis), 1.0)
    b_prev = jnp.where(keep, pltpu.roll(b, k, axis=axis), 0.0)
    return a * a_prev, a * b_prev + b


def _rglru_kernel(xa_ref, ga_ref, cw_ref, cb_ref, wa_ref, ba_ref, wx_ref, bx_ref, lam_ref,
                  o_ref, pad_s, a_s, b_s, c_s):
    s, w = xa_ref.shape
    tile = 8
    n_tiles = s // tile

    pad_s[0:tile, :] = jnp.zeros((tile, w), F32)
    pad_s[tile:tile + s, :] = xa_ref[...]
    acc = pad_s[tile:tile + s, :] * cw_ref[CONV_WIDTH - 1:CONV_WIDTH, :]
    for j in range(CONV_WIDTH - 1):
        back = CONV_WIDTH - 1 - j
        acc = acc + pad_s[tile - back:tile - back + s, :] * cw_ref[j:j + 1, :]
    xc = cb_ref[...] + acc

    xcb = xc.astype(BF16)
    r = _sigmoid(jnp.dot(xcb, wa_ref[...], preferred_element_type=F32) + ba_ref[...])
    i = _sigmoid(jnp.dot(xcb, wx_ref[...], preferred_element_type=F32) + bx_ref[...])
    z = -lam_ref[...]
    softplus = jnp.maximum(z, 0.0) + jnp.log1p(jnp.exp(-jnp.abs(z)))
    log_a = (-LRU_C) * r * softplus
    a = jnp.exp(log_a)
    m2 = (1.0 + a * a) * jnp.tanh(-log_a)
    mult = jnp.where(m2 > 0.0, m2 * lax.rsqrt(m2), 0.0)
    gated = i * xc
    b_s[...] = mult * gated
    b_s[0:1, :] = gated[0:1, :]

    a3 = a.reshape(n_tiles, tile, w)
    b3 = b_s[...].reshape(n_tiles, tile, w)
    sub = lax.broadcasted_iota(I32, (n_tiles, tile, w), 1)
    for k in (1, 2, 4):
        a3, b3 = _scan_step(a3, b3, k, 1, sub)
    a_s[...] = a3.reshape(s, w)
    b_s[...] = b3.reshape(s, w)

    at = a_s[pl.ds(tile - 1, n_tiles, stride=tile), :]
    bt = b_s[pl.ds(tile - 1, n_tiles, stride=tile), :]
    trow = lax.broadcasted_iota(I32, (n_tiles, w), 0)
    k = 1
    while k < n_tiles:
        at, bt = _scan_step(at, bt, k, 0, trow)
        k *= 2
    c_s[0:tile, :] = jnp.zeros((tile, w), F32)
    c_s[tile:tile + n_tiles, :] = bt

    def apply(t, carry):
        r0 = pl.multiple_of(t * tile, tile)
        before = c_s[pl.ds(tile - 1 + t, tile, stride=0), :]
        h = a_s[pl.ds(r0, tile), :] * before + b_s[pl.ds(r0, tile), :]
        ga = ga_ref[pl.ds(r0, tile), :]
        o_ref[pl.ds(r0, tile), :] = (h * (ga * _sigmoid(ga))).astype(o_ref.dtype)
        return carry

    lax.fori_loop(0, n_tiles, apply, 0, unroll=8)


def _rglru(pa3, cols, conv_w, conv_b, wa, ba, wx, bx, lam):
    bsz, s, _ = pa3.shape
    g, w = wa.shape[0], wa.shape[-1]
    xa_blk = _col_block(cols["xa"], w)
    ga_blk = _col_block(cols["ga"], w)
    vec = lambda: pl.BlockSpec((1, w), lambda b, j: (0, j))
    return pl.pallas_call(
        _rglru_kernel,
        grid=(bsz, g),
        in_specs=[
            pl.BlockSpec((None, s, w), lambda b, j: (b, 0, xa_blk + j)),
            pl.BlockSpec((None, s, w), lambda b, j: (b, 0, ga_blk + j)),
            pl.BlockSpec((CONV_WIDTH, w), lambda b, j: (0, j)),
            vec(),
            pl.BlockSpec((None, w, w), lambda b, j: (j, 0, 0)),
            vec(),
            pl.BlockSpec((None, w, w), lambda b, j: (j, 0, 0)),
            vec(),
            vec(),
        ],
        out_specs=pl.BlockSpec((None, s, w), lambda b, j: (b, 0, j)),
        out_shape=jax.ShapeDtypeStruct((bsz, s, g * w), BF16),
        scratch_shapes=[pltpu.VMEM((s + 8, w), F32), pltpu.VMEM((s, w), F32), pltpu.VMEM((s, w), F32),
                        pltpu.VMEM((s // 8 + 8, w), F32)],
        compiler_params=_cparams(("parallel", "parallel")),
        name="rglru",
    )(pa3, pa3, conv_w, conv_b, wa, ba, wx, bx, lam)


def _prep_kernel(q_ref, qi_ref, ckv_ref, sm_ref, wukt_ref, cg_ref, kg_ref, kb_ref,
                 qat_ref, qit_ref, c_ref, ct_ref, kn_ref, wit_ref):
    qt = q_ref[...].T
    scale = HEAD_DIM ** -0.5
    for h in range(ATT_HEADS):
        qa = jnp.dot(wukt_ref[h], qt[h * HEAD_DIM:(h + 1) * HEAD_DIM], preferred_element_type=F32)
        qat_ref[h] = (qa * scale).astype(BF16)
    qit = qi_ref[...].T
    for h in range(IDX_HEADS):
        qit_ref[h] = qit[h * IDX_DIM:(h + 1) * IDX_DIM, :]

    ckv = ckv_ref[...]
    c = ckv * lax.rsqrt(jnp.mean(ckv * ckv, axis=-1, keepdims=True) + EPS) * cg_ref[...]
    c_ref[...] = c.astype(BF16)
    ct_ref[...] = c.T.astype(BF16)

    sm = sm_ref[...]
    ki = sm[:, :IDX_DIM]
    mu = jnp.mean(ki, axis=-1, keepdims=True)
    var = jnp.mean(jnp.square(ki - mu), axis=-1, keepdims=True)
    kn = (ki - mu) * lax.rsqrt(var + EPS) * kg_ref[...] + kb_ref[...]
    kn_ref[...] = kn.astype(BF16)
    wit_ref[...] = sm.T[IDX_DIM:IDX_DIM + IDX_HEADS, :] * (IDX_HEADS ** -0.5 * IDX_DIM ** -0.5)


def _prep(pb3, pc3, cols, w_ukt, ckv_g, k_g, k_b):
    bsz, s, _ = pb3.shape
    att_w = ATT_HEADS * HEAD_DIM
    idx_w = IDX_HEADS * IDX_DIM
    q_blk = _col_block(cols["q"], att_w)
    qi_blk = _col_block(cols["qi"], idx_w)
    ckv_blk = _col_block(cols["ckv"], KV_LATENT)
    small_w = 128
    small_blk = _col_block(cols["small"], small_w)
    tm = Q_TILE
    const = lambda shape: pl.BlockSpec(shape, lambda b, i: (0,) * len(shape))
    return pl.pallas_call(
        _prep_kernel,
        grid=(bsz, s // tm),
        in_specs=[
            pl.BlockSpec((None, tm, att_w), lambda b, i: (b, i, q_blk)),
            pl.BlockSpec((None, tm, idx_w), lambda b, i: (b, i, qi_blk)),
            pl.BlockSpec((None, tm, KV_LATENT), lambda b, i: (b, i, ckv_blk)),
            pl.BlockSpec((None, tm, small_w), lambda b, i: (b, i, small_blk)),
            const(w_ukt.shape),
            const((1, KV_LATENT)),
            const((1, IDX_DIM)),
            const((1, IDX_DIM)),
        ],
        out_specs=[
            pl.BlockSpec((None, None, ATT_HEADS, KV_LATENT, Q_TILE), lambda b, i: (b, i, 0, 0, 0)),
            pl.BlockSpec((None, None, IDX_HEADS, IDX_DIM, Q_TILE), lambda b, i: (b, i, 0, 0, 0)),
            pl.BlockSpec((None, tm, KV_LATENT), lambda b, i: (b, i, 0)),
            pl.BlockSpec((None, KV_LATENT, tm), lambda b, i: (b, 0, i)),
            pl.BlockSpec((None, tm, IDX_DIM), lambda b, i: (b, i, 0)),
            pl.BlockSpec((None, IDX_HEADS, tm), lambda b, i: (b, 0, i)),
        ],
        out_shape=[
            jax.ShapeDtypeStruct((bsz, s // Q_TILE, ATT_HEADS, KV_LATENT, Q_TILE), BF16),
            jax.ShapeDtypeStruct((bsz, s // Q_TILE, IDX_HEADS, IDX_DIM, Q_TILE), BF16),
            jax.ShapeDtypeStruct((bsz, s, KV_LATENT), BF16),
            jax.ShapeDtypeStruct((bsz, KV_LATENT, s), BF16),
            jax.ShapeDtypeStruct((bsz, s, IDX_DIM), BF16),
            jax.ShapeDtypeStruct((bsz, IDX_HEADS, s), F32),
        ],
        compiler_params=_cparams(("parallel", "parallel")),
        name="prep",
    )(pb3, pb3, pc3, pc3, w_ukt, ckv_g, k_g, k_b)


def _tree_sum(parts):
    while len(parts) > 1:
        paired = [parts[i] + parts[i + 1] for i in range(0, len(parts) - 1, 2)]
        parts = paired + ([parts[-1]] if len(parts) % 2 else [])
    return parts[0]


def _sortable_to_f32(u):
    key = u ^ INT_MIN
    return lax.bitcast_convert_type(key ^ ((key >> 31) & 0x7FFFFFFF), F32)


def _kth_largest(score_ref, rows, k):
    chains = 4

    def step(i, u):
        cand = u | (jnp.int32(1) << (31 - i))
        cand_f = _sortable_to_f32(cand)
        accs = [None] * chains
        for r in range(rows // 8):
            hit = jnp.where(score_ref[r * 8:(r + 1) * 8, :] >= cand_f, 1.0, 0.0)
            accs[r % chains] = hit if accs[r % chains] is None else accs[r % chains] + hit
        cnt = jnp.sum(_tree_sum([a for a in accs if a is not None]), axis=0, keepdims=True)
        return jnp.where(cnt >= k, cand, u)

    return _sortable_to_f32(lax.fori_loop(0, 32, step, jnp.zeros((1, Q_TILE), I32)))


def _dsa_kernel(qit_ref, wit_ref, kn_ref, c_ref, ct_ref, qat_ref, gb_ref, bias_ref, wuvt_ref,
                o_ref, score_s, thr_s, lg_s, acc_s, topk):
    qb = pl.program_id(1)
    q_tiles = Q_TILE // K_CHUNK
    nkc = (qb + 1) * q_tiles
    nac = (nkc * K_CHUNK + ATT_CHUNK - 1) // ATT_CHUNK
    tiles = ATT_CHUNK // K_CHUNK

    kiota = lax.broadcasted_iota(I32, (K_CHUNK, Q_TILE), 0)
    qpos = qb * Q_TILE + lax.broadcasted_iota(I32, (K_CHUNK, Q_TILE), 1)

    def score_chunk(ac, carry):
        for t in range(tiles):
            k0 = pl.multiple_of(ac * ATT_CHUNK + t * K_CHUNK, K_CHUNK)
            kn = kn_ref[pl.ds(k0, K_CHUNK), :]
            acc = jnp.zeros((K_CHUNK, Q_TILE), F32)
            for h in range(IDX_HEADS):
                sc = jnp.dot(kn, qit_ref[h], preferred_element_type=F32)
                acc = acc + jnp.maximum(sc, 0.0) * wit_ref[h:h + 1, :]
            score_s[pl.ds(k0, K_CHUNK), :] = jnp.where(kiota + k0 <= qpos, acc, -jnp.inf)
        return carry

    lax.fori_loop(0, nac, score_chunk, 0)

    for v in range(1, score_s.shape[0] // ATT_CHUNK + 1):
        @pl.when(nac == v)
        def _(rows=v * ATT_CHUNK):
            thr = _kth_largest(score_s, rows, float(topk))
            thr = jnp.where(thr >= NEG, thr, NEG)
            thr_s[...] = jnp.broadcast_to(thr, thr_s.shape)

    thr = thr_s[0:1, :]

    def logit_chunk(ac, m8s):
        r0 = pl.multiple_of(ac * ATT_CHUNK, ATT_CHUNK)
        c_chunk = c_ref[pl.ds(r0, ATT_CHUNK), :]
        mbias = jnp.where(score_s[pl.ds(r0, ATT_CHUNK), :] >= thr, 0.0, NEG)
        near = [[jnp.clip(ac * tiles + t - (qb * q_tiles + j) + 2, 0, 2) for j in range(q_tiles)]
                for t in range(tiles)]
        out = []
        for h in range(ATT_HEADS):
            lg = jnp.dot(c_chunk, qat_ref[h], preferred_element_type=F32) + mbias
            lg = jnp.concatenate(
                [lg[t * K_CHUNK:(t + 1) * K_CHUNK]
                 + jnp.concatenate([bias_ref[h, near[t][j]] for j in range(q_tiles)], axis=1)
                 for t in range(tiles)], axis=0)
            lg_s[h, pl.ds(r0, ATT_CHUNK), :] = lg
            out.append(jnp.maximum(m8s[h], jnp.max(lg.reshape(ATT_CHUNK // 8, 8, Q_TILE), axis=0)))
        return tuple(out)

    m8s = lax.fori_loop(0, nac, logit_chunk,
                        tuple(jnp.full((8, Q_TILE), NEG, F32) for _ in range(ATT_HEADS)))
    ms = [jnp.max(m8, axis=0, keepdims=True) for m8 in m8s]

    acc_s[...] = jnp.zeros(acc_s.shape, F32)

    def pv_chunk(ac, l8s):
        r0 = pl.multiple_of(ac * ATT_CHUNK, ATT_CHUNK)
        ct_chunk = ct_ref[:, pl.ds(r0, ATT_CHUNK)]
        out = []
        for h in range(ATT_HEADS):
            pr = jnp.exp(lg_s[h, pl.ds(r0, ATT_CHUNK), :] - ms[h])
            out.append(l8s[h] + jnp.sum(pr.reshape(ATT_CHUNK // 8, 8, Q_TILE), axis=0))
            acc_s[h] += jnp.dot(ct_chunk, pr.astype(BF16), preferred_element_type=F32)
        return tuple(out)

    l8s = lax.fori_loop(0, nac, pv_chunk, tuple(jnp.zeros((8, Q_TILE), F32) for _ in range(ATT_HEADS)))

    for h in range(ATT_HEADS):
        denom = jnp.sum(l8s[h], axis=0, keepdims=True)
        o_t = acc_s[h] * (1.0 / denom)
        y_t = jnp.dot(wuvt_ref[h], o_t.astype(BF16), preferred_element_type=F32)
        gb = gb_ref[:, h * HEAD_DIM:(h + 1) * HEAD_DIM]
        o_ref[:, h * HEAD_DIM:(h + 1) * HEAD_DIM] = (y_t.T * (gb * _sigmoid(gb))).astype(o_ref.dtype)


def _dsa(qit, wit, kn, c, ct, qat, pa3, gb_blk, bias_tiles, wuvt, topk):
    bsz, s, _ = c.shape
    att_w = ATT_HEADS * HEAD_DIM
    assert s % ATT_CHUNK == 0 and s % Q_TILE == 0 and Q_TILE % K_CHUNK == 0 and ATT_CHUNK % K_CHUNK == 0
    const = lambda shape: pl.BlockSpec(shape, lambda b, i: (0,) * len(shape))
    per_tile = lambda shape: pl.BlockSpec((None, None) + shape, lambda b, i: (b, i) + (0,) * len(shape))
    return pl.pallas_call(
        functools.partial(_dsa_kernel, topk=topk),
        grid=(bsz, s // Q_TILE),
        in_specs=[
            per_tile(qit.shape[2:]),
            pl.BlockSpec((None, IDX_HEADS, Q_TILE), lambda b, i: (b, 0, i)),
            pl.BlockSpec((None, s, IDX_DIM), lambda b, i: (b, 0, 0)),
            pl.BlockSpec((None, s, KV_LATENT), lambda b, i: (b, 0, 0)),
            pl.BlockSpec((None, KV_LATENT, s), lambda b, i: (b, 0, 0)),
            per_tile(qat.shape[2:]),
            pl.BlockSpec((None, Q_TILE, att_w), lambda b, i: (b, i, gb_blk)),
            const(bias_tiles.shape),
            const(wuvt.shape),
        ],
        out_specs=pl.BlockSpec((None, Q_TILE, att_w), lambda b, i: (b, i, 0)),
        out_shape=jax.ShapeDtypeStruct((bsz, s, att_w), BF16),
        scratch_shapes=[
            pltpu.VMEM((s, Q_TILE), F32),
            pltpu.VMEM((8, Q_TILE), F32),
            pltpu.VMEM((ATT_HEADS, s, Q_TILE), F32),
            pltpu.VMEM((ATT_HEADS, KV_LATENT, Q_TILE), F32),
        ],
        compiler_params=_cparams(("parallel", "arbitrary")),
        name="dsa",
    )(qit, wit, kn, c, ct, qat, pa3, bias_tiles, wuvt)


def _outp_kernel(ya_ref, yb_ref, wa_ref, wb_ref, x_ref, g_ref, o_ref, *, final_norm):
    acc = jnp.dot(ya_ref[...], wa_ref[...], preferred_element_type=F32)
    acc = acc + jnp.dot(yb_ref[...], wb_ref[...], preferred_element_type=F32)
    x = x_ref[...] + acc
    if final_norm:
        x = x * lax.rsqrt(jnp.mean(x * x, axis=-1, keepdims=True) + EPS) * g_ref[...]
    o_ref[...] = x


def _outp(ya, yb, w_out, x2, g, final_norm, tm=256):
    m, d = x2.shape
    ka, kb = ya.shape[1], yb.shape[1]
    assert ka == kb and w_out.shape[0] == ka + kb
    return pl.pallas_call(
        functools.partial(_outp_kernel, final_norm=final_norm),
        grid=(m // tm,),
        in_specs=[
            pl.BlockSpec((tm, ka), lambda i: (i, 0)),
            pl.BlockSpec((tm, kb), lambda i: (i, 0)),
            pl.BlockSpec((ka, d), lambda i: (0, 0)),
            pl.BlockSpec((kb, d), lambda i: (1, 0)),
            pl.BlockSpec((tm, d), lambda i: (i, 0)),
            pl.BlockSpec((1, d), lambda i: (0, 0)),
        ],
        out_specs=pl.BlockSpec((tm, d), lambda i: (i, 0)),
        out_shape=jax.ShapeDtypeStruct((m, d), F32),
        compiler_params=_cparams(("parallel",)),
        name="outp",
    )(ya, yb, w_out, w_out, x2, g)


def _t5_bucket(dist):
    n = jnp.maximum(dist, 0)
    max_exact = REL_BUCKETS // 2
    nf = jnp.maximum(n, 1).astype(F32)
    large = max_exact + (jnp.log(nf / max_exact) / np.log(REL_MAX_DIST / max_exact)
                         * (REL_BUCKETS - max_exact)).astype(I32)
    large = jnp.minimum(large, REL_BUCKETS - 1)
    return jnp.where(n < max_exact, n, large)


def _bias_tiles(rel_bias):
    qw = K_CHUNK
    span = K_CHUNK + qw
    table = rel_bias[_t5_bucket(jnp.arange(span + 1, dtype=I32))].astype(F32)
    table = (table[:span] - table[span:]).T
    n = span + qw - 1
    a = jnp.concatenate([jnp.zeros((ATT_HEADS, qw - 1), F32), table], axis=1)
    shifted = jnp.tile(a, (1, span + 1))[:, :span * (n + 1)].reshape(ATT_HEADS, span, n + 1)
    tiles = shifted[:, ::-1, :qw].reshape(ATT_HEADS, 2, K_CHUNK, qw)
    return jnp.concatenate([jnp.zeros_like(tiles[:, :1]), tiles], axis=1)


def kernel(x, norm_g, w_in, conv_w, conv_b, lru_wa, lru_ba, lru_wx, lru_bx, lru_lambda, ckv_norm_g, idx_k_norm_g, idx_k_norm_b, w_uk, w_uv, w_out, rel_bias, final_norm_g):
    bsz, s, d = x.shape
    depth = w_in.shape[0]
    lru_w = lru_wa.shape[1] * lru_wa.shape[2]
    att_w = ATT_HEADS * HEAD_DIM
    idx_w = IDX_HEADS * IDX_DIM
    assert REL_MAX_DIST <= K_CHUNK
    assert lru_w == att_w == idx_w and att_w % KV_LATENT == 0
    topk = min(INDEX_TOPK, s // 4)

    o_q = 2 * lru_w
    o_ckv = o_q + att_w
    o_gb = o_ckv + KV_LATENT
    o_qi = o_gb + att_w
    o_ki = o_qi + idx_w
    tn = 512
    cols_a = {"xa": 0, "ga": lru_w, "gb": 2 * lru_w}
    cols_b = {"q": 0, "qi": att_w}
    cols_c = {"ckv": 0, "small": KV_LATENT}
    n_f32, n_bf16 = 3 * lru_w, att_w + idx_w

    bias_tiles = _bias_tiles(rel_bias)
    x2 = x.reshape(bsz * s, d)
    for l in range(depth):
        w_all_t = _wcast(w_in[l].T, [(0, o_q), (o_gb, o_qi), (o_q, o_ckv), (o_qi, o_ki), (o_ckv, o_gb),
                                     (o_ki, w_in.shape[2])])
        pa, pb, pc = _proj(x2, norm_g[l][None, :], w_all_t, n_f32, n_bf16, tn=tn)
        pa3 = pa.reshape(bsz, s, -1)
        pb3 = pb.reshape(bsz, s, -1)
        pc3 = pc.reshape(bsz, s, -1)

        ya = _rglru(pa3, cols_a, conv_w[l], conv_b[l][None, :], lru_wa[l].astype(BF16), lru_ba[l][None, :],
                    lru_wx[l].astype(BF16), lru_bx[l][None, :], lru_lambda[l][None, :])

        wukt = jnp.transpose(w_uk[l], (0, 2, 1)).astype(BF16)
        qat, qit, c, ct, kn, wit = _prep(pb3, pc3, {**cols_b, **cols_c}, wukt, ckv_norm_g[l][None, :],
                                          idx_k_norm_g[l][None, :], idx_k_norm_b[l][None, :])
        wuvt = jnp.transpose(w_uv[l], (0, 2, 1)).astype(BF16)
        yb = _dsa(qit, wit, kn, c, ct, qat, pa3, _col_block(cols_a["gb"], att_w), bias_tiles, wuvt, topk)

        x2 = _outp(ya.reshape(bsz * s, lru_w), yb.reshape(bsz * s, att_w), w_out[l].astype(BF16), x2,
                   final_norm_g[None, :], final_norm=(l == depth - 1))
    return x2.reshape(bsz, s, d)
```

```python
import functools

import numpy as np
import jax
import jax.numpy as jnp
from jax import lax
from jax.experimental import pallas as pl
from jax.experimental.pallas import tpu as pltpu

F32 = jnp.float32
BF16 = jnp.bfloat16
I32 = jnp.int32

LRU_BLOCKS = 8
CONV_WIDTH = 4
LRU_C = 8.0
ATT_HEADS = 8
HEAD_DIM = 128
KV_LATENT = 256
IDX_HEADS = 16
IDX_DIM = 64
INDEX_TOPK = 256
REL_BUCKETS = 32
REL_MAX_DIST = 128
EPS = 1e-6
LOG2E = float(np.log2(np.e))
ONES_ROWS = 16

Q_TILE = 256
K_CHUNK = 128
ATT_CHUNK = 256
NEG = float(np.finfo(np.float32).min)
INT_MIN = -(2 ** 31)
VMEM_LIMIT = 56 * 1024 * 1024


def _cparams(sem):
    return pltpu.CompilerParams(dimension_semantics=sem, vmem_limit_bytes=VMEM_LIMIT)


def _col_block(offset, width):
    assert offset % width == 0
    return offset // width


def _wcast(wt, order, tn=256):
    n, d = wt.shape
    src_blocks = []
    for start, stop in order:
        assert start % tn == 0 and (stop % tn == 0 or stop == n)
        src_blocks += list(range(start // tn, -(-stop // tn)))
    table = jnp.asarray(src_blocks, I32)
    grid_spec = pltpu.PrefetchScalarGridSpec(
        num_scalar_prefetch=1,
        grid=(len(src_blocks),),
        in_specs=[pl.BlockSpec((tn, d), lambda j, t: (t[j], 0))],
        out_specs=pl.BlockSpec((tn, d), lambda j, t: (j, 0)),
    )

    def body(t_ref, w_ref, o_ref):
        first = t_ref[pl.program_id(0)] * tn
        row = first + lax.broadcasted_iota(I32, w_ref.shape, 0)
        o_ref[...] = jnp.where(row < n, w_ref[...], 0.0).astype(o_ref.dtype)

    return pl.pallas_call(
        body,
        grid_spec=grid_spec,
        out_shape=jax.ShapeDtypeStruct((len(src_blocks) * tn, d), BF16),
        compiler_params=_cparams(("arbitrary",)),
        name="wcast",
    )(table, wt)


def _proj_kernel(x_ref, g_ref, w_ref, oa_ref, ob_ref, oc_ref, h_ref, *, na, nb):
    j = pl.program_id(1)
    nt = (((1,), (1,)), ((), ()))

    @pl.when(j == 0)
    def _():
        x = x_ref[...]
        y = x * lax.rsqrt(jnp.mean(x * x, axis=-1, keepdims=True) + EPS)
        h_ref[...] = (y * g_ref[...]).astype(BF16)

    @pl.when(j < na)
    def _():
        oa_ref[...] = lax.dot_general(h_ref[...], w_ref[...], nt, preferred_element_type=F32)

    @pl.when((j >= na) & (j < na + nb))
    def _():
        ob_ref[...] = lax.dot_general(h_ref[...], w_ref[...], nt, preferred_element_type=F32).astype(BF16)

    @pl.when(j >= na + nb)
    def _():
        oc_ref[...] = lax.dot_general(h_ref[...], w_ref[...], nt, preferred_element_type=F32)


def _proj(x2, g, w_all_t, n_f32, n_bf16, tm=1024, tn=512):
    m, d = x2.shape
    na, nb = n_f32 // tn, n_bf16 // tn
    assert n_f32 % tn == 0 and n_bf16 % tn == 0 and w_all_t.shape[0] == n_f32 + n_bf16 + tn
    return pl.pallas_call(
        functools.partial(_proj_kernel, na=na, nb=nb),
        grid=(m // tm, na + nb + 1),
        in_specs=[
            pl.BlockSpec((tm, d), lambda i, j: (i, 0)),
            pl.BlockSpec((1, d), lambda i, j: (0, 0)),
            pl.BlockSpec((tn, d), lambda i, j: (j, 0)),
        ],
        out_specs=[
            pl.BlockSpec((tm, tn), lambda i, j: (i, jnp.minimum(j, na - 1))),
            pl.BlockSpec((tm, tn), lambda i, j: (i, jnp.clip(j - na, 0, nb - 1))),
            pl.BlockSpec((tm, tn), lambda i, j: (i, 0)),
        ],
        out_shape=[
            jax.ShapeDtypeStruct((m, n_f32), F32),
            jax.ShapeDtypeStruct((m, n_bf16), BF16),
            jax.ShapeDtypeStruct((m, tn), F32),
        ],
        scratch_shapes=[pltpu.VMEM((tm, d), BF16)],
        compiler_params=_cparams(("parallel", "arbitrary")),
        name="proj",
    )(x2, g, w_all_t)


def _sigmoid(v):
    return 0.5 * jnp.tanh(0.5 * v) + 0.5


def _scan_step(a, b, k, axis, idx):
    keep = idx >= k
    a_prev = jnp.where(keep, pltpu.roll(a, k, axis=axis), 1.0)
    b_prev = jnp.where(keep, pltpu.roll(b, k, axis=axis), 0.0)
    return a * a_prev, a * b_prev + b


def _rglru_kernel(xa_ref, ga_ref, cw_ref, cb_ref, wa_ref, ba_ref, wx_ref, bx_ref, lam_ref,
                  o_ref, pad_s, a_s, b_s, c_s):
    s, w = xa_ref.shape
    tile = 8
    n_tiles = s // tile

    pad_s[0:tile, :] = jnp.zeros((tile, w), F32)
    pad_s[tile:tile + s, :] = xa_ref[...]
    acc = pad_s[tile:tile + s, :] * cw_ref[CONV_WIDTH - 1:CONV_WIDTH, :]
    for j in range(CONV_WIDTH - 1):
        back = CONV_WIDTH - 1 - j
        acc = acc + pad_s[tile - back:tile - back + s, :] * cw_ref[j:j + 1, :]
    xc = cb_ref[...] + acc

    xcb = xc.astype(BF16)
    r = _sigmoid(jnp.dot(xcb, wa_ref[...], preferred_element_type=F32) + ba_ref[...])
    i = _sigmoid(jnp.dot(xcb, wx_ref[...], preferred_element_type=F32) + bx_ref[...])
    z = -lam_ref[...]
    softplus = jnp.maximum(z, 0.0) + jnp.log1p(jnp.exp(-jnp.abs(z)))
    log_a = (-LRU_C) * r * softplus
    a = jnp.exp(log_a)
    m2 = (1.0 + a * a) * jnp.tanh(-log_a)
    mult = jnp.where(m2 > 0.0, m2 * lax.rsqrt(m2), 0.0)
    gated = i * xc
    b_s[...] = mult * gated
    b_s[0:1, :] = gated[0:1, :]

    a3 = a.reshape(n_tiles, tile, w)
    b3 = b_s[...].reshape(n_tiles, tile, w)
    sub = lax.broadcasted_iota(I32, (n_tiles, tile, w), 1)
    for k in (1, 2, 4):
        a3, b3 = _scan_step(a3, b3, k, 1, sub)
    a_s[...] = a3.reshape(s, w)
    b_s[...] = b3.reshape(s, w)

    at = a_s[pl.ds(tile - 1, n_tiles, stride=tile), :]
    bt = b_s[pl.ds(tile - 1, n_tiles, stride=tile), :]
    trow = lax.broadcasted_iota(I32, (n_tiles, w), 0)
    k = 1
    while k < n_tiles:
        at, bt = _scan_step(at, bt, k, 0, trow)
        k *= 2
    c_s[0:tile, :] = jnp.zeros((tile, w), F32)
    c_s[tile:tile + n_tiles, :] = bt

    def apply(t, carry):
        r0 = pl.multiple_of(t * tile, tile)
        before = c_s[pl.ds(tile - 1 + t, tile, stride=0), :]
        h = a_s[pl.ds(r0, tile), :] * before + b_s[pl.ds(r0, tile), :]
        ga = ga_ref[pl.ds(r0, tile), :]
        o_ref[pl.ds(r0, tile), :] = (h * (ga * _sigmoid(ga))).astype(o_ref.dtype)
        return carry

    lax.fori_loop(0, n_tiles, apply, 0, unroll=8)


def _rglru(pa3, cols, conv_w, conv_b, wa, ba, wx, bx, lam):
    bsz, s, _ = pa3.shape
    g, w = wa.shape[0], wa.shape[-1]
    xa_blk = _col_block(cols["xa"], w)
    ga_blk = _col_block(cols["ga"], w)
    vec = lambda: pl.BlockSpec((1, w), lambda b, j: (0, j))
    return pl.pallas_call(
        _rglru_kernel,
        grid=(bsz, g),
        in_specs=[
            pl.BlockSpec((None, s, w), lambda b, j: (b, 0, xa_blk + j)),
            pl.BlockSpec((None, s, w), lambda b, j: (b, 0, ga_blk + j)),
            pl.BlockSpec((CONV_WIDTH, w), lambda b, j: (0, j)),
            vec(),
            pl.BlockSpec((None, w, w), lambda b, j: (j, 0, 0)),
            vec(),
            pl.BlockSpec((None, w, w), lambda b, j: (j, 0, 0)),
            vec(),
            vec(),
        ],
        out_specs=pl.BlockSpec((None, s, w), lambda b, j: (b, 0, j)),
        out_shape=jax.ShapeDtypeStruct((bsz, s, g * w), BF16),
        scratch_shapes=[pltpu.VMEM((s + 8, w), F32), pltpu.VMEM((s, w), F32), pltpu.VMEM((s, w), F32),
                        pltpu.VMEM((s // 8 + 8, w), F32)],
        compiler_params=_cparams(("parallel", "parallel")),
        name="rglru",
    )(pa3, pa3, conv_w, conv_b, wa, ba, wx, bx, lam)


def _prep_kernel(q_ref, qi_ref, ckv_ref, sm_ref, wukt_ref, cg_ref, kg_ref, kb_ref,
                 qat_ref, qit_ref, c_ref, ct_ref, kn_ref, wit_ref):
    qt = q_ref[...].T
    scale = HEAD_DIM ** -0.5 * LOG2E
    for h in range(ATT_HEADS):
        qa = jnp.dot(wukt_ref[h], qt[h * HEAD_DIM:(h + 1) * HEAD_DIM], preferred_element_type=F32)
        qat_ref[h] = (qa * scale).astype(BF16)
    qit = qi_ref[...].T
    for h in range(IDX_HEADS):
        qit_ref[h] = qit[h * IDX_DIM:(h + 1) * IDX_DIM, :]

    ckv = ckv_ref[...]
    c = ckv * lax.rsqrt(jnp.mean(ckv * ckv, axis=-1, keepdims=True) + EPS) * cg_ref[...]
    c_ref[...] = c.astype(BF16)
    ct_ref[0:KV_LATENT, :] = c.T.astype(BF16)
    ct_ref[KV_LATENT:, :] = jnp.ones((ONES_ROWS, ct_ref.shape[1]), BF16)

    sm = sm_ref[...]
    ki = sm[:, :IDX_DIM]
    mu = jnp.mean(ki, axis=-1, keepdims=True)
    var = jnp.mean(jnp.square(ki - mu), axis=-1, keepdims=True)
    kn = (ki - mu) * lax.rsqrt(var + EPS) * kg_ref[...] + kb_ref[...]
    kn_ref[...] = kn.astype(BF16)
    wit_ref[...] = sm.T[IDX_DIM:IDX_DIM + IDX_HEADS, :] * (IDX_HEADS ** -0.5 * IDX_DIM ** -0.5)


def _prep(pb3, pc3, cols, w_ukt, ckv_g, k_g, k_b):
    bsz, s, _ = pb3.shape
    att_w = ATT_HEADS * HEAD_DIM
    idx_w = IDX_HEADS * IDX_DIM
    q_blk = _col_block(cols["q"], att_w)
    qi_blk = _col_block(cols["qi"], idx_w)
    ckv_blk = _col_block(cols["ckv"], KV_LATENT)
    small_w = 128
    small_blk = _col_block(cols["small"], small_w)
    tm = Q_TILE
    const = lambda shape: pl.BlockSpec(shape, lambda b, i: (0,) * len(shape))
    return pl.pallas_call(
        _prep_kernel,
        grid=(bsz, s // tm),
        in_specs=[
            pl.BlockSpec((None, tm, att_w), lambda b, i: (b, i, q_blk)),
            pl.BlockSpec((None, tm, idx_w), lambda b, i: (b, i, qi_blk)),
            pl.BlockSpec((None, tm, KV_LATENT), lambda b, i: (b, i, ckv_blk)),
            pl.BlockSpec((None, tm, small_w), lambda b, i: (b, i, small_blk)),
            const(w_ukt.shape),
            const((1, KV_LATENT)),
            const((1, IDX_DIM)),
            const((1, IDX_DIM)),
        ],
        out_specs=[
            pl.BlockSpec((None, None, ATT_HEADS, KV_LATENT, Q_TILE), lambda b, i: (b, i, 0, 0, 0)),
            pl.BlockSpec((None, None, IDX_HEADS, IDX_DIM, Q_TILE), lambda b, i: (b, i, 0, 0, 0)),
            pl.BlockSpec((None, tm, KV_LATENT), lambda b, i: (b, i, 0)),
            pl.BlockSpec((None, KV_LATENT + ONES_ROWS, tm), lambda b, i: (b, 0, i)),
            pl.BlockSpec((None, tm, IDX_DIM), lambda b, i: (b, i, 0)),
            pl.BlockSpec((None, IDX_HEADS, tm), lambda b, i: (b, 0, i)),
        ],
        out_shape=[
            jax.ShapeDtypeStruct((bsz, s // Q_TILE, ATT_HEADS, KV_LATENT, Q_TILE), BF16),
            jax.ShapeDtypeStruct((bsz, s // Q_TILE, IDX_HEADS, IDX_DIM, Q_TILE), BF16),
            jax.ShapeDtypeStruct((bsz, s, KV_LATENT), BF16),
            jax.ShapeDtypeStruct((bsz, KV_LATENT + ONES_ROWS, s), BF16),
            jax.ShapeDtypeStruct((bsz, s, IDX_DIM), BF16),
            jax.ShapeDtypeStruct((bsz, IDX_HEADS, s), F32),
        ],
        compiler_params=_cparams(("parallel", "parallel")),
        name="prep",
    )(pb3, pb3, pc3, pc3, w_ukt, ckv_g, k_g, k_b)


def _tree_sum(parts):
    while len(parts) > 1:
        paired = [parts[i] + parts[i + 1] for i in range(0, len(parts) - 1, 2)]
        parts = paired + ([parts[-1]] if len(parts) % 2 else [])
    return parts[0]


def _sortable_to_f32(u):
    key = u ^ INT_MIN
    return lax.bitcast_convert_type(key ^ ((key >> 31) & 0x7FFFFFFF), F32)


def _kth_largest(score_ref, rows, k):
    chains = 4

    def step(i, u):
        cand = u | (jnp.int32(1) << (31 - i))
        cand_f = _sortable_to_f32(cand)
        accs = [None] * chains
        for r in range(rows // 8):
            hit = jnp.where(score_ref[r * 8:(r + 1) * 8, :] >= cand_f, 1.0, 0.0)
            accs[r % chains] = hit if accs[r % chains] is None else accs[r % chains] + hit
        cnt = jnp.sum(_tree_sum([a for a in accs if a is not None]), axis=0, keepdims=True)
        return jnp.where(cnt >= k, cand, u)

    return _sortable_to_f32(lax.fori_loop(0, 32, step, jnp.zeros((1, Q_TILE), I32)))


def _loop_by_two(n, body, init):
    def pair(i, carry):
        return body(2 * i + 1, body(2 * i, carry))

    carry = lax.fori_loop(0, n // 2, pair, init)
    return lax.fori_loop(2 * (n // 2), n, body, carry)


def _dsa_kernel(qit_ref, wit_ref, kn_ref, c_ref, ct_ref, qat_ref, gb_ref, bias_ref, wuvt_ref,
                o_ref, score_s, thr_s, lg_s, acc_s, topk):
    qb = pl.program_id(1)
    q_tiles = Q_TILE // K_CHUNK
    nkc = (qb + 1) * q_tiles
    nac = (nkc * K_CHUNK + ATT_CHUNK - 1) // ATT_CHUNK
    tiles = ATT_CHUNK // K_CHUNK

    kiota = lax.broadcasted_iota(I32, (K_CHUNK, Q_TILE), 0)
    qpos = qb * Q_TILE + lax.broadcasted_iota(I32, (K_CHUNK, Q_TILE), 1)

    def score_chunk(ac, carry):
        for t in range(tiles):
            k0 = pl.multiple_of(ac * ATT_CHUNK + t * K_CHUNK, K_CHUNK)
            kn = kn_ref[pl.ds(k0, K_CHUNK), :]
            acc = jnp.zeros((K_CHUNK, Q_TILE), F32)
            for h in range(IDX_HEADS):
                sc = jnp.dot(kn, qit_ref[h], preferred_element_type=F32)
                acc = acc + jnp.maximum(sc, 0.0) * wit_ref[h:h + 1, :]
            score_s[pl.ds(k0, K_CHUNK), :] = jnp.where(kiota + k0 <= qpos, acc, -jnp.inf)
        return carry

    _loop_by_two(nac, score_chunk, 0)

    for v in range(1, score_s.shape[0] // ATT_CHUNK + 1):
        @pl.when(nac == v)
        def _(rows=v * ATT_CHUNK):
            thr = _kth_largest(score_s, rows, float(topk))
            thr = jnp.where(thr >= NEG, thr, NEG)
            thr_s[...] = jnp.broadcast_to(thr, thr_s.shape)

    thr = thr_s[0:1, :]

    def logit_chunk(ac, m8s):
        r0 = pl.multiple_of(ac * ATT_CHUNK, ATT_CHUNK)
        c_chunk = c_ref[pl.ds(r0, ATT_CHUNK), :]
        mbias = jnp.where(score_s[pl.ds(r0, ATT_CHUNK), :] >= thr, 0.0, NEG)
        near = [[jnp.clip(ac * tiles + t - (qb * q_tiles + j) + 2, 0, 2) for j in range(q_tiles)]
                for t in range(tiles)]
        out = []
        for h in range(ATT_HEADS):
            lg = jnp.dot(c_chunk, qat_ref[h], preferred_element_type=F32) + mbias
            lg = jnp.concatenate(
                [lg[t * K_CHUNK:(t + 1) * K_CHUNK]
                 + jnp.concatenate([bias_ref[h, near[t][j]] for j in range(q_tiles)], axis=1)
                 for t in range(tiles)], axis=0)
            lg_s[h, pl.ds(r0, ATT_CHUNK), :] = lg
            out.append(jnp.maximum(m8s[h], jnp.max(lg.reshape(ATT_CHUNK // 8, 8, Q_TILE), axis=0)))
        return tuple(out)

    m8s = _loop_by_two(nac, logit_chunk, tuple(jnp.full((8, Q_TILE), NEG, F32) for _ in range(ATT_HEADS)))
    ms = [jnp.max(m8, axis=0, keepdims=True) for m8 in m8s]

    acc_s[...] = jnp.zeros(acc_s.shape, F32)

    def pv_chunk(ac, carry):
        r0 = pl.multiple_of(ac * ATT_CHUNK, ATT_CHUNK)
        ct_chunk = ct_ref[:, pl.ds(r0, ATT_CHUNK)]
        for h in range(ATT_HEADS):
            pr = jnp.exp2(lg_s[h, pl.ds(r0, ATT_CHUNK), :] - ms[h])
            acc_s[h] += jnp.dot(ct_chunk, pr.astype(BF16), preferred_element_type=F32)
        return carry

    _loop_by_two(nac, pv_chunk, 0)

    for h in range(ATT_HEADS):
        denom = acc_s[h, KV_LATENT:KV_LATENT + 1, :]
        o_t = acc_s[h, 0:KV_LATENT, :] * (1.0 / denom)
        y_t = jnp.dot(wuvt_ref[h], o_t.astype(BF16), preferred_element_type=F32)
        gb = gb_ref[:, h * HEAD_DIM:(h + 1) * HEAD_DIM]
        o_ref[:, h * HEAD_DIM:(h + 1) * HEAD_DIM] = (y_t.T * (gb * _sigmoid(gb))).astype(o_ref.dtype)


def _dsa(qit, wit, kn, c, ct, qat, pa3, gb_blk, bias_tiles, wuvt, topk):
    bsz, s, _ = c.shape
    att_w = ATT_HEADS * HEAD_DIM
    assert s % ATT_CHUNK == 0 and s % Q_TILE == 0 and Q_TILE % K_CHUNK == 0 and ATT_CHUNK % K_CHUNK == 0
    const = lambda shape: pl.BlockSpec(shape, lambda b, i: (0,) * len(shape))
    per_tile = lambda shape: pl.BlockSpec((None, None) + shape, lambda b, i: (b, i) + (0,) * len(shape))
    return pl.pallas_call(
        functools.partial(_dsa_kernel, topk=topk),
        grid=(bsz, s // Q_TILE),
        in_specs=[
            per_tile(qit.shape[2:]),
            pl.BlockSpec((None, IDX_HEADS, Q_TILE), lambda b, i: (b, 0, i)),
            pl.BlockSpec((None, s, IDX_DIM), lambda b, i: (b, 0, 0)),
            pl.BlockSpec((None, s, KV_LATENT), lambda b, i: (b, 0, 0)),
            pl.BlockSpec((None, KV_LATENT + ONES_ROWS, s), lambda b, i: (b, 0, 0)),
            per_tile(qat.shape[2:]),
            pl.BlockSpec((None, Q_TILE, att_w), lambda b, i: (b, i, gb_blk)),
            const(bias_tiles.shape),
            const(wuvt.shape),
        ],
        out_specs=pl.BlockSpec((None, Q_TILE, att_w), lambda b, i: (b, i, 0)),
        out_shape=jax.ShapeDtypeStruct((bsz, s, att_w), BF16),
        scratch_shapes=[
            pltpu.VMEM((s, Q_TILE), F32),
            pltpu.VMEM((8, Q_TILE), F32),
            pltpu.VMEM((ATT_HEADS, s, Q_TILE), F32),
            pltpu.VMEM((ATT_HEADS, KV_LATENT + ONES_ROWS, Q_TILE), F32),
        ],
        compiler_params=_cparams(("parallel", "arbitrary")),
        name="dsa",
    )(qit, wit, kn, c, ct, qat, pa3, bias_tiles, wuvt)


def _outp_kernel(ya_ref, yb_ref, wa_ref, wb_ref, x_ref, g_ref, o_ref, *, final_norm):
    acc = jnp.dot(ya_ref[...], wa_ref[...], preferred_element_type=F32)
    acc = acc + jnp.dot(yb_ref[...], wb_ref[...], preferred_element_type=F32)
    x = x_ref[...] + acc
    if final_norm:
        x = x * lax.rsqrt(jnp.mean(x * x, axis=-1, keepdims=True) + EPS) * g_ref[...]
    o_ref[...] = x


def _outp(ya, yb, w_out, x2, g, final_norm, tm=512):
    m, d = x2.shape
    ka, kb = ya.shape[1], yb.shape[1]
    assert ka == kb and w_out.shape[0] == ka + kb
    return pl.pallas_call(
        functools.partial(_outp_kernel, final_norm=final_norm),
        grid=(m // tm,),
        in_specs=[
            pl.BlockSpec((tm, ka), lambda i: (i, 0)),
            pl.BlockSpec((tm, kb), lambda i: (i, 0)),
            pl.BlockSpec((ka, d), lambda i: (0, 0)),
            pl.BlockSpec((kb, d), lambda i: (1, 0)),
            pl.BlockSpec((tm, d), lambda i: (i, 0)),
            pl.BlockSpec((1, d), lambda i: (0, 0)),
        ],
        out_specs=pl.BlockSpec((tm, d), lambda i: (i, 0)),
        out_shape=jax.ShapeDtypeStruct((m, d), F32),
        compiler_params=_cparams(("parallel",)),
        name="outp",
    )(ya, yb, w_out, w_out, x2, g)


def _t5_bucket(dist):
    n = jnp.maximum(dist, 0)
    max_exact = REL_BUCKETS // 2
    nf = jnp.maximum(n, 1).astype(F32)
    large = max_exact + (jnp.log(nf / max_exact) / np.log(REL_MAX_DIST / max_exact)
                         * (REL_BUCKETS - max_exact)).astype(I32)
    large = jnp.minimum(large, REL_BUCKETS - 1)
    return jnp.where(n < max_exact, n, large)


def _bias_tiles(rel_bias):
    qw = K_CHUNK
    span = K_CHUNK + qw
    table = rel_bias[_t5_bucket(jnp.arange(span + 1, dtype=I32))].astype(F32)
    table = ((table[:span] - table[span:]) * LOG2E).T
    n = span + qw - 1
    a = jnp.concatenate([jnp.zeros((ATT_HEADS, qw - 1), F32), table], axis=1)
    shifted = jnp.tile(a, (1, span + 1))[:, :span * (n + 1)].reshape(ATT_HEADS, span, n + 1)
    tiles = shifted[:, ::-1, :qw].reshape(ATT_HEADS, 2, K_CHUNK, qw)
    return jnp.concatenate([jnp.zeros_like(tiles[:, :1]), tiles], axis=1)


def kernel(x, norm_g, w_in, conv_w, conv_b, lru_wa, lru_ba, lru_wx, lru_bx, lru_lambda, ckv_norm_g, idx_k_norm_g, idx_k_norm_b, w_uk, w_uv, w_out, rel_bias, final_norm_g):
    bsz, s, d = x.shape
    depth = w_in.shape[0]
    lru_w = lru_wa.shape[1] * lru_wa.shape[2]
    att_w = ATT_HEADS * HEAD_DIM
    idx_w = IDX_HEADS * IDX_DIM
    assert REL_MAX_DIST <= K_CHUNK
    assert lru_w == att_w == idx_w and att_w % KV_LATENT == 0
    topk = min(INDEX_TOPK, s // 4)

    o_q = 2 * lru_w
    o_ckv = o_q + att_w
    o_gb = o_ckv + KV_LATENT
    o_qi = o_gb + att_w
    o_ki = o_qi + idx_w
    tn = 512
    cols_a = {"xa": 0, "ga": lru_w, "gb": 2 * lru_w}
    cols_b = {"q": 0, "qi": att_w}
    cols_c = {"ckv": 0, "small": KV_LATENT}
    n_f32, n_bf16 = 3 * lru_w, att_w + idx_w

    bias_tiles = _bias_tiles(rel_bias)
    x2 = x.reshape(bsz * s, d)
    for l in range(depth):
        w_all_t = _wcast(w_in[l].T, [(0, o_q), (o_gb, o_qi), (o_q, o_ckv), (o_qi, o_ki), (o_ckv, o_gb),
                                     (o_ki, w_in.shape[2])])
        pa, pb, pc = _proj(x2, norm_g[l][None, :], w_all_t, n_f32, n_bf16, tn=tn)
        pa3 = pa.reshape(bsz, s, -1)
        pb3 = pb.reshape(bsz, s, -1)
        pc3 = pc.reshape(bsz, s, -1)

        ya = _rglru(pa3, cols_a, conv_w[l], conv_b[l][None, :], lru_wa[l].astype(BF16), lru_ba[l][None, :],
                    lru_wx[l].astype(BF16), lru_bx[l][None, :], lru_lambda[l][None, :])

        wukt = jnp.transpose(w_uk[l], (0, 2, 1)).astype(BF16)
        qat, qit, c, ct, kn, wit = _prep(pb3, pc3, {**cols_b, **cols_c}, wukt, ckv_norm_g[l][None, :],
                                          idx_k_norm_g[l][None, :], idx_k_norm_b[l][None, :])
        wuvt = jnp.transpose(w_uv[l], (0, 2, 1)).astype(BF16)
        yb = _dsa(qit, wit, kn, c, ct, qat, pa3, _col_block(cols_a["gb"], att_w), bias_tiles, wuvt, topk)

        x2 = _outp(ya.reshape(bsz * s, lru_w), yb.reshape(bsz * s, att_w), w_out[l].astype(BF16), x2,
                   final_norm_g[None, :], final_norm=(l == depth - 1))
    return x2.reshape(bsz, s, d)
```

```python
import functools

import numpy as np
import jax
import jax.numpy as jnp
from jax import lax
from jax.experimental import pallas as pl
from jax.experimental.pallas import tpu as pltpu

F32 = jnp.float32
BF16 = jnp.bfloat16
I32 = jnp.int32

LRU_BLOCKS = 8
CONV_WIDTH = 4
LRU_C = 8.0
ATT_HEADS = 8
HEAD_DIM = 128
KV_LATENT = 256
IDX_HEADS = 16
IDX_DIM = 64
INDEX_TOPK = 256
REL_BUCKETS = 32
REL_MAX_DIST = 128
EPS = 1e-6
LOG2E = float(np.log2(np.e))
ONES_ROWS = 16

Q_TILE = 256
K_CHUNK = 128
ATT_CHUNK = 256
NEG = float(np.finfo(np.float32).min)
INT_MIN = -(2 ** 31)
VMEM_LIMIT = 56 * 1024 * 1024


def _cparams(sem):
    return pltpu.CompilerParams(dimension_semantics=sem, vmem_limit_bytes=VMEM_LIMIT)


def _col_block(offset, width):
    assert offset % width == 0
    return offset // width


def _wcast(wt, order, tn=256):
    n, d = wt.shape
    n_whole = n // tn
    src_blocks = []
    for start, stop in order:
        assert start % tn == 0 and (stop % tn == 0 or stop == n)
        src_blocks += list(range(start // tn, -(-stop // tn)))
    tail = jnp.pad(wt[n_whole * tn:], ((0, (n_whole + 1) * tn - n), (0, 0)))
    table = jnp.asarray([blk if blk < n_whole else -1 for blk in src_blocks], I32)
    grid_spec = pltpu.PrefetchScalarGridSpec(
        num_scalar_prefetch=1,
        grid=(len(src_blocks),),
        in_specs=[pl.BlockSpec((tn, d), lambda j, t: (jnp.maximum(t[j], 0), 0)),
                  pl.BlockSpec((tn, d), lambda j, t: (0, 0))],
        out_specs=pl.BlockSpec((tn, d), lambda j, t: (j, 0)),
    )

    def body(t_ref, w_ref, tail_ref, o_ref):
        is_tail = t_ref[pl.program_id(0)] < 0

        @pl.when(is_tail)
        def _():
            o_ref[...] = tail_ref[...].astype(o_ref.dtype)

        @pl.when(jnp.logical_not(is_tail))
        def _():
            o_ref[...] = w_ref[...].astype(o_ref.dtype)

    return pl.pallas_call(
        body,
        grid_spec=grid_spec,
        out_shape=jax.ShapeDtypeStruct((len(src_blocks) * tn, d), BF16),
        compiler_params=_cparams(("arbitrary",)),
        name="wcast",
    )(table, wt, tail)


def _proj_kernel(x_ref, g_ref, w_ref, oa_ref, ob_ref, oc_ref, h_ref, *, na, nb):
    j = pl.program_id(1)
    nt = (((1,), (1,)), ((), ()))

    @pl.when(j == 0)
    def _():
        x = x_ref[...]
        y = x * lax.rsqrt(jnp.mean(x * x, axis=-1, keepdims=True) + EPS)
        h_ref[...] = (y * g_ref[...]).astype(BF16)

    @pl.when(j < na)
    def _():
        oa_ref[...] = lax.dot_general(h_ref[...], w_ref[...], nt, preferred_element_type=F32)

    @pl.when((j >= na) & (j < na + nb))
    def _():
        ob_ref[...] = lax.dot_general(h_ref[...], w_ref[...], nt, preferred_element_type=F32).astype(BF16)

    @pl.when(j >= na + nb)
    def _():
        oc_ref[...] = lax.dot_general(h_ref[...], w_ref[...], nt, preferred_element_type=F32)


def _proj(x2, g, w_all_t, n_f32, n_bf16, tm=1024, tn=512):
    m, d = x2.shape
    na, nb = n_f32 // tn, n_bf16 // tn
    assert n_f32 % tn == 0 and n_bf16 % tn == 0 and w_all_t.shape[0] == n_f32 + n_bf16 + tn
    return pl.pallas_call(
        functools.partial(_proj_kernel, na=na, nb=nb),
        grid=(m // tm, na + nb + 1),
        in_specs=[
            pl.BlockSpec((tm, d), lambda i, j: (i, 0)),
            pl.BlockSpec((1, d), lambda i, j: (0, 0)),
            pl.BlockSpec((tn, d), lambda i, j: (j, 0)),
        ],
        out_specs=[
            pl.BlockSpec((tm, tn), lambda i, j: (i, jnp.minimum(j, na - 1))),
            pl.BlockSpec((tm, tn), lambda i, j: (i, jnp.clip(j - na, 0, nb - 1))),
            pl.BlockSpec((tm, tn), lambda i, j: (i, 0)),
        ],
        out_shape=[
            jax.ShapeDtypeStruct((m, n_f32), F32),
            jax.ShapeDtypeStruct((m, n_bf16), BF16),
            jax.ShapeDtypeStruct((m, tn), F32),
        ],
        scratch_shapes=[pltpu.VMEM((tm, d), BF16)],
        compiler_params=_cparams(("parallel", "arbitrary")),
        name="proj",
    )(x2, g, w_all_t)


def _sigmoid(v):
    return 0.5 * jnp.tanh(0.5 * v) + 0.5


def _scan_step(a, b, k, axis, idx):
    keep = idx >= k
    a_prev = jnp.where(keep, pltpu.roll(a, k, axis=axis), 1.0)
    b_prev = jnp.where(keep, pltpu.roll(b, k, axis=axis), 0.0)
    return a * a_prev, a * b_prev + b


def _rglru_kernel(xa_ref, ga_ref, cw_ref, cb_ref, wa_ref, ba_ref, wx_ref, bx_ref, lam_ref,
                  o_ref, pad_s, a_s, b_s, c_s):
    w = wa_ref.shape[-1]
    for gi in range(wa_ref.shape[0]):
        lanes = pl.ds(gi * w, w)
        _rglru_block(xa_ref.at[:, lanes], ga_ref.at[:, lanes], cw_ref.at[:, lanes], cb_ref.at[:, lanes],
                     wa_ref.at[gi], ba_ref.at[:, lanes], wx_ref.at[gi], bx_ref.at[:, lanes],
                     lam_ref.at[:, lanes], o_ref.at[:, lanes], pad_s, a_s, b_s, c_s)


def _rglru_block(xa_ref, ga_ref, cw_ref, cb_ref, wa_ref, ba_ref, wx_ref, bx_ref, lam_ref,
                 o_ref, pad_s, a_s, b_s, c_s):
    s, w = xa_ref.shape
    tile = 8
    n_tiles = s // tile

    pad_s[0:tile, :] = jnp.zeros((tile, w), F32)
    pad_s[tile:tile + s, :] = xa_ref[...]
    acc = pad_s[tile:tile + s, :] * cw_ref[CONV_WIDTH - 1:CONV_WIDTH, :]
    for j in range(CONV_WIDTH - 1):
        back = CONV_WIDTH - 1 - j
        acc = acc + pad_s[tile - back:tile - back + s, :] * cw_ref[j:j + 1, :]
    xc = cb_ref[...] + acc

    xcb = xc.astype(BF16)
    r = _sigmoid(jnp.dot(xcb, wa_ref[...], preferred_element_type=F32) + ba_ref[...])
    i = _sigmoid(jnp.dot(xcb, wx_ref[...], preferred_element_type=F32) + bx_ref[...])
    z = -lam_ref[...]
    softplus = jnp.maximum(z, 0.0) + jnp.log1p(jnp.exp(-jnp.abs(z)))
    log_a = (-LRU_C) * r * softplus
    a = jnp.exp(log_a)
    m2 = (1.0 + a * a) * jnp.tanh(-log_a)
    mult = jnp.where(m2 > 0.0, m2 * lax.rsqrt(m2), 0.0)
    gated = i * xc
    b_s[...] = mult * gated
    b_s[0:1, :] = gated[0:1, :]

    a3 = a.reshape(n_tiles, tile, w)
    b3 = b_s[...].reshape(n_tiles, tile, w)
    sub = lax.broadcasted_iota(I32, (n_tiles, tile, w), 1)
    for k in (1, 2, 4):
        a3, b3 = _scan_step(a3, b3, k, 1, sub)
    a_s[...] = a3.reshape(s, w)
    b_s[...] = b3.reshape(s, w)

    at = a_s[pl.ds(tile - 1, n_tiles, stride=tile), :]
    bt = b_s[pl.ds(tile - 1, n_tiles, stride=tile), :]
    trow = lax.broadcasted_iota(I32, (n_tiles, w), 0)
    k = 1
    while k < n_tiles:
        at, bt = _scan_step(at, bt, k, 0, trow)
        k *= 2
    c_s[0:tile, :] = jnp.zeros((tile, w), F32)
    c_s[tile:tile + n_tiles, :] = bt

    def apply(t, carry):
        r0 = pl.multiple_of(t * tile, tile)
        before = c_s[pl.ds(tile - 1 + t, tile, stride=0), :]
        h = a_s[pl.ds(r0, tile), :] * before + b_s[pl.ds(r0, tile), :]
        ga = ga_ref[pl.ds(r0, tile), :]
        o_ref[pl.ds(r0, tile), :] = (h * (ga * _sigmoid(ga))).astype(o_ref.dtype)
        return carry

    lax.fori_loop(0, n_tiles, apply, 0, unroll=8)


def _rglru(pa3, cols, conv_w, conv_b, wa, ba, wx, bx, lam, blocks_per_step=4):
    bsz, s, _ = pa3.shape
    g, w = wa.shape[0], wa.shape[-1]
    assert g % blocks_per_step == 0
    bw = blocks_per_step * w
    xa_blk = _col_block(cols["xa"], bw)
    ga_blk = _col_block(cols["ga"], bw)
    vec = lambda: pl.BlockSpec((1, bw), lambda b, j: (0, j))
    gates = lambda: pl.BlockSpec((blocks_per_step, w, w), lambda b, j: (j, 0, 0))
    return pl.pallas_call(
        _rglru_kernel,
        grid=(bsz, g // blocks_per_step),
        in_specs=[
            pl.BlockSpec((None, s, bw), lambda b, j: (b, 0, xa_blk + j)),
            pl.BlockSpec((None, s, bw), lambda b, j: (b, 0, ga_blk + j)),
            pl.BlockSpec((CONV_WIDTH, bw), lambda b, j: (0, j)),
            vec(),
            gates(),
            vec(),
            gates(),
            vec(),
            vec(),
        ],
        out_specs=pl.BlockSpec((None, s, bw), lambda b, j: (b, 0, j)),
        out_shape=jax.ShapeDtypeStruct((bsz, s, g * w), BF16),
        scratch_shapes=[pltpu.VMEM((s + 8, w), F32), pltpu.VMEM((s, w), F32), pltpu.VMEM((s, w), F32),
                        pltpu.VMEM((s // 8 + 8, w), F32)],
        compiler_params=_cparams(("parallel", "parallel")),
        name="rglru",
    )(pa3, pa3, conv_w, conv_b, wa, ba, wx, bx, lam)


def _prep_kernel(q_ref, qi_ref, ckv_ref, sm_ref, wukt_ref, cg_ref, kg_ref, kb_ref,
                 qat_ref, qit_ref, c_ref, ct_ref, kn_ref, wit_ref):
    qt = q_ref[...].T
    scale = HEAD_DIM ** -0.5 * LOG2E
    for h in range(ATT_HEADS):
        qa = jnp.dot(wukt_ref[h], qt[h * HEAD_DIM:(h + 1) * HEAD_DIM], preferred_element_type=F32)
        qat_ref[h] = (qa * scale).astype(BF16)
    qit = qi_ref[...].T
    for h in range(IDX_HEADS):
        qit_ref[h] = qit[h * IDX_DIM:(h + 1) * IDX_DIM, :]

    ckv = ckv_ref[...]
    c = ckv * lax.rsqrt(jnp.mean(ckv * ckv, axis=-1, keepdims=True) + EPS) * cg_ref[...]
    c_ref[...] = c.astype(BF16)
    ct_ref[0:KV_LATENT, :] = c.T.astype(BF16)
    ct_ref[KV_LATENT:, :] = jnp.ones((ONES_ROWS, ct_ref.shape[1]), BF16)

    sm = sm_ref[...]
    ki = sm[:, :IDX_DIM]
    mu = jnp.mean(ki, axis=-1, keepdims=True)
    var = jnp.mean(jnp.square(ki - mu), axis=-1, keepdims=True)
    kn = (ki - mu) * lax.rsqrt(var + EPS) * kg_ref[...] + kb_ref[...]
    kn_ref[...] = kn.astype(BF16)
    wit_ref[...] = sm.T[IDX_DIM:IDX_DIM + IDX_HEADS, :] * (IDX_HEADS ** -0.5 * IDX_DIM ** -0.5)


def _prep(pb3, pc3, cols, w_ukt, ckv_g, k_g, k_b):
    bsz, s, _ = pb3.shape
    att_w = ATT_HEADS * HEAD_DIM
    idx_w = IDX_HEADS * IDX_DIM
    q_blk = _col_block(cols["q"], att_w)
    qi_blk = _col_block(cols["qi"], idx_w)
    ckv_blk = _col_block(cols["ckv"], KV_LATENT)
    small_w = 128
    small_blk = _col_block(cols["small"], small_w)
    tm = Q_TILE
    const = lambda shape: pl.BlockSpec(shape, lambda b, i: (0,) * len(shape))
    return pl.pallas_call(
        _prep_kernel,
        grid=(bsz, s // tm),
        in_specs=[
            pl.BlockSpec((None, tm, att_w), lambda b, i: (b, i, q_blk)),
            pl.BlockSpec((None, tm, idx_w), lambda b, i: (b, i, qi_blk)),
            pl.BlockSpec((None, tm, KV_LATENT), lambda b, i: (b, i, ckv_blk)),
            pl.BlockSpec((None, tm, small_w), lambda b, i: (b, i, small_blk)),
            const(w_ukt.shape),
            const((1, KV_LATENT)),
            const((1, IDX_DIM)),
            const((1, IDX_DIM)),
        ],
        out_specs=[
            pl.BlockSpec((None, None, ATT_HEADS, KV_LATENT, Q_TILE), lambda b, i: (b, i, 0, 0, 0)),
            pl.BlockSpec((None, None, IDX_HEADS, IDX_DIM, Q_TILE), lambda b, i: (b, i, 0, 0, 0)),
            pl.BlockSpec((None, tm, KV_LATENT), lambda b, i: (b, i, 0)),
            pl.BlockSpec((None, KV_LATENT + ONES_ROWS, tm), lambda b, i: (b, 0, i)),
            pl.BlockSpec((None, tm, IDX_DIM), lambda b, i: (b, i, 0)),
            pl.BlockSpec((None, IDX_HEADS, tm), lambda b, i: (b, 0, i)),
        ],
        out_shape=[
            jax.ShapeDtypeStruct((bsz, s // Q_TILE, ATT_HEADS, KV_LATENT, Q_TILE), BF16),
            jax.ShapeDtypeStruct((bsz, s // Q_TILE, IDX_HEADS, IDX_DIM, Q_TILE), BF16),
            jax.ShapeDtypeStruct((bsz, s, KV_LATENT), BF16),
            jax.ShapeDtypeStruct((bsz, KV_LATENT + ONES_ROWS, s), BF16),
            jax.ShapeDtypeStruct((bsz, s, IDX_DIM), BF16),
            jax.ShapeDtypeStruct((bsz, IDX_HEADS, s), F32),
        ],
        compiler_params=_cparams(("parallel", "parallel")),
        name="prep",
    )(pb3, pb3, pc3, pc3, w_ukt, ckv_g, k_g, k_b)


def _tree_sum(parts):
    while len(parts) > 1:
        paired = [parts[i] + parts[i + 1] for i in range(0, len(parts) - 1, 2)]
        parts = paired + ([parts[-1]] if len(parts) % 2 else [])
    return parts[0]


def _sortable_to_f32(u):
    key = u ^ INT_MIN
    return lax.bitcast_convert_type(key ^ ((key >> 31) & 0x7FFFFFFF), F32)


def _kth_largest(score_ref, rows, k):
    chains = 4

    def step(i, u):
        cand = u | (jnp.int32(1) << (31 - i))
        cand_f = _sortable_to_f32(cand)
        accs = [None] * chains
        for r in range(rows // 8):
            hit = jnp.where(score_ref[r * 8:(r + 1) * 8, :] >= cand_f, 1.0, 0.0)
            accs[r % chains] = hit if accs[r % chains] is None else accs[r % chains] + hit
        cnt = jnp.sum(_tree_sum([a for a in accs if a is not None]), axis=0, keepdims=True)
        return jnp.where(cnt >= k, cand, u)

    return _sortable_to_f32(lax.fori_loop(0, 32, step, jnp.zeros((1, Q_TILE), I32)))


def _loop_by_two(n, body, init):
    def pair(i, carry):
        return body(2 * i + 1, body(2 * i, carry))

    carry = lax.fori_loop(0, n // 2, pair, init)
    return lax.fori_loop(2 * (n // 2), n, body, carry)


def _dsa_kernel(qit_ref, wit_ref, kn_ref, c_ref, ct_ref, qat_ref, gb_ref, bias_ref, wuvt_ref,
                o_ref, score_s, thr_s, lg_s, acc_s, topk):
    qb = pl.program_id(1)
    q_tiles = Q_TILE // K_CHUNK
    nkc = (qb + 1) * q_tiles
    nac = (nkc * K_CHUNK + ATT_CHUNK - 1) // ATT_CHUNK
    tiles = ATT_CHUNK // K_CHUNK

    kiota = lax.broadcasted_iota(I32, (K_CHUNK, Q_TILE), 0)
    qpos = qb * Q_TILE + lax.broadcasted_iota(I32, (K_CHUNK, Q_TILE), 1)

    def score_chunk(ac, carry):
        for t in range(tiles):
            k0 = pl.multiple_of(ac * ATT_CHUNK + t * K_CHUNK, K_CHUNK)
            kn = kn_ref[pl.ds(k0, K_CHUNK), :]
            acc = jnp.zeros((K_CHUNK, Q_TILE), F32)
            for h in range(IDX_HEADS):
                sc = jnp.dot(kn, qit_ref[h], preferred_element_type=F32)
                acc = acc + jnp.maximum(sc, 0.0) * wit_ref[h:h + 1, :]
            score_s[pl.ds(k0, K_CHUNK), :] = jnp.where(kiota + k0 <= qpos, acc, -jnp.inf)
        return carry

    _loop_by_two(nac, score_chunk, 0)

    for v in range(1, score_s.shape[0] // ATT_CHUNK + 1):
        @pl.when(nac == v)
        def _(rows=v * ATT_CHUNK):
            thr = _kth_largest(score_s, rows, float(topk))
            thr = jnp.where(thr >= NEG, thr, NEG)
            thr_s[...] = jnp.broadcast_to(thr, thr_s.shape)

    thr = thr_s[0:1, :]

    def logit_chunk(ac, m8s):
        r0 = pl.multiple_of(ac * ATT_CHUNK, ATT_CHUNK)
        c_chunk = c_ref[pl.ds(r0, ATT_CHUNK), :]
        mbias = jnp.where(score_s[pl.ds(r0, ATT_CHUNK), :] >= thr, 0.0, NEG)
        near = [[jnp.clip(ac * tiles + t - (qb * q_tiles + j) + 2, 0, 2) for j in range(q_tiles)]
                for t in range(tiles)]
        out = []
        for h in range(ATT_HEADS):
            lg = jnp.dot(c_chunk, qat_ref[h], preferred_element_type=F32) + mbias
            lg = jnp.concatenate(
                [lg[t * K_CHUNK:(t + 1) * K_CHUNK]
                 + jnp.concatenate([bias_ref[h, near[t][j]] for j in range(q_tiles)], axis=1)
                 for t in range(tiles)], axis=0)
            lg_s[h, pl.ds(r0, ATT_CHUNK), :] = lg
            out.append(jnp.maximum(m8s[h], jnp.max(lg.reshape(ATT_CHUNK // 8, 8, Q_TILE), axis=0)))
        return tuple(out)

    m8s = _loop_by_two(nac, logit_chunk, tuple(jnp.full((8, Q_TILE), NEG, F32) for _ in range(ATT_HEADS)))
    ms = [jnp.max(m8, axis=0, keepdims=True) for m8 in m8s]

    acc_s[...] = jnp.zeros(acc_s.shape, F32)

    def pv_chunk(ac, carry):
        r0 = pl.multiple_of(ac * ATT_CHUNK, ATT_CHUNK)
        ct_chunk = ct_ref[:, pl.ds(r0, ATT_CHUNK)]
        for h in range(ATT_HEADS):
            pr = jnp.exp2(lg_s[h, pl.ds(r0, ATT_CHUNK), :] - ms[h])
            acc_s[h] += jnp.dot(ct_chunk, pr.astype(BF16), preferred_element_type=F32)
        return carry

    _loop_by_two(nac, pv_chunk, 0)

    for h in range(ATT_HEADS):
        denom = acc_s[h, KV_LATENT:KV_LATENT + 1, :]
        o_t = acc_s[h, 0:KV_LATENT, :] * (1.0 / denom)
        y_t = jnp.dot(wuvt_ref[h], o_t.astype(BF16), preferred_element_type=F32)
        gb = gb_ref[:, h * HEAD_DIM:(h + 1) * HEAD_DIM]
        o_ref[:, h * HEAD_DIM:(h + 1) * HEAD_DIM] = (y_t.T * (gb * _sigmoid(gb))).astype(o_ref.dtype)


def _dsa(qit, wit, kn, c, ct, qat, pa3, gb_blk, bias_tiles, wuvt, topk):
    bsz, s, _ = c.shape
    att_w = ATT_HEADS * HEAD_DIM
    assert s % ATT_CHUNK == 0 and s % Q_TILE == 0 and Q_TILE % K_CHUNK == 0 and ATT_CHUNK % K_CHUNK == 0
    const = lambda shape: pl.BlockSpec(shape, lambda b, i: (0,) * len(shape))
    per_tile = lambda shape: pl.BlockSpec((None, None) + shape, lambda b, i: (b, i) + (0,) * len(shape))
    return pl.pallas_call(
        functools.partial(_dsa_kernel, topk=topk),
        grid=(bsz, s // Q_TILE),
        in_specs=[
            per_tile(qit.shape[2:]),
            pl.BlockSpec((None, IDX_HEADS, Q_TILE), lambda b, i: (b, 0, i)),
            pl.BlockSpec((None, s, IDX_DIM), lambda b, i: (b, 0, 0)),
            pl.BlockSpec((None, s, KV_LATENT), lambda b, i: (b, 0, 0)),
            pl.BlockSpec((None, KV_LATENT + ONES_ROWS, s), lambda b, i: (b, 0, 0)),
            per_tile(qat.shape[2:]),
            pl.BlockSpec((None, Q_TILE, att_w), lambda b, i: (b, i, gb_blk)),
            const(bias_tiles.shape),
            const(wuvt.shape),
        ],
        out_specs=pl.BlockSpec((None, Q_TILE, att_w), lambda b, i: (b, i, 0)),
        out_shape=jax.ShapeDtypeStruct((bsz, s, att_w), BF16),
        scratch_shapes=[
            pltpu.VMEM((s, Q_TILE), F32),
            pltpu.VMEM((8, Q_TILE), F32),
            pltpu.VMEM((ATT_HEADS, s, Q_TILE), F32),
            pltpu.VMEM((ATT_HEADS, KV_LATENT + ONES_ROWS, Q_TILE), F32),
        ],
        compiler_params=_cparams(("parallel", "arbitrary")),
        name="dsa",
    )(qit, wit, kn, c, ct, qat, pa3, bias_tiles, wuvt)


def _outp_kernel(ya_ref, yb_ref, wa_ref, wb_ref, x_ref, g_ref, o_ref, *, final_norm):
    acc = jnp.dot(ya_ref[...], wa_ref[...], preferred_element_type=F32)
    acc = acc + jnp.dot(yb_ref[...], wb_ref[...], preferred_element_type=F32)
    x = x_ref[...] + acc
    if final_norm:
        x = x * lax.rsqrt(jnp.mean(x * x, axis=-1, keepdims=True) + EPS) * g_ref[...]
    o_ref[...] = x


def _outp(ya, yb, w_out, x2, g, final_norm, tm=512):
    m, d = x2.shape
    ka, kb = ya.shape[1], yb.shape[1]
    assert ka == kb and w_out.shape[0] == ka + kb
    return pl.pallas_call(
        functools.partial(_outp_kernel, final_norm=final_norm),
        grid=(m // tm,),
        in_specs=[
            pl.BlockSpec((tm, ka), lambda i: (i, 0)),
            pl.BlockSpec((tm, kb), lambda i: (i, 0)),
            pl.BlockSpec((ka, d), lambda i: (0, 0)),
            pl.BlockSpec((kb, d), lambda i: (1, 0)),
            pl.BlockSpec((tm, d), lambda i: (i, 0)),
            pl.BlockSpec((1, d), lambda i: (0, 0)),
        ],
        out_specs=pl.BlockSpec((tm, d), lambda i: (i, 0)),
        out_shape=jax.ShapeDtypeStruct((m, d), F32),
        compiler_params=_cparams(("parallel",)),
        name="outp",
    )(ya, yb, w_out, w_out, x2, g)


def _t5_bucket(dist):
    n = jnp.maximum(dist, 0)
    max_exact = REL_BUCKETS // 2
    nf = jnp.maximum(n, 1).astype(F32)
    large = max_exact + (jnp.log(nf / max_exact) / np.log(REL_MAX_DIST / max_exact)
                         * (REL_BUCKETS - max_exact)).astype(I32)
    large = jnp.minimum(large, REL_BUCKETS - 1)
    return jnp.where(n < max_exact, n, large)


def _bias_tiles(rel_bias):
    qw = K_CHUNK
    span = K_CHUNK + qw
    table = rel_bias[_t5_bucket(jnp.arange(span + 1, dtype=I32))].astype(F32)
    table = ((table[:span] - table[span:]) * LOG2E).T
    n = span + qw - 1
    a = jnp.concatenate([jnp.zeros((ATT_HEADS, qw - 1), F32), table], axis=1)
    shifted = jnp.tile(a, (1, span + 1))[:, :span * (n + 1)].reshape(ATT_HEADS, span, n + 1)
    tiles = shifted[:, ::-1, :qw].reshape(ATT_HEADS, 2, K_CHUNK, qw)
    return jnp.concatenate([jnp.zeros_like(tiles[:, :1]), tiles], axis=1)


def kernel(x, norm_g, w_in, conv_w, conv_b, lru_wa, lru_ba, lru_wx, lru_bx, lru_lambda, ckv_norm_g, idx_k_norm_g, idx_k_norm_b, w_uk, w_uv, w_out, rel_bias, final_norm_g):
    bsz, s, d = x.shape
    depth = w_in.shape[0]
    lru_w = lru_wa.shape[1] * lru_wa.shape[2]
    att_w = ATT_HEADS * HEAD_DIM
    idx_w = IDX_HEADS * IDX_DIM
    assert REL_MAX_DIST <= K_CHUNK
    assert lru_w == att_w == idx_w and att_w % KV_LATENT == 0
    topk = min(INDEX_TOPK, s // 4)

    o_q = 2 * lru_w
    o_ckv = o_q + att_w
    o_gb = o_ckv + KV_LATENT
    o_qi = o_gb + att_w
    o_ki = o_qi + idx_w
    tn = 512
    cols_a = {"xa": 0, "ga": lru_w, "gb": 2 * lru_w}
    cols_b = {"q": 0, "qi": att_w}
    cols_c = {"ckv": 0, "small": KV_LATENT}
    n_f32, n_bf16 = 3 * lru_w, att_w + idx_w

    bias_tiles = _bias_tiles(rel_bias)
    x2 = x.reshape(bsz * s, d)
    for l in range(depth):
        w_all_t = _wcast(w_in[l].T, [(0, o_q), (o_gb, o_qi), (o_q, o_ckv), (o_qi, o_ki), (o_ckv, o_gb),
                                     (o_ki, w_in.shape[2])])
        pa, pb, pc = _proj(x2, norm_g[l][None, :], w_all_t, n_f32, n_bf16, tn=tn)
        pa3 = pa.reshape(bsz, s, -1)
        pb3 = pb.reshape(bsz, s, -1)
        pc3 = pc.reshape(bsz, s, -1)

        ya = _rglru(pa3, cols_a, conv_w[l], conv_b[l][None, :], lru_wa[l].astype(BF16), lru_ba[l][None, :],
                    lru_wx[l].astype(BF16), lru_bx[l][None, :], lru_lambda[l][None, :])

        wukt = jnp.transpose(w_uk[l], (0, 2, 1)).astype(BF16)
        qat, qit, c, ct, kn, wit = _prep(pb3, pc3, {**cols_b, **cols_c}, wukt, ckv_norm_g[l][None, :],
                                          idx_k_norm_g[l][None, :], idx_k_norm_b[l][None, :])
        wuvt = jnp.transpose(w_uv[l], (0, 2, 1)).astype(BF16)
        yb = _dsa(qit, wit, kn, c, ct, qat, pa3, _col_block(cols_a["gb"], att_w), bias_tiles, wuvt, topk)

        x2 = _outp(ya.reshape(bsz * s, lru_w), yb.reshape(bsz * s, att_w), w_out[l].astype(BF16), x2,
                   final_norm_g[None, :], final_norm=(l == depth - 1))
    return x2.reshape(bsz, s, d)
```

```python
import functools

import numpy as np
import jax
import jax.numpy as jnp
from jax import lax
from jax.experimental import pallas as pl
from jax.experimental.pallas import tpu as pltpu

F32 = jnp.float32
BF16 = jnp.bfloat16
I32 = jnp.int32

LRU_BLOCKS = 8
CONV_WIDTH = 4
LRU_C = 8.0
ATT_HEADS = 8
HEAD_DIM = 128
KV_LATENT = 256
IDX_HEADS = 16
IDX_DIM = 64
INDEX_TOPK = 256
REL_BUCKETS = 32
REL_MAX_DIST = 128
EPS = 1e-6
LOG2E = float(np.log2(np.e))
ONES_ROWS = 16

Q_TILE = 256
K_CHUNK = 128
ATT_CHUNK = 256
NEG = float(np.finfo(np.float32).min)
INT_MIN = -(2 ** 31)
VMEM_LIMIT = 56 * 1024 * 1024


def _cparams(sem):
    return pltpu.CompilerParams(dimension_semantics=sem, vmem_limit_bytes=VMEM_LIMIT)


def _col_block(offset, width):
    assert offset % width == 0
    return offset // width


def _wcast(wt, order, tn=256):
    n, d = wt.shape
    n_whole = n // tn
    src_blocks = []
    for start, stop in order:
        assert start % tn == 0 and (stop % tn == 0 or stop == n)
        src_blocks += list(range(start // tn, -(-stop // tn)))
    tail = jnp.pad(wt[n_whole * tn:], ((0, (n_whole + 1) * tn - n), (0, 0)))
    table = jnp.asarray([blk if blk < n_whole else -1 for blk in src_blocks], I32)
    grid_spec = pltpu.PrefetchScalarGridSpec(
        num_scalar_prefetch=1,
        grid=(len(src_blocks),),
        in_specs=[pl.BlockSpec((tn, d), lambda j, t: (jnp.maximum(t[j], 0), 0)),
                  pl.BlockSpec((tn, d), lambda j, t: (0, 0))],
        out_specs=pl.BlockSpec((tn, d), lambda j, t: (j, 0)),
    )

    def body(t_ref, w_ref, tail_ref, o_ref):
        is_tail = t_ref[pl.program_id(0)] < 0

        @pl.when(is_tail)
        def _():
            o_ref[...] = tail_ref[...].astype(o_ref.dtype)

        @pl.when(jnp.logical_not(is_tail))
        def _():
            o_ref[...] = w_ref[...].astype(o_ref.dtype)

    return pl.pallas_call(
        body,
        grid_spec=grid_spec,
        out_shape=jax.ShapeDtypeStruct((len(src_blocks) * tn, d), BF16),
        compiler_params=_cparams(("arbitrary",)),
        name="wcast",
    )(table, wt, tail)


def _proj_kernel(x_ref, g_ref, w_ref, cg_ref, kg_ref, kb_ref,
                 oa_ref, ob_ref, c_ref, ct_ref, kn_ref, wit_ref, h_ref, *, na, nb):
    j = pl.program_id(1)
    nt = (((1,), (1,)), ((), ()))

    @pl.when(j == 0)
    def _():
        x = x_ref[...]
        y = x * lax.rsqrt(jnp.mean(x * x, axis=-1, keepdims=True) + EPS)
        h_ref[...] = (y * g_ref[...]).astype(BF16)

    @pl.when(j < na)
    def _():
        oa_ref[...] = lax.dot_general(h_ref[...], w_ref[...], nt, preferred_element_type=F32)

    @pl.when((j >= na) & (j < na + nb))
    def _():
        ob_ref[...] = lax.dot_general(h_ref[...], w_ref[...], nt, preferred_element_type=F32).astype(BF16)

    @pl.when(j >= na + nb)
    def _():
        tail = lax.dot_general(h_ref[...], w_ref[...], nt, preferred_element_type=F32)
        ckv = tail[:, :KV_LATENT]
        c = ckv * lax.rsqrt(jnp.mean(ckv * ckv, axis=-1, keepdims=True) + EPS) * cg_ref[...]
        c_ref[...] = c.astype(BF16)
        ct_ref[0:KV_LATENT, :] = c.T.astype(BF16)
        ct_ref[KV_LATENT:, :] = jnp.ones((ONES_ROWS, ct_ref.shape[1]), BF16)
        sm = tail[:, KV_LATENT:KV_LATENT + 128]
        ki = sm[:, :IDX_DIM]
        mu = jnp.mean(ki, axis=-1, keepdims=True)
        var = jnp.mean(jnp.square(ki - mu), axis=-1, keepdims=True)
        kn = (ki - mu) * lax.rsqrt(var + EPS) * kg_ref[...] + kb_ref[...]
        kn_ref[...] = kn.astype(BF16)
        wit_ref[...] = sm.T[IDX_DIM:IDX_DIM + IDX_HEADS, :] * (IDX_HEADS ** -0.5 * IDX_DIM ** -0.5)


def _proj(x2, g, w_all_t, ckv_g, k_g, k_b, n_f32, n_bf16, seq_len, tm=1024, tn=512):
    m, d = x2.shape
    na, nb = n_f32 // tn, n_bf16 // tn
    tps = seq_len // tm
    assert n_f32 % tn == 0 and n_bf16 % tn == 0 and w_all_t.shape[0] == n_f32 + n_bf16 + tn
    assert seq_len % tm == 0 and tn >= KV_LATENT + 128
    const = lambda shape: pl.BlockSpec(shape, lambda i, j: (0,) * len(shape))
    return pl.pallas_call(
        functools.partial(_proj_kernel, na=na, nb=nb),
        grid=(m // tm, na + nb + 1),
        in_specs=[
            pl.BlockSpec((tm, d), lambda i, j: (i, 0)),
            pl.BlockSpec((1, d), lambda i, j: (0, 0)),
            pl.BlockSpec((tn, d), lambda i, j: (j, 0)),
            const((1, KV_LATENT)),
            const((1, IDX_DIM)),
            const((1, IDX_DIM)),
        ],
        out_specs=[
            pl.BlockSpec((tm, tn), lambda i, j: (i, jnp.minimum(j, na - 1))),
            pl.BlockSpec((tm, tn), lambda i, j: (i, jnp.clip(j - na, 0, nb - 1))),
            pl.BlockSpec((tm, KV_LATENT), lambda i, j: (i, 0)),
            pl.BlockSpec((None, KV_LATENT + ONES_ROWS, tm), lambda i, j: (i // tps, 0, i % tps)),
            pl.BlockSpec((tm, IDX_DIM), lambda i, j: (i, 0)),
            pl.BlockSpec((None, IDX_HEADS, tm), lambda i, j: (i // tps, 0, i % tps)),
        ],
        out_shape=[
            jax.ShapeDtypeStruct((m, n_f32), F32),
            jax.ShapeDtypeStruct((m, n_bf16), BF16),
            jax.ShapeDtypeStruct((m, KV_LATENT), BF16),
            jax.ShapeDtypeStruct((m // seq_len, KV_LATENT + ONES_ROWS, seq_len), BF16),
            jax.ShapeDtypeStruct((m, IDX_DIM), BF16),
            jax.ShapeDtypeStruct((m // seq_len, IDX_HEADS, seq_len), F32),
        ],
        scratch_shapes=[pltpu.VMEM((tm, d), BF16)],
        compiler_params=_cparams(("parallel", "arbitrary")),
        name="proj",
    )(x2, g, w_all_t, ckv_g, k_g, k_b)


def _sigmoid(v):
    return 0.5 * jnp.tanh(0.5 * v) + 0.5


def _scan_step(a, b, k, axis, idx):
    keep = idx >= k
    a_prev = jnp.where(keep, pltpu.roll(a, k, axis=axis), 1.0)
    b_prev = jnp.where(keep, pltpu.roll(b, k, axis=axis), 0.0)
    return a * a_prev, a * b_prev + b


def _rglru_kernel(xa_ref, ga_ref, cw_ref, cb_ref, wa_ref, ba_ref, wx_ref, bx_ref, lam_ref,
                  o_ref, pad_s, a_s, b_s, c_s):
    s, w = xa_ref.shape
    tile = 8
    n_tiles = s // tile

    pad_s[0:tile, :] = jnp.zeros((tile, w), F32)
    pad_s[tile:tile + s, :] = xa_ref[...]
    acc = pad_s[tile:tile + s, :] * cw_ref[CONV_WIDTH - 1:CONV_WIDTH, :]
    for j in range(CONV_WIDTH - 1):
        back = CONV_WIDTH - 1 - j
        acc = acc + pad_s[tile - back:tile - back + s, :] * cw_ref[j:j + 1, :]
    xc = cb_ref[...] + acc

    xcb = xc.astype(BF16)
    r = _sigmoid(jnp.dot(xcb, wa_ref[...], preferred_element_type=F32) + ba_ref[...])
    i = _sigmoid(jnp.dot(xcb, wx_ref[...], preferred_element_type=F32) + bx_ref[...])
    z = -lam_ref[...]
    softplus = jnp.maximum(z, 0.0) + jnp.log1p(jnp.exp(-jnp.abs(z)))
    log_a = (-LRU_C) * r * softplus
    a = jnp.exp(log_a)
    m2 = (1.0 + a * a) * jnp.tanh(-log_a)
    mult = jnp.where(m2 > 0.0, m2 * lax.rsqrt(m2), 0.0)
    gated = i * xc
    b_s[...] = mult * gated
    b_s[0:1, :] = gated[0:1, :]

    a3 = a.reshape(n_tiles, tile, w)
    b3 = b_s[...].reshape(n_tiles, tile, w)
    sub = lax.broadcasted_iota(I32, (n_tiles, tile, w), 1)
    for k in (1, 2, 4):
        a3, b3 = _scan_step(a3, b3, k, 1, sub)
    a_s[...] = a3.reshape(s, w)
    b_s[...] = b3.reshape(s, w)

    at = a_s[pl.ds(tile - 1, n_tiles, stride=tile), :]
    bt = b_s[pl.ds(tile - 1, n_tiles, stride=tile), :]
    trow = lax.broadcasted_iota(I32, (n_tiles, w), 0)
    k = 1
    while k < n_tiles:
        at, bt = _scan_step(at, bt, k, 0, trow)
        k *= 2
    c_s[0:tile, :] = jnp.zeros((tile, w), F32)
    c_s[tile:tile + n_tiles, :] = bt

    def apply(t, carry):
        r0 = pl.multiple_of(t * tile, tile)
        before = c_s[pl.ds(tile - 1 + t, tile, stride=0), :]
        h = a_s[pl.ds(r0, tile), :] * before + b_s[pl.ds(r0, tile), :]
        ga = ga_ref[pl.ds(r0, tile), :]
        o_ref[pl.ds(r0, tile), :] = (h * (ga * _sigmoid(ga))).astype(o_ref.dtype)
        return carry

    lax.fori_loop(0, n_tiles, apply, 0, unroll=8)


def _rglru(pa3, cols, conv_w, conv_b, wa, ba, wx, bx, lam):
    bsz, s, _ = pa3.shape
    g, w = wa.shape[0], wa.shape[-1]
    xa_blk = _col_block(cols["xa"], w)
    ga_blk = _col_block(cols["ga"], w)
    vec = lambda: pl.BlockSpec((1, w), lambda b, j: (0, j))
    return pl.pallas_call(
        _rglru_kernel,
        grid=(bsz, g),
        in_specs=[
            pl.BlockSpec((None, s, w), lambda b, j: (b, 0, xa_blk + j)),
            pl.BlockSpec((None, s, w), lambda b, j: (b, 0, ga_blk + j)),
            pl.BlockSpec((CONV_WIDTH, w), lambda b, j: (0, j)),
            vec(),
            pl.BlockSpec((None, w, w), lambda b, j: (j, 0, 0)),
            vec(),
            pl.BlockSpec((None, w, w), lambda b, j: (j, 0, 0)),
            vec(),
            vec(),
        ],
        out_specs=pl.BlockSpec((None, s, w), lambda b, j: (b, 0, j)),
        out_shape=jax.ShapeDtypeStruct((bsz, s, g * w), BF16),
        scratch_shapes=[pltpu.VMEM((s + 8, w), F32), pltpu.VMEM((s, w), F32), pltpu.VMEM((s, w), F32),
                        pltpu.VMEM((s // 8 + 8, w), F32)],
        compiler_params=_cparams(("parallel", "parallel")),
        name="rglru",
    )(pa3, pa3, conv_w, conv_b, wa, ba, wx, bx, lam)


def _tree_sum(parts):
    while len(parts) > 1:
        paired = [parts[i] + parts[i + 1] for i in range(0, len(parts) - 1, 2)]
        parts = paired + ([parts[-1]] if len(parts) % 2 else [])
    return parts[0]


def _sortable_to_f32(u):
    key = u ^ INT_MIN
    return lax.bitcast_convert_type(key ^ ((key >> 31) & 0x7FFFFFFF), F32)


def _kth_largest(score_ref, rows, k):
    chains = 4

    def step(i, u):
        cand = u | (jnp.int32(1) << (31 - i))
        cand_f = _sortable_to_f32(cand)
        accs = [None] * chains
        for r in range(rows // 8):
            hit = jnp.where(score_ref[r * 8:(r + 1) * 8, :] >= cand_f, 1.0, 0.0)
            accs[r % chains] = hit if accs[r % chains] is None else accs[r % chains] + hit
        cnt = jnp.sum(_tree_sum([a for a in accs if a is not None]), axis=0, keepdims=True)
        return jnp.where(cnt >= k, cand, u)

    return _sortable_to_f32(lax.fori_loop(0, 32, step, jnp.zeros((1, Q_TILE), I32)))


def _loop_by_two(n, body, init):
    def pair(i, carry):
        return body(2 * i + 1, body(2 * i, carry))

    carry = lax.fori_loop(0, n // 2, pair, init)
    return lax.fori_loop(2 * (n // 2), n, body, carry)


def _dsa_kernel(q_ref, qi_ref, wukt_ref, wit_ref, kn_ref, c_ref, ct_ref, gb_ref, bias_ref, wuvt_ref,
                o_ref, qat_ref, qit_ref, score_s, thr_s, lg_s, acc_s, topk):
    qb = pl.program_id(1)

    qt = q_ref[...].T
    scale = HEAD_DIM ** -0.5 * LOG2E
    for h in range(ATT_HEADS):
        qa = jnp.dot(wukt_ref[h], qt[h * HEAD_DIM:(h + 1) * HEAD_DIM], preferred_element_type=F32)
        qat_ref[h] = (qa * scale).astype(BF16)
    qit = qi_ref[...].T
    for h in range(IDX_HEADS):
        qit_ref[h] = qit[h * IDX_DIM:(h + 1) * IDX_DIM, :]
    q_tiles = Q_TILE // K_CHUNK
    nkc = (qb + 1) * q_tiles
    nac = (nkc * K_CHUNK + ATT_CHUNK - 1) // ATT_CHUNK
    tiles = ATT_CHUNK // K_CHUNK

    kiota = lax.broadcasted_iota(I32, (K_CHUNK, Q_TILE), 0)
    qpos = qb * Q_TILE + lax.broadcasted_iota(I32, (K_CHUNK, Q_TILE), 1)

    def score_chunk(ac, carry):
        for t in range(tiles):
            k0 = pl.multiple_of(ac * ATT_CHUNK + t * K_CHUNK, K_CHUNK)
            kn = kn_ref[pl.ds(k0, K_CHUNK), :]
            acc = jnp.zeros((K_CHUNK, Q_TILE), F32)
            for h in range(IDX_HEADS):
                sc = jnp.dot(kn, qit_ref[h], preferred_element_type=F32)
                acc = acc + jnp.maximum(sc, 0.0) * wit_ref[h:h + 1, :]
            score_s[pl.ds(k0, K_CHUNK), :] = jnp.where(kiota + k0 <= qpos, acc, -jnp.inf)
        return carry

    _loop_by_two(nac, score_chunk, 0)

    for v in range(1, score_s.shape[0] // ATT_CHUNK + 1):
        @pl.when(nac == v)
        def _(rows=v * ATT_CHUNK):
            thr = _kth_largest(score_s, rows, float(topk))
            thr = jnp.where(thr >= NEG, thr, NEG)
            thr_s[...] = jnp.broadcast_to(thr, thr_s.shape)

    thr = thr_s[0:1, :]

    def logit_chunk(ac, m8s):
        r0 = pl.multiple_of(ac * ATT_CHUNK, ATT_CHUNK)
        c_chunk = c_ref[pl.ds(r0, ATT_CHUNK), :]
        mbias = jnp.where(score_s[pl.ds(r0, ATT_CHUNK), :] >= thr, 0.0, NEG)
        near = [[jnp.clip(ac * tiles + t - (qb * q_tiles + j) + 2, 0, 2) for j in range(q_tiles)]
                for t in range(tiles)]
        out = []
        for h in range(ATT_HEADS):
            lg = jnp.dot(c_chunk, qat_ref[h], preferred_element_type=F32) + mbias
            lg = jnp.concatenate(
                [lg[t * K_CHUNK:(t + 1) * K_CHUNK]
                 + jnp.concatenate([bias_ref[h, near[t][j]] for j in range(q_tiles)], axis=1)
                 for t in range(tiles)], axis=0)
            lg_s[h, pl.ds(r0, ATT_CHUNK), :] = lg
            out.append(jnp.maximum(m8s[h], jnp.max(lg.reshape(ATT_CHUNK // 8, 8, Q_TILE), axis=0)))
        return tuple(out)

    m8s = _loop_by_two(nac, logit_chunk, tuple(jnp.full((8, Q_TILE), NEG, F32) for _ in range(ATT_HEADS)))
    ms = [jnp.max(m8, axis=0, keepdims=True) for m8 in m8s]

    acc_s[...] = jnp.zeros(acc_s.shape, F32)

    def pv_chunk(ac, carry):
        r0 = pl.multiple_of(ac * ATT_CHUNK, ATT_CHUNK)
        ct_chunk = ct_ref[:, pl.ds(r0, ATT_CHUNK)]
        for h in range(ATT_HEADS):
            pr = jnp.exp2(lg_s[h, pl.ds(r0, ATT_CHUNK), :] - ms[h])
            acc_s[h] += jnp.dot(ct_chunk, pr.astype(BF16), preferred_element_type=F32)
        return carry

    _loop_by_two(nac, pv_chunk, 0)

    for h in range(ATT_HEADS):
        denom = acc_s[h, KV_LATENT:KV_LATENT + 1, :]
        o_t = acc_s[h, 0:KV_LATENT, :] * (1.0 / denom)
        y_t = jnp.dot(wuvt_ref[h], o_t.astype(BF16), preferred_element_type=F32)
        gb = gb_ref[:, h * HEAD_DIM:(h + 1) * HEAD_DIM]
        o_ref[:, h * HEAD_DIM:(h + 1) * HEAD_DIM] = (y_t.T * (gb * _sigmoid(gb))).astype(o_ref.dtype)


def _dsa(pb3, cols_b, w_ukt, wit, kn, c, ct, pa3, gb_blk, bias_tiles, wuvt, topk):
    bsz, s, _ = c.shape
    att_w = ATT_HEADS * HEAD_DIM
    idx_w = IDX_HEADS * IDX_DIM
    assert s % ATT_CHUNK == 0 and s % Q_TILE == 0 and Q_TILE % K_CHUNK == 0 and ATT_CHUNK % K_CHUNK == 0
    q_blk = _col_block(cols_b["q"], att_w)
    qi_blk = _col_block(cols_b["qi"], idx_w)
    const = lambda shape: pl.BlockSpec(shape, lambda b, i: (0,) * len(shape))
    return pl.pallas_call(
        functools.partial(_dsa_kernel, topk=topk),
        grid=(bsz, s // Q_TILE),
        in_specs=[
            pl.BlockSpec((None, Q_TILE, att_w), lambda b, i: (b, i, q_blk)),
            pl.BlockSpec((None, Q_TILE, idx_w), lambda b, i: (b, i, qi_blk)),
            const(w_ukt.shape),
            pl.BlockSpec((None, IDX_HEADS, Q_TILE), lambda b, i: (b, 0, i)),
            pl.BlockSpec((None, s, IDX_DIM), lambda b, i: (b, 0, 0)),
            pl.BlockSpec((None, s, KV_LATENT), lambda b, i: (b, 0, 0)),
            pl.BlockSpec((None, KV_LATENT + ONES_ROWS, s), lambda b, i: (b, 0, 0)),
            pl.BlockSpec((None, Q_TILE, att_w), lambda b, i: (b, i, gb_blk)),
            const(bias_tiles.shape),
            const(wuvt.shape),
        ],
        out_specs=pl.BlockSpec((None, Q_TILE, att_w), lambda b, i: (b, i, 0)),
        out_shape=jax.ShapeDtypeStruct((bsz, s, att_w), BF16),
        scratch_shapes=[
            pltpu.VMEM((ATT_HEADS, KV_LATENT, Q_TILE), BF16),
            pltpu.VMEM((IDX_HEADS, IDX_DIM, Q_TILE), BF16),
            pltpu.VMEM((s, Q_TILE), F32),
            pltpu.VMEM((8, Q_TILE), F32),
            pltpu.VMEM((ATT_HEADS, s, Q_TILE), F32),
            pltpu.VMEM((ATT_HEADS, KV_LATENT + ONES_ROWS, Q_TILE), F32),
        ],
        compiler_params=_cparams(("parallel", "arbitrary")),
        name="dsa",
    )(pb3, pb3, w_ukt, wit, kn, c, ct, pa3, bias_tiles, wuvt)


def _outp_kernel(ya_ref, yb_ref, wa_ref, wb_ref, x_ref, g_ref, o_ref, *, final_norm):
    acc = jnp.dot(ya_ref[...], wa_ref[...], preferred_element_type=F32)
    acc = acc + jnp.dot(yb_ref[...], wb_ref[...], preferred_element_type=F32)
    x = x_ref[...] + acc
    if final_norm:
        x = x * lax.rsqrt(jnp.mean(x * x, axis=-1, keepdims=True) + EPS) * g_ref[...]
    o_ref[...] = x


def _outp(ya, yb, w_out, x2, g, final_norm, tm=512):
    m, d = x2.shape
    ka, kb = ya.shape[1], yb.shape[1]
    assert ka == kb and w_out.shape[0] == ka + kb
    return pl.pallas_call(
        functools.partial(_outp_kernel, final_norm=final_norm),
        grid=(m // tm,),
        in_specs=[
            pl.BlockSpec((tm, ka), lambda i: (i, 0)),
            pl.BlockSpec((tm, kb), lambda i: (i, 0)),
            pl.BlockSpec((ka, d), lambda i: (0, 0)),
            pl.BlockSpec((kb, d), lambda i: (1, 0)),
            pl.BlockSpec((tm, d), lambda i: (i, 0)),
            pl.BlockSpec((1, d), lambda i: (0, 0)),
        ],
        out_specs=pl.BlockSpec((tm, d), lambda i: (i, 0)),
        out_shape=jax.ShapeDtypeStruct((m, d), F32),
        compiler_params=_cparams(("parallel",)),
        name="outp",
    )(ya, yb, w_out, w_out, x2, g)


def _t5_bucket(dist):
    n = jnp.maximum(dist, 0)
    max_exact = REL_BUCKETS // 2
    nf = jnp.maximum(n, 1).astype(F32)
    large = max_exact + (jnp.log(nf / max_exact) / np.log(REL_MAX_DIST / max_exact)
                         * (REL_BUCKETS - max_exact)).astype(I32)
    large = jnp.minimum(large, REL_BUCKETS - 1)
    return jnp.where(n < max_exact, n, large)


def _bias_tiles(rel_bias):
    qw = K_CHUNK
    span = K_CHUNK + qw
    table = rel_bias[_t5_bucket(jnp.arange(span + 1, dtype=I32))].astype(F32)
    table = ((table[:span] - table[span:]) * LOG2E).T
    n = span + qw - 1
    a = jnp.concatenate([jnp.zeros((ATT_HEADS, qw - 1), F32), table], axis=1)
    shifted = jnp.tile(a, (1, span + 1))[:, :span * (n + 1)].reshape(ATT_HEADS, span, n + 1)
    tiles = shifted[:, ::-1, :qw].reshape(ATT_HEADS, 2, K_CHUNK, qw)
    return jnp.concatenate([jnp.zeros_like(tiles[:, :1]), tiles], axis=1)


def kernel(x, norm_g, w_in, conv_w, conv_b, lru_wa, lru_ba, lru_wx, lru_bx, lru_lambda, ckv_norm_g, idx_k_norm_g, idx_k_norm_b, w_uk, w_uv, w_out, rel_bias, final_norm_g):
    bsz, s, d = x.shape
    depth = w_in.shape[0]
    lru_w = lru_wa.shape[1] * lru_wa.shape[2]
    att_w = ATT_HEADS * HEAD_DIM
    idx_w = IDX_HEADS * IDX_DIM
    assert REL_MAX_DIST <= K_CHUNK
    assert lru_w == att_w == idx_w and att_w % KV_LATENT == 0
    topk = min(INDEX_TOPK, s // 4)

    o_q = 2 * lru_w
    o_ckv = o_q + att_w
    o_gb = o_ckv + KV_LATENT
    o_qi = o_gb + att_w
    o_ki = o_qi + idx_w
    tn = 512
    cols_a = {"xa": 0, "ga": lru_w, "gb": 2 * lru_w}
    cols_b = {"q": 0, "qi": att_w}
    n_f32, n_bf16 = 3 * lru_w, att_w + idx_w

    bias_tiles = _bias_tiles(rel_bias)
    x2 = x.reshape(bsz * s, d)
    for l in range(depth):
        w_all_t = _wcast(w_in[l].T, [(0, o_q), (o_gb, o_qi), (o_q, o_ckv), (o_qi, o_ki), (o_ckv, o_gb),
                                     (o_ki, w_in.shape[2])])
        pa, pb, c, ct, kn, wit = _proj(x2, norm_g[l][None, :], w_all_t, ckv_norm_g[l][None, :],
                                       idx_k_norm_g[l][None, :], idx_k_norm_b[l][None, :], n_f32, n_bf16, s, tn=tn)
        pa3 = pa.reshape(bsz, s, -1)
        pb3 = pb.reshape(bsz, s, -1)

        ya = _rglru(pa3, cols_a, conv_w[l], conv_b[l][None, :], lru_wa[l].astype(BF16), lru_ba[l][None, :],
                    lru_wx[l].astype(BF16), lru_bx[l][None, :], lru_lambda[l][None, :])

        wukt = jnp.transpose(w_uk[l], (0, 2, 1)).astype(BF16)
        wuvt = jnp.transpose(w_uv[l], (0, 2, 1)).astype(BF16)
        yb = _dsa(pb3, cols_b, wukt, wit, kn.reshape(bsz, s, -1), c.reshape(bsz, s, -1), ct, pa3,
                  _col_block(cols_a["gb"], att_w), bias_tiles, wuvt, topk)

        x2 = _outp(ya.reshape(bsz * s, lru_w), yb.reshape(bsz * s, att_w), w_out[l].astype(BF16), x2,
                   final_norm_g[None, :], final_norm=(l == depth - 1))
    return x2.reshape(bsz, s, d)
```

```python
import functools

import numpy as np
import jax
import jax.numpy as jnp
from jax import lax
from jax.experimental import pallas as pl
from jax.experimental.pallas import tpu as pltpu

F32 = jnp.float32
BF16 = jnp.bfloat16
I32 = jnp.int32

LRU_BLOCKS = 8
CONV_WIDTH = 4
LRU_C = 8.0
ATT_HEADS = 8
HEAD_DIM = 128
KV_LATENT = 256
IDX_HEADS = 16
IDX_DIM = 64
INDEX_TOPK = 256
REL_BUCKETS = 32
REL_MAX_DIST = 128
EPS = 1e-6
LOG2E = float(np.log2(np.e))
ONES_ROWS = 16

Q_TILE = 256
K_CHUNK = 128
ATT_CHUNK = 256
NEG = float(np.finfo(np.float32).min)
INT_MIN = -(2 ** 31)
VMEM_LIMIT = 56 * 1024 * 1024


def _cparams(sem):
    return pltpu.CompilerParams(dimension_semantics=sem, vmem_limit_bytes=VMEM_LIMIT)


def _col_block(offset, width):
    assert offset % width == 0
    return offset // width


def _wcast(wt, order, tn=256):
    n, d = wt.shape
    n_whole = n // tn
    src_blocks = []
    for start, stop in order:
        assert start % tn == 0 and (stop % tn == 0 or stop == n)
        src_blocks += list(range(start // tn, -(-stop // tn)))
    tail = jnp.pad(wt[n_whole * tn:], ((0, (n_whole + 1) * tn - n), (0, 0)))
    table = jnp.asarray([blk if blk < n_whole else -1 for blk in src_blocks], I32)
    grid_spec = pltpu.PrefetchScalarGridSpec(
        num_scalar_prefetch=1,
        grid=(len(src_blocks),),
        in_specs=[pl.BlockSpec((tn, d), lambda j, t: (jnp.maximum(t[j], 0), 0)),
                  pl.BlockSpec((tn, d), lambda j, t: (0, 0))],
        out_specs=pl.BlockSpec((tn, d), lambda j, t: (j, 0)),
    )

    def body(t_ref, w_ref, tail_ref, o_ref):
        is_tail = t_ref[pl.program_id(0)] < 0

        @pl.when(is_tail)
        def _():
            o_ref[...] = tail_ref[...].astype(o_ref.dtype)

        @pl.when(jnp.logical_not(is_tail))
        def _():
            o_ref[...] = w_ref[...].astype(o_ref.dtype)

    return pl.pallas_call(
        body,
        grid_spec=grid_spec,
        out_shape=jax.ShapeDtypeStruct((len(src_blocks) * tn, d), BF16),
        compiler_params=_cparams(("arbitrary",)),
        name="wcast",
    )(table, wt, tail)


def _proj_kernel(x_ref, g_ref, w_ref, cg_ref, kg_ref, kb_ref,
                 oa_ref, ob_ref, c_ref, ct_ref, kn_ref, wit_ref, h_ref, *, na, nb):
    j = pl.program_id(1)
    nt = (((1,), (1,)), ((), ()))

    @pl.when(j == 0)
    def _():
        x = x_ref[...]
        y = x * lax.rsqrt(jnp.mean(x * x, axis=-1, keepdims=True) + EPS)
        h_ref[...] = (y * g_ref[...]).astype(BF16)

    @pl.when(j < na)
    def _():
        oa_ref[...] = lax.dot_general(h_ref[...], w_ref[...], nt, preferred_element_type=F32)

    @pl.when((j >= na) & (j < na + nb))
    def _():
        ob_ref[...] = lax.dot_general(h_ref[...], w_ref[...], nt, preferred_element_type=F32).astype(BF16)

    @pl.when(j >= na + nb)
    def _():
        tail = lax.dot_general(h_ref[...], w_ref[...], nt, preferred_element_type=F32)
        ckv = tail[:, :KV_LATENT]
        c = ckv * lax.rsqrt(jnp.mean(ckv * ckv, axis=-1, keepdims=True) + EPS) * cg_ref[...]
        c_ref[...] = c.astype(BF16)
        ct_ref[0:KV_LATENT, :] = c.T.astype(BF16)
        ct_ref[KV_LATENT:, :] = jnp.ones((ONES_ROWS, ct_ref.shape[1]), BF16)
        sm = tail[:, KV_LATENT:KV_LATENT + 128]
        ki = sm[:, :IDX_DIM]
        mu = jnp.mean(ki, axis=-1, keepdims=True)
        var = jnp.mean(jnp.square(ki - mu), axis=-1, keepdims=True)
        kn = (ki - mu) * lax.rsqrt(var + EPS) * kg_ref[...] + kb_ref[...]
        kn_ref[...] = kn.astype(BF16)
        wit_ref[...] = sm.T[IDX_DIM:IDX_DIM + IDX_HEADS, :] * (IDX_HEADS ** -0.5 * IDX_DIM ** -0.5)


def _proj(x2, g, w_all_t, ckv_g, k_g, k_b, n_f32, n_bf16, seq_len, tm=1024, tn=512):
    m, d = x2.shape
    na, nb = n_f32 // tn, n_bf16 // tn
    tps = seq_len // tm
    assert n_f32 % tn == 0 and n_bf16 % tn == 0 and w_all_t.shape[0] == n_f32 + n_bf16 + tn
    assert seq_len % tm == 0 and tn >= KV_LATENT + 128
    const = lambda shape: pl.BlockSpec(shape, lambda i, j: (0,) * len(shape))
    return pl.pallas_call(
        functools.partial(_proj_kernel, na=na, nb=nb),
        grid=(m // tm, na + nb + 1),
        in_specs=[
            pl.BlockSpec((tm, d), lambda i, j: (i, 0)),
            pl.BlockSpec((1, d), lambda i, j: (0, 0)),
            pl.BlockSpec((tn, d), lambda i, j: (j, 0)),
            const((1, KV_LATENT)),
            const((1, IDX_DIM)),
            const((1, IDX_DIM)),
        ],
        out_specs=[
            pl.BlockSpec((tm, tn), lambda i, j: (i, jnp.minimum(j, na - 1))),
            pl.BlockSpec((tm, tn), lambda i, j: (i, jnp.clip(j - na, 0, nb - 1))),
            pl.BlockSpec((tm, KV_LATENT), lambda i, j: (i, 0)),
            pl.BlockSpec((None, KV_LATENT + ONES_ROWS, tm), lambda i, j: (i // tps, 0, i % tps)),
            pl.BlockSpec((tm, IDX_DIM), lambda i, j: (i, 0)),
            pl.BlockSpec((None, IDX_HEADS, tm), lambda i, j: (i // tps, 0, i % tps)),
        ],
        out_shape=[
            jax.ShapeDtypeStruct((m, n_f32), F32),
            jax.ShapeDtypeStruct((m, n_bf16), BF16),
            jax.ShapeDtypeStruct((m, KV_LATENT), BF16),
            jax.ShapeDtypeStruct((m // seq_len, KV_LATENT + ONES_ROWS, seq_len), BF16),
            jax.ShapeDtypeStruct((m, IDX_DIM), BF16),
            jax.ShapeDtypeStruct((m // seq_len, IDX_HEADS, seq_len), F32),
        ],
        scratch_shapes=[pltpu.VMEM((tm, d), BF16)],
        compiler_params=_cparams(("parallel", "arbitrary")),
        name="proj",
    )(x2, g, w_all_t, ckv_g, k_g, k_b)


def _sigmoid(v):
    return 0.5 * jnp.tanh(0.5 * v) + 0.5


def _scan_step(a, b, k, axis, idx):
    keep = idx >= k
    a_prev = jnp.where(keep, pltpu.roll(a, k, axis=axis), 1.0)
    b_prev = jnp.where(keep, pltpu.roll(b, k, axis=axis), 0.0)
    return a * a_prev, a * b_prev + b


def _rglru_kernel(xa_ref, ga_ref, cw_ref, cb_ref, wa_ref, ba_ref, wx_ref, bx_ref, lam_ref,
                  o_ref, pad_s, a_s, b_s, c_s):
    s, w = xa_ref.shape
    tile = 8
    n_tiles = s // tile

    pad_s[0:tile, :] = jnp.zeros((tile, w), F32)
    pad_s[tile:tile + s, :] = xa_ref[...]
    acc = pad_s[tile:tile + s, :] * cw_ref[CONV_WIDTH - 1:CONV_WIDTH, :]
    for j in range(CONV_WIDTH - 1):
        back = CONV_WIDTH - 1 - j
        acc = acc + pad_s[tile - back:tile - back + s, :] * cw_ref[j:j + 1, :]
    xc = cb_ref[...] + acc

    xcb = xc.astype(BF16)
    tr = jnp.tanh(jnp.dot(xcb, wa_ref[...], preferred_element_type=F32) + ba_ref[...])
    ti = jnp.tanh(jnp.dot(xcb, wx_ref[...], preferred_element_type=F32) + bx_ref[...])
    i = 0.5 * ti + 0.5
    z = -lam_ref[...]
    softplus = jnp.maximum(z, 0.0) + jnp.log1p(jnp.exp(-jnp.abs(z)))
    half = (-0.5 * LRU_C) * softplus
    log_a = half * tr + half
    a = jnp.exp(log_a)
    m2 = (1.0 + a * a) * jnp.tanh(-log_a)
    mult = jnp.where(m2 > 0.0, m2 * lax.rsqrt(m2), 0.0)
    gated = i * xc
    b_s[...] = mult * gated
    b_s[0:1, :] = gated[0:1, :]

    a3 = a.reshape(n_tiles, tile, w)
    b3 = b_s[...].reshape(n_tiles, tile, w)
    sub = lax.broadcasted_iota(I32, (n_tiles, tile, w), 1)
    for k in (1, 2, 4):
        a3, b3 = _scan_step(a3, b3, k, 1, sub)
    a_s[...] = a3.reshape(s, w)
    b_s[...] = b3.reshape(s, w)

    at = a_s[pl.ds(tile - 1, n_tiles, stride=tile), :]
    bt = b_s[pl.ds(tile - 1, n_tiles, stride=tile), :]
    trow = lax.broadcasted_iota(I32, (n_tiles, w), 0)
    k = 1
    while k < n_tiles:
        at, bt = _scan_step(at, bt, k, 0, trow)
        k *= 2
    c_s[0:tile, :] = jnp.zeros((tile, w), F32)
    c_s[tile:tile + n_tiles, :] = bt

    def apply(t, carry):
        r0 = pl.multiple_of(t * tile, tile)
        before = c_s[pl.ds(tile - 1 + t, tile, stride=0), :]
        h = a_s[pl.ds(r0, tile), :] * before + b_s[pl.ds(r0, tile), :]
        gh = 0.5 * ga_ref[pl.ds(r0, tile), :]
        o_ref[pl.ds(r0, tile), :] = (h * (gh * (jnp.tanh(gh) + 1.0))).astype(o_ref.dtype)
        return carry

    lax.fori_loop(0, n_tiles, apply, 0, unroll=8)


def _rglru(pa3, cols, conv_w, conv_b, wa, ba, wx, bx, lam):
    bsz, s, _ = pa3.shape
    g, w = wa.shape[0], wa.shape[-1]
    xa_blk = _col_block(cols["xa"], w)
    ga_blk = _col_block(cols["ga"], w)
    vec = lambda: pl.BlockSpec((1, w), lambda b, j: (0, j))
    return pl.pallas_call(
        _rglru_kernel,
        grid=(bsz, g),
        in_specs=[
            pl.BlockSpec((None, s, w), lambda b, j: (b, 0, xa_blk + j)),
            pl.BlockSpec((None, s, w), lambda b, j: (b, 0, ga_blk + j)),
            pl.BlockSpec((CONV_WIDTH, w), lambda b, j: (0, j)),
            vec(),
            pl.BlockSpec((None, w, w), lambda b, j: (j, 0, 0)),
            vec(),
            pl.BlockSpec((None, w, w), lambda b, j: (j, 0, 0)),
            vec(),
            vec(),
        ],
        out_specs=pl.BlockSpec((None, s, w), lambda b, j: (b, 0, j)),
        out_shape=jax.ShapeDtypeStruct((bsz, s, g * w), BF16),
        scratch_shapes=[pltpu.VMEM((s + 8, w), F32), pltpu.VMEM((s, w), F32), pltpu.VMEM((s, w), F32),
                        pltpu.VMEM((s // 8 + 8, w), F32)],
        compiler_params=_cparams(("parallel", "parallel")),
        name="rglru",
    )(pa3, pa3, conv_w, conv_b, wa, ba, wx, bx, lam)


def _tree_sum(parts):
    while len(parts) > 1:
        paired = [parts[i] + parts[i + 1] for i in range(0, len(parts) - 1, 2)]
        parts = paired + ([parts[-1]] if len(parts) % 2 else [])
    return parts[0]


def _sortable_to_f32(u):
    key = u ^ INT_MIN
    return lax.bitcast_convert_type(key ^ ((key >> 31) & 0x7FFFFFFF), F32)


def _kth_largest(score_ref, rows, k):
    chains = 4

    def step(i, u):
        cand = u | (jnp.int32(1) << (31 - i))
        cand_f = _sortable_to_f32(cand)
        accs = [None] * chains
        for r in range(rows // 8):
            hit = jnp.where(score_ref[r * 8:(r + 1) * 8, :] >= cand_f, 1.0, 0.0)
            accs[r % chains] = hit if accs[r % chains] is None else accs[r % chains] + hit
        cnt = jnp.sum(_tree_sum([a for a in accs if a is not None]), axis=0, keepdims=True)
        return jnp.where(cnt >= k, cand, u)

    return _sortable_to_f32(lax.fori_loop(0, 32, step, jnp.zeros((1, Q_TILE), I32)))


def _loop_by_two(n, body, init):
    def trips(start, count, width, carry):
        def group(i, c):
            for t in range(width):
                c = body(start + width * i + t, c)
            return c
        return lax.fori_loop(0, count, group, carry)

    carry = trips(0, n // 4, 4, init)
    carry = trips(4 * (n // 4), (n % 4) // 2, 2, carry)
    return trips(2 * (n // 2), n % 2, 1, carry)


def _dsa_kernel(q_ref, qi_ref, wukt_ref, wit_ref, kn_ref, c_ref, ct_ref, gb_ref, bias_ref, wuvt_ref,
                o_ref, qat_ref, qit_ref, score_s, thr_s, lg_s, acc_s, topk):
    qb = pl.program_id(1)

    qt = q_ref[...].T
    scale = HEAD_DIM ** -0.5 * LOG2E
    for h in range(ATT_HEADS):
        qa = jnp.dot(wukt_ref[h], qt[h * HEAD_DIM:(h + 1) * HEAD_DIM], preferred_element_type=F32)
        qat_ref[h] = (qa * scale).astype(BF16)
    qit = qi_ref[...].T
    for h in range(IDX_HEADS):
        qit_ref[h] = qit[h * IDX_DIM:(h + 1) * IDX_DIM, :]
    q_tiles = Q_TILE // K_CHUNK
    nkc = (qb + 1) * q_tiles
    nac = (nkc * K_CHUNK + ATT_CHUNK - 1) // ATT_CHUNK
    tiles = ATT_CHUNK // K_CHUNK

    kiota = lax.broadcasted_iota(I32, (K_CHUNK, Q_TILE), 0)
    qpos = qb * Q_TILE + lax.broadcasted_iota(I32, (K_CHUNK, Q_TILE), 1)

    def score_chunk(ac, carry):
        for t in range(tiles):
            k0 = pl.multiple_of(ac * ATT_CHUNK + t * K_CHUNK, K_CHUNK)
            kn = kn_ref[pl.ds(k0, K_CHUNK), :]
            acc = jnp.zeros((K_CHUNK, Q_TILE), F32)
            for h in range(IDX_HEADS):
                sc = jnp.dot(kn, qit_ref[h], preferred_element_type=F32)
                acc = acc + jnp.maximum(sc, 0.0) * wit_ref[h:h + 1, :]
            score_s[pl.ds(k0, K_CHUNK), :] = jnp.where(kiota + k0 <= qpos, acc, -jnp.inf)
        return carry

    _loop_by_two(nac, score_chunk, 0)

    for v in range(1, score_s.shape[0] // ATT_CHUNK + 1):
        @pl.when(nac == v)
        def _(rows=v * ATT_CHUNK):
            thr = _kth_largest(score_s, rows, float(topk))
            thr = jnp.where(thr >= NEG, thr, NEG)
            thr_s[...] = jnp.broadcast_to(thr, thr_s.shape)

    thr = thr_s[0:1, :]

    def logit_chunk(ac, m8s):
        r0 = pl.multiple_of(ac * ATT_CHUNK, ATT_CHUNK)
        c_chunk = c_ref[pl.ds(r0, ATT_CHUNK), :]
        mbias = jnp.where(score_s[pl.ds(r0, ATT_CHUNK), :] >= thr, 0.0, NEG)
        near = [[jnp.clip(ac * tiles + t - (qb * q_tiles + j) + 2, 0, 2) for j in range(q_tiles)]
                for t in range(tiles)]
        out = []
        for h in range(ATT_HEADS):
            lg = jnp.dot(c_chunk, qat_ref[h], preferred_element_type=F32) + mbias
            lg = jnp.concatenate(
                [lg[t * K_CHUNK:(t + 1) * K_CHUNK]
                 + jnp.concatenate([bias_ref[h, near[t][j]] for j in range(q_tiles)], axis=1)
                 for t in range(tiles)], axis=0)
            lg_s[h, pl.ds(r0, ATT_CHUNK), :] = lg
            out.append(jnp.maximum(m8s[h], jnp.max(lg.reshape(ATT_CHUNK // 8, 8, Q_TILE), axis=0)))
        return tuple(out)

    m8s = _loop_by_two(nac, logit_chunk, tuple(jnp.full((8, Q_TILE), NEG, F32) for _ in range(ATT_HEADS)))
    ms = [jnp.max(m8, axis=0, keepdims=True) for m8 in m8s]

    acc_s[...] = jnp.zeros(acc_s.shape, F32)

    def pv_chunk(ac, carry):
        r0 = pl.multiple_of(ac * ATT_CHUNK, ATT_CHUNK)
        ct_chunk = ct_ref[:, pl.ds(r0, ATT_CHUNK)]
        for h in range(ATT_HEADS):
            pr = jnp.exp2(lg_s[h, pl.ds(r0, ATT_CHUNK), :] - ms[h])
            acc_s[h] += jnp.dot(ct_chunk, pr.astype(BF16), preferred_element_type=F32)
        return carry

    _loop_by_two(nac, pv_chunk, 0)

    for h in range(ATT_HEADS):
        denom = acc_s[h, KV_LATENT:KV_LATENT + 1, :]
        o_t = acc_s[h, 0:KV_LATENT, :] * (1.0 / denom)
        y_t = jnp.dot(wuvt_ref[h], o_t.astype(BF16), preferred_element_type=F32)
        gb = gb_ref[:, h * HEAD_DIM:(h + 1) * HEAD_DIM]
        o_ref[:, h * HEAD_DIM:(h + 1) * HEAD_DIM] = (y_t.T * (gb * _sigmoid(gb))).astype(o_ref.dtype)


def _dsa(pb3, cols_b, w_ukt, wit, kn, c, ct, pa3, gb_blk, bias_tiles, wuvt, topk):
    bsz, s, _ = c.shape
    att_w = ATT_HEADS * HEAD_DIM
    idx_w = IDX_HEADS * IDX_DIM
    assert s % ATT_CHUNK == 0 and s % Q_TILE == 0 and Q_TILE % K_CHUNK == 0 and ATT_CHUNK % K_CHUNK == 0
    q_blk = _col_block(cols_b["q"], att_w)
    qi_blk = _col_block(cols_b["qi"], idx_w)
    const = lambda shape: pl.BlockSpec(shape, lambda b, i: (0,) * len(shape))
    return pl.pallas_call(
        functools.partial(_dsa_kernel, topk=topk),
        grid=(bsz, s // Q_TILE),
        in_specs=[
            pl.BlockSpec((None, Q_TILE, att_w), lambda b, i: (b, i, q_blk)),
            pl.BlockSpec((None, Q_TILE, idx_w), lambda b, i: (b, i, qi_blk)),
            const(w_ukt.shape),
            pl.BlockSpec((None, IDX_HEADS, Q_TILE), lambda b, i: (b, 0, i)),
            pl.BlockSpec((None, s, IDX_DIM), lambda b, i: (b, 0, 0)),
            pl.BlockSpec((None, s, KV_LATENT), lambda b, i: (b, 0, 0)),
            pl.BlockSpec((None, KV_LATENT + ONES_ROWS, s), lambda b, i: (b, 0, 0)),
            pl.BlockSpec((None, Q_TILE, att_w), lambda b, i: (b, i, gb_blk)),
            const(bias_tiles.shape),
            const(wuvt.shape),
        ],
        out_specs=pl.BlockSpec((None, Q_TILE, att_w), lambda b, i: (b, i, 0)),
        out_shape=jax.ShapeDtypeStruct((bsz, s, att_w), BF16),
        scratch_shapes=[
            pltpu.VMEM((ATT_HEADS, KV_LATENT, Q_TILE), BF16),
            pltpu.VMEM((IDX_HEADS, IDX_DIM, Q_TILE), BF16),
            pltpu.VMEM((s, Q_TILE), F32),
            pltpu.VMEM((8, Q_TILE), F32),
            pltpu.VMEM((ATT_HEADS, s, Q_TILE), F32),
            pltpu.VMEM((ATT_HEADS, KV_LATENT + ONES_ROWS, Q_TILE), F32),
        ],
        compiler_params=_cparams(("parallel", "arbitrary")),
        name="dsa",
    )(pb3, pb3, w_ukt, wit, kn, c, ct, pa3, bias_tiles, wuvt)


def _outp_kernel(ya_ref, yb_ref, wa_ref, wb_ref, x_ref, g_ref, o_ref, *, final_norm):
    acc = jnp.dot(ya_ref[...], wa_ref[...], preferred_element_type=F32)
    acc = acc + jnp.dot(yb_ref[...], wb_ref[...], preferred_element_type=F32)
    x = x_ref[...] + acc
    if final_norm:
        x = x * lax.rsqrt(jnp.mean(x * x, axis=-1, keepdims=True) + EPS) * g_ref[...]
    o_ref[...] = x


def _outp(ya, yb, w_out, x2, g, final_norm, tm=512):
    m, d = x2.shape
    ka, kb = ya.shape[1], yb.shape[1]
    assert ka == kb and w_out.shape[0] == ka + kb
    return pl.pallas_call(
        functools.partial(_outp_kernel, final_norm=final_norm),
        grid=(m // tm,),
        in_specs=[
            pl.BlockSpec((tm, ka), lambda i: (i, 0)),
            pl.BlockSpec((tm, kb), lambda i: (i, 0)),
            pl.BlockSpec((ka, d), lambda i: (0, 0)),
            pl.BlockSpec((kb, d), lambda i: (1, 0)),
            pl.BlockSpec((tm, d), lambda i: (i, 0)),
            pl.BlockSpec((1, d), lambda i: (0, 0)),
        ],
        out_specs=pl.BlockSpec((tm, d), lambda i: (i, 0)),
        out_shape=jax.ShapeDtypeStruct((m, d), F32),
        compiler_params=_cparams(("parallel",)),
        name="outp",
    )(ya, yb, w_out, w_out, x2, g)


def _t5_bucket(dist):
    n = jnp.maximum(dist, 0)
    max_exact = REL_BUCKETS // 2
    nf = jnp.maximum(n, 1).astype(F32)
    large = max_exact + (jnp.log(nf / max_exact) / np.log(REL_MAX_DIST / max_exact)
                         * (REL_BUCKETS - max_exact)).astype(I32)
    large = jnp.minimum(large, REL_BUCKETS - 1)
    return jnp.where(n < max_exact, n, large)


def _bias_tiles(rel_bias):
    qw = K_CHUNK
    span = K_CHUNK + qw
    table = rel_bias[_t5_bucket(jnp.arange(span + 1, dtype=I32))].astype(F32)
    table = ((table[:span] - table[span:]) * LOG2E).T
    n = span + qw - 1
    a = jnp.concatenate([jnp.zeros((ATT_HEADS, qw - 1), F32), table], axis=1)
    shifted = jnp.tile(a, (1, span + 1))[:, :span * (n + 1)].reshape(ATT_HEADS, span, n + 1)
    tiles = shifted[:, ::-1, :qw].reshape(ATT_HEADS, 2, K_CHUNK, qw)
    return jnp.concatenate([jnp.zeros_like(tiles[:, :1]), tiles], axis=1)


def kernel(x, norm_g, w_in, conv_w, conv_b, lru_wa, lru_ba, lru_wx, lru_bx, lru_lambda, ckv_norm_g, idx_k_norm_g, idx_k_norm_b, w_uk, w_uv, w_out, rel_bias, final_norm_g):
    bsz, s, d = x.shape
    depth = w_in.shape[0]
    lru_w = lru_wa.shape[1] * lru_wa.shape[2]
    att_w = ATT_HEADS * HEAD_DIM
    idx_w = IDX_HEADS * IDX_DIM
    assert REL_MAX_DIST <= K_CHUNK
    assert lru_w == att_w == idx_w and att_w % KV_LATENT == 0
    topk = min(INDEX_TOPK, s // 4)

    o_q = 2 * lru_w
    o_ckv = o_q + att_w
    o_gb = o_ckv + KV_LATENT
    o_qi = o_gb + att_w
    o_ki = o_qi + idx_w
    tn = 512
    cols_a = {"xa": 0, "ga": lru_w, "gb": 2 * lru_w}
    cols_b = {"q": 0, "qi": att_w}
    n_f32, n_bf16 = 3 * lru_w, att_w + idx_w

    bias_tiles = _bias_tiles(rel_bias)
    x2 = x.reshape(bsz * s, d)
    for l in range(depth):
        w_all_t = _wcast(w_in[l].T, [(0, o_q), (o_gb, o_qi), (o_q, o_ckv), (o_qi, o_ki), (o_ckv, o_gb),
                                     (o_ki, w_in.shape[2])])
        pa, pb, c, ct, kn, wit = _proj(x2, norm_g[l][None, :], w_all_t, ckv_norm_g[l][None, :],
                                       idx_k_norm_g[l][None, :], idx_k_norm_b[l][None, :], n_f32, n_bf16, s, tn=tn)
        pa3 = pa.reshape(bsz, s, -1)
        pb3 = pb.reshape(bsz, s, -1)

        ya = _rglru(pa3, cols_a, conv_w[l], conv_b[l][None, :], (0.5 * lru_wa[l]).astype(BF16),
                    0.5 * lru_ba[l][None, :], (0.5 * lru_wx[l]).astype(BF16), 0.5 * lru_bx[l][None, :],
                    lru_lambda[l][None, :])

        wukt = jnp.transpose(w_uk[l], (0, 2, 1)).astype(BF16)
        wuvt = jnp.transpose(w_uv[l], (0, 2, 1)).astype(BF16)
        yb = _dsa(pb3, cols_b, wukt, wit, kn.reshape(bsz, s, -1), c.reshape(bsz, s, -1), ct, pa3,
                  _col_block(cols_a["gb"], att_w), bias_tiles, wuvt, topk)

        x2 = _outp(ya.reshape(bsz * s, lru_w), yb.reshape(bsz * s, att_w), w_out[l].astype(BF16), x2,
                   final_norm_g[None, :], final_norm=(l == depth - 1))
    return x2.reshape(bsz, s, d)
```

```python
import functools

import numpy as np
import jax
import jax.numpy as jnp
from jax import lax
from jax.experimental import pallas as pl
from jax.experimental.pallas import tpu as pltpu

F32 = jnp.float32
BF16 = jnp.bfloat16
I32 = jnp.int32

LRU_BLOCKS = 8
CONV_WIDTH = 4
LRU_C = 8.0
ATT_HEADS = 8
HEAD_DIM = 128
KV_LATENT = 256
IDX_HEADS = 16
IDX_DIM = 64
INDEX_TOPK = 256
REL_BUCKETS = 32
REL_MAX_DIST = 128
EPS = 1e-6
LOG2E = float(np.log2(np.e))
ONES_ROWS = 16

Q_TILE = 256
K_CHUNK = 128
ATT_CHUNK = 256
NEG = float(np.finfo(np.float32).min)
INT_MIN = -(2 ** 31)
VMEM_LIMIT = 56 * 1024 * 1024


def _cparams(sem):
    return pltpu.CompilerParams(dimension_semantics=sem, vmem_limit_bytes=VMEM_LIMIT)


def _col_block(offset, width):
    assert offset % width == 0
    return offset // width


def _wcast(wt, order, tn=256):
    n, d = wt.shape
    n_whole = n // tn
    src_blocks = []
    for start, stop in order:
        assert start % tn == 0 and (stop % tn == 0 or stop == n)
        src_blocks += list(range(start // tn, -(-stop // tn)))
    tail = jnp.pad(wt[n_whole * tn:], ((0, (n_whole + 1) * tn - n), (0, 0)))
    table = jnp.asarray([blk if blk < n_whole else -1 for blk in src_blocks], I32)
    grid_spec = pltpu.PrefetchScalarGridSpec(
        num_scalar_prefetch=1,
        grid=(len(src_blocks),),
        in_specs=[pl.BlockSpec((tn, d), lambda j, t: (jnp.maximum(t[j], 0), 0)),
                  pl.BlockSpec((tn, d), lambda j, t: (0, 0))],
        out_specs=pl.BlockSpec((tn, d), lambda j, t: (j, 0)),
    )

    def body(t_ref, w_ref, tail_ref, o_ref):
        is_tail = t_ref[pl.program_id(0)] < 0

        @pl.when(is_tail)
        def _():
            o_ref[...] = tail_ref[...].astype(o_ref.dtype)

        @pl.when(jnp.logical_not(is_tail))
        def _():
            o_ref[...] = w_ref[...].astype(o_ref.dtype)

    return pl.pallas_call(
        body,
        grid_spec=grid_spec,
        out_shape=jax.ShapeDtypeStruct((len(src_blocks) * tn, d), BF16),
        compiler_params=_cparams(("arbitrary",)),
        name="wcast",
    )(table, wt, tail)


def _proj_kernel(x_ref, g_ref, w_ref, cg_ref, kg_ref, kb_ref,
                 oa_ref, ob_ref, c_ref, ct_ref, kn_ref, wit_ref, h_ref, *, na, nb):
    j = pl.program_id(1)
    nt = (((1,), (1,)), ((), ()))

    @pl.when(j == 0)
    def _():
        x = x_ref[...]
        y = x * lax.rsqrt(jnp.mean(x * x, axis=-1, keepdims=True) + EPS)
        h_ref[...] = (y * g_ref[...]).astype(BF16)

    @pl.when(j < na)
    def _():
        oa_ref[...] = lax.dot_general(h_ref[...], w_ref[...], nt, preferred_element_type=F32)

    @pl.when((j >= na) & (j < na + nb))
    def _():
        ob_ref[...] = lax.dot_general(h_ref[...], w_ref[...], nt, preferred_element_type=F32).astype(BF16)

    @pl.when(j >= na + nb)
    def _():
        tail = lax.dot_general(h_ref[...], w_ref[...], nt, preferred_element_type=F32)
        ckv = tail[:, :KV_LATENT]
        c = ckv * lax.rsqrt(jnp.mean(ckv * ckv, axis=-1, keepdims=True) + EPS) * cg_ref[...]
        c_ref[...] = c.astype(BF16)
        ct_ref[0:KV_LATENT, :] = c.T.astype(BF16)
        ct_ref[KV_LATENT:, :] = jnp.ones((ONES_ROWS, ct_ref.shape[1]), BF16)
        sm = tail[:, KV_LATENT:KV_LATENT + 128]
        ki = sm[:, :IDX_DIM]
        mu = jnp.mean(ki, axis=-1, keepdims=True)
        var = jnp.mean(jnp.square(ki - mu), axis=-1, keepdims=True)
        kn = (ki - mu) * lax.rsqrt(var + EPS) * kg_ref[...] + kb_ref[...]
        kn_ref[...] = kn.astype(BF16)
        wit_ref[...] = sm.T[IDX_DIM:IDX_DIM + IDX_HEADS, :] * (IDX_HEADS ** -0.5 * IDX_DIM ** -0.5)


def _proj(x2, g, w_all_t, ckv_g, k_g, k_b, n_f32, n_bf16, seq_len, tm=1024, tn=512):
    m, d = x2.shape
    na, nb = n_f32 // tn, n_bf16 // tn
    tps = seq_len // tm
    assert n_f32 % tn == 0 and n_bf16 % tn == 0 and w_all_t.shape[0] == n_f32 + n_bf16 + tn
    assert seq_len % tm == 0 and tn >= KV_LATENT + 128
    const = lambda shape: pl.BlockSpec(shape, lambda i, j: (0,) * len(shape))
    return pl.pallas_call(
        functools.partial(_proj_kernel, na=na, nb=nb),
        grid=(m // tm, na + nb + 1),
        in_specs=[
            pl.BlockSpec((tm, d), lambda i, j: (i, 0)),
            pl.BlockSpec((1, d), lambda i, j: (0, 0)),
            pl.BlockSpec((tn, d), lambda i, j: (j, 0)),
            const((1, KV_LATENT)),
            const((1, IDX_DIM)),
            const((1, IDX_DIM)),
        ],
        out_specs=[
            pl.BlockSpec((tm, tn), lambda i, j: (i, jnp.minimum(j, na - 1))),
            pl.BlockSpec((tm, tn), lambda i, j: (i, jnp.clip(j - na, 0, nb - 1))),
            pl.BlockSpec((tm, KV_LATENT), lambda i, j: (i, 0)),
            pl.BlockSpec((None, KV_LATENT + ONES_ROWS, tm), lambda i, j: (i // tps, 0, i % tps)),
            pl.BlockSpec((tm, IDX_DIM), lambda i, j: (i, 0)),
            pl.BlockSpec((None, IDX_HEADS, tm), lambda i, j: (i // tps, 0, i % tps)),
        ],
        out_shape=[
            jax.ShapeDtypeStruct((m, n_f32), F32),
            jax.ShapeDtypeStruct((m, n_bf16), BF16),
            jax.ShapeDtypeStruct((m, KV_LATENT), BF16),
            jax.ShapeDtypeStruct((m // seq_len, KV_LATENT + ONES_ROWS, seq_len), BF16),
            jax.ShapeDtypeStruct((m, IDX_DIM), BF16),
            jax.ShapeDtypeStruct((m // seq_len, IDX_HEADS, seq_len), F32),
        ],
        scratch_shapes=[pltpu.VMEM((tm, d), BF16)],
        compiler_params=_cparams(("parallel", "arbitrary")),
        name="proj",
    )(x2, g, w_all_t, ckv_g, k_g, k_b)


def _sigmoid(v):
    return 0.5 * jnp.tanh(0.5 * v) + 0.5


def _scan_step(a, b, k, axis, idx):
    keep = idx >= k
    a_prev = jnp.where(keep, pltpu.roll(a, k, axis=axis), 1.0)
    b_prev = jnp.where(keep, pltpu.roll(b, k, axis=axis), 0.0)
    return a * a_prev, a * b_prev + b


def _rglru_kernel(xa_ref, ga_ref, cw_ref, cb_ref, wa_ref, ba_ref, wx_ref, bx_ref, lam_ref,
                  o_ref, pad_s, a_s, b_s, c_s):
    s, w = xa_ref.shape
    tile = 8
    n_tiles = s // tile

    pad_s[0:tile, :] = jnp.zeros((tile, w), F32)
    pad_s[tile:tile + s, :] = xa_ref[...]
    acc = pad_s[tile:tile + s, :] * cw_ref[CONV_WIDTH - 1:CONV_WIDTH, :]
    for j in range(CONV_WIDTH - 1):
        back = CONV_WIDTH - 1 - j
        acc = acc + pad_s[tile - back:tile - back + s, :] * cw_ref[j:j + 1, :]
    xc = cb_ref[...] + acc

    xcb = xc.astype(BF16)
    tr = jnp.tanh(jnp.dot(xcb, wa_ref[...], preferred_element_type=F32) + ba_ref[...])
    ti = jnp.tanh(jnp.dot(xcb, wx_ref[...], preferred_element_type=F32) + bx_ref[...])
    i = 0.5 * ti + 0.5
    z = -lam_ref[...]
    softplus = jnp.maximum(z, 0.0) + jnp.log1p(jnp.exp(-jnp.abs(z)))
    half = (-0.5 * LRU_C) * softplus
    log_a = half * tr + half
    a = jnp.exp(log_a)
    m2 = (1.0 + a * a) * jnp.tanh(-log_a)
    mult = jnp.where(m2 > 0.0, m2 * lax.rsqrt(m2), 0.0)
    gated = i * xc
    b_s[...] = mult * gated
    b_s[0:1, :] = gated[0:1, :]

    a3 = a.reshape(n_tiles, tile, w)
    b3 = b_s[...].reshape(n_tiles, tile, w)
    sub = lax.broadcasted_iota(I32, (n_tiles, tile, w), 1)
    for k in (1, 2, 4):
        a3, b3 = _scan_step(a3, b3, k, 1, sub)
    a_s[...] = a3.reshape(s, w)
    b_s[...] = b3.reshape(s, w)

    at = a_s[pl.ds(tile - 1, n_tiles, stride=tile), :]
    bt = b_s[pl.ds(tile - 1, n_tiles, stride=tile), :]
    trow = lax.broadcasted_iota(I32, (n_tiles, w), 0)
    k = 1
    while k < n_tiles:
        at, bt = _scan_step(at, bt, k, 0, trow)
        k *= 2
    c_s[0:tile, :] = jnp.zeros((tile, w), F32)
    c_s[tile:tile + n_tiles, :] = bt

    def apply(t, carry):
        r0 = pl.multiple_of(t * tile, tile)
        before = c_s[pl.ds(tile - 1 + t, tile, stride=0), :]
        h = a_s[pl.ds(r0, tile), :] * before + b_s[pl.ds(r0, tile), :]
        gh = 0.5 * ga_ref[pl.ds(r0, tile), :]
        o_ref[pl.ds(r0, tile), :] = (h * (gh * (jnp.tanh(gh) + 1.0))).astype(o_ref.dtype)
        return carry

    lax.fori_loop(0, n_tiles, apply, 0, unroll=8)


def _rglru(pa3, cols, conv_w, conv_b, wa, ba, wx, bx, lam):
    bsz, s, _ = pa3.shape
    g, w = wa.shape[0], wa.shape[-1]
    xa_blk = _col_block(cols["xa"], w)
    ga_blk = _col_block(cols["ga"], w)
    vec = lambda: pl.BlockSpec((1, w), lambda b, j: (0, j))
    return pl.pallas_call(
        _rglru_kernel,
        grid=(bsz, g),
        in_specs=[
            pl.BlockSpec((None, s, w), lambda b, j: (b, 0, xa_blk + j)),
            pl.BlockSpec((None, s, w), lambda b, j: (b, 0, ga_blk + j)),
            pl.BlockSpec((CONV_WIDTH, w), lambda b, j: (0, j)),
            vec(),
            pl.BlockSpec((None, w, w), lambda b, j: (j, 0, 0)),
            vec(),
            pl.BlockSpec((None, w, w), lambda b, j: (j, 0, 0)),
            vec(),
            vec(),
        ],
        out_specs=pl.BlockSpec((None, s, w), lambda b, j: (b, 0, j)),
        out_shape=jax.ShapeDtypeStruct((bsz, s, g * w), BF16),
        scratch_shapes=[pltpu.VMEM((s + 8, w), F32), pltpu.VMEM((s, w), F32), pltpu.VMEM((s, w), F32),
                        pltpu.VMEM((s // 8 + 8, w), F32)],
        compiler_params=_cparams(("parallel", "parallel")),
        name="rglru",
    )(pa3, pa3, conv_w, conv_b, wa, ba, wx, bx, lam)


def _tree_sum(parts):
    while len(parts) > 1:
        paired = [parts[i] + parts[i + 1] for i in range(0, len(parts) - 1, 2)]
        parts = paired + ([parts[-1]] if len(parts) % 2 else [])
    return parts[0]


def _sortable_to_f32(u):
    key = u ^ INT_MIN
    return lax.bitcast_convert_type(key ^ ((key >> 31) & 0x7FFFFFFF), F32)


def _count_ge(ref, rows, cand, pack):
    chains = 4
    one, zero = jnp.ones((), ref.dtype), jnp.zeros((), ref.dtype)
    accs = [None] * chains
    for r in range(rows // pack):
        hit = jnp.where(ref[r * pack:(r + 1) * pack, :] >= cand, one, zero)
        accs[r % chains] = hit if accs[r % chains] is None else accs[r % chains] + hit
    parts = [a.astype(F32) for a in accs if a is not None]
    return jnp.sum(_tree_sum(parts), axis=0, keepdims=True)


def _kth_largest(score_ref, score16_ref, rows, k):
    def step16(i, u):
        cand = u | (jnp.int32(1) << (15 - i))
        cand_f = _sortable_to_f32(cand << 16).astype(BF16)
        return jnp.where(_count_ge(score16_ref, rows, cand_f, 16) >= k, cand, u)

    hi = lax.fori_loop(0, 16, step16, jnp.zeros((1, Q_TILE), I32))
    keeps = _count_ge(score_ref, rows, _sortable_to_f32(hi << 16), 8) >= k
    hi = jnp.where(keeps, hi, jnp.maximum(hi - 1, 0))

    def step(i, u):
        cand = u | (jnp.int32(1) << (15 - i))
        return jnp.where(_count_ge(score_ref, rows, _sortable_to_f32(cand), 8) >= k, cand, u)

    return _sortable_to_f32(lax.fori_loop(0, 16, step, hi << 16))


def _loop_by_two(n, body, init):
    def trips(start, count, width, carry):
        def group(i, c):
            for t in range(width):
                c = body(start + width * i + t, c)
            return c
        return lax.fori_loop(0, count, group, carry)

    carry = trips(0, n // 4, 4, init)
    carry = trips(4 * (n // 4), (n % 4) // 2, 2, carry)
    return trips(2 * (n // 2), n % 2, 1, carry)


def _dsa_kernel(q_ref, qi_ref, wukt_ref, wit_ref, kn_ref, c_ref, ct_ref, gb_ref, bias_ref, wuvt_ref,
                o_ref, qat_ref, qit_ref, score_s, score16_s, thr_s, lg_s, acc_s, topk):
    qb = pl.program_id(1)

    qt = q_ref[...].T
    scale = HEAD_DIM ** -0.5 * LOG2E
    for h in range(ATT_HEADS):
        qa = jnp.dot(wukt_ref[h], qt[h * HEAD_DIM:(h + 1) * HEAD_DIM], preferred_element_type=F32)
        qat_ref[h] = (qa * scale).astype(BF16)
    qit = qi_ref[...].T
    for h in range(IDX_HEADS):
        qit_ref[h] = qit[h * IDX_DIM:(h + 1) * IDX_DIM, :]
    q_tiles = Q_TILE // K_CHUNK
    nkc = (qb + 1) * q_tiles
    nac = (nkc * K_CHUNK + ATT_CHUNK - 1) // ATT_CHUNK
    tiles = ATT_CHUNK // K_CHUNK

    kiota = lax.broadcasted_iota(I32, (K_CHUNK, Q_TILE), 0)
    qpos = qb * Q_TILE + lax.broadcasted_iota(I32, (K_CHUNK, Q_TILE), 1)

    def score_chunk(ac, carry):
        for t in range(tiles):
            k0 = pl.multiple_of(ac * ATT_CHUNK + t * K_CHUNK, K_CHUNK)
            kn = kn_ref[pl.ds(k0, K_CHUNK), :]
            acc = jnp.zeros((K_CHUNK, Q_TILE), F32)
            for h in range(IDX_HEADS):
                sc = jnp.dot(kn, qit_ref[h], preferred_element_type=F32)
                acc = acc + jnp.maximum(sc, 0.0) * wit_ref[h:h + 1, :]
            masked = jnp.where(kiota + k0 <= qpos, acc, -jnp.inf)
            score_s[pl.ds(k0, K_CHUNK), :] = masked
            score16_s[pl.ds(k0, K_CHUNK), :] = masked.astype(BF16)
        return carry

    _loop_by_two(nac, score_chunk, 0)

    for v in range(1, score_s.shape[0] // ATT_CHUNK + 1):
        @pl.when(nac == v)
        def _(rows=v * ATT_CHUNK):
            thr = _kth_largest(score_s, score16_s, rows, float(topk))
            thr = jnp.where(thr >= NEG, thr, NEG)
            thr_s[...] = jnp.broadcast_to(thr, thr_s.shape)

    thr = thr_s[0:1, :]

    def logit_chunk(ac, m8s):
        r0 = pl.multiple_of(ac * ATT_CHUNK, ATT_CHUNK)
        c_chunk = c_ref[pl.ds(r0, ATT_CHUNK), :]
        mbias = jnp.where(score_s[pl.ds(r0, ATT_CHUNK), :] >= thr, 0.0, NEG)
        near = [[jnp.clip(ac * tiles + t - (qb * q_tiles + j) + 2, 0, 2) for j in range(q_tiles)]
                for t in range(tiles)]
        out = []
        for h in range(ATT_HEADS):
            lg = jnp.dot(c_chunk, qat_ref[h], preferred_element_type=F32) + mbias
            lg = jnp.concatenate(
                [lg[t * K_CHUNK:(t + 1) * K_CHUNK]
                 + jnp.concatenate([bias_ref[h, near[t][j]] for j in range(q_tiles)], axis=1)
                 for t in range(tiles)], axis=0)
            lg_s[h, pl.ds(r0, ATT_CHUNK), :] = lg
            out.append(jnp.maximum(m8s[h], jnp.max(lg.reshape(ATT_CHUNK // 8, 8, Q_TILE), axis=0)))
        return tuple(out)

    m8s = _loop_by_two(nac, logit_chunk, tuple(jnp.full((8, Q_TILE), NEG, F32) for _ in range(ATT_HEADS)))
    ms = [jnp.max(m8, axis=0, keepdims=True) for m8 in m8s]

    acc_s[...] = jnp.zeros(acc_s.shape, F32)

    def pv_chunk(ac, carry):
        r0 = pl.multiple_of(ac * ATT_CHUNK, ATT_CHUNK)
        ct_chunk = ct_ref[:, pl.ds(r0, ATT_CHUNK)]
        for h in range(ATT_HEADS):
            pr = jnp.exp2(lg_s[h, pl.ds(r0, ATT_CHUNK), :] - ms[h])
            acc_s[h] += jnp.dot(ct_chunk, pr.astype(BF16), preferred_element_type=F32)
        return carry

    _loop_by_two(nac, pv_chunk, 0)

    for h in range(ATT_HEADS):
        denom = acc_s[h, KV_LATENT:KV_LATENT + 1, :]
        o_t = acc_s[h, 0:KV_LATENT, :] * (1.0 / denom)
        y_t = jnp.dot(wuvt_ref[h], o_t.astype(BF16), preferred_element_type=F32)
        gb = gb_ref[:, h * HEAD_DIM:(h + 1) * HEAD_DIM]
        o_ref[:, h * HEAD_DIM:(h + 1) * HEAD_DIM] = (y_t.T * (gb * _sigmoid(gb))).astype(o_ref.dtype)


def _dsa(pb3, cols_b, w_ukt, wit, kn, c, ct, pa3, gb_blk, bias_tiles, wuvt, topk):
    bsz, s, _ = c.shape
    att_w = ATT_HEADS * HEAD_DIM
    idx_w = IDX_HEADS * IDX_DIM
    assert s % ATT_CHUNK == 0 and s % Q_TILE == 0 and Q_TILE % K_CHUNK == 0 and ATT_CHUNK % K_CHUNK == 0
    assert s // 16 // 4 < 256
    q_blk = _col_block(cols_b["q"], att_w)
    qi_blk = _col_block(cols_b["qi"], idx_w)
    const = lambda shape: pl.BlockSpec(shape, lambda b, i: (0,) * len(shape))
    return pl.pallas_call(
        functools.partial(_dsa_kernel, topk=topk),
        grid=(bsz, s // Q_TILE),
        in_specs=[
            pl.BlockSpec((None, Q_TILE, att_w), lambda b, i: (b, i, q_blk)),
            pl.BlockSpec((None, Q_TILE, idx_w), lambda b, i: (b, i, qi_blk)),
            const(w_ukt.shape),
            pl.BlockSpec((None, IDX_HEADS, Q_TILE), lambda b, i: (b, 0, i)),
            pl.BlockSpec((None, s, IDX_DIM), lambda b, i: (b, 0, 0)),
            pl.BlockSpec((None, s, KV_LATENT), lambda b, i: (b, 0, 0)),
            pl.BlockSpec((None, KV_LATENT + ONES_ROWS, s), lambda b, i: (b, 0, 0)),
            pl.BlockSpec((None, Q_TILE, att_w), lambda b, i: (b, i, gb_blk)),
            const(bias_tiles.shape),
            const(wuvt.shape),
        ],
        out_specs=pl.BlockSpec((None, Q_TILE, att_w), lambda b, i: (b, i, 0)),
        out_shape=jax.ShapeDtypeStruct((bsz, s, att_w), BF16),
        scratch_shapes=[
            pltpu.VMEM((ATT_HEADS, KV_LATENT, Q_TILE), BF16),
            pltpu.VMEM((IDX_HEADS, IDX_DIM, Q_TILE), BF16),
            pltpu.VMEM((s, Q_TILE), F32),
            pltpu.VMEM((s, Q_TILE), BF16),
            pltpu.VMEM((8, Q_TILE), F32),
            pltpu.VMEM((ATT_HEADS, s, Q_TILE), F32),
            pltpu.VMEM((ATT_HEADS, KV_LATENT + ONES_ROWS, Q_TILE), F32),
        ],
        compiler_params=_cparams(("parallel", "arbitrary")),
        name="dsa",
    )(pb3, pb3, w_ukt, wit, kn, c, ct, pa3, bias_tiles, wuvt)


def _outp_kernel(ya_ref, yb_ref, wa_ref, wb_ref, x_ref, g_ref, o_ref, *, final_norm):
    acc = jnp.dot(ya_ref[...], wa_ref[...], preferred_element_type=F32)
    acc = acc + jnp.dot(yb_ref[...], wb_ref[...], preferred_element_type=F32)
    x = x_ref[...] + acc
    if final_norm:
        x = x * lax.rsqrt(jnp.mean(x * x, axis=-1, keepdims=True) + EPS) * g_ref[...]
    o_ref[...] = x


def _outp(ya, yb, w_out, x2, g, final_norm, tm=512):
    m, d = x2.shape
    ka, kb = ya.shape[1], yb.shape[1]
    assert ka == kb and w_out.shape[0] == ka + kb
    return pl.pallas_call(
        functools.partial(_outp_kernel, final_norm=final_norm),
        grid=(m // tm,),
        in_specs=[
            pl.BlockSpec((tm, ka), lambda i: (i, 0)),
            pl.BlockSpec((tm, kb), lambda i: (i, 0)),
            pl.BlockSpec((ka, d), lambda i: (0, 0)),
            pl.BlockSpec((kb, d), lambda i: (1, 0)),
            pl.BlockSpec((tm, d), lambda i: (i, 0)),
            pl.BlockSpec((1, d), lambda i: (0, 0)),
        ],
        out_specs=pl.BlockSpec((tm, d), lambda i: (i, 0)),
        out_shape=jax.ShapeDtypeStruct((m, d), F32),
        compiler_params=_cparams(("parallel",)),
        name="outp",
    )(ya, yb, w_out, w_out, x2, g)


def _t5_bucket(dist):
    n = jnp.maximum(dist, 0)
    max_exact = REL_BUCKETS // 2
    nf = jnp.maximum(n, 1).astype(F32)
    large = max_exact + (jnp.log(nf / max_exact) / np.log(REL_MAX_DIST / max_exact)
                         * (REL_BUCKETS - max_exact)).astype(I32)
    large = jnp.minimum(large, REL_BUCKETS - 1)
    return jnp.where(n < max_exact, n, large)


def _bias_tiles(rel_bias):
    qw = K_CHUNK
    span = K_CHUNK + qw
    table = rel_bias[_t5_bucket(jnp.arange(span + 1, dtype=I32))].astype(F32)
    table = ((table[:span] - table[span:]) * LOG2E).T
    n = span + qw - 1
    a = jnp.concatenate([jnp.zeros((ATT_HEADS, qw - 1), F32), table], axis=1)
    shifted = jnp.tile(a, (1, span + 1))[:, :span * (n + 1)].reshape(ATT_HEADS, span, n + 1)
    tiles = shifted[:, ::-1, :qw].reshape(ATT_HEADS, 2, K_CHUNK, qw)
    return jnp.concatenate([jnp.zeros_like(tiles[:, :1]), tiles], axis=1)


def kernel(x, norm_g, w_in, conv_w, conv_b, lru_wa, lru_ba, lru_wx, lru_bx, lru_lambda, ckv_norm_g, idx_k_norm_g, idx_k_norm_b, w_uk, w_uv, w_out, rel_bias, final_norm_g):
    bsz, s, d = x.shape
    depth = w_in.shape[0]
    lru_w = lru_wa.shape[1] * lru_wa.shape[2]
    att_w = ATT_HEADS * HEAD_DIM
    idx_w = IDX_HEADS * IDX_DIM
    assert REL_MAX_DIST <= K_CHUNK
    assert lru_w == att_w == idx_w and att_w % KV_LATENT == 0
    topk = min(INDEX_TOPK, s // 4)

    o_q = 2 * lru_w
    o_ckv = o_q + att_w
    o_gb = o_ckv + KV_LATENT
    o_qi = o_gb + att_w
    o_ki = o_qi + idx_w
    tn = 512
    cols_a = {"xa": 0, "ga": lru_w, "gb": 2 * lru_w}
    cols_b = {"q": 0, "qi": att_w}
    n_f32, n_bf16 = 3 * lru_w, att_w + idx_w

    bias_tiles = _bias_tiles(rel_bias)
    x2 = x.reshape(bsz * s, d)
    for l in range(depth):
        w_all_t = _wcast(w_in[l].T, [(0, o_q), (o_gb, o_qi), (o_q, o_ckv), (o_qi, o_ki), (o_ckv, o_gb),
                                     (o_ki, w_in.shape[2])])
        pa, pb, c, ct, kn, wit = _proj(x2, norm_g[l][None, :], w_all_t, ckv_norm_g[l][None, :],
                                       idx_k_norm_g[l][None, :], idx_k_norm_b[l][None, :], n_f32, n_bf16, s, tn=tn)
        pa3 = pa.reshape(bsz, s, -1)
        pb3 = pb.reshape(bsz, s, -1)

        ya = _rglru(pa3, cols_a, conv_w[l], conv_b[l][None, :], (0.5 * lru_wa[l]).astype(BF16),
                    0.5 * lru_ba[l][None, :], (0.5 * lru_wx[l]).astype(BF16), 0.5 * lru_bx[l][None, :],
                    lru_lambda[l][None, :])

        wukt = jnp.transpose(w_uk[l], (0, 2, 1)).astype(BF16)
        wuvt = jnp.transpose(w_uv[l], (0, 2, 1)).astype(BF16)
        yb = _dsa(pb3, cols_b, wukt, wit, kn.reshape(bsz, s, -1), c.reshape(bsz, s, -1), ct, pa3,
                  _col_block(cols_a["gb"], att_w), bias_tiles, wuvt, topk)

        x2 = _outp(ya.reshape(bsz * s, lru_w), yb.reshape(bsz * s, att_w), w_out[l].astype(BF16), x2,
                   final_norm_g[None, :], final_norm=(l == depth - 1))
    return x2.reshape(bsz, s, d)
```

```python
import functools

import numpy as np
import jax
import jax.numpy as jnp
from jax import lax
from jax.experimental import pallas as pl
from jax.experimental.pallas import tpu as pltpu

F32 = jnp.float32
BF16 = jnp.bfloat16
I32 = jnp.int32

LRU_BLOCKS = 8
CONV_WIDTH = 4
LRU_C = 8.0
ATT_HEADS = 8
HEAD_DIM = 128
KV_LATENT = 256
IDX_HEADS = 16
IDX_DIM = 64
INDEX_TOPK = 256
REL_BUCKETS = 32
REL_MAX_DIST = 128
EPS = 1e-6
LOG2E = float(np.log2(np.e))
ONES_ROWS = 16

Q_TILE = 256
K_CHUNK = 128
ATT_CHUNK = 256
NEG = float(np.finfo(np.float32).min)
INT_MIN = -(2 ** 31)
VMEM_LIMIT = 56 * 1024 * 1024


def _cparams(sem):
    return pltpu.CompilerParams(dimension_semantics=sem, vmem_limit_bytes=VMEM_LIMIT)


def _col_block(offset, width):
    assert offset % width == 0
    return offset // width


def _proj_kernel(t_ref, x_ref, g_ref, wlo_ref, whi_ref, tail_ref, cg_ref, kg_ref, kb_ref,
                 oa_ref, ob_ref, c_ref, ct_ref, kn_ref, wit_ref, h_ref, *, na, nb):
    j = pl.program_id(1)
    nt = (((1,), (1,)), ((), ()))

    def w_tile():
        lo = jnp.where(t_ref[2 * j] < 0, tail_ref[...], wlo_ref[...])
        hi = jnp.where(t_ref[2 * j + 1] < 0, tail_ref[...], whi_ref[...])
        return jnp.concatenate([lo, hi], axis=0).astype(BF16)

    @pl.when(j == 0)
    def _():
        x = x_ref[...]
        y = x * lax.rsqrt(jnp.mean(x * x, axis=-1, keepdims=True) + EPS)
        h_ref[...] = (y * g_ref[...]).astype(BF16)

    @pl.when(j < na)
    def _():
        oa_ref[...] = lax.dot_general(h_ref[...], w_tile(), nt, preferred_element_type=F32)

    @pl.when((j >= na) & (j < na + nb))
    def _():
        ob_ref[...] = lax.dot_general(h_ref[...], w_tile(), nt, preferred_element_type=F32).astype(BF16)

    @pl.when(j >= na + nb)
    def _():
        tail = lax.dot_general(h_ref[...], w_tile(), nt, preferred_element_type=F32)
        ckv = tail[:, :KV_LATENT]
        c = ckv * lax.rsqrt(jnp.mean(ckv * ckv, axis=-1, keepdims=True) + EPS) * cg_ref[...]
        c_ref[...] = c.astype(BF16)
        ct_ref[0:KV_LATENT, :] = c.T.astype(BF16)
        ct_ref[KV_LATENT:, :] = jnp.ones((ONES_ROWS, ct_ref.shape[1]), BF16)
        sm = tail[:, KV_LATENT:KV_LATENT + 128]
        ki = sm[:, :IDX_DIM]
        mu = jnp.mean(ki, axis=-1, keepdims=True)
        var = jnp.mean(jnp.square(ki - mu), axis=-1, keepdims=True)
        kn = (ki - mu) * lax.rsqrt(var + EPS) * kg_ref[...] + kb_ref[...]
        kn_ref[...] = kn.astype(BF16)
        wit_ref[...] = sm.T[IDX_DIM:IDX_DIM + IDX_HEADS, :] * (IDX_HEADS ** -0.5 * IDX_DIM ** -0.5)


def _proj(x2, g, w_t, order, ckv_g, k_g, k_b, n_f32, n_bf16, seq_len, tm=1024, tn=512):
    m, d = x2.shape
    n = w_t.shape[0]
    th = tn // 2
    n_whole = n // th
    src = []
    for start, stop in order:
        assert start % th == 0 and (stop % th == 0 or stop == n)
        src += list(range(start // th, -(-stop // th)))
    na, nb = n_f32 // tn, n_bf16 // tn
    assert n_f32 % tn == 0 and n_bf16 % tn == 0 and len(src) * th == n_f32 + n_bf16 + tn
    tail = jnp.pad(w_t[n_whole * th:], ((0, (n_whole + 1) * th - n), (0, 0)))
    table = jnp.asarray([blk if blk < n_whole else -1 for blk in src], I32)
    tps = seq_len // tm
    assert seq_len % tm == 0 and tn >= KV_LATENT + 128
    const = lambda shape: pl.BlockSpec(shape, lambda i, j, t: (0,) * len(shape))
    grid_spec = pltpu.PrefetchScalarGridSpec(
        num_scalar_prefetch=1,
        grid=(m // tm, na + nb + 1),
        in_specs=[
            pl.BlockSpec((tm, d), lambda i, j, t: (i, 0)),
            const((1, d)),
            pl.BlockSpec((th, d), lambda i, j, t: (jnp.maximum(t[2 * j], 0), 0)),
            pl.BlockSpec((th, d), lambda i, j, t: (jnp.maximum(t[2 * j + 1], 0), 0)),
            const((th, d)),
            const((1, KV_LATENT)),
            const((1, IDX_DIM)),
            const((1, IDX_DIM)),
        ],
        out_specs=[
            pl.BlockSpec((tm, tn), lambda i, j, t: (i, jnp.minimum(j, na - 1))),
            pl.BlockSpec((tm, tn), lambda i, j, t: (i, jnp.clip(j - na, 0, nb - 1))),
            pl.BlockSpec((tm, KV_LATENT), lambda i, j, t: (i, 0)),
            pl.BlockSpec((None, KV_LATENT + ONES_ROWS, tm), lambda i, j, t: (i // tps, 0, i % tps)),
            pl.BlockSpec((tm, IDX_DIM), lambda i, j, t: (i, 0)),
            pl.BlockSpec((None, IDX_HEADS, tm), lambda i, j, t: (i // tps, 0, i % tps)),
        ],
        scratch_shapes=[pltpu.VMEM((tm, d), BF16)],
    )
    return pl.pallas_call(
        functools.partial(_proj_kernel, na=na, nb=nb),
        grid_spec=grid_spec,
        out_shape=[
            jax.ShapeDtypeStruct((m, n_f32), F32),
            jax.ShapeDtypeStruct((m, n_bf16), BF16),
            jax.ShapeDtypeStruct((m, KV_LATENT), BF16),
            jax.ShapeDtypeStruct((m // seq_len, KV_LATENT + ONES_ROWS, seq_len), BF16),
            jax.ShapeDtypeStruct((m, IDX_DIM), BF16),
            jax.ShapeDtypeStruct((m // seq_len, IDX_HEADS, seq_len), F32),
        ],
        compiler_params=_cparams(("parallel", "arbitrary")),
        name="proj",
    )(table, x2, g, w_t, w_t, tail, ckv_g, k_g, k_b)


def _sigmoid(v):
    return 0.5 * jnp.tanh(0.5 * v) + 0.5


def _scan_step(a, b, k, axis, idx):
    keep = idx >= k
    a_prev = jnp.where(keep, pltpu.roll(a, k, axis=axis), 1.0)
    b_prev = jnp.where(keep, pltpu.roll(b, k, axis=axis), 0.0)
    return a * a_prev, a * b_prev + b


def _rglru_kernel(xa_ref, ga_ref, cw_ref, cb_ref, wa_ref, ba_ref, wx_ref, bx_ref, lam_ref,
                  o_ref, pad_s, a_s, b_s, c_s):
    s, w = xa_ref.shape
    tile = 8
    n_tiles = s // tile

    pad_s[0:tile, :] = jnp.zeros((tile, w), F32)
    pad_s[tile:tile + s, :] = xa_ref[...]
    acc = pad_s[tile:tile + s, :] * cw_ref[CONV_WIDTH - 1:CONV_WIDTH, :]
    for j in range(CONV_WIDTH - 1):
        back = CONV_WIDTH - 1 - j
        acc = acc + pad_s[tile - back:tile - back + s, :] * cw_ref[j:j + 1, :]
    xc = cb_ref[...] + acc

    xcb = xc.astype(BF16)
    tr = jnp.tanh(jnp.dot(xcb, wa_ref[...], preferred_element_type=F32) + ba_ref[...])
    ti = jnp.tanh(jnp.dot(xcb, wx_ref[...], preferred_element_type=F32) + bx_ref[...])
    i = 0.5 * ti + 0.5
    z = -lam_ref[...]
    softplus = jnp.maximum(z, 0.0) + jnp.log1p(jnp.exp(-jnp.abs(z)))
    half = (-0.5 * LRU_C) * softplus
    log_a = half * tr + half
    a = jnp.exp(log_a)
    m2 = (1.0 + a * a) * jnp.tanh(-log_a)
    mult = jnp.where(m2 > 0.0, m2 * lax.rsqrt(m2), 0.0)
    gated = i * xc
    b_s[...] = mult * gated
    b_s[0:1, :] = gated[0:1, :]

    a3 = a.reshape(n_tiles, tile, w)
    b3 = b_s[...].reshape(n_tiles, tile, w)
    sub = lax.broadcasted_iota(I32, (n_tiles, tile, w), 1)
    for k in (1, 2, 4):
        a3, b3 = _scan_step(a3, b3, k, 1, sub)
    a_s[...] = a3.reshape(s, w)
    b_s[...] = b3.reshape(s, w)

    at = a_s[pl.ds(tile - 1, n_tiles, stride=tile), :]
    bt = b_s[pl.ds(tile - 1, n_tiles, stride=tile), :]
    trow = lax.broadcasted_iota(I32, (n_tiles, w), 0)
    k = 1
    while k < n_tiles:
        at, bt = _scan_step(at, bt, k, 0, trow)
        k *= 2
    c_s[0:tile, :] = jnp.zeros((tile, w), F32)
    c_s[tile:tile + n_tiles, :] = bt

    def apply(t, carry):
        r0 = pl.multiple_of(t * tile, tile)
        before = c_s[pl.ds(tile - 1 + t, tile, stride=0), :]
        h = a_s[pl.ds(r0, tile), :] * before + b_s[pl.ds(r0, tile), :]
        gh = 0.5 * ga_ref[pl.ds(r0, tile), :]
        o_ref[pl.ds(r0, tile), :] = (h * (gh * (jnp.tanh(gh) + 1.0))).astype(o_ref.dtype)
        return carry

    lax.fori_loop(0, n_tiles, apply, 0, unroll=8)


def _rglru(pa3, cols, conv_w, conv_b, wa, ba, wx, bx, lam):
    bsz, s, _ = pa3.shape
    g, w = wa.shape[0], wa.shape[-1]
    xa_blk = _col_block(cols["xa"], w)
    ga_blk = _col_block(cols["ga"], w)
    vec = lambda: pl.BlockSpec((1, w), lambda b, j: (0, j))
    return pl.pallas_call(
        _rglru_kernel,
        grid=(bsz, g),
        in_specs=[
            pl.BlockSpec((None, s, w), lambda b, j: (b, 0, xa_blk + j)),
            pl.BlockSpec((None, s, w), lambda b, j: (b, 0, ga_blk + j)),
            pl.BlockSpec((CONV_WIDTH, w), lambda b, j: (0, j)),
            vec(),
            pl.BlockSpec((None, w, w), lambda b, j: (j, 0, 0)),
            vec(),
            pl.BlockSpec((None, w, w), lambda b, j: (j, 0, 0)),
            vec(),
            vec(),
        ],
        out_specs=pl.BlockSpec((None, s, w), lambda b, j: (b, 0, j)),
        out_shape=jax.ShapeDtypeStruct((bsz, s, g * w), BF16),
        scratch_shapes=[pltpu.VMEM((s + 8, w), F32), pltpu.VMEM((s, w), F32), pltpu.VMEM((s, w), F32),
                        pltpu.VMEM((s // 8 + 8, w), F32)],
        compiler_params=_cparams(("parallel", "parallel")),
        name="rglru",
    )(pa3, pa3, conv_w, conv_b, wa, ba, wx, bx, lam)


def _tree_sum(parts):
    while len(parts) > 1:
        paired = [parts[i] + parts[i + 1] for i in range(0, len(parts) - 1, 2)]
        parts = paired + ([parts[-1]] if len(parts) % 2 else [])
    return parts[0]


def _sortable_to_f32(u):
    key = u ^ INT_MIN
    return lax.bitcast_convert_type(key ^ ((key >> 31) & 0x7FFFFFFF), F32)


def _count_ge(ref, rows, cand, pack):
    chains = 4
    one, zero = jnp.ones((), ref.dtype), jnp.zeros((), ref.dtype)
    accs = [None] * chains
    for r in range(rows // pack):
        hit = jnp.where(ref[r * pack:(r + 1) * pack, :] >= cand, one, zero)
        accs[r % chains] = hit if accs[r % chains] is None else accs[r % chains] + hit
    parts = [a.astype(F32) for a in accs if a is not None]
    return jnp.sum(_tree_sum(parts), axis=0, keepdims=True)


def _kth_largest(score_ref, score16_ref, rows, k):
    def step16(i, u):
        cand = u | (jnp.int32(1) << (15 - i))
        cand_f = _sortable_to_f32(cand << 16).astype(BF16)
        return jnp.where(_count_ge(score16_ref, rows, cand_f, 16) >= k, cand, u)

    hi = lax.fori_loop(0, 16, step16, jnp.zeros((1, Q_TILE), I32))
    keeps = _count_ge(score_ref, rows, _sortable_to_f32(hi << 16), 8) >= k
    hi = jnp.where(keeps, hi, jnp.maximum(hi - 1, 0))

    def step(i, u):
        cand = u | (jnp.int32(1) << (15 - i))
        return jnp.where(_count_ge(score_ref, rows, _sortable_to_f32(cand), 8) >= k, cand, u)

    return _sortable_to_f32(lax.fori_loop(0, 16, step, hi << 16))


def _loop_by_two(n, body, init):
    def trips(start, count, width, carry):
        def group(i, c):
            for t in range(width):
                c = body(start + width * i + t, c)
            return c
        return lax.fori_loop(0, count, group, carry)

    carry = trips(0, n // 4, 4, init)
    carry = trips(4 * (n // 4), (n % 4) // 2, 2, carry)
    return trips(2 * (n // 2), n % 2, 1, carry)


def _dsa_kernel(q_ref, qi_ref, wukt_ref, wit_ref, kn_ref, c_ref, ct_ref, gb_ref, bias_ref, wuvt_ref,
                o_ref, qat_ref, qit_ref, score_s, score16_s, thr_s, lg_s, acc_s, topk):
    qb = pl.program_id(1)

    qt = q_ref[...].T
    scale = HEAD_DIM ** -0.5 * LOG2E
    for h in range(ATT_HEADS):
        qa = jnp.dot(wukt_ref[h], qt[h * HEAD_DIM:(h + 1) * HEAD_DIM], preferred_element_type=F32)
        qat_ref[h] = (qa * scale).astype(BF16)
    qit = qi_ref[...].T
    for h in range(IDX_HEADS):
        qit_ref[h] = qit[h * IDX_DIM:(h + 1) * IDX_DIM, :]
    q_tiles = Q_TILE // K_CHUNK
    nkc = (qb + 1) * q_tiles
    nac = (nkc * K_CHUNK + ATT_CHUNK - 1) // ATT_CHUNK
    tiles = ATT_CHUNK // K_CHUNK

    kiota = lax.broadcasted_iota(I32, (K_CHUNK, Q_TILE), 0)
    qpos = qb * Q_TILE + lax.broadcasted_iota(I32, (K_CHUNK, Q_TILE), 1)

    def score_chunk(ac, carry):
        for t in range(tiles):
            k0 = pl.multiple_of(ac * ATT_CHUNK + t * K_CHUNK, K_CHUNK)
            kn = kn_ref[pl.ds(k0, K_CHUNK), :]
            acc = jnp.zeros((K_CHUNK, Q_TILE), F32)
            for h in range(IDX_HEADS):
                sc = jnp.dot(kn, qit_ref[h], preferred_element_type=F32)
                acc = acc + jnp.maximum(sc, 0.0) * wit_ref[h:h + 1, :]
            masked = jnp.where(kiota + k0 <= qpos, acc, -jnp.inf)
            score_s[pl.ds(k0, K_CHUNK), :] = masked
            score16_s[pl.ds(k0, K_CHUNK), :] = masked.astype(BF16)
        return carry

    _loop_by_two(nac, score_chunk, 0)

    for v in range(1, score_s.shape[0] // ATT_CHUNK + 1):
        @pl.when(nac == v)
        def _(rows=v * ATT_CHUNK):
            thr = _kth_largest(score_s, score16_s, rows, float(topk))
            thr = jnp.where(thr >= NEG, thr, NEG)
            thr_s[...] = jnp.broadcast_to(thr, thr_s.shape)

    thr = thr_s[0:1, :]

    def logit_chunk(ac, m8s):
        r0 = pl.multiple_of(ac * ATT_CHUNK, ATT_CHUNK)
        c_chunk = c_ref[pl.ds(r0, ATT_CHUNK), :]
        mbias = jnp.where(score_s[pl.ds(r0, ATT_CHUNK), :] >= thr, 0.0, NEG)
        near = [[jnp.clip(ac * tiles + t - (qb * q_tiles + j) + 2, 0, 2) for j in range(q_tiles)]
                for t in range(tiles)]
        out = []
        for h in range(ATT_HEADS):
            lg = jnp.dot(c_chunk, qat_ref[h], preferred_element_type=F32) + mbias
            lg = jnp.concatenate(
                [lg[t * K_CHUNK:(t + 1) * K_CHUNK]
                 + jnp.concatenate([bias_ref[h, near[t][j]] for j in range(q_tiles)], axis=1)
                 for t in range(tiles)], axis=0)
            lg_s[h, pl.ds(r0, ATT_CHUNK), :] = lg
            out.append(jnp.maximum(m8s[h], jnp.max(lg.reshape(ATT_CHUNK // 8, 8, Q_TILE), axis=0)))
        return tuple(out)

    m8s = _loop_by_two(nac, logit_chunk, tuple(jnp.full((8, Q_TILE), NEG, F32) for _ in range(ATT_HEADS)))
    ms = [jnp.max(m8, axis=0, keepdims=True) for m8 in m8s]

    acc_s[...] = jnp.zeros(acc_s.shape, F32)

    def pv_chunk(ac, carry):
        r0 = pl.multiple_of(ac * ATT_CHUNK, ATT_CHUNK)
        ct_chunk = ct_ref[:, pl.ds(r0, ATT_CHUNK)]
        for h in range(ATT_HEADS):
            pr = jnp.exp2(lg_s[h, pl.ds(r0, ATT_CHUNK), :] - ms[h])
            acc_s[h] += jnp.dot(ct_chunk, pr.astype(BF16), preferred_element_type=F32)
        return carry

    _loop_by_two(nac, pv_chunk, 0)

    for h in range(ATT_HEADS):
        denom = acc_s[h, KV_LATENT:KV_LATENT + 1, :]
        o_t = acc_s[h, 0:KV_LATENT, :] * (1.0 / denom)
        y_t = jnp.dot(wuvt_ref[h], o_t.astype(BF16), preferred_element_type=F32)
        gb = gb_ref[:, h * HEAD_DIM:(h + 1) * HEAD_DIM]
        o_ref[:, h * HEAD_DIM:(h + 1) * HEAD_DIM] = (y_t.T * (gb * _sigmoid(gb))).astype(o_ref.dtype)


def _dsa(pb3, cols_b, w_ukt, wit, kn, c, ct, pa3, gb_blk, bias_tiles, wuvt, topk):
    bsz, s, _ = c.shape
    att_w = ATT_HEADS * HEAD_DIM
    idx_w = IDX_HEADS * IDX_DIM
    assert s % ATT_CHUNK == 0 and s % Q_TILE == 0 and Q_TILE % K_CHUNK == 0 and ATT_CHUNK % K_CHUNK == 0
    assert s // 16 // 4 < 256
    q_blk = _col_block(cols_b["q"], att_w)
    qi_blk = _col_block(cols_b["qi"], idx_w)
    const = lambda shape: pl.BlockSpec(shape, lambda b, i: (0,) * len(shape))
    return pl.pallas_call(
        functools.partial(_dsa_kernel, topk=topk),
        grid=(bsz, s // Q_TILE),
        in_specs=[
            pl.BlockSpec((None, Q_TILE, att_w), lambda b, i: (b, i, q_blk)),
            pl.BlockSpec((None, Q_TILE, idx_w), lambda b, i: (b, i, qi_blk)),
            const(w_ukt.shape),
            pl.BlockSpec((None, IDX_HEADS, Q_TILE), lambda b, i: (b, 0, i)),
            pl.BlockSpec((None, s, IDX_DIM), lambda b, i: (b, 0, 0)),
            pl.BlockSpec((None, s, KV_LATENT), lambda b, i: (b, 0, 0)),
            pl.BlockSpec((None, KV_LATENT + ONES_ROWS, s), lambda b, i: (b, 0, 0)),
            pl.BlockSpec((None, Q_TILE, att_w), lambda b, i: (b, i, gb_blk)),
            const(bias_tiles.shape),
            const(wuvt.shape),
        ],
        out_specs=pl.BlockSpec((None, Q_TILE, att_w), lambda b, i: (b, i, 0)),
        out_shape=jax.ShapeDtypeStruct((bsz, s, att_w), BF16),
        scratch_shapes=[
            pltpu.VMEM((ATT_HEADS, KV_LATENT, Q_TILE), BF16),
            pltpu.VMEM((IDX_HEADS, IDX_DIM, Q_TILE), BF16),
            pltpu.VMEM((s, Q_TILE), F32),
            pltpu.VMEM((s, Q_TILE), BF16),
            pltpu.VMEM((8, Q_TILE), F32),
            pltpu.VMEM((ATT_HEADS, s, Q_TILE), F32),
            pltpu.VMEM((ATT_HEADS, KV_LATENT + ONES_ROWS, Q_TILE), F32),
        ],
        compiler_params=_cparams(("parallel", "arbitrary")),
        name="dsa",
    )(pb3, pb3, w_ukt, wit, kn, c, ct, pa3, bias_tiles, wuvt)


def _outp_kernel(ya_ref, yb_ref, wa_ref, wb_ref, x_ref, g_ref, o_ref, *, final_norm):
    acc = jnp.dot(ya_ref[...], wa_ref[...], preferred_element_type=F32)
    acc = acc + jnp.dot(yb_ref[...], wb_ref[...], preferred_element_type=F32)
    x = x_ref[...] + acc
    if final_norm:
        x = x * lax.rsqrt(jnp.mean(x * x, axis=-1, keepdims=True) + EPS) * g_ref[...]
    o_ref[...] = x


def _outp(ya, yb, w_out, x2, g, final_norm, tm=512):
    m, d = x2.shape
    ka, kb = ya.shape[1], yb.shape[1]
    assert ka == kb and w_out.shape[0] == ka + kb
    return pl.pallas_call(
        functools.partial(_outp_kernel, final_norm=final_norm),
        grid=(m // tm,),
        in_specs=[
            pl.BlockSpec((tm, ka), lambda i: (i, 0)),
            pl.BlockSpec((tm, kb), lambda i: (i, 0)),
            pl.BlockSpec((ka, d), lambda i: (0, 0)),
            pl.BlockSpec((kb, d), lambda i: (1, 0)),
            pl.BlockSpec((tm, d), lambda i: (i, 0)),
            pl.BlockSpec((1, d), lambda i: (0, 0)),
        ],
        out_specs=pl.BlockSpec((tm, d), lambda i: (i, 0)),
        out_shape=jax.ShapeDtypeStruct((m, d), F32),
        compiler_params=_cparams(("parallel",)),
        name="outp",
    )(ya, yb, w_out, w_out, x2, g)


def _t5_bucket(dist):
    n = jnp.maximum(dist, 0)
    max_exact = REL_BUCKETS // 2
    nf = jnp.maximum(n, 1).astype(F32)
    large = max_exact + (jnp.log(nf / max_exact) / np.log(REL_MAX_DIST / max_exact)
                         * (REL_BUCKETS - max_exact)).astype(I32)
    large = jnp.minimum(large, REL_BUCKETS - 1)
    return jnp.where(n < max_exact, n, large)


def _bias_tiles(rel_bias):
    qw = K_CHUNK
    span = K_CHUNK + qw
    table = rel_bias[_t5_bucket(jnp.arange(span + 1, dtype=I32))].astype(F32)
    table = ((table[:span] - table[span:]) * LOG2E).T
    n = span + qw - 1
    a = jnp.concatenate([jnp.zeros((ATT_HEADS, qw - 1), F32), table], axis=1)
    shifted = jnp.tile(a, (1, span + 1))[:, :span * (n + 1)].reshape(ATT_HEADS, span, n + 1)
    tiles = shifted[:, ::-1, :qw].reshape(ATT_HEADS, 2, K_CHUNK, qw)
    return jnp.concatenate([jnp.zeros_like(tiles[:, :1]), tiles], axis=1)


def kernel(x, norm_g, w_in, conv_w, conv_b, lru_wa, lru_ba, lru_wx, lru_bx, lru_lambda, ckv_norm_g, idx_k_norm_g, idx_k_norm_b, w_uk, w_uv, w_out, rel_bias, final_norm_g):
    bsz, s, d = x.shape
    depth = w_in.shape[0]
    lru_w = lru_wa.shape[1] * lru_wa.shape[2]
    att_w = ATT_HEADS * HEAD_DIM
    idx_w = IDX_HEADS * IDX_DIM
    assert REL_MAX_DIST <= K_CHUNK
    assert lru_w == att_w == idx_w and att_w % KV_LATENT == 0
    topk = min(INDEX_TOPK, s // 4)

    o_q = 2 * lru_w
    o_ckv = o_q + att_w
    o_gb = o_ckv + KV_LATENT
    o_qi = o_gb + att_w
    o_ki = o_qi + idx_w
    tn = 512
    cols_a = {"xa": 0, "ga": lru_w, "gb": 2 * lru_w}
    cols_b = {"q": 0, "qi": att_w}
    n_f32, n_bf16 = 3 * lru_w, att_w + idx_w

    bias_tiles = _bias_tiles(rel_bias)
    x2 = x.reshape(bsz * s, d)
    for l in range(depth):
        order = [(0, o_q), (o_gb, o_qi), (o_q, o_ckv), (o_qi, o_ki), (o_ckv, o_gb), (o_ki, w_in.shape[2])]
        pa, pb, c, ct, kn, wit = _proj(x2, norm_g[l][None, :], w_in[l].T, order, ckv_norm_g[l][None, :],
                                       idx_k_norm_g[l][None, :], idx_k_norm_b[l][None, :], n_f32, n_bf16, s, tn=tn)
        pa3 = pa.reshape(bsz, s, -1)
        pb3 = pb.reshape(bsz, s, -1)

        ya = _rglru(pa3, cols_a, conv_w[l], conv_b[l][None, :], (0.5 * lru_wa[l]).astype(BF16),
                    0.5 * lru_ba[l][None, :], (0.5 * lru_wx[l]).astype(BF16), 0.5 * lru_bx[l][None, :],
                    lru_lambda[l][None, :])

        wukt = jnp.transpose(w_uk[l], (0, 2, 1)).astype(BF16)
        wuvt = jnp.transpose(w_uv[l], (0, 2, 1)).astype(BF16)
        yb = _dsa(pb3, cols_b, wukt, wit, kn.reshape(bsz, s, -1), c.reshape(bsz, s, -1), ct, pa3,
                  _col_block(cols_a["gb"], att_w), bias_tiles, wuvt, topk)

        x2 = _outp(ya.reshape(bsz * s, lru_w), yb.reshape(bsz * s, att_w), w_out[l].astype(BF16), x2,
                   final_norm_g[None, :], final_norm=(l == depth - 1))
    return x2.reshape(bsz, s, d)
```

```python
import functools

import numpy as np
import jax
import jax.numpy as jnp
from jax import lax
from jax.experimental import pallas as pl
from jax.experimental.pallas import tpu as pltpu

F32 = jnp.float32
BF16 = jnp.bfloat16
I32 = jnp.int32

LRU_BLOCKS = 8
CONV_WIDTH = 4
LRU_C = 8.0
ATT_HEADS = 8
HEAD_DIM = 128
KV_LATENT = 256
IDX_HEADS = 16
IDX_DIM = 64
INDEX_TOPK = 256
REL_BUCKETS = 32
REL_MAX_DIST = 128
EPS = 1e-6
LOG2E = float(np.log2(np.e))
ONES_ROWS = 16

Q_TILE = 256
K_CHUNK = 128
ATT_CHUNK = 256
NEG = float(np.finfo(np.float32).min)
INT_MIN = -(2 ** 31)
VMEM_LIMIT = 56 * 1024 * 1024


def _cparams(sem):
    return pltpu.CompilerParams(dimension_semantics=sem, vmem_limit_bytes=VMEM_LIMIT)


def _col_block(offset, width):
    assert offset % width == 0
    return offset // width


def _proj_kernel(t_ref, x_ref, g_ref, wlo_ref, whi_ref, tail_ref, cg_ref, kg_ref, kb_ref,
                 oa_ref, ob_ref, c_ref, ct_ref, kn_ref, wit_ref, h_ref, *, na, nb):
    j = pl.program_id(1)
    r = pl.program_id(2)
    nt = (((1,), (1,)), ((), ()))

    def w_tile():
        lo = jnp.where(t_ref[2 * j] < 0, tail_ref[...], wlo_ref[...])
        hi = jnp.where(t_ref[2 * j + 1] < 0, tail_ref[...], whi_ref[...])
        return jnp.concatenate([lo, hi], axis=0).astype(BF16)

    @pl.when(j == 0)
    def _():
        x = x_ref[...]
        y = x * lax.rsqrt(jnp.mean(x * x, axis=-1, keepdims=True) + EPS)
        h_ref[r] = (y * g_ref[...]).astype(BF16)

    @pl.when(j < na)
    def _():
        oa_ref[...] = lax.dot_general(h_ref[r], w_tile(), nt, preferred_element_type=F32)

    @pl.when((j >= na) & (j < na + nb))
    def _():
        ob_ref[...] = lax.dot_general(h_ref[r], w_tile(), nt, preferred_element_type=F32).astype(BF16)

    @pl.when(j >= na + nb)
    def _():
        tail = lax.dot_general(h_ref[r], w_tile(), nt, preferred_element_type=F32)
        ckv = tail[:, :KV_LATENT]
        c = ckv * lax.rsqrt(jnp.mean(ckv * ckv, axis=-1, keepdims=True) + EPS) * cg_ref[...]
        c_ref[...] = c.astype(BF16)
        ct_ref[0:KV_LATENT, :] = c.T.astype(BF16)
        ct_ref[KV_LATENT:, :] = jnp.ones((ONES_ROWS, ct_ref.shape[1]), BF16)
        sm = tail[:, KV_LATENT:KV_LATENT + 128]
        ki = sm[:, :IDX_DIM]
        mu = jnp.mean(ki, axis=-1, keepdims=True)
        var = jnp.mean(jnp.square(ki - mu), axis=-1, keepdims=True)
        kn = (ki - mu) * lax.rsqrt(var + EPS) * kg_ref[...] + kb_ref[...]
        kn_ref[...] = kn.astype(BF16)
        wit_ref[...] = sm.T[IDX_DIM:IDX_DIM + IDX_HEADS, :] * (IDX_HEADS ** -0.5 * IDX_DIM ** -0.5)


def _proj(x2, g, w_t, order, ckv_g, k_g, k_b, n_f32, n_bf16, seq_len, tm=1024, tn=512, group=2):
    m, d = x2.shape
    n = w_t.shape[0]
    th = tn // 2
    n_whole = n // th
    src = []
    for start, stop in order:
        assert start % th == 0 and (stop % th == 0 or stop == n)
        src += list(range(start // th, -(-stop // th)))
    na, nb = n_f32 // tn, n_bf16 // tn
    assert n_f32 % tn == 0 and n_bf16 % tn == 0 and len(src) * th == n_f32 + n_bf16 + tn
    tail = jnp.pad(w_t[n_whole * th:], ((0, (n_whole + 1) * th - n), (0, 0)))
    table = jnp.asarray([blk if blk < n_whole else -1 for blk in src], I32)
    tps = seq_len // tm
    assert seq_len % tm == 0 and tn >= KV_LATENT + 128
    assert (m // tm) % group == 0
    nj = na + nb + 1
    const = lambda shape: pl.BlockSpec(shape, lambda i, j, r, t: (0,) * len(shape))
    last = group - 1
    row = lambda i, r_eff: i * group + r_eff
    per_seq = lambda g: (g // tps, 0, g % tps)
    tail_row = lambda i, j, r: row(i, jnp.where(j < nj - 1, 0, r))
    grid_spec = pltpu.PrefetchScalarGridSpec(
        num_scalar_prefetch=1,
        grid=(m // tm // group, nj, group),
        in_specs=[
            pl.BlockSpec((tm, d), lambda i, j, r, t: (row(i, jnp.where(j == 0, r, last)), 0)),
            const((1, d)),
            pl.BlockSpec((th, d), lambda i, j, r, t: (jnp.maximum(t[2 * j], 0), 0)),
            pl.BlockSpec((th, d), lambda i, j, r, t: (jnp.maximum(t[2 * j + 1], 0), 0)),
            const((th, d)),
            const((1, KV_LATENT)),
            const((1, IDX_DIM)),
            const((1, IDX_DIM)),
        ],
        out_specs=[
            pl.BlockSpec((tm, tn), lambda i, j, r, t: (row(i, jnp.where(j < na, r, last)), jnp.minimum(j, na - 1))),
            pl.BlockSpec((tm, tn), lambda i, j, r, t: (
                row(i, jnp.where(j < na, 0, jnp.where(j < na + nb, r, last))), jnp.clip(j - na, 0, nb - 1))),
            pl.BlockSpec((tm, KV_LATENT), lambda i, j, r, t: (tail_row(i, j, r), 0)),
            pl.BlockSpec((None, KV_LATENT + ONES_ROWS, tm), lambda i, j, r, t: per_seq(tail_row(i, j, r))),
            pl.BlockSpec((tm, IDX_DIM), lambda i, j, r, t: (tail_row(i, j, r), 0)),
            pl.BlockSpec((None, IDX_HEADS, tm), lambda i, j, r, t: per_seq(tail_row(i, j, r))),
        ],
        scratch_shapes=[pltpu.VMEM((group, tm, d), BF16)],
    )
    return pl.pallas_call(
        functools.partial(_proj_kernel, na=na, nb=nb),
        grid_spec=grid_spec,
        out_shape=[
            jax.ShapeDtypeStruct((m, n_f32), F32),
            jax.ShapeDtypeStruct((m, n_bf16), BF16),
            jax.ShapeDtypeStruct((m, KV_LATENT), BF16),
            jax.ShapeDtypeStruct((m // seq_len, KV_LATENT + ONES_ROWS, seq_len), BF16),
            jax.ShapeDtypeStruct((m, IDX_DIM), BF16),
            jax.ShapeDtypeStruct((m // seq_len, IDX_HEADS, seq_len), F32),
        ],
        compiler_params=_cparams(("arbitrary", "arbitrary", "arbitrary")),
        name="proj",
    )(table, x2, g, w_t, w_t, tail, ckv_g, k_g, k_b)


def _sigmoid(v):
    return 0.5 * jnp.tanh(0.5 * v) + 0.5


def _scan_step(a, b, k, axis, idx):
    keep = idx >= k
    a_prev = jnp.where(keep, pltpu.roll(a, k, axis=axis), 1.0)
    b_prev = jnp.where(keep, pltpu.roll(b, k, axis=axis), 0.0)
    return a * a_prev, a * b_prev + b


def _rglru_kernel(xa_ref, ga_ref, cw_ref, cb_ref, wa_ref, ba_ref, wx_ref, bx_ref, lam_ref,
                  o_ref, pad_s, a_s, b_s, c_s):
    s, w = xa_ref.shape
    tile = 8
    n_tiles = s // tile

    pad_s[0:tile, :] = jnp.zeros((tile, w), F32)
    pad_s[tile:tile + s, :] = xa_ref[...]
    acc = pad_s[tile:tile + s, :] * cw_ref[CONV_WIDTH - 1:CONV_WIDTH, :]
    for j in range(CONV_WIDTH - 1):
        back = CONV_WIDTH - 1 - j
        acc = acc + pad_s[tile - back:tile - back + s, :] * cw_ref[j:j + 1, :]
    xc = cb_ref[...] + acc

    xcb = xc.astype(BF16)
    tr = jnp.tanh(jnp.dot(xcb, wa_ref[...], preferred_element_type=F32) + ba_ref[...])
    ti = jnp.tanh(jnp.dot(xcb, wx_ref[...], preferred_element_type=F32) + bx_ref[...])
    i = 0.5 * ti + 0.5
    z = -lam_ref[...]
    softplus = jnp.maximum(z, 0.0) + jnp.log1p(jnp.exp(-jnp.abs(z)))
    half = (-0.5 * LRU_C) * softplus
    log_a = half * tr + half
    a = jnp.exp(log_a)
    m2 = (1.0 + a * a) * jnp.tanh(-log_a)
    mult = jnp.where(m2 > 0.0, m2 * lax.rsqrt(m2), 0.0)
    gated = i * xc
    b_s[...] = mult * gated
    b_s[0:1, :] = gated[0:1, :]

    a3 = a.reshape(n_tiles, tile, w)
    b3 = b_s[...].reshape(n_tiles, tile, w)
    sub = lax.broadcasted_iota(I32, (n_tiles, tile, w), 1)
    for k in (1, 2, 4):
        a3, b3 = _scan_step(a3, b3, k, 1, sub)
    a_s[...] = a3.reshape(s, w)
    b_s[...] = b3.reshape(s, w)

    at = a_s[pl.ds(tile - 1, n_tiles, stride=tile), :]
    bt = b_s[pl.ds(tile - 1, n_tiles, stride=tile), :]
    trow = lax.broadcasted_iota(I32, (n_tiles, w), 0)
    k = 1
    while k < n_tiles:
        at, bt = _scan_step(at, bt, k, 0, trow)
        k *= 2
    c_s[0:tile, :] = jnp.zeros((tile, w), F32)
    c_s[tile:tile + n_tiles, :] = bt

    def apply(t, carry):
        r0 = pl.multiple_of(t * tile, tile)
        before = c_s[pl.ds(tile - 1 + t, tile, stride=0), :]
        h = a_s[pl.ds(r0, tile), :] * before + b_s[pl.ds(r0, tile), :]
        gh = 0.5 * ga_ref[pl.ds(r0, tile), :]
        o_ref[pl.ds(r0, tile), :] = (h * (gh * (jnp.tanh(gh) + 1.0))).astype(o_ref.dtype)
        return carry

    lax.fori_loop(0, n_tiles, apply, 0, unroll=8)


def _rglru(pa3, cols, conv_w, conv_b, wa, ba, wx, bx, lam):
    bsz, s, _ = pa3.shape
    g, w = wa.shape[0], wa.shape[-1]
    xa_blk = _col_block(cols["xa"], w)
    ga_blk = _col_block(cols["ga"], w)
    vec = lambda: pl.BlockSpec((1, w), lambda b, j: (0, j))
    return pl.pallas_call(
        _rglru_kernel,
        grid=(bsz, g),
        in_specs=[
            pl.BlockSpec((None, s, w), lambda b, j: (b, 0, xa_blk + j)),
            pl.BlockSpec((None, s, w), lambda b, j: (b, 0, ga_blk + j)),
            pl.BlockSpec((CONV_WIDTH, w), lambda b, j: (0, j)),
            vec(),
            pl.BlockSpec((None, w, w), lambda b, j: (j, 0, 0)),
            vec(),
            pl.BlockSpec((None, w, w), lambda b, j: (j, 0, 0)),
            vec(),
            vec(),
        ],
        out_specs=pl.BlockSpec((None, s, w), lambda b, j: (b, 0, j)),
        out_shape=jax.ShapeDtypeStruct((bsz, s, g * w), BF16),
        scratch_shapes=[pltpu.VMEM((s + 8, w), F32), pltpu.VMEM((s, w), F32), pltpu.VMEM((s, w), F32),
                        pltpu.VMEM((s // 8 + 8, w), F32)],
        compiler_params=_cparams(("parallel", "parallel")),
        name="rglru",
    )(pa3, pa3, conv_w, conv_b, wa, ba, wx, bx, lam)


def _tree_sum(parts):
    while len(parts) > 1:
        paired = [parts[i] + parts[i + 1] for i in range(0, len(parts) - 1, 2)]
        parts = paired + ([parts[-1]] if len(parts) % 2 else [])
    return parts[0]


def _sortable_to_f32(u):
    key = u ^ INT_MIN
    return lax.bitcast_convert_type(key ^ ((key >> 31) & 0x7FFFFFFF), F32)


def _count_ge(ref, rows, cand, pack):
    chains = 4
    one, zero = jnp.ones((), ref.dtype), jnp.zeros((), ref.dtype)
    accs = [None] * chains
    for r in range(rows // pack):
        hit = jnp.where(ref[r * pack:(r + 1) * pack, :] >= cand, one, zero)
        accs[r % chains] = hit if accs[r % chains] is None else accs[r % chains] + hit
    parts = [a.astype(F32) for a in accs if a is not None]
    return jnp.sum(_tree_sum(parts), axis=0, keepdims=True)


def _kth_largest(score_ref, score16_ref, rows, k):
    def step16(i, u):
        cand = u | (jnp.int32(1) << (15 - i))
        cand_f = _sortable_to_f32(cand << 16).astype(BF16)
        return jnp.where(_count_ge(score16_ref, rows, cand_f, 16) >= k, cand, u)

    hi = lax.fori_loop(0, 16, step16, jnp.zeros((1, Q_TILE), I32))
    keeps = _count_ge(score_ref, rows, _sortable_to_f32(hi << 16), 8) >= k
    hi = jnp.where(keeps, hi, jnp.maximum(hi - 1, 0))

    def step(i, u):
        cand = u | (jnp.int32(1) << (15 - i))
        return jnp.where(_count_ge(score_ref, rows, _sortable_to_f32(cand), 8) >= k, cand, u)

    return _sortable_to_f32(lax.fori_loop(0, 16, step, hi << 16))


def _loop_by_two(n, body, init):
    def trips(start, count, width, carry):
        def group(i, c):
            for t in range(width):
                c = body(start + width * i + t, c)
            return c
        return lax.fori_loop(0, count, group, carry)

    carry = trips(0, n // 4, 4, init)
    carry = trips(4 * (n // 4), (n % 4) // 2, 2, carry)
    return trips(2 * (n // 2), n % 2, 1, carry)


def _dsa_kernel(q_ref, qi_ref, wukt_ref, wit_ref, kn_ref, c_ref, ct_ref, gb_ref, bias_ref, wuvt_ref,
                o_ref, qat_ref, qit_ref, score_s, score16_s, thr_s, lg_s, acc_s, topk):
    qb = pl.program_id(1)

    qt = q_ref[...].T
    scale = HEAD_DIM ** -0.5 * LOG2E
    for h in range(ATT_HEADS):
        qa = jnp.dot(wukt_ref[h], qt[h * HEAD_DIM:(h + 1) * HEAD_DIM], preferred_element_type=F32)
        qat_ref[h] = (qa * scale).astype(BF16)
    qit = qi_ref[...].T
    for h in range(IDX_HEADS):
        qit_ref[h] = qit[h * IDX_DIM:(h + 1) * IDX_DIM, :]
    q_tiles = Q_TILE // K_CHUNK
    nkc = (qb + 1) * q_tiles
    nac = (nkc * K_CHUNK + ATT_CHUNK - 1) // ATT_CHUNK
    tiles = ATT_CHUNK // K_CHUNK

    kiota = lax.broadcasted_iota(I32, (K_CHUNK, Q_TILE), 0)
    qpos = qb * Q_TILE + lax.broadcasted_iota(I32, (K_CHUNK, Q_TILE), 1)

    def score_chunk(ac, carry):
        for t in range(tiles):
            k0 = pl.multiple_of(ac * ATT_CHUNK + t * K_CHUNK, K_CHUNK)
            kn = kn_ref[pl.ds(k0, K_CHUNK), :]
            acc = jnp.zeros((K_CHUNK, Q_TILE), F32)
            for h in range(IDX_HEADS):
                sc = jnp.dot(kn, qit_ref[h], preferred_element_type=F32)
                acc = acc + jnp.maximum(sc, 0.0) * wit_ref[h:h + 1, :]
            masked = jnp.where(kiota + k0 <= qpos, acc, -jnp.inf)
            score_s[pl.ds(k0, K_CHUNK), :] = masked
            score16_s[pl.ds(k0, K_CHUNK), :] = masked.astype(BF16)
        return carry

    _loop_by_two(nac, score_chunk, 0)

    for v in range(1, score_s.shape[0] // ATT_CHUNK + 1):
        @pl.when(nac == v)
        def _(rows=v * ATT_CHUNK):
            thr = _kth_largest(score_s, score16_s, rows, float(topk))
            thr = jnp.where(thr >= NEG, thr, NEG)
            thr_s[...] = jnp.broadcast_to(thr, thr_s.shape)

    thr = thr_s[0:1, :]

    def logit_chunk(ac, m8s):
        r0 = pl.multiple_of(ac * ATT_CHUNK, ATT_CHUNK)
        c_chunk = c_ref[pl.ds(r0, ATT_CHUNK), :]
        mbias = jnp.where(score_s[pl.ds(r0, ATT_CHUNK), :] >= thr, 0.0, NEG)
        near = [[jnp.clip(ac * tiles + t - (qb * q_tiles + j) + 2, 0, 2) for j in range(q_tiles)]
                for t in range(tiles)]
        out = []
        for h in range(ATT_HEADS):
            lg = jnp.dot(c_chunk, qat_ref[h], preferred_element_type=F32) + mbias
            lg = jnp.concatenate(
                [lg[t * K_CHUNK:(t + 1) * K_CHUNK]
                 + jnp.concatenate([bias_ref[h, near[t][j]] for j in range(q_tiles)], axis=1)
                 for t in range(tiles)], axis=0)
            lg_s[h, pl.ds(r0, ATT_CHUNK), :] = lg
            out.append(jnp.maximum(m8s[h], jnp.max(lg.reshape(ATT_CHUNK // 8, 8, Q_TILE), axis=0)))
        return tuple(out)

    m8s = _loop_by_two(nac, logit_chunk, tuple(jnp.full((8, Q_TILE), NEG, F32) for _ in range(ATT_HEADS)))
    ms = [jnp.max(m8, axis=0, keepdims=True) for m8 in m8s]

    acc_s[...] = jnp.zeros(acc_s.shape, F32)

    def pv_chunk(ac, carry):
        r0 = pl.multiple_of(ac * ATT_CHUNK, ATT_CHUNK)
        ct_chunk = ct_ref[:, pl.ds(r0, ATT_CHUNK)]
        for h in range(ATT_HEADS):
            pr = jnp.exp2(lg_s[h, pl.ds(r0, ATT_CHUNK), :] - ms[h])
            acc_s[h] += jnp.dot(ct_chunk, pr.astype(BF16), preferred_element_type=F32)
        return carry

    _loop_by_two(nac, pv_chunk, 0)

    for h in range(ATT_HEADS):
        denom = acc_s[h, KV_LATENT:KV_LATENT + 1, :]
        o_t = acc_s[h, 0:KV_LATENT, :] * (1.0 / denom)
        y_t = jnp.dot(wuvt_ref[h], o_t.astype(BF16), preferred_element_type=F32)
        gb = gb_ref[:, h * HEAD_DIM:(h + 1) * HEAD_DIM]
        o_ref[:, h * HEAD_DIM:(h + 1) * HEAD_DIM] = (y_t.T * (gb * _sigmoid(gb))).astype(o_ref.dtype)


def _dsa(pb3, cols_b, w_ukt, wit, kn, c, ct, pa3, gb_blk, bias_tiles, wuvt, topk):
    bsz, s, _ = c.shape
    att_w = ATT_HEADS * HEAD_DIM
    idx_w = IDX_HEADS * IDX_DIM
    assert s % ATT_CHUNK == 0 and s % Q_TILE == 0 and Q_TILE % K_CHUNK == 0 and ATT_CHUNK % K_CHUNK == 0
    assert s // 16 // 4 < 256
    q_blk = _col_block(cols_b["q"], att_w)
    qi_blk = _col_block(cols_b["qi"], idx_w)
    const = lambda shape: pl.BlockSpec(shape, lambda b, i: (0,) * len(shape))
    return pl.pallas_call(
        functools.partial(_dsa_kernel, topk=topk),
        grid=(bsz, s // Q_TILE),
        in_specs=[
            pl.BlockSpec((None, Q_TILE, att_w), lambda b, i: (b, i, q_blk)),
            pl.BlockSpec((None, Q_TILE, idx_w), lambda b, i: (b, i, qi_blk)),
            const(w_ukt.shape),
            pl.BlockSpec((None, IDX_HEADS, Q_TILE), lambda b, i: (b, 0, i)),
            pl.BlockSpec((None, s, IDX_DIM), lambda b, i: (b, 0, 0)),
            pl.BlockSpec((None, s, KV_LATENT), lambda b, i: (b, 0, 0)),
            pl.BlockSpec((None, KV_LATENT + ONES_ROWS, s), lambda b, i: (b, 0, 0)),
            pl.BlockSpec((None, Q_TILE, att_w), lambda b, i: (b, i, gb_blk)),
            const(bias_tiles.shape),
            const(wuvt.shape),
        ],
        out_specs=pl.BlockSpec((None, Q_TILE, att_w), lambda b, i: (b, i, 0)),
        out_shape=jax.ShapeDtypeStruct((bsz, s, att_w), BF16),
        scratch_shapes=[
            pltpu.VMEM((ATT_HEADS, KV_LATENT, Q_TILE), BF16),
            pltpu.VMEM((IDX_HEADS, IDX_DIM, Q_TILE), BF16),
            pltpu.VMEM((s, Q_TILE), F32),
            pltpu.VMEM((s, Q_TILE), BF16),
            pltpu.VMEM((8, Q_TILE), F32),
            pltpu.VMEM((ATT_HEADS, s, Q_TILE), F32),
            pltpu.VMEM((ATT_HEADS, KV_LATENT + ONES_ROWS, Q_TILE), F32),
        ],
        compiler_params=_cparams(("parallel", "arbitrary")),
        name="dsa",
    )(pb3, pb3, w_ukt, wit, kn, c, ct, pa3, bias_tiles, wuvt)


def _outp_kernel(ya_ref, yb_ref, wa_ref, wb_ref, x_ref, g_ref, o_ref, *, final_norm):
    acc = jnp.dot(ya_ref[...], wa_ref[...], preferred_element_type=F32)
    acc = acc + jnp.dot(yb_ref[...], wb_ref[...], preferred_element_type=F32)
    x = x_ref[...] + acc
    if final_norm:
        x = x * lax.rsqrt(jnp.mean(x * x, axis=-1, keepdims=True) + EPS) * g_ref[...]
    o_ref[...] = x


def _outp(ya, yb, w_out, x2, g, final_norm, tm=512):
    m, d = x2.shape
    ka, kb = ya.shape[1], yb.shape[1]
    assert ka == kb and w_out.shape[0] == ka + kb
    return pl.pallas_call(
        functools.partial(_outp_kernel, final_norm=final_norm),
        grid=(m // tm,),
        in_specs=[
            pl.BlockSpec((tm, ka), lambda i: (i, 0)),
            pl.BlockSpec((tm, kb), lambda i: (i, 0)),
            pl.BlockSpec((ka, d), lambda i: (0, 0)),
            pl.BlockSpec((kb, d), lambda i: (1, 0)),
            pl.BlockSpec((tm, d), lambda i: (i, 0)),
            pl.BlockSpec((1, d), lambda i: (0, 0)),
        ],
        out_specs=pl.BlockSpec((tm, d), lambda i: (i, 0)),
        out_shape=jax.ShapeDtypeStruct((m, d), F32),
        compiler_params=_cparams(("parallel",)),
        name="outp",
    )(ya, yb, w_out, w_out, x2, g)


def _t5_bucket(dist):
    n = jnp.maximum(dist, 0)
    max_exact = REL_BUCKETS // 2
    nf = jnp.maximum(n, 1).astype(F32)
    large = max_exact + (jnp.log(nf / max_exact) / np.log(REL_MAX_DIST / max_exact)
                         * (REL_BUCKETS - max_exact)).astype(I32)
    large = jnp.minimum(large, REL_BUCKETS - 1)
    return jnp.where(n < max_exact, n, large)


def _bias_tiles(rel_bias):
    qw = K_CHUNK
    span = K_CHUNK + qw
    table = rel_bias[_t5_bucket(jnp.arange(span + 1, dtype=I32))].astype(F32)
    table = ((table[:span] - table[span:]) * LOG2E).T
    n = span + qw - 1
    a = jnp.concatenate([jnp.zeros((ATT_HEADS, qw - 1), F32), table], axis=1)
    shifted = jnp.tile(a, (1, span + 1))[:, :span * (n + 1)].reshape(ATT_HEADS, span, n + 1)
    tiles = shifted[:, ::-1, :qw].reshape(ATT_HEADS, 2, K_CHUNK, qw)
    return jnp.concatenate([jnp.zeros_like(tiles[:, :1]), tiles], axis=1)


def kernel(x, norm_g, w_in, conv_w, conv_b, lru_wa, lru_ba, lru_wx, lru_bx, lru_lambda, ckv_norm_g, idx_k_norm_g, idx_k_norm_b, w_uk, w_uv, w_out, rel_bias, final_norm_g):
    bsz, s, d = x.shape
    depth = w_in.shape[0]
    lru_w = lru_wa.shape[1] * lru_wa.shape[2]
    att_w = ATT_HEADS * HEAD_DIM
    idx_w = IDX_HEADS * IDX_DIM
    assert REL_MAX_DIST <= K_CHUNK
    assert lru_w == att_w == idx_w and att_w % KV_LATENT == 0
    topk = min(INDEX_TOPK, s // 4)

    o_q = 2 * lru_w
    o_ckv = o_q + att_w
    o_gb = o_ckv + KV_LATENT
    o_qi = o_gb + att_w
    o_ki = o_qi + idx_w
    tn = 512
    cols_a = {"xa": 0, "ga": lru_w, "gb": 2 * lru_w}
    cols_b = {"q": 0, "qi": att_w}
    n_f32, n_bf16 = 3 * lru_w, att_w + idx_w

    bias_tiles = _bias_tiles(rel_bias)
    x2 = x.reshape(bsz * s, d)
    for l in range(depth):
        order = [(0, o_q), (o_gb, o_qi), (o_q, o_ckv), (o_qi, o_ki), (o_ckv, o_gb), (o_ki, w_in.shape[2])]
        pa, pb, c, ct, kn, wit = _proj(x2, norm_g[l][None, :], w_in[l].T, order, ckv_norm_g[l][None, :],
                                       idx_k_norm_g[l][None, :], idx_k_norm_b[l][None, :], n_f32, n_bf16, s, tn=tn)
        pa3 = pa.reshape(bsz, s, -1)
        pb3 = pb.reshape(bsz, s, -1)

        ya = _rglru(pa3, cols_a, conv_w[l], conv_b[l][None, :], (0.5 * lru_wa[l]).astype(BF16),
                    0.5 * lru_ba[l][None, :], (0.5 * lru_wx[l]).astype(BF16), 0.5 * lru_bx[l][None, :],
                    lru_lambda[l][None, :])

        wukt = jnp.transpose(w_uk[l], (0, 2, 1)).astype(BF16)
        wuvt = jnp.transpose(w_uv[l], (0, 2, 1)).astype(BF16)
        yb = _dsa(pb3, cols_b, wukt, wit, kn.reshape(bsz, s, -1), c.reshape(bsz, s, -1), ct, pa3,
                  _col_block(cols_a["gb"], att_w), bias_tiles, wuvt, topk)

        x2 = _outp(ya.reshape(bsz * s, lru_w), yb.reshape(bsz * s, att_w), w_out[l].astype(BF16), x2,
                   final_norm_g[None, :], final_norm=(l == depth - 1))
    return x2.reshape(bsz, s, d)
```

```python
import functools

import numpy as np
import jax
import jax.numpy as jnp
from jax import lax
from jax.experimental import pallas as pl
from jax.experimental.pallas import tpu as pltpu

F32 = jnp.float32
BF16 = jnp.bfloat16
I32 = jnp.int32

LRU_BLOCKS = 8
CONV_WIDTH = 4
LRU_C = 8.0
ATT_HEADS = 8
HEAD_DIM = 128
KV_LATENT = 256
IDX_HEADS = 16
IDX_DIM = 64
INDEX_TOPK = 256
REL_BUCKETS = 32
REL_MAX_DIST = 128
EPS = 1e-6
LOG2E = float(np.log2(np.e))
ONES_ROWS = 16

Q_TILE = 256
K_CHUNK = 128
ATT_CHUNK = 256
NEG = float(np.finfo(np.float32).min)
INT_MIN = -(2 ** 31)
VMEM_LIMIT = 56 * 1024 * 1024


def _cparams(sem):
    return pltpu.CompilerParams(dimension_semantics=sem, vmem_limit_bytes=VMEM_LIMIT)


def _col_block(offset, width):
    assert offset % width == 0
    return offset // width


def _proj_kernel(t_ref, x_ref, g_ref, wlo_ref, whi_ref, tail_ref, cg_ref, kg_ref, kb_ref,
                 oa_ref, ob_ref, c_ref, ct_ref, kn_ref, wit_ref, h_ref, *, na, nb):
    j = pl.program_id(1)
    r = pl.program_id(2)
    nt = (((1,), (1,)), ((), ()))

    def w_tile():
        lo = jnp.where(t_ref[2 * j] < 0, tail_ref[...], wlo_ref[...])
        hi = jnp.where(t_ref[2 * j + 1] < 0, tail_ref[...], whi_ref[...])
        return jnp.concatenate([lo, hi], axis=0).astype(BF16)

    @pl.when(j == 0)
    def _():
        x = x_ref[...]
        y = x * lax.rsqrt(jnp.mean(x * x, axis=-1, keepdims=True) + EPS)
        h_ref[r] = (y * g_ref[...]).astype(BF16)

    @pl.when(j < na)
    def _():
        oa_ref[...] = lax.dot_general(h_ref[r], w_tile(), nt, preferred_element_type=F32)

    @pl.when((j >= na) & (j < na + nb))
    def _():
        ob_ref[...] = lax.dot_general(h_ref[r], w_tile(), nt, preferred_element_type=F32).astype(BF16)

    @pl.when(j >= na + nb)
    def _():
        used = KV_LATENT + 128
        tail = lax.dot_general(h_ref[r], w_tile()[:used], nt, preferred_element_type=F32)
        ckv = tail[:, :KV_LATENT]
        c = ckv * lax.rsqrt(jnp.mean(ckv * ckv, axis=-1, keepdims=True) + EPS) * cg_ref[...]
        c_ref[...] = c.astype(BF16)
        ct_ref[0:KV_LATENT, :] = c.T.astype(BF16)
        ct_ref[KV_LATENT:, :] = jnp.ones((ONES_ROWS, ct_ref.shape[1]), BF16)
        sm = tail[:, KV_LATENT:KV_LATENT + 128]
        ki = sm[:, :IDX_DIM]
        mu = jnp.mean(ki, axis=-1, keepdims=True)
        var = jnp.mean(jnp.square(ki - mu), axis=-1, keepdims=True)
        kn = (ki - mu) * lax.rsqrt(var + EPS) * kg_ref[...] + kb_ref[...]
        kn_ref[...] = kn.astype(BF16)
        wit_ref[...] = sm.T[IDX_DIM:IDX_DIM + IDX_HEADS, :] * (IDX_HEADS ** -0.5 * IDX_DIM ** -0.5)


def _proj(x2, g, w_t, order, ckv_g, k_g, k_b, n_f32, n_bf16, seq_len, tm=1024, tn=512, group=2):
    m, d = x2.shape
    n = w_t.shape[0]
    th = tn // 2
    n_whole = n // th
    src = []
    for start, stop in order:
        assert start % th == 0 and (stop % th == 0 or stop == n)
        src += list(range(start // th, -(-stop // th)))
    na, nb = n_f32 // tn, n_bf16 // tn
    assert n_f32 % tn == 0 and n_bf16 % tn == 0 and len(src) * th == n_f32 + n_bf16 + tn
    tail = jnp.pad(w_t[n_whole * th:], ((0, (n_whole + 1) * th - n), (0, 0)))
    table = jnp.asarray([blk if blk < n_whole else -1 for blk in src], I32)
    tps = seq_len // tm
    assert seq_len % tm == 0 and tn >= KV_LATENT + 128
    assert (m // tm) % group == 0
    nj = na + nb + 1
    const = lambda shape: pl.BlockSpec(shape, lambda i, j, r, t: (0,) * len(shape))
    last = group - 1
    row = lambda i, r_eff: i * group + r_eff
    per_seq = lambda g: (g // tps, 0, g % tps)
    tail_row = lambda i, j, r: row(i, jnp.where(j < nj - 1, 0, r))
    grid_spec = pltpu.PrefetchScalarGridSpec(
        num_scalar_prefetch=1,
        grid=(m // tm // group, nj, group),
        in_specs=[
            pl.BlockSpec((tm, d), lambda i, j, r, t: (row(i, jnp.where(j == 0, r, last)), 0)),
            const((1, d)),
            pl.BlockSpec((th, d), lambda i, j, r, t: (jnp.maximum(t[2 * j], 0), 0)),
            pl.BlockSpec((th, d), lambda i, j, r, t: (jnp.maximum(t[2 * j + 1], 0), 0)),
            const((th, d)),
            const((1, KV_LATENT)),
            const((1, IDX_DIM)),
            const((1, IDX_DIM)),
        ],
        out_specs=[
            pl.BlockSpec((tm, tn), lambda i, j, r, t: (row(i, jnp.where(j < na, r, last)), jnp.minimum(j, na - 1))),
            pl.BlockSpec((tm, tn), lambda i, j, r, t: (
                row(i, jnp.where(j < na, 0, jnp.where(j < na + nb, r, last))), jnp.clip(j - na, 0, nb - 1))),
            pl.BlockSpec((tm, KV_LATENT), lambda i, j, r, t: (tail_row(i, j, r), 0)),
            pl.BlockSpec((None, KV_LATENT + ONES_ROWS, tm), lambda i, j, r, t: per_seq(tail_row(i, j, r))),
            pl.BlockSpec((tm, IDX_DIM), lambda i, j, r, t: (tail_row(i, j, r), 0)),
            pl.BlockSpec((None, IDX_HEADS, tm), lambda i, j, r, t: per_seq(tail_row(i, j, r))),
        ],
        scratch_shapes=[pltpu.VMEM((group, tm, d), BF16)],
    )
    return pl.pallas_call(
        functools.partial(_proj_kernel, na=na, nb=nb),
        grid_spec=grid_spec,
        out_shape=[
            jax.ShapeDtypeStruct((m, n_f32), F32),
            jax.ShapeDtypeStruct((m, n_bf16), BF16),
            jax.ShapeDtypeStruct((m, KV_LATENT), BF16),
            jax.ShapeDtypeStruct((m // seq_len, KV_LATENT + ONES_ROWS, seq_len), BF16),
            jax.ShapeDtypeStruct((m, IDX_DIM), BF16),
            jax.ShapeDtypeStruct((m // seq_len, IDX_HEADS, seq_len), F32),
        ],
        compiler_params=_cparams(("arbitrary", "arbitrary", "arbitrary")),
        name="proj",
    )(table, x2, g, w_t, w_t, tail, ckv_g, k_g, k_b)


def _sigmoid(v):
    return 0.5 * jnp.tanh(0.5 * v) + 0.5


def _scan_step(a, b, k, axis, idx):
    keep = idx >= k
    a_prev = jnp.where(keep, pltpu.roll(a, k, axis=axis), 1.0)
    b_prev = jnp.where(keep, pltpu.roll(b, k, axis=axis), 0.0)
    return a * a_prev, a * b_prev + b


def _rglru_kernel(xa_ref, ga_ref, cw_ref, cb_ref, wa_ref, ba_ref, wx_ref, bx_ref, lam_ref,
                  o_ref, pad_s, a_s, b_s, c_s):
    s, w = xa_ref.shape
    tile = 8
    n_tiles = s // tile

    pad_s[0:tile, :] = jnp.zeros((tile, w), F32)
    pad_s[tile:tile + s, :] = xa_ref[...]
    acc = pad_s[tile:tile + s, :] * cw_ref[CONV_WIDTH - 1:CONV_WIDTH, :]
    for j in range(CONV_WIDTH - 1):
        back = CONV_WIDTH - 1 - j
        acc = acc + pad_s[tile - back:tile - back + s, :] * cw_ref[j:j + 1, :]
    xc = cb_ref[...] + acc

    xcb = xc.astype(BF16)
    tr = jnp.tanh(jnp.dot(xcb, wa_ref[...], preferred_element_type=F32) + ba_ref[...])
    ti = jnp.tanh(jnp.dot(xcb, wx_ref[...], preferred_element_type=F32) + bx_ref[...])
    i = 0.5 * ti + 0.5
    z = -lam_ref[...]
    softplus = jnp.maximum(z, 0.0) + jnp.log1p(jnp.exp(-jnp.abs(z)))
    half = (-0.5 * LRU_C) * softplus
    log_a = half * tr + half
    a = jnp.exp(log_a)
    m2 = (1.0 + a * a) * jnp.tanh(-log_a)
    mult = jnp.where(m2 > 0.0, m2 * lax.rsqrt(m2), 0.0)
    gated = i * xc
    b_s[...] = mult * gated
    b_s[0:1, :] = gated[0:1, :]

    a3 = a.reshape(n_tiles, tile, w)
    b3 = b_s[...].reshape(n_tiles, tile, w)
    sub = lax.broadcasted_iota(I32, (n_tiles, tile, w), 1)
    for k in (1, 2, 4):
        a3, b3 = _scan_step(a3, b3, k, 1, sub)
    a_s[...] = a3.reshape(s, w)
    b_s[...] = b3.reshape(s, w)

    at = a_s[pl.ds(tile - 1, n_tiles, stride=tile), :]
    bt = b_s[pl.ds(tile - 1, n_tiles, stride=tile), :]
    trow = lax.broadcasted_iota(I32, (n_tiles, w), 0)
    k = 1
    while k < n_tiles:
        at, bt = _scan_step(at, bt, k, 0, trow)
        k *= 2
    c_s[0:tile, :] = jnp.zeros((tile, w), F32)
    c_s[tile:tile + n_tiles, :] = bt

    def apply(t, carry):
        r0 = pl.multiple_of(t * tile, tile)
        before = c_s[pl.ds(tile - 1 + t, tile, stride=0), :]
        h = a_s[pl.ds(r0, tile), :] * before + b_s[pl.ds(r0, tile), :]
        gh = 0.5 * ga_ref[pl.ds(r0, tile), :]
        o_ref[pl.ds(r0, tile), :] = (h * (gh * (jnp.tanh(gh) + 1.0))).astype(o_ref.dtype)
        return carry

    lax.fori_loop(0, n_tiles, apply, 0, unroll=8)


def _rglru(pa3, cols, conv_w, conv_b, wa, ba, wx, bx, lam):
    bsz, s, _ = pa3.shape
    g, w = wa.shape[0], wa.shape[-1]
    xa_blk = _col_block(cols["xa"], w)
    ga_blk = _col_block(cols["ga"], w)
    vec = lambda: pl.BlockSpec((1, w), lambda b, j: (0, j))
    return pl.pallas_call(
        _rglru_kernel,
        grid=(bsz, g),
        in_specs=[
            pl.BlockSpec((None, s, w), lambda b, j: (b, 0, xa_blk + j)),
            pl.BlockSpec((None, s, w), lambda b, j: (b, 0, ga_blk + j)),
            pl.BlockSpec((CONV_WIDTH, w), lambda b, j: (0, j)),
            vec(),
            pl.BlockSpec((None, w, w), lambda b, j: (j, 0, 0)),
            vec(),
            pl.BlockSpec((None, w, w), lambda b, j: (j, 0, 0)),
            vec(),
            vec(),
        ],
        out_specs=pl.BlockSpec((None, s, w), lambda b, j: (b, 0, j)),
        out_shape=jax.ShapeDtypeStruct((bsz, s, g * w), BF16),
        scratch_shapes=[pltpu.VMEM((s + 8, w), F32), pltpu.VMEM((s, w), F32), pltpu.VMEM((s, w), F32),
                        pltpu.VMEM((s // 8 + 8, w), F32)],
        compiler_params=_cparams(("parallel", "parallel")),
        name="rglru",
    )(pa3, pa3, conv_w, conv_b, wa, ba, wx, bx, lam)


def _tree_sum(parts):
    while len(parts) > 1:
        paired = [parts[i] + parts[i + 1] for i in range(0, len(parts) - 1, 2)]
        parts = paired + ([parts[-1]] if len(parts) % 2 else [])
    return parts[0]


def _sortable_to_f32(u):
    key = u ^ INT_MIN
    return lax.bitcast_convert_type(key ^ ((key >> 31) & 0x7FFFFFFF), F32)


def _count_ge(ref, rows, cand, pack):
    chains = 4
    one, zero = jnp.ones((), ref.dtype), jnp.zeros((), ref.dtype)
    accs = [None] * chains
    for r in range(rows // pack):
        hit = jnp.where(ref[r * pack:(r + 1) * pack, :] >= cand, one, zero)
        accs[r % chains] = hit if accs[r % chains] is None else accs[r % chains] + hit
    parts = [a.astype(F32) for a in accs if a is not None]
    return jnp.sum(_tree_sum(parts), axis=0, keepdims=True)


def _kth_largest(score_ref, score16_ref, rows, k):
    def step16(i, u):
        cand = u | (jnp.int32(1) << (15 - i))
        cand_f = _sortable_to_f32(cand << 16).astype(BF16)
        return jnp.where(_count_ge(score16_ref, rows, cand_f, 16) >= k, cand, u)

    hi = lax.fori_loop(0, 16, step16, jnp.zeros((1, Q_TILE), I32))
    at_hi = _count_ge(score_ref, rows, _sortable_to_f32(hi << 16), 8)
    keeps = at_hi >= k
    hi = jnp.where(keeps, hi, jnp.maximum(hi - 1, 0))

    per_trip = 4

    def trip(state):
        i, u, at_u, _ = state
        for t in range(per_trip):
            cand = u | (jnp.int32(1) << (15 - (i + t)))
            cnt = _count_ge(score_ref, rows, _sortable_to_f32(cand), 8)
            take = cnt >= k
            u = jnp.where(take, cand, u)
            at_u = jnp.where(take, cnt, at_u)
        open_lanes = jnp.sum((at_u != k).astype(I32))
        return i + per_trip, u, at_u, open_lanes

    unknown = jnp.full((1, Q_TILE), -1.0, F32)
    state = (jnp.int32(0), hi << 16, jnp.where(keeps, at_hi, unknown), jnp.int32(1))
    _, u, _, _ = lax.while_loop(lambda st: (st[0] < 16) & (st[3] > 0), trip, state)
    return _sortable_to_f32(u)


def _loop_by_two(n, body, init):
    def trips(start, count, width, carry):
        def group(i, c):
            for t in range(width):
                c = body(start + width * i + t, c)
            return c
        return lax.fori_loop(0, count, group, carry)

    carry = trips(0, n // 4, 4, init)
    carry = trips(4 * (n // 4), (n % 4) // 2, 2, carry)
    return trips(2 * (n // 2), n % 2, 1, carry)


def _dsa_kernel(q_ref, qi_ref, wukt_ref, wit_ref, kn_ref, c_ref, ct_ref, gb_ref, bias_ref, wuvt_ref,
                o_ref, qat_ref, qit_ref, score_s, score16_s, thr_s, lg_s, acc_s, topk):
    qb = pl.program_id(1)

    qt = q_ref[...].T
    scale = HEAD_DIM ** -0.5 * LOG2E
    for h in range(ATT_HEADS):
        qa = jnp.dot(wukt_ref[h], qt[h * HEAD_DIM:(h + 1) * HEAD_DIM], preferred_element_type=F32)
        qat_ref[h] = (qa * scale).astype(BF16)
    qit = qi_ref[...].T
    for h in range(IDX_HEADS):
        qit_ref[h] = qit[h * IDX_DIM:(h + 1) * IDX_DIM, :]
    q_tiles = Q_TILE // K_CHUNK
    nkc = (qb + 1) * q_tiles
    nac = (nkc * K_CHUNK + ATT_CHUNK - 1) // ATT_CHUNK
    tiles = ATT_CHUNK // K_CHUNK

    kiota = lax.broadcasted_iota(I32, (K_CHUNK, Q_TILE), 0)
    qpos = qb * Q_TILE + lax.broadcasted_iota(I32, (K_CHUNK, Q_TILE), 1)

    def score_chunk(ac, carry):
        for t in range(tiles):
            k0 = pl.multiple_of(ac * ATT_CHUNK + t * K_CHUNK, K_CHUNK)
            kn = kn_ref[pl.ds(k0, K_CHUNK), :]
            acc = jnp.zeros((K_CHUNK, Q_TILE), F32)
            for h in range(IDX_HEADS):
                sc = jnp.dot(kn, qit_ref[h], preferred_element_type=F32)
                acc = acc + jnp.maximum(sc, 0.0) * wit_ref[h:h + 1, :]
            masked = jnp.where(kiota + k0 <= qpos, acc, -jnp.inf)
            score_s[pl.ds(k0, K_CHUNK), :] = masked
            score16_s[pl.ds(k0, K_CHUNK), :] = masked.astype(BF16)
        return carry

    _loop_by_two(nac, score_chunk, 0)

    for v in range(1, score_s.shape[0] // ATT_CHUNK + 1):
        @pl.when(nac == v)
        def _(rows=v * ATT_CHUNK):
            thr = _kth_largest(score_s, score16_s, rows, float(topk))
            thr = jnp.where(thr >= NEG, thr, NEG)
            thr_s[...] = jnp.broadcast_to(thr, thr_s.shape)

    thr = thr_s[0:1, :]

    def logit_chunk(ac, m8s):
        r0 = pl.multiple_of(ac * ATT_CHUNK, ATT_CHUNK)
        c_chunk = c_ref[pl.ds(r0, ATT_CHUNK), :]
        mbias = jnp.where(score_s[pl.ds(r0, ATT_CHUNK), :] >= thr, 0.0, NEG)
        near = [[jnp.clip(ac * tiles + t - (qb * q_tiles + j) + 2, 0, 2) for j in range(q_tiles)]
                for t in range(tiles)]
        out = []
        for h in range(ATT_HEADS):
            lg = jnp.dot(c_chunk, qat_ref[h], preferred_element_type=F32) + mbias
            lg = jnp.concatenate(
                [lg[t * K_CHUNK:(t + 1) * K_CHUNK]
                 + jnp.concatenate([bias_ref[h, near[t][j]] for j in range(q_tiles)], axis=1)
                 for t in range(tiles)], axis=0)
            lg_s[h, pl.ds(r0, ATT_CHUNK), :] = lg
            out.append(jnp.maximum(m8s[h], jnp.max(lg.reshape(ATT_CHUNK // 8, 8, Q_TILE), axis=0)))
        return tuple(out)

    m8s = _loop_by_two(nac, logit_chunk, tuple(jnp.full((8, Q_TILE), NEG, F32) for _ in range(ATT_HEADS)))
    ms = [jnp.max(m8, axis=0, keepdims=True) for m8 in m8s]

    acc_s[...] = jnp.zeros(acc_s.shape, F32)

    def pv_chunk(ac, carry):
        r0 = pl.multiple_of(ac * ATT_CHUNK, ATT_CHUNK)
        ct_chunk = ct_ref[:, pl.ds(r0, ATT_CHUNK)]
        for h in range(ATT_HEADS):
            pr = jnp.exp2(lg_s[h, pl.ds(r0, ATT_CHUNK), :] - ms[h])
            acc_s[h] += jnp.dot(ct_chunk, pr.astype(BF16), preferred_element_type=F32)
        return carry

    _loop_by_two(nac, pv_chunk, 0)

    for h in range(ATT_HEADS):
        denom = acc_s[h, KV_LATENT:KV_LATENT + 1, :]
        o_t = acc_s[h, 0:KV_LATENT, :] * (1.0 / denom)
        y_t = jnp.dot(wuvt_ref[h], o_t.astype(BF16), preferred_element_type=F32)
        gb = gb_ref[:, h * HEAD_DIM:(h + 1) * HEAD_DIM]
        o_ref[:, h * HEAD_DIM:(h + 1) * HEAD_DIM] = (y_t.T * (gb * _sigmoid(gb))).astype(o_ref.dtype)


def _dsa(pb3, cols_b, w_ukt, wit, kn, c, ct, pa3, gb_blk, bias_tiles, wuvt, topk):
    bsz, s, _ = c.shape
    att_w = ATT_HEADS * HEAD_DIM
    idx_w = IDX_HEADS * IDX_DIM
    assert s % ATT_CHUNK == 0 and s % Q_TILE == 0 and Q_TILE % K_CHUNK == 0 and ATT_CHUNK % K_CHUNK == 0
    assert s // 16 // 4 < 256
    q_blk = _col_block(cols_b["q"], att_w)
    qi_blk = _col_block(cols_b["qi"], idx_w)
    const = lambda shape: pl.BlockSpec(shape, lambda b, i: (0,) * len(shape))
    return pl.pallas_call(
        functools.partial(_dsa_kernel, topk=topk),
        grid=(bsz, s // Q_TILE),
        in_specs=[
            pl.BlockSpec((None, Q_TILE, att_w), lambda b, i: (b, i, q_blk)),
            pl.BlockSpec((None, Q_TILE, idx_w), lambda b, i: (b, i, qi_blk)),
            const(w_ukt.shape),
            pl.BlockSpec((None, IDX_HEADS, Q_TILE), lambda b, i: (b, 0, i)),
            pl.BlockSpec((None, s, IDX_DIM), lambda b, i: (b, 0, 0)),
            pl.BlockSpec((None, s, KV_LATENT), lambda b, i: (b, 0, 0)),
            pl.BlockSpec((None, KV_LATENT + ONES_ROWS, s), lambda b, i: (b, 0, 0)),
            pl.BlockSpec((None, Q_TILE, att_w), lambda b, i: (b, i, gb_blk)),
            const(bias_tiles.shape),
            const(wuvt.shape),
        ],
        out_specs=pl.BlockSpec((None, Q_TILE, att_w), lambda b, i: (b, i, 0)),
        out_shape=jax.ShapeDtypeStruct((bsz, s, att_w), BF16),
        scratch_shapes=[
            pltpu.VMEM((ATT_HEADS, KV_LATENT, Q_TILE), BF16),
            pltpu.VMEM((IDX_HEADS, IDX_DIM, Q_TILE), BF16),
            pltpu.VMEM((s, Q_TILE), F32),
            pltpu.VMEM((s, Q_TILE), BF16),
            pltpu.VMEM((8, Q_TILE), F32),
            pltpu.VMEM((ATT_HEADS, s, Q_TILE), F32),
            pltpu.VMEM((ATT_HEADS, KV_LATENT + ONES_ROWS, Q_TILE), F32),
        ],
        compiler_params=_cparams(("parallel", "arbitrary")),
        name="dsa",
    )(pb3, pb3, w_ukt, wit, kn, c, ct, pa3, bias_tiles, wuvt)


def _outp_kernel(ya_ref, yb_ref, wa_ref, wb_ref, x_ref, g_ref, o_ref, *, final_norm):
    acc = jnp.dot(ya_ref[...], wa_ref[...], preferred_element_type=F32)
    acc = acc + jnp.dot(yb_ref[...], wb_ref[...], preferred_element_type=F32)
    x = x_ref[...] + acc
    if final_norm:
        x = x * lax.rsqrt(jnp.mean(x * x, axis=-1, keepdims=True) + EPS) * g_ref[...]
    o_ref[...] = x


def _outp(ya, yb, w_out, x2, g, final_norm, tm=512):
    m, d = x2.shape
    ka, kb = ya.shape[1], yb.shape[1]
    assert ka == kb and w_out.shape[0] == ka + kb
    return pl.pallas_call(
        functools.partial(_outp_kernel, final_norm=final_norm),
        grid=(m // tm,),
        in_specs=[
            pl.BlockSpec((tm, ka), lambda i: (i, 0)),
            pl.BlockSpec((tm, kb), lambda i: (i, 0)),
            pl.BlockSpec((ka, d), lambda i: (0, 0)),
            pl.BlockSpec((kb, d), lambda i: (1, 0)),
            pl.BlockSpec((tm, d), lambda i: (i, 0)),
            pl.BlockSpec((1, d), lambda i: (0, 0)),
        ],
        out_specs=pl.BlockSpec((tm, d), lambda i: (i, 0)),
        out_shape=jax.ShapeDtypeStruct((m, d), F32),
        compiler_params=_cparams(("parallel",)),
        name="outp",
    )(ya, yb, w_out, w_out, x2, g)


def _t5_bucket(dist):
    n = jnp.maximum(dist, 0)
    max_exact = REL_BUCKETS // 2
    nf = jnp.maximum(n, 1).astype(F32)
    large = max_exact + (jnp.log(nf / max_exact) / np.log(REL_MAX_DIST / max_exact)
                         * (REL_BUCKETS - max_exact)).astype(I32)
    large = jnp.minimum(large, REL_BUCKETS - 1)
    return jnp.where(n < max_exact, n, large)


def _bias_tiles(rel_bias):
    qw = K_CHUNK
    span = K_CHUNK + qw
    table = rel_bias[_t5_bucket(jnp.arange(span + 1, dtype=I32))].astype(F32)
    table = ((table[:span] - table[span:]) * LOG2E).T
    n = span + qw - 1
    a = jnp.concatenate([jnp.zeros((ATT_HEADS, qw - 1), F32), table], axis=1)
    shifted = jnp.tile(a, (1, span + 1))[:, :span * (n + 1)].reshape(ATT_HEADS, span, n + 1)
    tiles = shifted[:, ::-1, :qw].reshape(ATT_HEADS, 2, K_CHUNK, qw)
    return jnp.concatenate([jnp.zeros_like(tiles[:, :1]), tiles], axis=1)


def kernel(x, norm_g, w_in, conv_w, conv_b, lru_wa, lru_ba, lru_wx, lru_bx, lru_lambda, ckv_norm_g, idx_k_norm_g, idx_k_norm_b, w_uk, w_uv, w_out, rel_bias, final_norm_g):
    bsz, s, d = x.shape
    depth = w_in.shape[0]
    lru_w = lru_wa.shape[1] * lru_wa.shape[2]
    att_w = ATT_HEADS * HEAD_DIM
    idx_w = IDX_HEADS * IDX_DIM
    assert REL_MAX_DIST <= K_CHUNK
    assert lru_w == att_w == idx_w and att_w % KV_LATENT == 0
    topk = min(INDEX_TOPK, s // 4)

    o_q = 2 * lru_w
    o_ckv = o_q + att_w
    o_gb = o_ckv + KV_LATENT
    o_qi = o_gb + att_w
    o_ki = o_qi + idx_w
    tn = 512
    cols_a = {"xa": 0, "ga": lru_w, "gb": 2 * lru_w}
    cols_b = {"q": 0, "qi": att_w}
    n_f32, n_bf16 = 3 * lru_w, att_w + idx_w

    bias_tiles = _bias_tiles(rel_bias)
    x2 = x.reshape(bsz * s, d)
    for l in range(depth):
        order = [(0, o_q), (o_gb, o_qi), (o_q, o_ckv), (o_qi, o_ki), (o_ckv, o_gb), (o_ki, w_in.shape[2])]
        pa, pb, c, ct, kn, wit = _proj(x2, norm_g[l][None, :], w_in[l].T, order, ckv_norm_g[l][None, :],
                                       idx_k_norm_g[l][None, :], idx_k_norm_b[l][None, :], n_f32, n_bf16, s, tn=tn)
        pa3 = pa.reshape(bsz, s, -1)
        pb3 = pb.reshape(bsz, s, -1)

        ya = _rglru(pa3, cols_a, conv_w[l], conv_b[l][None, :], (0.5 * lru_wa[l]).astype(BF16),
                    0.5 * lru_ba[l][None, :], (0.5 * lru_wx[l]).astype(BF16), 0.5 * lru_bx[l][None, :],
                    lru_lambda[l][None, :])

        wukt = jnp.transpose(w_uk[l], (0, 2, 1)).astype(BF16)
        wuvt = jnp.transpose(w_uv[l], (0, 2, 1)).astype(BF16)
        yb = _dsa(pb3, cols_b, wukt, wit, kn.reshape(bsz, s, -1), c.reshape(bsz, s, -1), ct, pa3,
                  _col_block(cols_a["gb"], att_w), bias_tiles, wuvt, topk)

        x2 = _outp(ya.reshape(bsz * s, lru_w), yb.reshape(bsz * s, att_w), w_out[l].astype(BF16), x2,
                   final_norm_g[None, :], final_norm=(l == depth - 1))
    return x2.reshape(bsz, s, d)
```

```python
import functools

import numpy as np
import jax
import jax.numpy as jnp
from jax import lax
from jax.experimental import pallas as pl
from jax.experimental.pallas import tpu as pltpu

F32 = jnp.float32
BF16 = jnp.bfloat16
I32 = jnp.int32

LRU_BLOCKS = 8
CONV_WIDTH = 4
LRU_C = 8.0
ATT_HEADS = 8
HEAD_DIM = 128
KV_LATENT = 256
IDX_HEADS = 16
IDX_DIM = 64
INDEX_TOPK = 256
REL_BUCKETS = 32
REL_MAX_DIST = 128
EPS = 1e-6
LOG2E = float(np.log2(np.e))
ONES_ROWS = 16

Q_TILE = 256
K_CHUNK = 128
ATT_CHUNK = 256
NEG = float(np.finfo(np.float32).min)
INT_MIN = -(2 ** 31)
VMEM_LIMIT = 56 * 1024 * 1024


def _cparams(sem):
    return pltpu.CompilerParams(dimension_semantics=sem, vmem_limit_bytes=VMEM_LIMIT)


def _col_block(offset, width):
    assert offset % width == 0
    return offset // width


def _proj_kernel(t_ref, x_ref, g_ref, wlo_ref, whi_ref, tail_ref, cg_ref, kg_ref, kb_ref,
                 oa_ref, ob_ref, c_ref, ct_ref, kn_ref, wit_ref, h_ref, *, na, nb):
    j = pl.program_id(1)
    r = pl.program_id(2)
    nt = (((1,), (1,)), ((), ()))

    def w_tile():
        lo = jnp.where(t_ref[2 * j] < 0, tail_ref[...], wlo_ref[...])
        hi = jnp.where(t_ref[2 * j + 1] < 0, tail_ref[...], whi_ref[...])
        return jnp.concatenate([lo, hi], axis=0).astype(BF16)

    @pl.when(j == 0)
    def _():
        x = x_ref[...]
        y = x * lax.rsqrt(jnp.mean(x * x, axis=-1, keepdims=True) + EPS)
        h_ref[r] = (y * g_ref[...]).astype(BF16)

    @pl.when(j < na)
    def _():
        oa_ref[...] = lax.dot_general(h_ref[r], w_tile(), nt, preferred_element_type=F32)

    @pl.when((j >= na) & (j < na + nb))
    def _():
        ob_ref[...] = lax.dot_general(h_ref[r], w_tile(), nt, preferred_element_type=F32).astype(BF16)

    @pl.when(j >= na + nb)
    def _():
        used = KV_LATENT + 128
        tail = lax.dot_general(h_ref[r], w_tile()[:used], nt, preferred_element_type=F32)
        ckv = tail[:, :KV_LATENT]
        c = ckv * lax.rsqrt(jnp.mean(ckv * ckv, axis=-1, keepdims=True) + EPS) * cg_ref[...]
        c_ref[...] = c.astype(BF16)
        ct_ref[0:KV_LATENT, :] = c.T.astype(BF16)
        ct_ref[KV_LATENT:, :] = jnp.ones((ONES_ROWS, ct_ref.shape[1]), BF16)
        sm = tail[:, KV_LATENT:KV_LATENT + 128]
        ki = sm[:, :IDX_DIM]
        mu = jnp.mean(ki, axis=-1, keepdims=True)
        var = jnp.mean(jnp.square(ki - mu), axis=-1, keepdims=True)
        kn = (ki - mu) * lax.rsqrt(var + EPS) * kg_ref[...] + kb_ref[...]
        kn_ref[...] = kn.astype(BF16)
        wit_ref[...] = sm.T[IDX_DIM:IDX_DIM + IDX_HEADS, :] * (IDX_HEADS ** -0.5 * IDX_DIM ** -0.5)


def _proj(x2, g, w_t, order, ckv_g, k_g, k_b, n_f32, n_bf16, seq_len, tm=1024, tn=512, group=2):
    m, d = x2.shape
    n = w_t.shape[0]
    th = tn // 2
    n_whole = n // th
    src = []
    for start, stop in order:
        assert start % th == 0 and (stop % th == 0 or stop == n)
        src += list(range(start // th, -(-stop // th)))
    na, nb = n_f32 // tn, n_bf16 // tn
    assert n_f32 % tn == 0 and n_bf16 % tn == 0 and len(src) * th == n_f32 + n_bf16 + tn
    tail = jnp.pad(w_t[n_whole * th:], ((0, (n_whole + 1) * th - n), (0, 0)))
    table = jnp.asarray([blk if blk < n_whole else -1 for blk in src], I32)
    tps = seq_len // tm
    assert seq_len % tm == 0 and tn >= KV_LATENT + 128
    assert (m // tm) % group == 0
    nj = na + nb + 1
    const = lambda shape: pl.BlockSpec(shape, lambda i, j, r, t: (0,) * len(shape))
    last = group - 1
    row = lambda i, r_eff: i * group + r_eff
    per_seq = lambda g: (g // tps, 0, g % tps)
    tail_row = lambda i, j, r: row(i, jnp.where(j < nj - 1, 0, r))
    grid_spec = pltpu.PrefetchScalarGridSpec(
        num_scalar_prefetch=1,
        grid=(m // tm // group, nj, group),
        in_specs=[
            pl.BlockSpec((tm, d), lambda i, j, r, t: (row(i, jnp.where(j == 0, r, last)), 0)),
            const((1, d)),
            pl.BlockSpec((th, d), lambda i, j, r, t: (jnp.maximum(t[2 * j], 0), 0)),
            pl.BlockSpec((th, d), lambda i, j, r, t: (jnp.maximum(t[2 * j + 1], 0), 0)),
            const((th, d)),
            const((1, KV_LATENT)),
            const((1, IDX_DIM)),
            const((1, IDX_DIM)),
        ],
        out_specs=[
            pl.BlockSpec((tm, tn), lambda i, j, r, t: (row(i, jnp.where(j < na, r, last)), jnp.minimum(j, na - 1))),
            pl.BlockSpec((tm, tn), lambda i, j, r, t: (
                row(i, jnp.where(j < na, 0, jnp.where(j < na + nb, r, last))), jnp.clip(j - na, 0, nb - 1))),
            pl.BlockSpec((tm, KV_LATENT), lambda i, j, r, t: (tail_row(i, j, r), 0)),
            pl.BlockSpec((None, KV_LATENT + ONES_ROWS, tm), lambda i, j, r, t: per_seq(tail_row(i, j, r))),
            pl.BlockSpec((tm, IDX_DIM), lambda i, j, r, t: (tail_row(i, j, r), 0)),
            pl.BlockSpec((None, IDX_HEADS, tm), lambda i, j, r, t: per_seq(tail_row(i, j, r))),
        ],
        scratch_shapes=[pltpu.VMEM((group, tm, d), BF16)],
    )
    return pl.pallas_call(
        functools.partial(_proj_kernel, na=na, nb=nb),
        grid_spec=grid_spec,
        out_shape=[
            jax.ShapeDtypeStruct((m, n_f32), F32),
            jax.ShapeDtypeStruct((m, n_bf16), BF16),
            jax.ShapeDtypeStruct((m, KV_LATENT), BF16),
            jax.ShapeDtypeStruct((m // seq_len, KV_LATENT + ONES_ROWS, seq_len), BF16),
            jax.ShapeDtypeStruct((m, IDX_DIM), BF16),
            jax.ShapeDtypeStruct((m // seq_len, IDX_HEADS, seq_len), F32),
        ],
        compiler_params=_cparams(("arbitrary", "arbitrary", "arbitrary")),
        name="proj",
    )(table, x2, g, w_t, w_t, tail, ckv_g, k_g, k_b)


def _sigmoid(v):
    return 0.5 * jnp.tanh(0.5 * v) + 0.5


def _scan_step(a, b, k, axis, idx):
    keep = idx >= k
    a_prev = jnp.where(keep, pltpu.roll(a, k, axis=axis), 1.0)
    b_prev = jnp.where(keep, pltpu.roll(b, k, axis=axis), 0.0)
    return a * a_prev, a * b_prev + b


def _rglru_kernel(xa_ref, ga_ref, cw_ref, cb_ref, wa_ref, ba_ref, wx_ref, bx_ref, lam_ref,
                  o_ref, pad_s, a_s, b_s, c_s):
    s, w = xa_ref.shape
    tile = 8
    n_tiles = s // tile

    pad_s[0:tile, :] = jnp.zeros((tile, w), F32)
    pad_s[tile:tile + s, :] = xa_ref[...]
    acc = pad_s[tile:tile + s, :] * cw_ref[CONV_WIDTH - 1:CONV_WIDTH, :]
    for j in range(CONV_WIDTH - 1):
        back = CONV_WIDTH - 1 - j
        acc = acc + pad_s[tile - back:tile - back + s, :] * cw_ref[j:j + 1, :]
    xc = cb_ref[...] + acc

    xcb = xc.astype(BF16)
    tr = jnp.tanh(jnp.dot(xcb, wa_ref[...], preferred_element_type=F32) + ba_ref[...])
    ti = jnp.tanh(jnp.dot(xcb, wx_ref[...], preferred_element_type=F32) + bx_ref[...])
    i = 0.5 * ti + 0.5
    z = -lam_ref[...]
    softplus = jnp.maximum(z, 0.0) + jnp.log1p(jnp.exp(-jnp.abs(z)))
    half = (-0.5 * LRU_C) * softplus
    log_a = half * tr + half
    a = jnp.exp(log_a)
    m2 = (1.0 + a * a) * jnp.tanh(-log_a)
    mult = jnp.where(m2 > 0.0, m2 * lax.rsqrt(m2), 0.0)
    gated = i * xc
    b_s[...] = mult * gated
    b_s[0:1, :] = gated[0:1, :]

    a3 = a.reshape(n_tiles, tile, w)
    b3 = b_s[...].reshape(n_tiles, tile, w)
    sub = lax.broadcasted_iota(I32, (n_tiles, tile, w), 1)
    for k in (1, 2, 4):
        a3, b3 = _scan_step(a3, b3, k, 1, sub)
    a_s[...] = a3.reshape(s, w)
    b_s[...] = b3.reshape(s, w)

    at = a_s[pl.ds(tile - 1, n_tiles, stride=tile), :]
    bt = b_s[pl.ds(tile - 1, n_tiles, stride=tile), :]
    trow = lax.broadcasted_iota(I32, (n_tiles, w), 0)
    k = 1
    while k < n_tiles:
        at, bt = _scan_step(at, bt, k, 0, trow)
        k *= 2
    c_s[0:tile, :] = jnp.zeros((tile, w), F32)
    c_s[tile:tile + n_tiles, :] = bt

    def apply(t, carry):
        r0 = pl.multiple_of(t * tile, tile)
        before = c_s[pl.ds(tile - 1 + t, tile, stride=0), :]
        h = a_s[pl.ds(r0, tile), :] * before + b_s[pl.ds(r0, tile), :]
        gh = 0.5 * ga_ref[pl.ds(r0, tile), :]
        o_ref[pl.ds(r0, tile), :] = (h * (gh * (jnp.tanh(gh) + 1.0))).astype(o_ref.dtype)
        return carry

    lax.fori_loop(0, n_tiles, apply, 0, unroll=8)


def _rglru(pa3, cols, conv_w, conv_b, wa, ba, wx, bx, lam):
    bsz, s, _ = pa3.shape
    g, w = wa.shape[0], wa.shape[-1]
    xa_blk = _col_block(cols["xa"], w)
    ga_blk = _col_block(cols["ga"], w)
    vec = lambda: pl.BlockSpec((1, w), lambda b, j: (0, j))
    return pl.pallas_call(
        _rglru_kernel,
        grid=(bsz, g),
        in_specs=[
            pl.BlockSpec((None, s, w), lambda b, j: (b, 0, xa_blk + j)),
            pl.BlockSpec((None, s, w), lambda b, j: (b, 0, ga_blk + j)),
            pl.BlockSpec((CONV_WIDTH, w), lambda b, j: (0, j)),
            vec(),
            pl.BlockSpec((None, w, w), lambda b, j: (j, 0, 0)),
            vec(),
            pl.BlockSpec((None, w, w), lambda b, j: (j, 0, 0)),
            vec(),
            vec(),
        ],
        out_specs=pl.BlockSpec((None, s, w), lambda b, j: (b, 0, j)),
        out_shape=jax.ShapeDtypeStruct((bsz, s, g * w), BF16),
        scratch_shapes=[pltpu.VMEM((s + 8, w), F32), pltpu.VMEM((s, w), F32), pltpu.VMEM((s, w), F32),
                        pltpu.VMEM((s // 8 + 8, w), F32)],
        compiler_params=_cparams(("parallel", "parallel")),
        name="rglru",
    )(pa3, pa3, conv_w, conv_b, wa, ba, wx, bx, lam)


def _tree_sum(parts):
    while len(parts) > 1:
        paired = [parts[i] + parts[i + 1] for i in range(0, len(parts) - 1, 2)]
        parts = paired + ([parts[-1]] if len(parts) % 2 else [])
    return parts[0]


def _sortable_to_f32(u):
    key = u ^ INT_MIN
    return lax.bitcast_convert_type(key ^ ((key >> 31) & 0x7FFFFFFF), F32)


def _count_ge(ref, rows, cand, pack):
    chains = 4
    one, zero = jnp.ones((), ref.dtype), jnp.zeros((), ref.dtype)
    accs = [None] * chains
    for r in range(rows // pack):
        hit = jnp.where(ref[r * pack:(r + 1) * pack, :] >= cand, one, zero)
        accs[r % chains] = hit if accs[r % chains] is None else accs[r % chains] + hit
    parts = [a.astype(F32) for a in accs if a is not None]
    return jnp.sum(_tree_sum(parts), axis=0, keepdims=True)


def _kth_largest(score_ref, score16_ref, rows, k):
    def step16(i, u):
        cand = u | (jnp.int32(1) << (15 - i))
        cand_f = _sortable_to_f32(cand << 16).astype(BF16)
        return jnp.where(_count_ge(score16_ref, rows, cand_f, 16) >= k, cand, u)

    hi = lax.fori_loop(0, 16, step16, jnp.zeros((1, Q_TILE), I32))
    at_hi = _count_ge(score_ref, rows, _sortable_to_f32(hi << 16), 8)
    keeps = at_hi >= k
    hi = jnp.where(keeps, hi, jnp.maximum(hi - 1, 0))

    per_trip = 4

    def trip(state):
        i, u, at_u, _ = state
        for t in range(per_trip):
            cand = u | (jnp.int32(1) << (15 - (i + t)))
            cnt = _count_ge(score_ref, rows, _sortable_to_f32(cand), 8)
            take = cnt >= k
            u = jnp.where(take, cand, u)
            at_u = jnp.where(take, cnt, at_u)
        open_lanes = jnp.sum((at_u != k).astype(I32))
        return i + per_trip, u, at_u, open_lanes

    unknown = jnp.full((1, Q_TILE), -1.0, F32)
    state = (jnp.int32(0), hi << 16, jnp.where(keeps, at_hi, unknown), jnp.int32(1))
    _, u, _, _ = lax.while_loop(lambda st: (st[0] < 16) & (st[3] > 0), trip, state)
    return _sortable_to_f32(u)


def _loop_by_two(n, body, init):
    def trips(start, count, width, carry):
        def group(i, c):
            for t in range(width):
                c = body(start + width * i + t, c)
            return c
        return lax.fori_loop(0, count, group, carry)

    carry = trips(0, n // 4, 4, init)
    carry = trips(4 * (n // 4), (n % 4) // 2, 2, carry)
    return trips(2 * (n // 2), n % 2, 1, carry)


def _dsa_kernel(q_ref, qi_ref, wukt_ref, wit_ref, kn_ref, c_ref, ct_ref, gb_ref, bias_ref, wuvt_ref,
                o_ref, qat_ref, qit_ref, score_s, score16_s, thr_s, lg_s, acc_s, topk):
    qb = pl.program_id(1)

    qt = q_ref[...].T
    scale = HEAD_DIM ** -0.5 * LOG2E
    for h in range(ATT_HEADS):
        qa = jnp.dot(wukt_ref[h], qt[h * HEAD_DIM:(h + 1) * HEAD_DIM], preferred_element_type=F32)
        qat_ref[h] = (qa * scale).astype(BF16)
    qit = qi_ref[...].T
    for h in range(IDX_HEADS):
        qit_ref[h] = qit[h * IDX_DIM:(h + 1) * IDX_DIM, :]
    q_tiles = Q_TILE // K_CHUNK
    nkc = (qb + 1) * q_tiles
    nac = (nkc * K_CHUNK + ATT_CHUNK - 1) // ATT_CHUNK
    tiles = ATT_CHUNK // K_CHUNK

    kiota = lax.broadcasted_iota(I32, (K_CHUNK, Q_TILE), 0)
    qpos = qb * Q_TILE + lax.broadcasted_iota(I32, (K_CHUNK, Q_TILE), 1)

    def score_chunk(ac, carry):
        for t in range(tiles):
            k0 = pl.multiple_of(ac * ATT_CHUNK + t * K_CHUNK, K_CHUNK)
            kn = kn_ref[pl.ds(k0, K_CHUNK), :]
            acc = jnp.zeros((K_CHUNK, Q_TILE), F32)
            for h in range(IDX_HEADS):
                sc = jnp.dot(kn, qit_ref[h], preferred_element_type=F32)
                acc = acc + jnp.maximum(sc, 0.0) * wit_ref[h:h + 1, :]
            masked = jnp.where(kiota + k0 <= qpos, acc, -jnp.inf)
            score_s[pl.ds(k0, K_CHUNK), :] = masked
            score16_s[pl.ds(k0, K_CHUNK), :] = masked.astype(BF16)
        return carry

    _loop_by_two(nac, score_chunk, 0)

    for v in range(1, score_s.shape[0] // ATT_CHUNK + 1):
        @pl.when(nac == v)
        def _(rows=v * ATT_CHUNK):
            thr = _kth_largest(score_s, score16_s, rows, float(topk))
            thr = jnp.where(thr >= NEG, thr, NEG)
            thr_s[...] = jnp.broadcast_to(thr, thr_s.shape)

    thr = thr_s[0:1, :]

    def logit_chunk(ac, m8s):
        r0 = pl.multiple_of(ac * ATT_CHUNK, ATT_CHUNK)
        c_chunk = c_ref[pl.ds(r0, ATT_CHUNK), :]
        mbias = jnp.where(score_s[pl.ds(r0, ATT_CHUNK), :] >= thr, 0.0, NEG)
        near = [[jnp.clip(ac * tiles + t - (qb * q_tiles + j) + 2, 0, 2) for j in range(q_tiles)]
                for t in range(tiles)]
        out = []
        for h in range(ATT_HEADS):
            lg = jnp.dot(c_chunk, qat_ref[h], preferred_element_type=F32) + mbias
            lg = jnp.concatenate(
                [lg[t * K_CHUNK:(t + 1) * K_CHUNK]
                 + jnp.concatenate([bias_ref[h, near[t][j]] for j in range(q_tiles)], axis=1)
                 for t in range(tiles)], axis=0)
            lg_s[h, pl.ds(r0, ATT_CHUNK), :] = lg
            out.append(jnp.maximum(m8s[h], jnp.max(lg.reshape(ATT_CHUNK // 8, 8, Q_TILE), axis=0)))
        return tuple(out)

    m8s = _loop_by_two(nac, logit_chunk, tuple(jnp.full((8, Q_TILE), NEG, F32) for _ in range(ATT_HEADS)))
    ms = [jnp.max(m8, axis=0, keepdims=True) for m8 in m8s]

    acc_s[...] = jnp.zeros(acc_s.shape, F32)

    def pv_chunk(ac, carry):
        r0 = pl.multiple_of(ac * ATT_CHUNK, ATT_CHUNK)
        ct_chunk = ct_ref[:, pl.ds(r0, ATT_CHUNK)]
        for h in range(ATT_HEADS):
            pr = jnp.exp2(lg_s[h, pl.ds(r0, ATT_CHUNK), :] - ms[h])
            acc_s[h] += jnp.dot(ct_chunk, pr.astype(BF16), preferred_element_type=F32)
        return carry

    _loop_by_two(nac, pv_chunk, 0)

    for h in range(ATT_HEADS):
        denom = acc_s[h, KV_LATENT:KV_LATENT + 1, :]
        o_t = acc_s[h, 0:KV_LATENT, :] * (1.0 / denom)
        y_t = jnp.dot(wuvt_ref[h], o_t.astype(BF16), preferred_element_type=F32)
        gb = gb_ref[:, h * HEAD_DIM:(h + 1) * HEAD_DIM]
        o_ref[:, h * HEAD_DIM:(h + 1) * HEAD_DIM] = (y_t.T * (gb * _sigmoid(gb))).astype(o_ref.dtype)


def _dsa(pb3, cols_b, w_ukt, wit, kn, c, ct, pa3, gb_blk, bias_tiles, wuvt, topk):
    bsz, s, _ = c.shape
    att_w = ATT_HEADS * HEAD_DIM
    idx_w = IDX_HEADS * IDX_DIM
    assert s % ATT_CHUNK == 0 and s % Q_TILE == 0 and Q_TILE % K_CHUNK == 0 and ATT_CHUNK % K_CHUNK == 0
    assert s // 16 // 4 < 256
    q_blk = _col_block(cols_b["q"], att_w)
    qi_blk = _col_block(cols_b["qi"], idx_w)
    const = lambda shape: pl.BlockSpec(shape, lambda b, i: (0,) * len(shape))
    return pl.pallas_call(
        functools.partial(_dsa_kernel, topk=topk),
        grid=(bsz, s // Q_TILE),
        in_specs=[
            pl.BlockSpec((None, Q_TILE, att_w), lambda b, i: (b, i, q_blk)),
            pl.BlockSpec((None, Q_TILE, idx_w), lambda b, i: (b, i, qi_blk)),
            const(w_ukt.shape),
            pl.BlockSpec((None, IDX_HEADS, Q_TILE), lambda b, i: (b, 0, i)),
            pl.BlockSpec((None, s, IDX_DIM), lambda b, i: (b, 0, 0)),
            pl.BlockSpec((None, s, KV_LATENT), lambda b, i: (b, 0, 0)),
            pl.BlockSpec((None, KV_LATENT + ONES_ROWS, s), lambda b, i: (b, 0, 0)),
            pl.BlockSpec((None, Q_TILE, att_w), lambda b, i: (b, i, gb_blk)),
            const(bias_tiles.shape),
            const(wuvt.shape),
        ],
        out_specs=pl.BlockSpec((None, Q_TILE, att_w), lambda b, i: (b, i, 0)),
        out_shape=jax.ShapeDtypeStruct((bsz, s, att_w), BF16),
        scratch_shapes=[
            pltpu.VMEM((ATT_HEADS, KV_LATENT, Q_TILE), BF16),
            pltpu.VMEM((IDX_HEADS, IDX_DIM, Q_TILE), BF16),
            pltpu.VMEM((s, Q_TILE), F32),
            pltpu.VMEM((s, Q_TILE), BF16),
            pltpu.VMEM((8, Q_TILE), F32),
            pltpu.VMEM((ATT_HEADS, s, Q_TILE), F32),
            pltpu.VMEM((ATT_HEADS, KV_LATENT + ONES_ROWS, Q_TILE), F32),
        ],
        compiler_params=_cparams(("parallel", "arbitrary")),
        name="dsa",
    )(pb3, pb3, w_ukt, wit, kn, c, ct, pa3, bias_tiles, wuvt)


def _outp_kernel(ya_ref, yb_ref, w_ref, x_ref, g_ref, o_ref, wbf_s, *, final_norm):
    @pl.when(pl.program_id(0) == 0)
    def _():
        wbf_s[...] = w_ref[...].astype(BF16)

    ka = ya_ref.shape[1]
    acc = jnp.dot(ya_ref[...], wbf_s[0:ka, :], preferred_element_type=F32)
    acc = acc + jnp.dot(yb_ref[...], wbf_s[ka:, :], preferred_element_type=F32)
    x = x_ref[...] + acc
    if final_norm:
        x = x * lax.rsqrt(jnp.mean(x * x, axis=-1, keepdims=True) + EPS) * g_ref[...]
    o_ref[...] = x


def _outp(ya, yb, w_out, x2, g, final_norm, tm=512):
    m, d = x2.shape
    ka, kb = ya.shape[1], yb.shape[1]
    assert w_out.shape == (ka + kb, d)
    return pl.pallas_call(
        functools.partial(_outp_kernel, final_norm=final_norm),
        grid=(m // tm,),
        in_specs=[
            pl.BlockSpec((tm, ka), lambda i: (i, 0)),
            pl.BlockSpec((tm, kb), lambda i: (i, 0)),
            pl.BlockSpec((ka + kb, d), lambda i: (0, 0), pipeline_mode=pl.Buffered(1)),
            pl.BlockSpec((tm, d), lambda i: (i, 0)),
            pl.BlockSpec((1, d), lambda i: (0, 0)),
        ],
        out_specs=pl.BlockSpec((tm, d), lambda i: (i, 0)),
        out_shape=jax.ShapeDtypeStruct((m, d), F32),
        scratch_shapes=[pltpu.VMEM((ka + kb, d), BF16)],
        compiler_params=_cparams(("arbitrary",)),
        name="outp",
    )(ya, yb, w_out, x2, g)


def _t5_bucket(dist):
    n = jnp.maximum(dist, 0)
    max_exact = REL_BUCKETS // 2
    nf = jnp.maximum(n, 1).astype(F32)
    large = max_exact + (jnp.log(nf / max_exact) / np.log(REL_MAX_DIST / max_exact)
                         * (REL_BUCKETS - max_exact)).astype(I32)
    large = jnp.minimum(large, REL_BUCKETS - 1)
    return jnp.where(n < max_exact, n, large)


def _bias_tiles(rel_bias):
    qw = K_CHUNK
    span = K_CHUNK + qw
    table = rel_bias[_t5_bucket(jnp.arange(span + 1, dtype=I32))].astype(F32)
    table = ((table[:span] - table[span:]) * LOG2E).T
    n = span + qw - 1
    a = jnp.concatenate([jnp.zeros((ATT_HEADS, qw - 1), F32), table], axis=1)
    shifted = jnp.tile(a, (1, span + 1))[:, :span * (n + 1)].reshape(ATT_HEADS, span, n + 1)
    tiles = shifted[:, ::-1, :qw].reshape(ATT_HEADS, 2, K_CHUNK, qw)
    return jnp.concatenate([jnp.zeros_like(tiles[:, :1]), tiles], axis=1)


def kernel(x, norm_g, w_in, conv_w, conv_b, lru_wa, lru_ba, lru_wx, lru_bx, lru_lambda, ckv_norm_g, idx_k_norm_g, idx_k_norm_b, w_uk, w_uv, w_out, rel_bias, final_norm_g):
    bsz, s, d = x.shape
    depth = w_in.shape[0]
    lru_w = lru_wa.shape[1] * lru_wa.shape[2]
    att_w = ATT_HEADS * HEAD_DIM
    idx_w = IDX_HEADS * IDX_DIM
    assert REL_MAX_DIST <= K_CHUNK
    assert lru_w == att_w == idx_w and att_w % KV_LATENT == 0
    topk = min(INDEX_TOPK, s // 4)

    o_q = 2 * lru_w
    o_ckv = o_q + att_w
    o_gb = o_ckv + KV_LATENT
    o_qi = o_gb + att_w
    o_ki = o_qi + idx_w
    tn = 512
    cols_a = {"xa": 0, "ga": lru_w, "gb": 2 * lru_w}
    cols_b = {"q": 0, "qi": att_w}
    n_f32, n_bf16 = 3 * lru_w, att_w + idx_w

    bias_tiles = _bias_tiles(rel_bias)
    x2 = x.reshape(bsz * s, d)
    for l in range(depth):
        order = [(0, o_q), (o_gb, o_qi), (o_q, o_ckv), (o_qi, o_ki), (o_ckv, o_gb), (o_ki, w_in.shape[2])]
        pa, pb, c, ct, kn, wit = _proj(x2, norm_g[l][None, :], w_in[l].T, order, ckv_norm_g[l][None, :],
                                       idx_k_norm_g[l][None, :], idx_k_norm_b[l][None, :], n_f32, n_bf16, s, tn=tn)
        pa3 = pa.reshape(bsz, s, -1)
        pb3 = pb.reshape(bsz, s, -1)

        ya = _rglru(pa3, cols_a, conv_w[l], conv_b[l][None, :], (0.5 * lru_wa[l]).astype(BF16),
                    0.5 * lru_ba[l][None, :], (0.5 * lru_wx[l]).astype(BF16), 0.5 * lru_bx[l][None, :],
                    lru_lambda[l][None, :])

        wukt = jnp.transpose(w_uk[l], (0, 2, 1)).astype(BF16)
        wuvt = jnp.transpose(w_uv[l], (0, 2, 1)).astype(BF16)
        yb = _dsa(pb3, cols_b, wukt, wit, kn.reshape(bsz, s, -1), c.reshape(bsz, s, -1), ct, pa3,
                  _col_block(cols_a["gb"], att_w), bias_tiles, wuvt, topk)

        x2 = _outp(ya.reshape(bsz * s, lru_w), yb.reshape(bsz * s, att_w), w_out[l], x2,
                   final_norm_g[None, :], final_norm=(l == depth - 1))
    return x2.reshape(bsz, s, d)
```

```python
import functools

import numpy as np
import jax
import jax.numpy as jnp
from jax import lax
from jax.experimental import pallas as pl
from jax.experimental.pallas import tpu as pltpu

F32 = jnp.float32
BF16 = jnp.bfloat16
I32 = jnp.int32

LRU_BLOCKS = 8
CONV_WIDTH = 4
LRU_C = 8.0
ATT_HEADS = 8
HEAD_DIM = 128
KV_LATENT = 256
IDX_HEADS = 16
IDX_DIM = 64
INDEX_TOPK = 256
REL_BUCKETS = 32
REL_MAX_DIST = 128
EPS = 1e-6
LOG2E = float(np.log2(np.e))
ONES_ROWS = 16

Q_TILE = 256
K_CHUNK = 128
ATT_CHUNK = 256
NEG = float(np.finfo(np.float32).min)
INT_MIN = -(2 ** 31)
VMEM_LIMIT = 56 * 1024 * 1024


def _cparams(sem):
    return pltpu.CompilerParams(dimension_semantics=sem, vmem_limit_bytes=VMEM_LIMIT)


def _col_block(offset, width):
    assert offset % width == 0
    return offset // width


def _proj_kernel(t_ref, x_ref, g_ref, wlo_ref, whi_ref, tail_ref, cg_ref, kg_ref, kb_ref,
                 oa_ref, ob_ref, c_ref, ct_ref, kn_ref, wit_ref, h_ref, *, na, nb):
    j = pl.program_id(1)
    r = pl.program_id(2)
    nt = (((1,), (1,)), ((), ()))

    def w_tile():
        lo = jnp.where(t_ref[2 * j] < 0, tail_ref[...], wlo_ref[...])
        hi = jnp.where(t_ref[2 * j + 1] < 0, tail_ref[...], whi_ref[...])
        return jnp.concatenate([lo, hi], axis=0).astype(BF16)

    @pl.when(j == 0)
    def _():
        x = x_ref[...]
        y = x * lax.rsqrt(jnp.mean(x * x, axis=-1, keepdims=True) + EPS)
        h = (y * g_ref[...]).astype(BF16)
        h_ref[r] = h
        oa_ref[...] = lax.dot_general(h, w_tile(), nt, preferred_element_type=F32)

    @pl.when((j > 0) & (j < na))
    def _():
        oa_ref[...] = lax.dot_general(h_ref[r], w_tile(), nt, preferred_element_type=F32)

    @pl.when((j >= na) & (j < na + nb))
    def _():
        ob_ref[...] = lax.dot_general(h_ref[r], w_tile(), nt, preferred_element_type=F32).astype(BF16)

    @pl.when(j >= na + nb)
    def _():
        used = KV_LATENT + 128
        tail = lax.dot_general(h_ref[r], w_tile()[:used], nt, preferred_element_type=F32)
        ckv = tail[:, :KV_LATENT]
        c = ckv * lax.rsqrt(jnp.mean(ckv * ckv, axis=-1, keepdims=True) + EPS) * cg_ref[...]
        c_ref[...] = c.astype(BF16)
        ct_ref[0:KV_LATENT, :] = c.T.astype(BF16)
        ct_ref[KV_LATENT:, :] = jnp.ones((ONES_ROWS, ct_ref.shape[1]), BF16)
        sm = tail[:, KV_LATENT:KV_LATENT + 128]
        ki = sm[:, :IDX_DIM]
        mu = jnp.mean(ki, axis=-1, keepdims=True)
        var = jnp.mean(jnp.square(ki - mu), axis=-1, keepdims=True)
        kn = (ki - mu) * lax.rsqrt(var + EPS) * kg_ref[...] + kb_ref[...]
        kn_ref[...] = kn.astype(BF16)
        wit_ref[...] = sm.T[IDX_DIM:IDX_DIM + IDX_HEADS, :] * (IDX_HEADS ** -0.5 * IDX_DIM ** -0.5)


def _proj(x2, g, w_t, order, ckv_g, k_g, k_b, n_f32, n_bf16, seq_len, tm=1024, tn=512, group=2):
    m, d = x2.shape
    n = w_t.shape[0]
    th = tn // 2
    n_whole = n // th
    src = []
    for start, stop in order:
        assert start % th == 0 and (stop % th == 0 or stop == n)
        src += list(range(start // th, -(-stop // th)))
    na, nb = n_f32 // tn, n_bf16 // tn
    assert n_f32 % tn == 0 and n_bf16 % tn == 0 and len(src) * th == n_f32 + n_bf16 + tn
    tail = jnp.pad(w_t[n_whole * th:], ((0, (n_whole + 1) * th - n), (0, 0)))
    table = jnp.asarray([blk if blk < n_whole else -1 for blk in src], I32)
    tps = seq_len // tm
    assert seq_len % tm == 0 and tn >= KV_LATENT + 128
    assert (m // tm) % group == 0
    nj = na + nb + 1
    const = lambda shape: pl.BlockSpec(shape, lambda i, j, r, t: (0,) * len(shape))
    last = group - 1
    row = lambda i, r_eff: i * group + r_eff
    per_seq = lambda g: (g // tps, 0, g % tps)
    tail_row = lambda i, j, r: row(i, jnp.where(j < nj - 1, 0, r))
    grid_spec = pltpu.PrefetchScalarGridSpec(
        num_scalar_prefetch=1,
        grid=(m // tm // group, nj, group),
        in_specs=[
            pl.BlockSpec((tm, d), lambda i, j, r, t: (row(i, jnp.where(j == 0, r, last)), 0)),
            const((1, d)),
            pl.BlockSpec((th, d), lambda i, j, r, t: (jnp.maximum(t[2 * j], 0), 0)),
            pl.BlockSpec((th, d), lambda i, j, r, t: (jnp.maximum(t[2 * j + 1], 0), 0)),
            const((th, d)),
            const((1, KV_LATENT)),
            const((1, IDX_DIM)),
            const((1, IDX_DIM)),
        ],
        out_specs=[
            pl.BlockSpec((tm, tn), lambda i, j, r, t: (row(i, jnp.where(j < na, r, last)), jnp.minimum(j, na - 1))),
            pl.BlockSpec((tm, tn), lambda i, j, r, t: (
                row(i, jnp.where(j < na, 0, jnp.where(j < na + nb, r, last))), jnp.clip(j - na, 0, nb - 1))),
            pl.BlockSpec((tm, KV_LATENT), lambda i, j, r, t: (tail_row(i, j, r), 0)),
            pl.BlockSpec((None, KV_LATENT + ONES_ROWS, tm), lambda i, j, r, t: per_seq(tail_row(i, j, r))),
            pl.BlockSpec((tm, IDX_DIM), lambda i, j, r, t: (tail_row(i, j, r), 0)),
            pl.BlockSpec((None, IDX_HEADS, tm), lambda i, j, r, t: per_seq(tail_row(i, j, r))),
        ],
        scratch_shapes=[pltpu.VMEM((group, tm, d), BF16)],
    )
    return pl.pallas_call(
        functools.partial(_proj_kernel, na=na, nb=nb),
        grid_spec=grid_spec,
        out_shape=[
            jax.ShapeDtypeStruct((m, n_f32), F32),
            jax.ShapeDtypeStruct((m, n_bf16), BF16),
            jax.ShapeDtypeStruct((m, KV_LATENT), BF16),
            jax.ShapeDtypeStruct((m // seq_len, KV_LATENT + ONES_ROWS, seq_len), BF16),
            jax.ShapeDtypeStruct((m, IDX_DIM), BF16),
            jax.ShapeDtypeStruct((m // seq_len, IDX_HEADS, seq_len), F32),
        ],
        compiler_params=_cparams(("arbitrary", "arbitrary", "arbitrary")),
        name="proj",
    )(table, x2, g, w_t, w_t, tail, ckv_g, k_g, k_b)


def _sigmoid(v):
    return 0.5 * jnp.tanh(0.5 * v) + 0.5


def _scan_step(a, b, k, axis, idx):
    keep = idx >= k
    a_prev = jnp.where(keep, pltpu.roll(a, k, axis=axis), 1.0)
    b_prev = jnp.where(keep, pltpu.roll(b, k, axis=axis), 0.0)
    return a * a_prev, a * b_prev + b


def _rglru_kernel(xa_ref, ga_ref, cw_ref, cb_ref, wa_ref, ba_ref, wx_ref, bx_ref, lam_ref,
                  o_ref, pad_s, a_s, b_s, c_s):
    s, w = xa_ref.shape
    tile = 8
    n_tiles = s // tile

    pad_s[0:tile, :] = jnp.zeros((tile, w), F32)
    pad_s[tile:tile + s, :] = xa_ref[...]
    acc = pad_s[tile:tile + s, :] * cw_ref[CONV_WIDTH - 1:CONV_WIDTH, :]
    for j in range(CONV_WIDTH - 1):
        back = CONV_WIDTH - 1 - j
        acc = acc + pad_s[tile - back:tile - back + s, :] * cw_ref[j:j + 1, :]
    xc = cb_ref[...] + acc

    xcb = xc.astype(BF16)
    tr = jnp.tanh(jnp.dot(xcb, wa_ref[...], preferred_element_type=F32) + ba_ref[...])
    ti = jnp.tanh(jnp.dot(xcb, wx_ref[...], preferred_element_type=F32) + bx_ref[...])
    i = 0.5 * ti + 0.5
    z = -lam_ref[...]
    softplus = jnp.maximum(z, 0.0) + jnp.log1p(jnp.exp(-jnp.abs(z)))
    half = (-0.5 * LRU_C) * softplus
    log_a = half * tr + half
    a = jnp.exp(log_a)
    m2 = (1.0 + a * a) * jnp.tanh(-log_a)
    mult = jnp.where(m2 > 0.0, m2 * lax.rsqrt(m2), 0.0)
    gated = i * xc
    b_s[...] = mult * gated
    b_s[0:1, :] = gated[0:1, :]

    a3 = a.reshape(n_tiles, tile, w)
    b3 = b_s[...].reshape(n_tiles, tile, w)
    sub = lax.broadcasted_iota(I32, (n_tiles, tile, w), 1)
    for k in (1, 2, 4):
        a3, b3 = _scan_step(a3, b3, k, 1, sub)
    a_s[...] = a3.reshape(s, w)
    b_s[...] = b3.reshape(s, w)

    at = a_s[pl.ds(tile - 1, n_tiles, stride=tile), :]
    bt = b_s[pl.ds(tile - 1, n_tiles, stride=tile), :]
    trow = lax.broadcasted_iota(I32, (n_tiles, w), 0)
    k = 1
    while k < n_tiles:
        at, bt = _scan_step(at, bt, k, 0, trow)
        k *= 2
    c_s[0:tile, :] = jnp.zeros((tile, w), F32)
    c_s[tile:tile + n_tiles, :] = bt

    def apply(t, carry):
        r0 = pl.multiple_of(t * tile, tile)
        before = c_s[pl.ds(tile - 1 + t, tile, stride=0), :]
        h = a_s[pl.ds(r0, tile), :] * before + b_s[pl.ds(r0, tile), :]
        gh = 0.5 * ga_ref[pl.ds(r0, tile), :]
        o_ref[pl.ds(r0, tile), :] = (h * (gh * (jnp.tanh(gh) + 1.0))).astype(o_ref.dtype)
        return carry

    lax.fori_loop(0, n_tiles, apply, 0, unroll=8)


def _rglru(pa3, cols, conv_w, conv_b, wa, ba, wx, bx, lam):
    bsz, s, _ = pa3.shape
    g, w = wa.shape[0], wa.shape[-1]
    xa_blk = _col_block(cols["xa"], w)
    ga_blk = _col_block(cols["ga"], w)
    vec = lambda: pl.BlockSpec((1, w), lambda b, j: (0, j))
    return pl.pallas_call(
        _rglru_kernel,
        grid=(bsz, g),
        in_specs=[
            pl.BlockSpec((None, s, w), lambda b, j: (b, 0, xa_blk + j)),
            pl.BlockSpec((None, s, w), lambda b, j: (b, 0, ga_blk + j)),
            pl.BlockSpec((CONV_WIDTH, w), lambda b, j: (0, j)),
            vec(),
            pl.BlockSpec((None, w, w), lambda b, j: (j, 0, 0)),
            vec(),
            pl.BlockSpec((None, w, w), lambda b, j: (j, 0, 0)),
            vec(),
            vec(),
        ],
        out_specs=pl.BlockSpec((None, s, w), lambda b, j: (b, 0, j)),
        out_shape=jax.ShapeDtypeStruct((bsz, s, g * w), BF16),
        scratch_shapes=[pltpu.VMEM((s + 8, w), F32), pltpu.VMEM((s, w), F32), pltpu.VMEM((s, w), F32),
                        pltpu.VMEM((s // 8 + 8, w), F32)],
        compiler_params=_cparams(("parallel", "parallel")),
        name="rglru",
    )(pa3, pa3, conv_w, conv_b, wa, ba, wx, bx, lam)


def _tree_sum(parts):
    while len(parts) > 1:
        paired = [parts[i] + parts[i + 1] for i in range(0, len(parts) - 1, 2)]
        parts = paired + ([parts[-1]] if len(parts) % 2 else [])
    return parts[0]


def _sortable_to_f32(u):
    key = u ^ INT_MIN
    return lax.bitcast_convert_type(key ^ ((key >> 31) & 0x7FFFFFFF), F32)


def _count_ge(ref, rows, cand, pack):
    chains = 4
    one, zero = jnp.ones((), ref.dtype), jnp.zeros((), ref.dtype)
    accs = [None] * chains
    for r in range(rows // pack):
        hit = jnp.where(ref[r * pack:(r + 1) * pack, :] >= cand, one, zero)
        accs[r % chains] = hit if accs[r % chains] is None else accs[r % chains] + hit
    parts = [a.astype(F32) for a in accs if a is not None]
    return jnp.sum(_tree_sum(parts), axis=0, keepdims=True)


def _kth_largest(score_ref, score16_ref, rows, k):
    def step16(i, u):
        cand = u | (jnp.int32(1) << (15 - i))
        cand_f = _sortable_to_f32(cand << 16).astype(BF16)
        return jnp.where(_count_ge(score16_ref, rows, cand_f, 16) >= k, cand, u)

    hi = lax.fori_loop(0, 16, step16, jnp.zeros((1, Q_TILE), I32))
    at_hi = _count_ge(score_ref, rows, _sortable_to_f32(hi << 16), 8)
    keeps = at_hi >= k
    hi = jnp.where(keeps, hi, jnp.maximum(hi - 1, 0))

    per_trip = 4

    def trip(state):
        i, u, at_u, _ = state
        for t in range(per_trip):
            cand = u | (jnp.int32(1) << (15 - (i + t)))
            cnt = _count_ge(score_ref, rows, _sortable_to_f32(cand), 8)
            take = cnt >= k
            u = jnp.where(take, cand, u)
            at_u = jnp.where(take, cnt, at_u)
        open_lanes = jnp.sum((at_u != k).astype(I32))
        return i + per_trip, u, at_u, open_lanes

    unknown = jnp.full((1, Q_TILE), -1.0, F32)
    state = (jnp.int32(0), hi << 16, jnp.where(keeps, at_hi, unknown), jnp.int32(1))
    _, u, _, _ = lax.while_loop(lambda st: (st[0] < 16) & (st[3] > 0), trip, state)
    return _sortable_to_f32(u)


def _loop_by_two(n, body, init):
    def trips(start, count, width, carry):
        def group(i, c):
            for t in range(width):
                c = body(start + width * i + t, c)
            return c
        return lax.fori_loop(0, count, group, carry)

    carry = trips(0, n // 4, 4, init)
    carry = trips(4 * (n // 4), (n % 4) // 2, 2, carry)
    return trips(2 * (n // 2), n % 2, 1, carry)


def _dsa_kernel(q_ref, qi_ref, wukt_ref, wit_ref, kn_ref, c_ref, ct_ref, gb_ref, bias_ref, wuvt_ref,
                o_ref, qat_ref, qit_ref, score_s, score16_s, thr_s, lg_s, acc_s, topk):
    qb = pl.program_id(1)

    qt = q_ref[...].T
    scale = HEAD_DIM ** -0.5 * LOG2E
    for h in range(ATT_HEADS):
        qa = jnp.dot(wukt_ref[h], qt[h * HEAD_DIM:(h + 1) * HEAD_DIM], preferred_element_type=F32)
        qat_ref[h] = (qa * scale).astype(BF16)
    qit = qi_ref[...].T
    for h in range(IDX_HEADS):
        qit_ref[h] = qit[h * IDX_DIM:(h + 1) * IDX_DIM, :]
    q_tiles = Q_TILE // K_CHUNK
    nkc = (qb + 1) * q_tiles
    nac = (nkc * K_CHUNK + ATT_CHUNK - 1) // ATT_CHUNK
    tiles = ATT_CHUNK // K_CHUNK

    kiota = lax.broadcasted_iota(I32, (K_CHUNK, Q_TILE), 0)
    qpos = qb * Q_TILE + lax.broadcasted_iota(I32, (K_CHUNK, Q_TILE), 1)

    def score_chunk(ac, carry):
        for t in range(tiles):
            k0 = pl.multiple_of(ac * ATT_CHUNK + t * K_CHUNK, K_CHUNK)
            kn = kn_ref[pl.ds(k0, K_CHUNK), :]
            acc = jnp.zeros((K_CHUNK, Q_TILE), F32)
            for h in range(IDX_HEADS):
                sc = jnp.dot(kn, qit_ref[h], preferred_element_type=F32)
                acc = acc + jnp.maximum(sc, 0.0) * wit_ref[h:h + 1, :]
            masked = jnp.where(kiota + k0 <= qpos, acc, -jnp.inf)
            score_s[pl.ds(k0, K_CHUNK), :] = masked
            score16_s[pl.ds(k0, K_CHUNK), :] = masked.astype(BF16)
        return carry

    _loop_by_two(nac, score_chunk, 0)

    for v in range(1, score_s.shape[0] // ATT_CHUNK + 1):
        @pl.when(nac == v)
        def _(rows=v * ATT_CHUNK):
            thr = _kth_largest(score_s, score16_s, rows, float(topk))
            thr = jnp.where(thr >= NEG, thr, NEG)
            thr_s[...] = jnp.broadcast_to(thr, thr_s.shape)

    thr = thr_s[0:1, :]

    def logit_chunk(ac, m8s):
        r0 = pl.multiple_of(ac * ATT_CHUNK, ATT_CHUNK)
        c_chunk = c_ref[pl.ds(r0, ATT_CHUNK), :]
        mbias = jnp.where(score_s[pl.ds(r0, ATT_CHUNK), :] >= thr, 0.0, NEG)
        near = [[jnp.clip(ac * tiles + t - (qb * q_tiles + j) + 2, 0, 2) for j in range(q_tiles)]
                for t in range(tiles)]
        out = []
        for h in range(ATT_HEADS):
            lg = jnp.dot(c_chunk, qat_ref[h], preferred_element_type=F32) + mbias
            lg = jnp.concatenate(
                [lg[t * K_CHUNK:(t + 1) * K_CHUNK]
                 + jnp.concatenate([bias_ref[h, near[t][j]] for j in range(q_tiles)], axis=1)
                 for t in range(tiles)], axis=0)
            lg_s[h, pl.ds(r0, ATT_CHUNK), :] = lg
            out.append(jnp.maximum(m8s[h], jnp.max(lg.reshape(ATT_CHUNK // 8, 8, Q_TILE), axis=0)))
        return tuple(out)

    m8s = _loop_by_two(nac, logit_chunk, tuple(jnp.full((8, Q_TILE), NEG, F32) for _ in range(ATT_HEADS)))
    ms = [jnp.max(m8, axis=0, keepdims=True) for m8 in m8s]

    acc_s[...] = jnp.zeros(acc_s.shape, F32)

    def pv_chunk(ac, carry):
        r0 = pl.multiple_of(ac * ATT_CHUNK, ATT_CHUNK)
        ct_chunk = ct_ref[:, pl.ds(r0, ATT_CHUNK)]
        for h in range(ATT_HEADS):
            pr = jnp.exp2(lg_s[h, pl.ds(r0, ATT_CHUNK), :] - ms[h])
            acc_s[h] += jnp.dot(ct_chunk, pr.astype(BF16), preferred_element_type=F32)
        return carry

    _loop_by_two(nac, pv_chunk, 0)

    for h in range(ATT_HEADS):
        denom = acc_s[h, KV_LATENT:KV_LATENT + 1, :]
        o_t = acc_s[h, 0:KV_LATENT, :] * (1.0 / denom)
        y_t = jnp.dot(wuvt_ref[h], o_t.astype(BF16), preferred_element_type=F32)
        gb = gb_ref[:, h * HEAD_DIM:(h + 1) * HEAD_DIM]
        o_ref[:, h * HEAD_DIM:(h + 1) * HEAD_DIM] = (y_t.T * (gb * _sigmoid(gb))).astype(o_ref.dtype)


def _dsa(pb3, cols_b, w_ukt, wit, kn, c, ct, pa3, gb_blk, bias_tiles, wuvt, topk):
    bsz, s, _ = c.shape
    att_w = ATT_HEADS * HEAD_DIM
    idx_w = IDX_HEADS * IDX_DIM
    assert s % ATT_CHUNK == 0 and s % Q_TILE == 0 and Q_TILE % K_CHUNK == 0 and ATT_CHUNK % K_CHUNK == 0
    assert s // 16 // 4 < 256
    q_blk = _col_block(cols_b["q"], att_w)
    qi_blk = _col_block(cols_b["qi"], idx_w)
    const = lambda shape: pl.BlockSpec(shape, lambda b, i: (0,) * len(shape))
    return pl.pallas_call(
        functools.partial(_dsa_kernel, topk=topk),
        grid=(bsz, s // Q_TILE),
        in_specs=[
            pl.BlockSpec((None, Q_TILE, att_w), lambda b, i: (b, i, q_blk)),
            pl.BlockSpec((None, Q_TILE, idx_w), lambda b, i: (b, i, qi_blk)),
            const(w_ukt.shape),
            pl.BlockSpec((None, IDX_HEADS, Q_TILE), lambda b, i: (b, 0, i)),
            pl.BlockSpec((None, s, IDX_DIM), lambda b, i: (b, 0, 0)),
            pl.BlockSpec((None, s, KV_LATENT), lambda b, i: (b, 0, 0)),
            pl.BlockSpec((None, KV_LATENT + ONES_ROWS, s), lambda b, i: (b, 0, 0)),
            pl.BlockSpec((None, Q_TILE, att_w), lambda b, i: (b, i, gb_blk)),
            const(bias_tiles.shape),
            const(wuvt.shape),
        ],
        out_specs=pl.BlockSpec((None, Q_TILE, att_w), lambda b, i: (b, i, 0)),
        out_shape=jax.ShapeDtypeStruct((bsz, s, att_w), BF16),
        scratch_shapes=[
            pltpu.VMEM((ATT_HEADS, KV_LATENT, Q_TILE), BF16),
            pltpu.VMEM((IDX_HEADS, IDX_DIM, Q_TILE), BF16),
            pltpu.VMEM((s, Q_TILE), F32),
            pltpu.VMEM((s, Q_TILE), BF16),
            pltpu.VMEM((8, Q_TILE), F32),
            pltpu.VMEM((ATT_HEADS, s, Q_TILE), F32),
            pltpu.VMEM((ATT_HEADS, KV_LATENT + ONES_ROWS, Q_TILE), F32),
        ],
        compiler_params=_cparams(("parallel", "arbitrary")),
        name="dsa",
    )(pb3, pb3, w_ukt, wit, kn, c, ct, pa3, bias_tiles, wuvt)


def _outp_kernel(ya_ref, yb_ref, w_ref, x_ref, g_ref, o_ref, wbf_s, *, final_norm):
    @pl.when(pl.program_id(0) == 0)
    def _():
        wbf_s[...] = w_ref[...].astype(BF16)

    ka = ya_ref.shape[1]
    acc = jnp.dot(ya_ref[...], wbf_s[0:ka, :], preferred_element_type=F32)
    acc = acc + jnp.dot(yb_ref[...], wbf_s[ka:, :], preferred_element_type=F32)
    x = x_ref[...] + acc
    if final_norm:
        x = x * lax.rsqrt(jnp.mean(x * x, axis=-1, keepdims=True) + EPS) * g_ref[...]
    o_ref[...] = x


def _outp(ya, yb, w_out, x2, g, final_norm, tm=512):
    m, d = x2.shape
    ka, kb = ya.shape[1], yb.shape[1]
    assert w_out.shape == (ka + kb, d)
    return pl.pallas_call(
        functools.partial(_outp_kernel, final_norm=final_norm),
        grid=(m // tm,),
        in_specs=[
            pl.BlockSpec((tm, ka), lambda i: (i, 0)),
            pl.BlockSpec((tm, kb), lambda i: (i, 0)),
            pl.BlockSpec((ka + kb, d), lambda i: (0, 0), pipeline_mode=pl.Buffered(1)),
            pl.BlockSpec((tm, d), lambda i: (i, 0)),
            pl.BlockSpec((1, d), lambda i: (0, 0)),
        ],
        out_specs=pl.BlockSpec((tm, d), lambda i: (i, 0)),
        out_shape=jax.ShapeDtypeStruct((m, d), F32),
        scratch_shapes=[pltpu.VMEM((ka + kb, d), BF16)],
        compiler_params=_cparams(("arbitrary",)),
        name="outp",
    )(ya, yb, w_out, x2, g)


def _t5_bucket(dist):
    n = jnp.maximum(dist, 0)
    max_exact = REL_BUCKETS // 2
    nf = jnp.maximum(n, 1).astype(F32)
    large = max_exact + (jnp.log(nf / max_exact) / np.log(REL_MAX_DIST / max_exact)
                         * (REL_BUCKETS - max_exact)).astype(I32)
    large = jnp.minimum(large, REL_BUCKETS - 1)
    return jnp.where(n < max_exact, n, large)


def _bias_tiles(rel_bias):
    qw = K_CHUNK
    span = K_CHUNK + qw
    table = rel_bias[_t5_bucket(jnp.arange(span + 1, dtype=I32))].astype(F32)
    table = ((table[:span] - table[span:]) * LOG2E).T
    n = span + qw - 1
    a = jnp.concatenate([jnp.zeros((ATT_HEADS, qw - 1), F32), table], axis=1)
    shifted = jnp.tile(a, (1, span + 1))[:, :span * (n + 1)].reshape(ATT_HEADS, span, n + 1)
    tiles = shifted[:, ::-1, :qw].reshape(ATT_HEADS, 2, K_CHUNK, qw)
    return jnp.concatenate([jnp.zeros_like(tiles[:, :1]), tiles], axis=1)


def kernel(x, norm_g, w_in, conv_w, conv_b, lru_wa, lru_ba, lru_wx, lru_bx, lru_lambda, ckv_norm_g, idx_k_norm_g, idx_k_norm_b, w_uk, w_uv, w_out, rel_bias, final_norm_g):
    bsz, s, d = x.shape
    depth = w_in.shape[0]
    lru_w = lru_wa.shape[1] * lru_wa.shape[2]
    att_w = ATT_HEADS * HEAD_DIM
    idx_w = IDX_HEADS * IDX_DIM
    assert REL_MAX_DIST <= K_CHUNK
    assert lru_w == att_w == idx_w and att_w % KV_LATENT == 0
    topk = min(INDEX_TOPK, s // 4)

    o_q = 2 * lru_w
    o_ckv = o_q + att_w
    o_gb = o_ckv + KV_LATENT
    o_qi = o_gb + att_w
    o_ki = o_qi + idx_w
    tn = 512
    cols_a = {"xa": 0, "ga": lru_w, "gb": 2 * lru_w}
    cols_b = {"q": 0, "qi": att_w}
    n_f32, n_bf16 = 3 * lru_w, att_w + idx_w

    bias_tiles = _bias_tiles(rel_bias)
    x2 = x.reshape(bsz * s, d)
    for l in range(depth):
        order = [(0, o_q), (o_gb, o_qi), (o_q, o_ckv), (o_qi, o_ki), (o_ckv, o_gb), (o_ki, w_in.shape[2])]
        pa, pb, c, ct, kn, wit = _proj(x2, norm_g[l][None, :], w_in[l].T, order, ckv_norm_g[l][None, :],
                                       idx_k_norm_g[l][None, :], idx_k_norm_b[l][None, :], n_f32, n_bf16, s, tn=tn)
        pa3 = pa.reshape(bsz, s, -1)
        pb3 = pb.reshape(bsz, s, -1)

        ya = _rglru(pa3, cols_a, conv_w[l], conv_b[l][None, :], (0.5 * lru_wa[l]).astype(BF16),
                    0.5 * lru_ba[l][None, :], (0.5 * lru_wx[l]).astype(BF16), 0.5 * lru_bx[l][None, :],
                    lru_lambda[l][None, :])

        wukt = jnp.transpose(w_uk[l], (0, 2, 1)).astype(BF16)
        wuvt = jnp.transpose(w_uv[l], (0, 2, 1)).astype(BF16)
        yb = _dsa(pb3, cols_b, wukt, wit, kn.reshape(bsz, s, -1), c.reshape(bsz, s, -1), ct, pa3,
                  _col_block(cols_a["gb"], att_w), bias_tiles, wuvt, topk)

        x2 = _outp(ya.reshape(bsz * s, lru_w), yb.reshape(bsz * s, att_w), w_out[l], x2,
                   final_norm_g[None, :], final_norm=(l == depth - 1))
    return x2.reshape(bsz, s, d)
```

```python
import functools

import numpy as np
import jax
import jax.numpy as jnp
from jax import lax
from jax.experimental import pallas as pl
from jax.experimental.pallas import tpu as pltpu

F32 = jnp.float32
BF16 = jnp.bfloat16
I32 = jnp.int32

CONV_WIDTH = 4
LRU_C = 8.0
ATT_HEADS = 8
HEAD_DIM = 128
KV_LATENT = 256
IDX_HEADS = 16
IDX_DIM = 64
INDEX_TOPK = 256
REL_BUCKETS = 32
REL_MAX_DIST = 128
EPS = 1e-6
LOG2E = float(np.log2(np.e))
ONES_ROWS = 16

Q_TILE = 256
K_CHUNK = 128
ATT_CHUNK = 256
NEG = float(np.finfo(np.float32).min)
INT_MIN = -(2 ** 31)
VMEM_LIMIT = 56 * 1024 * 1024


def _cparams(sem):
    return pltpu.CompilerParams(dimension_semantics=sem, vmem_limit_bytes=VMEM_LIMIT)


def _col_block(offset, width):
    assert offset % width == 0
    return offset // width


def _proj_kernel(t_ref, x_ref, g_ref, wlo_ref, whi_ref, tail_ref, cg_ref, kg_ref, kb_ref,
                 oa_ref, ob_ref, c_ref, ct_ref, kn_ref, wit_ref, h_ref, *, na, nb):
    j = pl.program_id(1)
    r = pl.program_id(2)
    nt = (((1,), (1,)), ((), ()))

    def w_tile():
        lo = jnp.where(t_ref[2 * j] < 0, tail_ref[...], wlo_ref[...])
        hi = jnp.where(t_ref[2 * j + 1] < 0, tail_ref[...], whi_ref[...])
        return jnp.concatenate([lo, hi], axis=0).astype(BF16)

    @pl.when(j == 0)
    def _():
        x = x_ref[...]
        y = x * lax.rsqrt(jnp.mean(x * x, axis=-1, keepdims=True) + EPS)
        h = (y * g_ref[...]).astype(BF16)
        h_ref[r] = h
        oa_ref[...] = lax.dot_general(h, w_tile(), nt, preferred_element_type=F32)

    @pl.when((j > 0) & (j < na))
    def _():
        oa_ref[...] = lax.dot_general(h_ref[r], w_tile(), nt, preferred_element_type=F32)

    @pl.when((j >= na) & (j < na + nb))
    def _():
        ob_ref[...] = lax.dot_general(h_ref[r], w_tile(), nt, preferred_element_type=F32).astype(BF16)

    @pl.when(j >= na + nb)
    def _():
        used = KV_LATENT + 128
        tail = lax.dot_general(h_ref[r], w_tile()[:used], nt, preferred_element_type=F32)
        ckv = tail[:, :KV_LATENT]
        c = ckv * lax.rsqrt(jnp.mean(ckv * ckv, axis=-1, keepdims=True) + EPS) * cg_ref[...]
        c_ref[...] = c.astype(BF16)
        ct_ref[0:KV_LATENT, :] = c.T.astype(BF16)
        ct_ref[KV_LATENT:, :] = jnp.ones((ONES_ROWS, ct_ref.shape[1]), BF16)
        sm = tail[:, KV_LATENT:KV_LATENT + 128]
        ki = sm[:, :IDX_DIM]
        mu = jnp.mean(ki, axis=-1, keepdims=True)
        var = jnp.mean(jnp.square(ki - mu), axis=-1, keepdims=True)
        kn = (ki - mu) * lax.rsqrt(var + EPS) * kg_ref[...] + kb_ref[...]
        kn_ref[...] = kn.astype(BF16)
        wit_ref[...] = sm.T[IDX_DIM:IDX_DIM + IDX_HEADS, :] * (IDX_HEADS ** -0.5 * IDX_DIM ** -0.5)


def _proj(x2, g, w_t, order, ckv_g, k_g, k_b, n_f32, n_bf16, seq_len, tm=1024, tn=512, group=2):
    m, d = x2.shape
    n = w_t.shape[0]
    th = tn // 2
    n_whole = n // th
    src = []
    for start, stop in order:
        assert start % th == 0 and (stop % th == 0 or stop == n)
        src += list(range(start // th, -(-stop // th)))
    na, nb = n_f32 // tn, n_bf16 // tn
    assert n_f32 % tn == 0 and n_bf16 % tn == 0 and len(src) * th == n_f32 + n_bf16 + tn
    tail = jnp.pad(w_t[n_whole * th:], ((0, (n_whole + 1) * th - n), (0, 0)))
    table = jnp.asarray([blk if blk < n_whole else -1 for blk in src], I32)
    tps = seq_len // tm
    assert seq_len % tm == 0 and tn >= KV_LATENT + 128
    assert (m // tm) % group == 0
    nj = na + nb + 1
    const = lambda shape: pl.BlockSpec(shape, lambda i, j, r, t: (0,) * len(shape))
    last = group - 1
    row = lambda i, r_eff: i * group + r_eff
    per_seq = lambda g: (g // tps, 0, g % tps)
    tail_row = lambda i, j, r: row(i, jnp.where(j < nj - 1, 0, r))
    grid_spec = pltpu.PrefetchScalarGridSpec(
        num_scalar_prefetch=1,
        grid=(m // tm // group, nj, group),
        in_specs=[
            pl.BlockSpec((tm, d), lambda i, j, r, t: (row(i, jnp.where(j == 0, r, last)), 0)),
            const((1, d)),
            pl.BlockSpec((th, d), lambda i, j, r, t: (jnp.maximum(t[2 * j], 0), 0)),
            pl.BlockSpec((th, d), lambda i, j, r, t: (jnp.maximum(t[2 * j + 1], 0), 0)),
            const((th, d)),
            const((1, KV_LATENT)),
            const((1, IDX_DIM)),
            const((1, IDX_DIM)),
        ],
        out_specs=[
            pl.BlockSpec((tm, tn), lambda i, j, r, t: (row(i, jnp.where(j < na, r, last)), jnp.minimum(j, na - 1))),
            pl.BlockSpec((tm, tn), lambda i, j, r, t: (
                row(i, jnp.where(j < na, 0, jnp.where(j < na + nb, r, last))), jnp.clip(j - na, 0, nb - 1))),
            pl.BlockSpec((tm, KV_LATENT), lambda i, j, r, t: (tail_row(i, j, r), 0)),
            pl.BlockSpec((None, KV_LATENT + ONES_ROWS, tm), lambda i, j, r, t: per_seq(tail_row(i, j, r))),
            pl.BlockSpec((tm, IDX_DIM), lambda i, j, r, t: (tail_row(i, j, r), 0)),
            pl.BlockSpec((None, IDX_HEADS, tm), lambda i, j, r, t: per_seq(tail_row(i, j, r))),
        ],
        scratch_shapes=[pltpu.VMEM((group, tm, d), BF16)],
    )
    return pl.pallas_call(
        functools.partial(_proj_kernel, na=na, nb=nb),
        grid_spec=grid_spec,
        out_shape=[
            jax.ShapeDtypeStruct((m, n_f32), F32),
            jax.ShapeDtypeStruct((m, n_bf16), BF16),
            jax.ShapeDtypeStruct((m, KV_LATENT), BF16),
            jax.ShapeDtypeStruct((m // seq_len, KV_LATENT + ONES_ROWS, seq_len), BF16),
            jax.ShapeDtypeStruct((m, IDX_DIM), BF16),
            jax.ShapeDtypeStruct((m // seq_len, IDX_HEADS, seq_len), F32),
        ],
        compiler_params=_cparams(("arbitrary", "arbitrary", "arbitrary")),
        name="proj",
    )(table, x2, g, w_t, w_t, tail, ckv_g, k_g, k_b)


def _sigmoid(v):
    return 0.5 * jnp.tanh(0.5 * v) + 0.5


def _scan_step(a, b, k, axis, idx):
    keep = idx >= k
    a_prev = jnp.where(keep, pltpu.roll(a, k, axis=axis), 1.0)
    b_prev = jnp.where(keep, pltpu.roll(b, k, axis=axis), 0.0)
    return a * a_prev, a * b_prev + b


def _rglru_kernel(xa_ref, ga_ref, cw_ref, cb_ref, wa_ref, ba_ref, wx_ref, bx_ref, lam_ref,
                  o_ref, pad_s, a_s, b_s, c_s):
    s, w = xa_ref.shape
    tile = 8
    n_tiles = s // tile

    pad_s[0:tile, :] = jnp.zeros((tile, w), F32)
    pad_s[tile:tile + s, :] = xa_ref[...]
    acc = pad_s[tile:tile + s, :] * cw_ref[CONV_WIDTH - 1:CONV_WIDTH, :]
    for j in range(CONV_WIDTH - 1):
        back = CONV_WIDTH - 1 - j
        acc = acc + pad_s[tile - back:tile - back + s, :] * cw_ref[j:j + 1, :]
    xc = cb_ref[...] + acc

    xcb = xc.astype(BF16)
    tr = jnp.tanh(jnp.dot(xcb, wa_ref[...], preferred_element_type=F32) + ba_ref[...])
    ti = jnp.tanh(jnp.dot(xcb, wx_ref[...], preferred_element_type=F32) + bx_ref[...])
    i = 0.5 * ti + 0.5
    z = -lam_ref[...]
    softplus = jnp.maximum(z, 0.0) + jnp.log1p(jnp.exp(-jnp.abs(z)))
    half = (-0.5 * LRU_C) * softplus
    log_a = half * tr + half
    a = jnp.exp(log_a)
    m2 = (1.0 + a * a) * jnp.tanh(-log_a)
    mult = jnp.where(m2 > 0.0, m2 * lax.rsqrt(m2), 0.0)
    gated = i * xc
    b_s[...] = mult * gated
    b_s[0:1, :] = gated[0:1, :]

    a3 = a.reshape(n_tiles, tile, w)
    b3 = b_s[...].reshape(n_tiles, tile, w)
    sub = lax.broadcasted_iota(I32, (n_tiles, tile, w), 1)
    for k in (1, 2, 4):
        a3, b3 = _scan_step(a3, b3, k, 1, sub)
    a_s[...] = a3.reshape(s, w)
    b_s[...] = b3.reshape(s, w)

    at = a_s[pl.ds(tile - 1, n_tiles, stride=tile), :]
    bt = b_s[pl.ds(tile - 1, n_tiles, stride=tile), :]
    trow = lax.broadcasted_iota(I32, (n_tiles, w), 0)
    k = 1
    while k < n_tiles:
        at, bt = _scan_step(at, bt, k, 0, trow)
        k *= 2
    c_s[0:tile, :] = jnp.zeros((tile, w), F32)
    c_s[tile:tile + n_tiles, :] = bt

    def apply(t, carry):
        r0 = pl.multiple_of(t * tile, tile)
        before = c_s[pl.ds(tile - 1 + t, tile, stride=0), :]
        h = a_s[pl.ds(r0, tile), :] * before + b_s[pl.ds(r0, tile), :]
        gh = 0.5 * ga_ref[pl.ds(r0, tile), :]
        o_ref[pl.ds(r0, tile), :] = (h * (gh * (jnp.tanh(gh) + 1.0))).astype(o_ref.dtype)
        return carry

    lax.fori_loop(0, n_tiles, apply, 0, unroll=8)


def _rglru(pa3, cols, conv_w, conv_b, wa, ba, wx, bx, lam):
    bsz, s, _ = pa3.shape
    g, w = wa.shape[0], wa.shape[-1]
    xa_blk = _col_block(cols["xa"], w)
    ga_blk = _col_block(cols["ga"], w)
    vec = lambda: pl.BlockSpec((1, w), lambda b, j: (0, j))
    return pl.pallas_call(
        _rglru_kernel,
        grid=(bsz, g),
        in_specs=[
            pl.BlockSpec((None, s, w), lambda b, j: (b, 0, xa_blk + j)),
            pl.BlockSpec((None, s, w), lambda b, j: (b, 0, ga_blk + j)),
            pl.BlockSpec((CONV_WIDTH, w), lambda b, j: (0, j)),
            vec(),
            pl.BlockSpec((None, w, w), lambda b, j: (j, 0, 0)),
            vec(),
            pl.BlockSpec((None, w, w), lambda b, j: (j, 0, 0)),
            vec(),
            vec(),
        ],
        out_specs=pl.BlockSpec((None, s, w), lambda b, j: (b, 0, j)),
        out_shape=jax.ShapeDtypeStruct((bsz, s, g * w), BF16),
        scratch_shapes=[pltpu.VMEM((s + 8, w), F32), pltpu.VMEM((s, w), F32), pltpu.VMEM((s, w), F32),
                        pltpu.VMEM((s // 8 + 8, w), F32)],
        compiler_params=_cparams(("parallel", "parallel")),
        name="rglru",
    )(pa3, pa3, conv_w, conv_b, wa, ba, wx, bx, lam)


def _tree_sum(parts):
    while len(parts) > 1:
        paired = [parts[i] + parts[i + 1] for i in range(0, len(parts) - 1, 2)]
        parts = paired + ([parts[-1]] if len(parts) % 2 else [])
    return parts[0]


def _sortable_to_f32(u):
    key = u ^ INT_MIN
    return lax.bitcast_convert_type(key ^ ((key >> 31) & 0x7FFFFFFF), F32)


def _count_ge(ref, rows, cand, pack):
    chains = 4
    one, zero = jnp.ones((), ref.dtype), jnp.zeros((), ref.dtype)
    accs = [None] * chains
    for r in range(rows // pack):
        hit = jnp.where(ref[r * pack:(r + 1) * pack, :] >= cand, one, zero)
        accs[r % chains] = hit if accs[r % chains] is None else accs[r % chains] + hit
    parts = [a.astype(F32) for a in accs if a is not None]
    return jnp.sum(_tree_sum(parts), axis=0, keepdims=True)


def _kth_largest(score_ref, score16_ref, rows, k):
    def step16(i, u):
        cand = u | (jnp.int32(1) << (15 - i))
        cand_f = _sortable_to_f32(cand << 16).astype(BF16)
        return jnp.where(_count_ge(score16_ref, rows, cand_f, 16) >= k, cand, u)

    hi = lax.fori_loop(0, 16, step16, jnp.zeros((1, Q_TILE), I32))
    at_hi = _count_ge(score_ref, rows, _sortable_to_f32(hi << 16), 8)
    keeps = at_hi >= k
    hi = jnp.where(keeps, hi, jnp.maximum(hi - 1, 0))

    per_trip = 4

    def trip(state):
        i, u, at_u, _ = state
        for t in range(per_trip):
            cand = u | (jnp.int32(1) << (15 - (i + t)))
            cnt = _count_ge(score_ref, rows, _sortable_to_f32(cand), 8)
            take = cnt >= k
            u = jnp.where(take, cand, u)
            at_u = jnp.where(take, cnt, at_u)
        open_lanes = jnp.sum((at_u != k).astype(I32))
        return i + per_trip, u, at_u, open_lanes

    unknown = jnp.full((1, Q_TILE), -1.0, F32)
    state = (jnp.int32(0), hi << 16, jnp.where(keeps, at_hi, unknown), jnp.int32(1))
    _, u, _, _ = lax.while_loop(lambda st: (st[0] < 16) & (st[3] > 0), trip, state)
    return _sortable_to_f32(u)


def _loop_in_trips(n, body, init):
    def trips(start, count, width, carry):
        def group(i, c):
            for t in range(width):
                c = body(start + width * i + t, c)
            return c
        return lax.fori_loop(0, count, group, carry)

    carry = trips(0, n // 4, 4, init)
    carry = trips(4 * (n // 4), (n % 4) // 2, 2, carry)
    return trips(2 * (n // 2), n % 2, 1, carry)


def _dsa_kernel(q_ref, qi_ref, wukt_ref, wit_ref, kn_ref, c_ref, ct_ref, gb_ref, bias_ref, wuvt_ref,
                o_ref, qat_ref, qit_ref, score_s, score16_s, thr_s, lg_s, acc_s, topk):
    qb = pl.program_id(1)

    qt = q_ref[...].T
    scale = HEAD_DIM ** -0.5 * LOG2E
    for h in range(ATT_HEADS):
        qa = jnp.dot(wukt_ref[h], qt[h * HEAD_DIM:(h + 1) * HEAD_DIM], preferred_element_type=F32)
        qat_ref[h] = (qa * scale).astype(BF16)
    qit = qi_ref[...].T
    for h in range(IDX_HEADS):
        qit_ref[h] = qit[h * IDX_DIM:(h + 1) * IDX_DIM, :]
    q_tiles = Q_TILE // K_CHUNK
    nkc = (qb + 1) * q_tiles
    nac = (nkc * K_CHUNK + ATT_CHUNK - 1) // ATT_CHUNK
    tiles = ATT_CHUNK // K_CHUNK

    kiota = lax.broadcasted_iota(I32, (K_CHUNK, Q_TILE), 0)
    qpos = qb * Q_TILE + lax.broadcasted_iota(I32, (K_CHUNK, Q_TILE), 1)

    def score_chunk(ac, carry):
        for t in range(tiles):
            k0 = pl.multiple_of(ac * ATT_CHUNK + t * K_CHUNK, K_CHUNK)
            kn = kn_ref[pl.ds(k0, K_CHUNK), :]
            acc = jnp.zeros((K_CHUNK, Q_TILE), F32)
            for h in range(IDX_HEADS):
                sc = jnp.dot(kn, qit_ref[h], preferred_element_type=F32)
                acc = acc + jnp.maximum(sc, 0.0) * wit_ref[h:h + 1, :]
            masked = jnp.where(kiota + k0 <= qpos, acc, -jnp.inf)
            score_s[pl.ds(k0, K_CHUNK), :] = masked
            score16_s[pl.ds(k0, K_CHUNK), :] = masked.astype(BF16)
        return carry

    _loop_in_trips(nac, score_chunk, 0)

    for v in range(1, score_s.shape[0] // ATT_CHUNK + 1):
        @pl.when(nac == v)
        def _(rows=v * ATT_CHUNK):
            thr = _kth_largest(score_s, score16_s, rows, float(topk))
            thr = jnp.where(thr >= NEG, thr, NEG)
            thr_s[...] = jnp.broadcast_to(thr, thr_s.shape)

    thr = thr_s[0:1, :]

    def logit_chunk(ac, m8s):
        r0 = pl.multiple_of(ac * ATT_CHUNK, ATT_CHUNK)
        c_chunk = c_ref[pl.ds(r0, ATT_CHUNK), :]
        mbias = jnp.where(score_s[pl.ds(r0, ATT_CHUNK), :] >= thr, 0.0, NEG)
        near = [[jnp.clip(ac * tiles + t - (qb * q_tiles + j) + 2, 0, 2) for j in range(q_tiles)]
                for t in range(tiles)]
        out = []
        for h in range(ATT_HEADS):
            lg = jnp.dot(c_chunk, qat_ref[h], preferred_element_type=F32) + mbias
            lg = jnp.concatenate(
                [lg[t * K_CHUNK:(t + 1) * K_CHUNK]
                 + jnp.concatenate([bias_ref[h, near[t][j]] for j in range(q_tiles)], axis=1)
                 for t in range(tiles)], axis=0)
            lg_s[h, pl.ds(r0, ATT_CHUNK), :] = lg
            out.append(jnp.maximum(m8s[h], jnp.max(lg.reshape(ATT_CHUNK // 8, 8, Q_TILE), axis=0)))
        return tuple(out)

    m8s = _loop_in_trips(nac, logit_chunk, tuple(jnp.full((8, Q_TILE), NEG, F32) for _ in range(ATT_HEADS)))
    ms = [jnp.max(m8, axis=0, keepdims=True) for m8 in m8s]

    acc_s[...] = jnp.zeros(acc_s.shape, F32)

    def pv_chunk(ac, carry):
        r0 = pl.multiple_of(ac * ATT_CHUNK, ATT_CHUNK)
        ct_chunk = ct_ref[:, pl.ds(r0, ATT_CHUNK)]
        for h in range(ATT_HEADS):
            pr = jnp.exp2(lg_s[h, pl.ds(r0, ATT_CHUNK), :] - ms[h])
            acc_s[h] += jnp.dot(ct_chunk, pr.astype(BF16), preferred_element_type=F32)
        return carry

    _loop_in_trips(nac, pv_chunk, 0)

    for h in range(ATT_HEADS):
        denom = acc_s[h, KV_LATENT:KV_LATENT + 1, :]
        o_t = acc_s[h, 0:KV_LATENT, :] * (1.0 / denom)
        y_t = jnp.dot(wuvt_ref[h], o_t.astype(BF16), preferred_element_type=F32)
        gb = gb_ref[:, h * HEAD_DIM:(h + 1) * HEAD_DIM]
        o_ref[:, h * HEAD_DIM:(h + 1) * HEAD_DIM] = (y_t.T * (gb * _sigmoid(gb))).astype(o_ref.dtype)


def _dsa(pb3, cols_b, w_ukt, wit, kn, c, ct, pa3, gb_blk, bias_tiles, wuvt, topk):
    bsz, s, _ = c.shape
    att_w = ATT_HEADS * HEAD_DIM
    idx_w = IDX_HEADS * IDX_DIM
    assert s % ATT_CHUNK == 0 and s % Q_TILE == 0 and Q_TILE % K_CHUNK == 0 and ATT_CHUNK % K_CHUNK == 0
    assert s // 16 // 4 < 256
    q_blk = _col_block(cols_b["q"], att_w)
    qi_blk = _col_block(cols_b["qi"], idx_w)
    const = lambda shape: pl.BlockSpec(shape, lambda b, i: (0,) * len(shape))
    return pl.pallas_call(
        functools.partial(_dsa_kernel, topk=topk),
        grid=(bsz, s // Q_TILE),
        in_specs=[
            pl.BlockSpec((None, Q_TILE, att_w), lambda b, i: (b, i, q_blk)),
            pl.BlockSpec((None, Q_TILE, idx_w), lambda b, i: (b, i, qi_blk)),
            const(w_ukt.shape),
            pl.BlockSpec((None, IDX_HEADS, Q_TILE), lambda b, i: (b, 0, i)),
            pl.BlockSpec((None, s, IDX_DIM), lambda b, i: (b, 0, 0)),
            pl.BlockSpec((None, s, KV_LATENT), lambda b, i: (b, 0, 0)),
            pl.BlockSpec((None, KV_LATENT + ONES_ROWS, s), lambda b, i: (b, 0, 0)),
            pl.BlockSpec((None, Q_TILE, att_w), lambda b, i: (b, i, gb_blk)),
            const(bias_tiles.shape),
            const(wuvt.shape),
        ],
        out_specs=pl.BlockSpec((None, Q_TILE, att_w), lambda b, i: (b, i, 0)),
        out_shape=jax.ShapeDtypeStruct((bsz, s, att_w), BF16),
        scratch_shapes=[
            pltpu.VMEM((ATT_HEADS, KV_LATENT, Q_TILE), BF16),
            pltpu.VMEM((IDX_HEADS, IDX_DIM, Q_TILE), BF16),
            pltpu.VMEM((s, Q_TILE), F32),
            pltpu.VMEM((s, Q_TILE), BF16),
            pltpu.VMEM((8, Q_TILE), F32),
            pltpu.VMEM((ATT_HEADS, s, Q_TILE), F32),
            pltpu.VMEM((ATT_HEADS, KV_LATENT + ONES_ROWS, Q_TILE), F32),
        ],
        compiler_params=_cparams(("parallel", "arbitrary")),
        name="dsa",
    )(pb3, pb3, w_ukt, wit, kn, c, ct, pa3, bias_tiles, wuvt)


def _outp_kernel(ya_ref, yb_ref, w_ref, x_ref, g_ref, o_ref, wbf_s, *, final_norm):
    @pl.when(pl.program_id(0) == 0)
    def _():
        wbf_s[...] = w_ref[...].astype(BF16)

    ka = ya_ref.shape[1]
    acc = jnp.dot(ya_ref[...], wbf_s[0:ka, :], preferred_element_type=F32)
    acc = acc + jnp.dot(yb_ref[...], wbf_s[ka:, :], preferred_element_type=F32)
    x = x_ref[...] + acc
    if final_norm:
        x = x * lax.rsqrt(jnp.mean(x * x, axis=-1, keepdims=True) + EPS) * g_ref[...]
    o_ref[...] = x


def _outp(ya, yb, w_out, x2, g, final_norm, tm=512):
    m, d = x2.shape
    ka, kb = ya.shape[1], yb.shape[1]
    assert w_out.shape == (ka + kb, d)
    return pl.pallas_call(
        functools.partial(_outp_kernel, final_norm=final_norm),
        grid=(m // tm,),
        in_specs=[
            pl.BlockSpec((tm, ka), lambda i: (i, 0)),
            pl.BlockSpec((tm, kb), lambda i: (i, 0)),
            pl.BlockSpec((ka + kb, d), lambda i: (0, 0), pipeline_mode=pl.Buffered(1)),
            pl.BlockSpec((tm, d), lambda i: (i, 0)),
            pl.BlockSpec((1, d), lambda i: (0, 0)),
        ],
        out_specs=pl.BlockSpec((tm, d), lambda i: (i, 0)),
        out_shape=jax.ShapeDtypeStruct((m, d), F32),
        scratch_shapes=[pltpu.VMEM((ka + kb, d), BF16)],
        compiler_params=_cparams(("arbitrary",)),
        name="outp",
    )(ya, yb, w_out, x2, g)


def _t5_bucket(dist):
    n = jnp.maximum(dist, 0)
    max_exact = REL_BUCKETS // 2
    nf = jnp.maximum(n, 1).astype(F32)
    large = max_exact + (jnp.log(nf / max_exact) / np.log(REL_MAX_DIST / max_exact)
                         * (REL_BUCKETS - max_exact)).astype(I32)
    large = jnp.minimum(large, REL_BUCKETS - 1)
    return jnp.where(n < max_exact, n, large)


def _bias_tiles(rel_bias):
    qw = K_CHUNK
    span = K_CHUNK + qw
    table = rel_bias[_t5_bucket(jnp.arange(span + 1, dtype=I32))].astype(F32)
    table = ((table[:span] - table[span:]) * LOG2E).T
    n = span + qw - 1
    a = jnp.concatenate([jnp.zeros((ATT_HEADS, qw - 1), F32), table], axis=1)
    shifted = jnp.tile(a, (1, span + 1))[:, :span * (n + 1)].reshape(ATT_HEADS, span, n + 1)
    tiles = shifted[:, ::-1, :qw].reshape(ATT_HEADS, 2, K_CHUNK, qw)
    return jnp.concatenate([jnp.zeros_like(tiles[:, :1]), tiles], axis=1)


def kernel(x, norm_g, w_in, conv_w, conv_b, lru_wa, lru_ba, lru_wx, lru_bx, lru_lambda, ckv_norm_g, idx_k_norm_g, idx_k_norm_b, w_uk, w_uv, w_out, rel_bias, final_norm_g):
    bsz, s, d = x.shape
    depth = w_in.shape[0]
    lru_w = lru_wa.shape[1] * lru_wa.shape[2]
    att_w = ATT_HEADS * HEAD_DIM
    idx_w = IDX_HEADS * IDX_DIM
    assert REL_MAX_DIST <= K_CHUNK
    assert lru_w == att_w == idx_w and att_w % KV_LATENT == 0
    topk = min(INDEX_TOPK, s // 4)

    o_q = 2 * lru_w
    o_ckv = o_q + att_w
    o_gb = o_ckv + KV_LATENT
    o_qi = o_gb + att_w
    o_ki = o_qi + idx_w
    tn = 512
    cols_a = {"xa": 0, "ga": lru_w, "gb": 2 * lru_w}
    cols_b = {"q": 0, "qi": att_w}
    n_f32, n_bf16 = 3 * lru_w, att_w + idx_w

    bias_tiles = _bias_tiles(rel_bias)
    x2 = x.reshape(bsz * s, d)
    for l in range(depth):
        order = [(0, o_q), (o_gb, o_qi), (o_q, o_ckv), (o_qi, o_ki), (o_ckv, o_gb), (o_ki, w_in.shape[2])]
        pa, pb, c, ct, kn, wit = _proj(x2, norm_g[l][None, :], w_in[l].T, order, ckv_norm_g[l][None, :],
                                       idx_k_norm_g[l][None, :], idx_k_norm_b[l][None, :], n_f32, n_bf16, s, tn=tn)
        pa3 = pa.reshape(bsz, s, -1)
        pb3 = pb.reshape(bsz, s, -1)

        ya = _rglru(pa3, cols_a, conv_w[l], conv_b[l][None, :], (0.5 * lru_wa[l]).astype(BF16),
                    0.5 * lru_ba[l][None, :], (0.5 * lru_wx[l]).astype(BF16), 0.5 * lru_bx[l][None, :],
                    lru_lambda[l][None, :])

        wukt = jnp.transpose(w_uk[l], (0, 2, 1)).astype(BF16)
        wuvt = jnp.transpose(w_uv[l], (0, 2, 1)).astype(BF16)
        yb = _dsa(pb3, cols_b, wukt, wit, kn.reshape(bsz, s, -1), c.reshape(bsz, s, -1), ct, pa3,
                  _col_block(cols_a["gb"], att_w), bias_tiles, wuvt, topk)

        x2 = _outp(ya.reshape(bsz * s, lru_w), yb.reshape(bsz * s, att_w), w_out[l], x2,
                   final_norm_g[None, :], final_norm=(l == depth - 1))
    return x2.reshape(bsz, s, d)
```

```python
import functools

import numpy as np
import jax
import jax.numpy as jnp
from jax import lax
from jax.experimental import pallas as pl
from jax.experimental.pallas import tpu as pltpu

F32 = jnp.float32
BF16 = jnp.bfloat16
I32 = jnp.int32

CONV_WIDTH = 4
LRU_C = 8.0
ATT_HEADS = 8
HEAD_DIM = 128
KV_LATENT = 256
IDX_HEADS = 16
IDX_DIM = 64
INDEX_TOPK = 256
REL_BUCKETS = 32
REL_MAX_DIST = 128
EPS = 1e-6
LOG2E = float(np.log2(np.e))
ONES_ROWS = 16

Q_TILE = 256
K_CHUNK = 128
ATT_CHUNK = 256
NEG = float(np.finfo(np.float32).min)
INT_MIN = -(2 ** 31)
VMEM_LIMIT = 56 * 1024 * 1024


def _cparams(sem):
    return pltpu.CompilerParams(dimension_semantics=sem, vmem_limit_bytes=VMEM_LIMIT)


def _col_block(offset, width):
    assert offset % width == 0
    return offset // width


def _proj_kernel(t_ref, x_ref, g_ref, wlo_ref, whi_ref, tail_ref, cg_ref, kg_ref, kb_ref,
                 oa_ref, ob_ref, c_ref, ct_ref, kn_ref, wit_ref, h_ref, *, na, nb):
    j = pl.program_id(1)
    r = pl.program_id(2)
    nt = (((1,), (1,)), ((), ()))

    def w_tile():
        lo = jnp.where(t_ref[2 * j] < 0, tail_ref[...], wlo_ref[...])
        hi = jnp.where(t_ref[2 * j + 1] < 0, tail_ref[...], whi_ref[...])
        return jnp.concatenate([lo, hi], axis=0).astype(BF16)

    @pl.when(j == 0)
    def _():
        x = x_ref[...]
        y = x * lax.rsqrt(jnp.mean(x * x, axis=-1, keepdims=True) + EPS)
        h = (y * g_ref[...]).astype(BF16)
        h_ref[r] = h
        oa_ref[...] = lax.dot_general(h, w_tile(), nt, preferred_element_type=F32)

    @pl.when((j > 0) & (j < na))
    def _():
        oa_ref[...] = lax.dot_general(h_ref[r], w_tile(), nt, preferred_element_type=F32)

    @pl.when((j >= na) & (j < na + nb))
    def _():
        ob_ref[...] = lax.dot_general(h_ref[r], w_tile(), nt, preferred_element_type=F32).astype(BF16)

    @pl.when(j >= na + nb)
    def _():
        used = KV_LATENT + 128
        tail = lax.dot_general(h_ref[r], w_tile()[:used], nt, preferred_element_type=F32)
        ckv = tail[:, :KV_LATENT]
        c = ckv * lax.rsqrt(jnp.mean(ckv * ckv, axis=-1, keepdims=True) + EPS) * cg_ref[...]
        c_ref[...] = c.astype(BF16)
        ct_ref[0:KV_LATENT, :] = c.T.astype(BF16)
        ct_ref[KV_LATENT:, :] = jnp.ones((ONES_ROWS, ct_ref.shape[1]), BF16)
        sm = tail[:, KV_LATENT:KV_LATENT + 128]
        ki = sm[:, :IDX_DIM]
        mu = jnp.mean(ki, axis=-1, keepdims=True)
        var = jnp.mean(jnp.square(ki - mu), axis=-1, keepdims=True)
        kn = (ki - mu) * lax.rsqrt(var + EPS) * kg_ref[...] + kb_ref[...]
        kn_ref[...] = kn.astype(BF16)
        wit_ref[...] = sm.T[IDX_DIM:IDX_DIM + IDX_HEADS, :] * (IDX_HEADS ** -0.5 * IDX_DIM ** -0.5)


def _proj(x2, g, w_t, order, ckv_g, k_g, k_b, n_f32, n_bf16, seq_len, tm=1024, tn=512, group=2):
    m, d = x2.shape
    n = w_t.shape[0]
    th = tn // 2
    n_whole = n // th
    src = []
    for start, stop in order:
        assert start % th == 0 and (stop % th == 0 or stop == n)
        src += list(range(start // th, -(-stop // th)))
    na, nb = n_f32 // tn, n_bf16 // tn
    assert n_f32 % tn == 0 and n_bf16 % tn == 0 and len(src) * th == n_f32 + n_bf16 + tn
    tail = jnp.pad(w_t[n_whole * th:], ((0, (n_whole + 1) * th - n), (0, 0)))
    table = jnp.asarray([blk if blk < n_whole else -1 for blk in src], I32)
    tps = seq_len // tm
    assert seq_len % tm == 0 and tn >= KV_LATENT + 128
    assert (m // tm) % group == 0
    nj = na + nb + 1
    const = lambda shape: pl.BlockSpec(shape, lambda i, j, r, t: (0,) * len(shape))
    last = group - 1
    row = lambda i, r_eff: i * group + r_eff
    per_seq = lambda g: (g // tps, 0, g % tps)
    tail_row = lambda i, j, r: row(i, jnp.where(j < nj - 1, 0, r))
    grid_spec = pltpu.PrefetchScalarGridSpec(
        num_scalar_prefetch=1,
        grid=(m // tm // group, nj, group),
        in_specs=[
            pl.BlockSpec((tm, d), lambda i, j, r, t: (row(i, jnp.where(j == 0, r, last)), 0)),
            const((1, d)),
            pl.BlockSpec((th, d), lambda i, j, r, t: (jnp.maximum(t[2 * j], 0), 0)),
            pl.BlockSpec((th, d), lambda i, j, r, t: (jnp.maximum(t[2 * j + 1], 0), 0)),
            const((th, d)),
            const((1, KV_LATENT)),
            const((1, IDX_DIM)),
            const((1, IDX_DIM)),
        ],
        out_specs=[
            pl.BlockSpec((tm, tn), lambda i, j, r, t: (row(i, jnp.where(j < na, r, last)), jnp.minimum(j, na - 1))),
            pl.BlockSpec((tm, tn), lambda i, j, r, t: (
                row(i, jnp.where(j < na, 0, jnp.where(j < na + nb, r, last))), jnp.clip(j - na, 0, nb - 1))),
            pl.BlockSpec((tm, KV_LATENT), lambda i, j, r, t: (tail_row(i, j, r), 0)),
            pl.BlockSpec((None, KV_LATENT + ONES_ROWS, tm), lambda i, j, r, t: per_seq(tail_row(i, j, r))),
            pl.BlockSpec((tm, IDX_DIM), lambda i, j, r, t: (tail_row(i, j, r), 0)),
            pl.BlockSpec((None, IDX_HEADS, tm), lambda i, j, r, t: per_seq(tail_row(i, j, r))),
        ],
        scratch_shapes=[pltpu.VMEM((group, tm, d), BF16)],
    )
    return pl.pallas_call(
        functools.partial(_proj_kernel, na=na, nb=nb),
        grid_spec=grid_spec,
        out_shape=[
            jax.ShapeDtypeStruct((m, n_f32), F32),
            jax.ShapeDtypeStruct((m, n_bf16), BF16),
            jax.ShapeDtypeStruct((m, KV_LATENT), BF16),
            jax.ShapeDtypeStruct((m // seq_len, KV_LATENT + ONES_ROWS, seq_len), BF16),
            jax.ShapeDtypeStruct((m, IDX_DIM), BF16),
            jax.ShapeDtypeStruct((m // seq_len, IDX_HEADS, seq_len), F32),
        ],
        compiler_params=_cparams(("arbitrary", "arbitrary", "arbitrary")),
        name="proj",
    )(table, x2, g, w_t, w_t, tail, ckv_g, k_g, k_b)


def _sigmoid(v):
    return 0.5 * jnp.tanh(0.5 * v) + 0.5


def _scan_step(a, b, k, axis, idx):
    keep = idx >= k
    a_prev = jnp.where(keep, pltpu.roll(a, k, axis=axis), 1.0)
    b_prev = jnp.where(keep, pltpu.roll(b, k, axis=axis), 0.0)
    return a * a_prev, a * b_prev + b


def _rglru_kernel(xa_ref, ga_ref, cw_ref, cb_ref, wa_ref, ba_ref, wx_ref, bx_ref, lam_ref,
                  o_ref, pad_s, a_s, b_s, c_s):
    s, w = xa_ref.shape
    tile = 8
    n_tiles = s // tile

    pad_s[0:tile, :] = jnp.zeros((tile, w), F32)
    pad_s[tile:tile + s, :] = xa_ref[...]
    acc = pad_s[tile:tile + s, :] * cw_ref[CONV_WIDTH - 1:CONV_WIDTH, :]
    for j in range(CONV_WIDTH - 1):
        back = CONV_WIDTH - 1 - j
        acc = acc + pad_s[tile - back:tile - back + s, :] * cw_ref[j:j + 1, :]
    xc = cb_ref[...] + acc

    xcb = xc.astype(BF16)
    tr = jnp.tanh(jnp.dot(xcb, wa_ref[...], preferred_element_type=F32) + ba_ref[...])
    ti = jnp.tanh(jnp.dot(xcb, wx_ref[...], preferred_element_type=F32) + bx_ref[...])
    i = 0.5 * ti + 0.5
    z = -lam_ref[...]
    softplus = jnp.maximum(z, 0.0) + jnp.log1p(jnp.exp(-jnp.abs(z)))
    half = (-0.5 * LRU_C) * softplus
    log_a = half * tr + half
    a = jnp.exp(log_a)
    m2 = (1.0 + a * a) * jnp.tanh(-log_a)
    mult = jnp.where(m2 > 0.0, m2 * lax.rsqrt(m2), 0.0)
    gated = i * xc
    b_s[...] = mult * gated
    b_s[0:1, :] = gated[0:1, :]

    a3 = a.reshape(n_tiles, tile, w)
    b3 = b_s[...].reshape(n_tiles, tile, w)
    sub = lax.broadcasted_iota(I32, (n_tiles, tile, w), 1)
    for k in (1, 2, 4):
        a3, b3 = _scan_step(a3, b3, k, 1, sub)
    a_s[...] = a3.reshape(s, w)
    b_s[...] = b3.reshape(s, w)

    at = a_s[pl.ds(tile - 1, n_tiles, stride=tile), :]
    bt = b_s[pl.ds(tile - 1, n_tiles, stride=tile), :]
    trow = lax.broadcasted_iota(I32, (n_tiles, w), 0)
    k = 1
    while k < n_tiles:
        at, bt = _scan_step(at, bt, k, 0, trow)
        k *= 2
    c_s[0:tile, :] = jnp.zeros((tile, w), F32)
    c_s[tile:tile + n_tiles, :] = bt

    for t in range(n_tiles):
        rows = slice(t * tile, (t + 1) * tile)
        before = c_s[pl.ds(tile - 1 + t, tile, stride=0), :]
        h = a_s[rows, :] * before + b_s[rows, :]
        gh = 0.5 * ga_ref[rows, :]
        o_ref[rows, :] = (h * (gh * (jnp.tanh(gh) + 1.0))).astype(o_ref.dtype)


def _rglru(pa3, cols, conv_w, conv_b, wa, ba, wx, bx, lam):
    bsz, s, _ = pa3.shape
    g, w = wa.shape[0], wa.shape[-1]
    xa_blk = _col_block(cols["xa"], w)
    ga_blk = _col_block(cols["ga"], w)
    vec = lambda: pl.BlockSpec((1, w), lambda b, j: (0, j))
    return pl.pallas_call(
        _rglru_kernel,
        grid=(bsz, g),
        in_specs=[
            pl.BlockSpec((None, s, w), lambda b, j: (b, 0, xa_blk + j)),
            pl.BlockSpec((None, s, w), lambda b, j: (b, 0, ga_blk + j)),
            pl.BlockSpec((CONV_WIDTH, w), lambda b, j: (0, j)),
            vec(),
            pl.BlockSpec((None, w, w), lambda b, j: (j, 0, 0)),
            vec(),
            pl.BlockSpec((None, w, w), lambda b, j: (j, 0, 0)),
            vec(),
            vec(),
        ],
        out_specs=pl.BlockSpec((None, s, w), lambda b, j: (b, 0, j)),
        out_shape=jax.ShapeDtypeStruct((bsz, s, g * w), BF16),
        scratch_shapes=[pltpu.VMEM((s + 8, w), F32), pltpu.VMEM((s, w), F32), pltpu.VMEM((s, w), F32),
                        pltpu.VMEM((s // 8 + 8, w), F32)],
        compiler_params=_cparams(("parallel", "parallel")),
        name="rglru",
    )(pa3, pa3, conv_w, conv_b, wa, ba, wx, bx, lam)


def _tree_sum(parts):
    while len(parts) > 1:
        paired = [parts[i] + parts[i + 1] for i in range(0, len(parts) - 1, 2)]
        parts = paired + ([parts[-1]] if len(parts) % 2 else [])
    return parts[0]


def _sortable_to_f32(u):
    key = u ^ INT_MIN
    return lax.bitcast_convert_type(key ^ ((key >> 31) & 0x7FFFFFFF), F32)


def _count_ge(ref, rows, cand, pack):
    chains = 4
    one, zero = jnp.ones((), ref.dtype), jnp.zeros((), ref.dtype)
    accs = [None] * chains
    for r in range(rows // pack):
        hit = jnp.where(ref[r * pack:(r + 1) * pack, :] >= cand, one, zero)
        accs[r % chains] = hit if accs[r % chains] is None else accs[r % chains] + hit
    parts = [a.astype(F32) for a in accs if a is not None]
    return jnp.sum(_tree_sum(parts), axis=0, keepdims=True)


def _kth_largest(score_ref, score16_ref, rows, k):
    def step16(i, u):
        cand = u | (jnp.int32(1) << (15 - i))
        cand_f = _sortable_to_f32(cand << 16).astype(BF16)
        return jnp.where(_count_ge(score16_ref, rows, cand_f, 16) >= k, cand, u)

    hi = lax.fori_loop(0, 16, step16, jnp.zeros((1, Q_TILE), I32))
    at_hi = _count_ge(score_ref, rows, _sortable_to_f32(hi << 16), 8)
    keeps = at_hi >= k
    hi = jnp.where(keeps, hi, jnp.maximum(hi - 1, 0))

    per_trip = 4

    def trip(state):
        i, u, at_u, _ = state
        for t in range(per_trip):
            cand = u | (jnp.int32(1) << (15 - (i + t)))
            cnt = _count_ge(score_ref, rows, _sortable_to_f32(cand), 8)
            take = cnt >= k
            u = jnp.where(take, cand, u)
            at_u = jnp.where(take, cnt, at_u)
        open_lanes = jnp.sum((at_u != k).astype(I32))
        return i + per_trip, u, at_u, open_lanes

    unknown = jnp.full((1, Q_TILE), -1.0, F32)
    state = (jnp.int32(0), hi << 16, jnp.where(keeps, at_hi, unknown), jnp.int32(1))
    _, u, _, _ = lax.while_loop(lambda st: (st[0] < 16) & (st[3] > 0), trip, state)
    return _sortable_to_f32(u)


def _loop_in_trips(n, body, init):
    def trips(start, count, width, carry):
        def group(i, c):
            for t in range(width):
                c = body(start + width * i + t, c)
            return c
        return lax.fori_loop(0, count, group, carry)

    carry = trips(0, n // 4, 4, init)
    carry = trips(4 * (n // 4), (n % 4) // 2, 2, carry)
    return trips(2 * (n // 2), n % 2, 1, carry)


def _dsa_kernel(q_ref, qi_ref, wukt_ref, wit_ref, kn_ref, c_ref, ct_ref, gb_ref, bias_ref, wuvt_ref,
                o_ref, qat_ref, qit_ref, score_s, score16_s, thr_s, lg_s, acc_s, topk):
    qb = pl.program_id(1)

    qt = q_ref[...].T
    scale = HEAD_DIM ** -0.5 * LOG2E
    for h in range(ATT_HEADS):
        qa = jnp.dot(wukt_ref[h], qt[h * HEAD_DIM:(h + 1) * HEAD_DIM], preferred_element_type=F32)
        qat_ref[h] = (qa * scale).astype(BF16)
    qit = qi_ref[...].T
    for h in range(IDX_HEADS):
        qit_ref[h] = qit[h * IDX_DIM:(h + 1) * IDX_DIM, :]
    q_tiles = Q_TILE // K_CHUNK
    nkc = (qb + 1) * q_tiles
    nac = (nkc * K_CHUNK + ATT_CHUNK - 1) // ATT_CHUNK
    tiles = ATT_CHUNK // K_CHUNK

    kiota = lax.broadcasted_iota(I32, (K_CHUNK, Q_TILE), 0)
    qpos = qb * Q_TILE + lax.broadcasted_iota(I32, (K_CHUNK, Q_TILE), 1)

    def score_chunk(ac, carry):
        for t in range(tiles):
            k0 = pl.multiple_of(ac * ATT_CHUNK + t * K_CHUNK, K_CHUNK)
            kn = kn_ref[pl.ds(k0, K_CHUNK), :]
            acc = jnp.zeros((K_CHUNK, Q_TILE), F32)
            for h in range(IDX_HEADS):
                sc = jnp.dot(kn, qit_ref[h], preferred_element_type=F32)
                acc = acc + jnp.maximum(sc, 0.0) * wit_ref[h:h + 1, :]
            masked = jnp.where(kiota + k0 <= qpos, acc, -jnp.inf)
            score_s[pl.ds(k0, K_CHUNK), :] = masked
            score16_s[pl.ds(k0, K_CHUNK), :] = masked.astype(BF16)
        return carry

    _loop_in_trips(nac, score_chunk, 0)

    for v in range(1, score_s.shape[0] // ATT_CHUNK + 1):
        @pl.when(nac == v)
        def _(rows=v * ATT_CHUNK):
            thr = _kth_largest(score_s, score16_s, rows, float(topk))
            thr = jnp.where(thr >= NEG, thr, NEG)
            thr_s[...] = jnp.broadcast_to(thr, thr_s.shape)

    thr = thr_s[0:1, :]

    def logit_chunk(ac, m8s):
        r0 = pl.multiple_of(ac * ATT_CHUNK, ATT_CHUNK)
        c_chunk = c_ref[pl.ds(r0, ATT_CHUNK), :]
        mbias = jnp.where(score_s[pl.ds(r0, ATT_CHUNK), :] >= thr, 0.0, NEG)
        near = [[jnp.clip(ac * tiles + t - (qb * q_tiles + j) + 2, 0, 2) for j in range(q_tiles)]
                for t in range(tiles)]
        out = []
        for h in range(ATT_HEADS):
            lg = jnp.dot(c_chunk, qat_ref[h], preferred_element_type=F32) + mbias
            lg = jnp.concatenate(
                [lg[t * K_CHUNK:(t + 1) * K_CHUNK]
                 + jnp.concatenate([bias_ref[h, near[t][j]] for j in range(q_tiles)], axis=1)
                 for t in range(tiles)], axis=0)
            lg_s[h, pl.ds(r0, ATT_CHUNK), :] = lg
            out.append(jnp.maximum(m8s[h], jnp.max(lg.reshape(ATT_CHUNK // 8, 8, Q_TILE), axis=0)))
        return tuple(out)

    m8s = _loop_in_trips(nac, logit_chunk, tuple(jnp.full((8, Q_TILE), NEG, F32) for _ in range(ATT_HEADS)))
    ms = [jnp.max(m8, axis=0, keepdims=True) for m8 in m8s]

    acc_s[...] = jnp.zeros(acc_s.shape, F32)

    def pv_chunk(ac, carry):
        r0 = pl.multiple_of(ac * ATT_CHUNK, ATT_CHUNK)
        ct_chunk = ct_ref[:, pl.ds(r0, ATT_CHUNK)]
        for h in range(ATT_HEADS):
            pr = jnp.exp2(lg_s[h, pl.ds(r0, ATT_CHUNK), :] - ms[h])
            acc_s[h] += jnp.dot(ct_chunk, pr.astype(BF16), preferred_element_type=F32)
        return carry

    _loop_in_trips(nac, pv_chunk, 0)

    for h in range(ATT_HEADS):
        denom = acc_s[h, KV_LATENT:KV_LATENT + 1, :]
        o_t = acc_s[h, 0:KV_LATENT, :] * (1.0 / denom)
        y_t = jnp.dot(wuvt_ref[h], o_t.astype(BF16), preferred_element_type=F32)
        gb = gb_ref[:, h * HEAD_DIM:(h + 1) * HEAD_DIM]
        o_ref[:, h * HEAD_DIM:(h + 1) * HEAD_DIM] = (y_t.T * (gb * _sigmoid(gb))).astype(o_ref.dtype)


def _dsa(pb3, cols_b, w_ukt, wit, kn, c, ct, pa3, gb_blk, bias_tiles, wuvt, topk):
    bsz, s, _ = c.shape
    att_w = ATT_HEADS * HEAD_DIM
    idx_w = IDX_HEADS * IDX_DIM
    assert s % ATT_CHUNK == 0 and s % Q_TILE == 0 and Q_TILE % K_CHUNK == 0 and ATT_CHUNK % K_CHUNK == 0
    assert s // 16 // 4 < 256
    q_blk = _col_block(cols_b["q"], att_w)
    qi_blk = _col_block(cols_b["qi"], idx_w)
    const = lambda shape: pl.BlockSpec(shape, lambda b, i: (0,) * len(shape))
    return pl.pallas_call(
        functools.partial(_dsa_kernel, topk=topk),
        grid=(bsz, s // Q_TILE),
        in_specs=[
            pl.BlockSpec((None, Q_TILE, att_w), lambda b, i: (b, i, q_blk)),
            pl.BlockSpec((None, Q_TILE, idx_w), lambda b, i: (b, i, qi_blk)),
            const(w_ukt.shape),
            pl.BlockSpec((None, IDX_HEADS, Q_TILE), lambda b, i: (b, 0, i)),
            pl.BlockSpec((None, s, IDX_DIM), lambda b, i: (b, 0, 0)),
            pl.BlockSpec((None, s, KV_LATENT), lambda b, i: (b, 0, 0)),
            pl.BlockSpec((None, KV_LATENT + ONES_ROWS, s), lambda b, i: (b, 0, 0)),
            pl.BlockSpec((None, Q_TILE, att_w), lambda b, i: (b, i, gb_blk)),
            const(bias_tiles.shape),
            const(wuvt.shape),
        ],
        out_specs=pl.BlockSpec((None, Q_TILE, att_w), lambda b, i: (b, i, 0)),
        out_shape=jax.ShapeDtypeStruct((bsz, s, att_w), BF16),
        scratch_shapes=[
            pltpu.VMEM((ATT_HEADS, KV_LATENT, Q_TILE), BF16),
            pltpu.VMEM((IDX_HEADS, IDX_DIM, Q_TILE), BF16),
            pltpu.VMEM((s, Q_TILE), F32),
            pltpu.VMEM((s, Q_TILE), BF16),
            pltpu.VMEM((8, Q_TILE), F32),
            pltpu.VMEM((ATT_HEADS, s, Q_TILE), F32),
            pltpu.VMEM((ATT_HEADS, KV_LATENT + ONES_ROWS, Q_TILE), F32),
        ],
        compiler_params=_cparams(("parallel", "arbitrary")),
        name="dsa",
    )(pb3, pb3, w_ukt, wit, kn, c, ct, pa3, bias_tiles, wuvt)


def _outp_kernel(ya_ref, yb_ref, w_ref, x_ref, g_ref, o_ref, wbf_s, *, final_norm):
    @pl.when(pl.program_id(0) == 0)
    def _():
        wbf_s[...] = w_ref[...].astype(BF16)

    ka = ya_ref.shape[1]
    acc = jnp.dot(ya_ref[...], wbf_s[0:ka, :], preferred_element_type=F32)
    acc = acc + jnp.dot(yb_ref[...], wbf_s[ka:, :], preferred_element_type=F32)
    x = x_ref[...] + acc
    if final_norm:
        x = x * lax.rsqrt(jnp.mean(x * x, axis=-1, keepdims=True) + EPS) * g_ref[...]
    o_ref[...] = x


def _outp(ya, yb, w_out, x2, g, final_norm, tm=512):
    m, d = x2.shape
    ka, kb = ya.shape[1], yb.shape[1]
    assert w_out.shape == (ka + kb, d)
    return pl.pallas_call(
        functools.partial(_outp_kernel, final_norm=final_norm),
        grid=(m // tm,),
        in_specs=[
            pl.BlockSpec((tm, ka), lambda i: (i, 0)),
            pl.BlockSpec((tm, kb), lambda i: (i, 0)),
            pl.BlockSpec((ka + kb, d), lambda i: (0, 0), pipeline_mode=pl.Buffered(1)),
            pl.BlockSpec((tm, d), lambda i: (i, 0)),
            pl.BlockSpec((1, d), lambda i: (0, 0)),
        ],
        out_specs=pl.BlockSpec((tm, d), lambda i: (i, 0)),
        out_shape=jax.ShapeDtypeStruct((m, d), F32),
        scratch_shapes=[pltpu.VMEM((ka + kb, d), BF16)],
        compiler_params=_cparams(("arbitrary",)),
        name="outp",
    )(ya, yb, w_out, x2, g)


def _t5_bucket(dist):
    n = jnp.maximum(dist, 0)
    max_exact = REL_BUCKETS // 2
    nf = jnp.maximum(n, 1).astype(F32)
    large = max_exact + (jnp.log(nf / max_exact) / np.log(REL_MAX_DIST / max_exact)
                         * (REL_BUCKETS - max_exact)).astype(I32)
    large = jnp.minimum(large, REL_BUCKETS - 1)
    return jnp.where(n < max_exact, n, large)


def _bias_tiles(rel_bias):
    qw = K_CHUNK
    span = K_CHUNK + qw
    table = rel_bias[_t5_bucket(jnp.arange(span + 1, dtype=I32))].astype(F32)
    table = ((table[:span] - table[span:]) * LOG2E).T
    n = span + qw - 1
    a = jnp.concatenate([jnp.zeros((ATT_HEADS, qw - 1), F32), table], axis=1)
    shifted = jnp.tile(a, (1, span + 1))[:, :span * (n + 1)].reshape(ATT_HEADS, span, n + 1)
    tiles = shifted[:, ::-1, :qw].reshape(ATT_HEADS, 2, K_CHUNK, qw)
    return jnp.concatenate([jnp.zeros_like(tiles[:, :1]), tiles], axis=1)


def kernel(x, norm_g, w_in, conv_w, conv_b, lru_wa, lru_ba, lru_wx, lru_bx, lru_lambda, ckv_norm_g, idx_k_norm_g, idx_k_norm_b, w_uk, w_uv, w_out, rel_bias, final_norm_g):
    bsz, s, d = x.shape
    depth = w_in.shape[0]
    lru_w = lru_wa.shape[1] * lru_wa.shape[2]
    att_w = ATT_HEADS * HEAD_DIM
    idx_w = IDX_HEADS * IDX_DIM
    assert REL_MAX_DIST <= K_CHUNK
    assert lru_w == att_w == idx_w and att_w % KV_LATENT == 0
    topk = min(INDEX_TOPK, s // 4)

    o_q = 2 * lru_w
    o_ckv = o_q + att_w
    o_gb = o_ckv + KV_LATENT
    o_qi = o_gb + att_w
    o_ki = o_qi + idx_w
    tn = 512
    cols_a = {"xa": 0, "ga": lru_w, "gb": 2 * lru_w}
    cols_b = {"q": 0, "qi": att_w}
    n_f32, n_bf16 = 3 * lru_w, att_w + idx_w

    bias_tiles = _bias_tiles(rel_bias)
    x2 = x.reshape(bsz * s, d)
    for l in range(depth):
        order = [(0, o_q), (o_gb, o_qi), (o_q, o_ckv), (o_qi, o_ki), (o_ckv, o_gb), (o_ki, w_in.shape[2])]
        pa, pb, c, ct, kn, wit = _proj(x2, norm_g[l][None, :], w_in[l].T, order, ckv_norm_g[l][None, :],
                                       idx_k_norm_g[l][None, :], idx_k_norm_b[l][None, :], n_f32, n_bf16, s, tn=tn)
        pa3 = pa.reshape(bsz, s, -1)
        pb3 = pb.reshape(bsz, s, -1)

        ya = _rglru(pa3, cols_a, conv_w[l], conv_b[l][None, :], (0.5 * lru_wa[l]).astype(BF16),
                    0.5 * lru_ba[l][None, :], (0.5 * lru_wx[l]).astype(BF16), 0.5 * lru_bx[l][None, :],
                    lru_lambda[l][None, :])

        wukt = jnp.transpose(w_uk[l], (0, 2, 1)).astype(BF16)
        wuvt = jnp.transpose(w_uv[l], (0, 2, 1)).astype(BF16)
        yb = _dsa(pb3, cols_b, wukt, wit, kn.reshape(bsz, s, -1), c.reshape(bsz, s, -1), ct, pa3,
                  _col_block(cols_a["gb"], att_w), bias_tiles, wuvt, topk)

        x2 = _outp(ya.reshape(bsz * s, lru_w), yb.reshape(bsz * s, att_w), w_out[l], x2,
                   final_norm_g[None, :], final_norm=(l == depth - 1))
    return x2.reshape(bsz, s, d)
```

```python
import functools

import numpy as np
import jax
import jax.numpy as jnp
from jax import lax
from jax.experimental import pallas as pl
from jax.experimental.pallas import tpu as pltpu

F32 = jnp.float32
BF16 = jnp.bfloat16
I32 = jnp.int32

CONV_WIDTH = 4
LRU_C = 8.0
ATT_HEADS = 8
HEAD_DIM = 128
KV_LATENT = 256
IDX_HEADS = 16
IDX_DIM = 64
INDEX_TOPK = 256
REL_BUCKETS = 32
REL_MAX_DIST = 128
EPS = 1e-6
LOG2E = float(np.log2(np.e))
ONES_ROWS = 16

Q_TILE = 256
K_CHUNK = 128
ATT_CHUNK = 256
NEG = float(np.finfo(np.float32).min)
INT_MIN = -(2 ** 31)
VMEM_LIMIT = 56 * 1024 * 1024


def _cparams(sem):
    return pltpu.CompilerParams(dimension_semantics=sem, vmem_limit_bytes=VMEM_LIMIT)


def _col_block(offset, width):
    assert offset % width == 0
    return offset // width


def _proj_kernel(t_ref, x_ref, g_ref, wlo_ref, whi_ref, tail_ref, cg_ref, kg_ref, kb_ref,
                 oa_ref, ob_ref, c_ref, ct_ref, kn_ref, wit_ref, h_ref, *, na, nb):
    j = pl.program_id(1)
    r = pl.program_id(2)
    nt = (((1,), (1,)), ((), ()))

    def w_tile():
        lo = jnp.where(t_ref[2 * j] < 0, tail_ref[...], wlo_ref[...])
        hi = jnp.where(t_ref[2 * j + 1] < 0, tail_ref[...], whi_ref[...])
        return jnp.concatenate([lo, hi], axis=0).astype(BF16)

    @pl.when(j == 0)
    def _():
        x = x_ref[...]
        y = x * lax.rsqrt(jnp.mean(x * x, axis=-1, keepdims=True) + EPS)
        h = (y * g_ref[...]).astype(BF16)
        h_ref[r] = h
        oa_ref[...] = lax.dot_general(h, w_tile(), nt, preferred_element_type=F32)

    @pl.when((j > 0) & (j < na))
    def _():
        oa_ref[...] = lax.dot_general(h_ref[r], w_tile(), nt, preferred_element_type=F32)

    @pl.when((j >= na) & (j < na + nb))
    def _():
        ob_ref[...] = lax.dot_general(h_ref[r], w_tile(), nt, preferred_element_type=F32).astype(BF16)

    @pl.when(j >= na + nb)
    def _():
        used = KV_LATENT + 128
        tail = lax.dot_general(h_ref[r], w_tile()[:used], nt, preferred_element_type=F32)
        ckv = tail[:, :KV_LATENT]
        c = ckv * lax.rsqrt(jnp.mean(ckv * ckv, axis=-1, keepdims=True) + EPS) * cg_ref[...]
        c_ref[...] = c.astype(BF16)
        ct_ref[0:KV_LATENT, :] = c.T.astype(BF16)
        ct_ref[KV_LATENT:, :] = jnp.ones((ONES_ROWS, ct_ref.shape[1]), BF16)
        sm = tail[:, KV_LATENT:KV_LATENT + 128]
        ki = sm[:, :IDX_DIM]
        mu = jnp.mean(ki, axis=-1, keepdims=True)
        var = jnp.mean(jnp.square(ki - mu), axis=-1, keepdims=True)
        kn = (ki - mu) * lax.rsqrt(var + EPS) * kg_ref[...] + kb_ref[...]
        kn_ref[...] = kn.astype(BF16)
        wit_ref[...] = sm.T[IDX_DIM:IDX_DIM + IDX_HEADS, :] * (IDX_HEADS ** -0.5 * IDX_DIM ** -0.5)


def _proj(x2, g, w_t, order, ckv_g, k_g, k_b, n_f32, n_bf16, seq_len, tm=1024, tn=512, group=2):
    m, d = x2.shape
    n = w_t.shape[0]
    th = tn // 2
    n_whole = n // th
    src = []
    for start, stop in order:
        assert start % th == 0 and (stop % th == 0 or stop == n)
        src += list(range(start // th, -(-stop // th)))
    na, nb = n_f32 // tn, n_bf16 // tn
    assert n_f32 % tn == 0 and n_bf16 % tn == 0 and len(src) * th == n_f32 + n_bf16 + tn
    tail = jnp.pad(w_t[n_whole * th:], ((0, (n_whole + 1) * th - n), (0, 0)))
    table = jnp.asarray([blk if blk < n_whole else -1 for blk in src], I32)
    tps = seq_len // tm
    assert seq_len % tm == 0 and tn >= KV_LATENT + 128
    assert (m // tm) % group == 0
    nj = na + nb + 1
    const = lambda shape: pl.BlockSpec(shape, lambda i, j, r, t: (0,) * len(shape))
    last = group - 1
    row = lambda i, r_eff: i * group + r_eff
    per_seq = lambda g: (g // tps, 0, g % tps)
    tail_row = lambda i, j, r: row(i, jnp.where(j < nj - 1, 0, r))
    grid_spec = pltpu.PrefetchScalarGridSpec(
        num_scalar_prefetch=1,
        grid=(m // tm // group, nj, group),
        in_specs=[
            pl.BlockSpec((tm, d), lambda i, j, r, t: (row(i, jnp.where(j == 0, r, last)), 0)),
            const((1, d)),
            pl.BlockSpec((th, d), lambda i, j, r, t: (jnp.maximum(t[2 * j], 0), 0)),
            pl.BlockSpec((th, d), lambda i, j, r, t: (jnp.maximum(t[2 * j + 1], 0), 0)),
            const((th, d)),
            const((1, KV_LATENT)),
            const((1, IDX_DIM)),
            const((1, IDX_DIM)),
        ],
        out_specs=[
            pl.BlockSpec((tm, tn), lambda i, j, r, t: (row(i, jnp.where(j < na, r, last)), jnp.minimum(j, na - 1))),
            pl.BlockSpec((tm, tn), lambda i, j, r, t: (
                row(i, jnp.where(j < na, 0, jnp.where(j < na + nb, r, last))), jnp.clip(j - na, 0, nb - 1))),
            pl.BlockSpec((tm, KV_LATENT), lambda i, j, r, t: (tail_row(i, j, r), 0)),
            pl.BlockSpec((None, KV_LATENT + ONES_ROWS, tm), lambda i, j, r, t: per_seq(tail_row(i, j, r))),
            pl.BlockSpec((tm, IDX_DIM), lambda i, j, r, t: (tail_row(i, j, r), 0)),
            pl.BlockSpec((None, IDX_HEADS, tm), lambda i, j, r, t: per_seq(tail_row(i, j, r))),
        ],
        scratch_shapes=[pltpu.VMEM((group, tm, d), BF16)],
    )
    return pl.pallas_call(
        functools.partial(_proj_kernel, na=na, nb=nb),
        grid_spec=grid_spec,
        out_shape=[
            jax.ShapeDtypeStruct((m, n_f32), F32),
            jax.ShapeDtypeStruct((m, n_bf16), BF16),
            jax.ShapeDtypeStruct((m, KV_LATENT), BF16),
            jax.ShapeDtypeStruct((m // seq_len, KV_LATENT + ONES_ROWS, seq_len), BF16),
            jax.ShapeDtypeStruct((m, IDX_DIM), BF16),
            jax.ShapeDtypeStruct((m // seq_len, IDX_HEADS, seq_len), F32),
        ],
        compiler_params=_cparams(("arbitrary", "arbitrary", "arbitrary")),
        name="proj",
    )(table, x2, g, w_t, w_t, tail, ckv_g, k_g, k_b)


def _sigmoid(v):
    return 0.5 * jnp.tanh(0.5 * v) + 0.5


def _scan_step(a, b, k, axis, idx):
    keep = idx >= k
    a_prev = jnp.where(keep, pltpu.roll(a, k, axis=axis), 1.0)
    b_prev = jnp.where(keep, pltpu.roll(b, k, axis=axis), 0.0)
    return a * a_prev, a * b_prev + b


def _rglru_kernel(xa_ref, ga_ref, cw_ref, cb_ref, wa_ref, ba_ref, wx_ref, bx_ref, lam_ref,
                  o_ref, pad_s, a_s, b_s, c_s):
    s, w = xa_ref.shape
    tile = 8
    n_tiles = s // tile

    pad_s[0:tile, :] = jnp.zeros((tile, w), F32)
    pad_s[tile:tile + s, :] = xa_ref[...]
    acc = pad_s[tile:tile + s, :] * cw_ref[CONV_WIDTH - 1:CONV_WIDTH, :]
    for j in range(CONV_WIDTH - 1):
        back = CONV_WIDTH - 1 - j
        acc = acc + pad_s[tile - back:tile - back + s, :] * cw_ref[j:j + 1, :]
    xc = cb_ref[...] + acc

    xcb = xc.astype(BF16)
    tr = jnp.tanh(jnp.dot(xcb, wa_ref[...], preferred_element_type=F32) + ba_ref[...])
    ti = jnp.tanh(jnp.dot(xcb, wx_ref[...], preferred_element_type=F32) + bx_ref[...])
    i = 0.5 * ti + 0.5
    z = -lam_ref[...]
    softplus = jnp.maximum(z, 0.0) + jnp.log1p(jnp.exp(-jnp.abs(z)))
    half = (-0.5 * LRU_C) * softplus
    log_a = half * tr + half
    a = jnp.exp(log_a)
    m2 = (1.0 + a * a) * jnp.tanh(-log_a)
    mult = jnp.where(m2 > 0.0, m2 * lax.rsqrt(m2), 0.0)
    gated = i * xc
    b_s[...] = mult * gated
    b_s[0:1, :] = gated[0:1, :]

    a3 = a.reshape(n_tiles, tile, w)
    b3 = b_s[...].reshape(n_tiles, tile, w)
    sub = lax.broadcasted_iota(I32, (n_tiles, tile, w), 1)
    for k in (1, 2, 4):
        a3, b3 = _scan_step(a3, b3, k, 1, sub)
    a_s[...] = a3.reshape(s, w)
    b_s[...] = b3.reshape(s, w)

    at = a_s[pl.ds(tile - 1, n_tiles, stride=tile), :]
    bt = b_s[pl.ds(tile - 1, n_tiles, stride=tile), :]
    trow = lax.broadcasted_iota(I32, (n_tiles, w), 0)
    k = 1
    while k < n_tiles:
        at, bt = _scan_step(at, bt, k, 0, trow)
        k *= 2
    c_s[0:tile, :] = jnp.zeros((tile, w), F32)
    c_s[tile:tile + n_tiles, :] = bt

    for t in range(n_tiles):
        rows = slice(t * tile, (t + 1) * tile)
        before = c_s[pl.ds(tile - 1 + t, tile, stride=0), :]
        h = a_s[rows, :] * before + b_s[rows, :]
        gh = 0.5 * ga_ref[rows, :]
        o_ref[rows, :] = (h * (gh * (jnp.tanh(gh) + 1.0))).astype(o_ref.dtype)


def _rglru(pa3, cols, conv_w, conv_b, wa, ba, wx, bx, lam):
    bsz, s, _ = pa3.shape
    g, w = wa.shape[0], wa.shape[-1]
    xa_blk = _col_block(cols["xa"], w)
    ga_blk = _col_block(cols["ga"], w)
    vec = lambda: pl.BlockSpec((1, w), lambda b, j: (0, j))
    return pl.pallas_call(
        _rglru_kernel,
        grid=(bsz, g),
        in_specs=[
            pl.BlockSpec((None, s, w), lambda b, j: (b, 0, xa_blk + j)),
            pl.BlockSpec((None, s, w), lambda b, j: (b, 0, ga_blk + j)),
            pl.BlockSpec((CONV_WIDTH, w), lambda b, j: (0, j)),
            vec(),
            pl.BlockSpec((None, w, w), lambda b, j: (j, 0, 0)),
            vec(),
            pl.BlockSpec((None, w, w), lambda b, j: (j, 0, 0)),
            vec(),
            vec(),
        ],
        out_specs=pl.BlockSpec((None, s, w), lambda b, j: (b, 0, j)),
        out_shape=jax.ShapeDtypeStruct((bsz, s, g * w), BF16),
        scratch_shapes=[pltpu.VMEM((s + 8, w), F32), pltpu.VMEM((s, w), F32), pltpu.VMEM((s, w), F32),
                        pltpu.VMEM((s // 8 + 8, w), F32)],
        compiler_params=_cparams(("parallel", "parallel")),
        name="rglru",
    )(pa3, pa3, conv_w, conv_b, wa, ba, wx, bx, lam)


def _tree_sum(parts):
    while len(parts) > 1:
        paired = [parts[i] + parts[i + 1] for i in range(0, len(parts) - 1, 2)]
        parts = paired + ([parts[-1]] if len(parts) % 2 else [])
    return parts[0]


def _sortable_to_f32(u):
    key = u ^ INT_MIN
    return lax.bitcast_convert_type(key ^ ((key >> 31) & 0x7FFFFFFF), F32)


def _count_ge(ref, rows, cand, pack):
    chains = 4
    one, zero = jnp.ones((), ref.dtype), jnp.zeros((), ref.dtype)
    accs = [None] * chains
    for r in range(rows // pack):
        hit = jnp.where(ref[r * pack:(r + 1) * pack, :] >= cand, one, zero)
        accs[r % chains] = hit if accs[r % chains] is None else accs[r % chains] + hit
    parts = [a.astype(F32) for a in accs if a is not None]
    return jnp.sum(_tree_sum(parts), axis=0, keepdims=True)


def _kth_largest(score_ref, score16_ref, rows, k):
    def step16(i, u):
        cand = u | (jnp.int32(1) << (15 - i))
        cand_f = _sortable_to_f32(cand << 16).astype(BF16)
        return jnp.where(_count_ge(score16_ref, rows, cand_f, 16) >= k, cand, u)

    hi = lax.fori_loop(0, 16, step16, jnp.zeros((1, Q_TILE), I32))
    at_hi = _count_ge(score_ref, rows, _sortable_to_f32(hi << 16), 8)
    keeps = at_hi >= k
    hi = jnp.where(keeps, hi, jnp.maximum(hi - 1, 0))

    per_trip = 4

    def trip(state):
        i, u, at_u, _ = state
        for t in range(per_trip):
            cand = u | (jnp.int32(1) << (15 - (i + t)))
            cnt = _count_ge(score_ref, rows, _sortable_to_f32(cand), 8)
            take = cnt >= k
            u = jnp.where(take, cand, u)
            at_u = jnp.where(take, cnt, at_u)
        open_lanes = jnp.sum((at_u != k).astype(I32))
        return i + per_trip, u, at_u, open_lanes

    unknown = jnp.full((1, Q_TILE), -1.0, F32)
    state = (jnp.int32(0), hi << 16, jnp.where(keeps, at_hi, unknown), jnp.int32(1))
    _, u, _, _ = lax.while_loop(lambda st: (st[0] < 16) & (st[3] > 0), trip, state)
    return _sortable_to_f32(u)


def _loop_in_trips(n, body, init):
    def trips(start, count, width, carry):
        def group(i, c):
            for t in range(width):
                c = body(start + width * i + t, c)
            return c
        return lax.fori_loop(0, count, group, carry)

    carry = trips(0, n // 4, 4, init)
    carry = trips(4 * (n // 4), (n % 4) // 2, 2, carry)
    return trips(2 * (n // 2), n % 2, 1, carry)


def _dsa_kernel(q_ref, qi_ref, wukt_ref, wit_ref, kn_ref, c_ref, ct_ref, gb_ref, bias_ref, wuvt_ref,
                o_ref, qat_ref, qit_ref, score_s, score16_s, thr_s, lg_s, acc_s, topk):
    qb = pl.program_id(1)

    qt = q_ref[...].T
    scale = HEAD_DIM ** -0.5 * LOG2E
    for h in range(ATT_HEADS):
        qa = jnp.dot(wukt_ref[h], qt[h * HEAD_DIM:(h + 1) * HEAD_DIM], preferred_element_type=F32)
        qat_ref[h] = (qa * scale).astype(BF16)
    qit = qi_ref[...].T
    for h in range(IDX_HEADS):
        qit_ref[h] = qit[h * IDX_DIM:(h + 1) * IDX_DIM, :]
    q_tiles = Q_TILE // K_CHUNK
    nkc = (qb + 1) * q_tiles
    nac = (nkc * K_CHUNK + ATT_CHUNK - 1) // ATT_CHUNK
    tiles = ATT_CHUNK // K_CHUNK

    kiota = lax.broadcasted_iota(I32, (K_CHUNK, Q_TILE), 0)
    qpos = qb * Q_TILE + lax.broadcasted_iota(I32, (K_CHUNK, Q_TILE), 1)

    def score_chunk(ac, carry):
        for t in range(tiles):
            k0 = pl.multiple_of(ac * ATT_CHUNK + t * K_CHUNK, K_CHUNK)
            kn = kn_ref[pl.ds(k0, K_CHUNK), :]
            acc = jnp.zeros((K_CHUNK, Q_TILE), F32)
            for h in range(IDX_HEADS):
                sc = jnp.dot(kn, qit_ref[h], preferred_element_type=F32)
                acc = acc + jnp.maximum(sc, 0.0) * wit_ref[h:h + 1, :]
            masked = jnp.where(kiota + k0 <= qpos, acc, -jnp.inf)
            score_s[pl.ds(k0, K_CHUNK), :] = masked
            score16_s[pl.ds(k0, K_CHUNK), :] = masked.astype(BF16)
        return carry

    _loop_in_trips(nac, score_chunk, 0)

    for v in range(1, score_s.shape[0] // ATT_CHUNK + 1):
        @pl.when(nac == v)
        def _(rows=v * ATT_CHUNK):
            if rows <= topk:
                thr = jnp.full((1, Q_TILE), NEG, F32)
            else:
                thr = _kth_largest(score_s, score16_s, rows, float(topk))
                thr = jnp.where(thr >= NEG, thr, NEG)
            thr_s[...] = jnp.broadcast_to(thr, thr_s.shape)

    thr = thr_s[0:1, :]

    def logit_chunk(ac, m8s):
        r0 = pl.multiple_of(ac * ATT_CHUNK, ATT_CHUNK)
        c_chunk = c_ref[pl.ds(r0, ATT_CHUNK), :]
        mbias = jnp.where(score_s[pl.ds(r0, ATT_CHUNK), :] >= thr, 0.0, NEG)
        near = [[jnp.clip(ac * tiles + t - (qb * q_tiles + j) + 2, 0, 2) for j in range(q_tiles)]
                for t in range(tiles)]
        out = []
        for h in range(ATT_HEADS):
            lg = jnp.dot(c_chunk, qat_ref[h], preferred_element_type=F32) + mbias
            lg = jnp.concatenate(
                [lg[t * K_CHUNK:(t + 1) * K_CHUNK]
                 + jnp.concatenate([bias_ref[h, near[t][j]] for j in range(q_tiles)], axis=1)
                 for t in range(tiles)], axis=0)
            lg_s[h, pl.ds(r0, ATT_CHUNK), :] = lg
            out.append(jnp.maximum(m8s[h], jnp.max(lg.reshape(ATT_CHUNK // 8, 8, Q_TILE), axis=0)))
        return tuple(out)

    m8s = _loop_in_trips(nac, logit_chunk, tuple(jnp.full((8, Q_TILE), NEG, F32) for _ in range(ATT_HEADS)))
    ms = [jnp.max(m8, axis=0, keepdims=True) for m8 in m8s]

    acc_s[...] = jnp.zeros(acc_s.shape, F32)

    def pv_chunk(ac, carry):
        r0 = pl.multiple_of(ac * ATT_CHUNK, ATT_CHUNK)
        ct_chunk = ct_ref[:, pl.ds(r0, ATT_CHUNK)]
        for h in range(ATT_HEADS):
            pr = jnp.exp2(lg_s[h, pl.ds(r0, ATT_CHUNK), :] - ms[h])
            acc_s[h] += jnp.dot(ct_chunk, pr.astype(BF16), preferred_element_type=F32)
        return carry

    _loop_in_trips(nac, pv_chunk, 0)

    for h in range(ATT_HEADS):
        denom = acc_s[h, KV_LATENT:KV_LATENT + 1, :]
        o_t = acc_s[h, 0:KV_LATENT, :] * (1.0 / denom)
        y_t = jnp.dot(wuvt_ref[h], o_t.astype(BF16), preferred_element_type=F32)
        gb = gb_ref[:, h * HEAD_DIM:(h + 1) * HEAD_DIM]
        o_ref[:, h * HEAD_DIM:(h + 1) * HEAD_DIM] = (y_t.T * (gb * _sigmoid(gb))).astype(o_ref.dtype)


def _dsa(pb3, cols_b, w_ukt, wit, kn, c, ct, pa3, gb_blk, bias_tiles, wuvt, topk):
    bsz, s, _ = c.shape
    att_w = ATT_HEADS * HEAD_DIM
    idx_w = IDX_HEADS * IDX_DIM
    assert s % ATT_CHUNK == 0 and s % Q_TILE == 0 and Q_TILE % K_CHUNK == 0 and ATT_CHUNK % K_CHUNK == 0
    assert s // 16 // 4 < 256
    q_blk = _col_block(cols_b["q"], att_w)
    qi_blk = _col_block(cols_b["qi"], idx_w)
    const = lambda shape: pl.BlockSpec(shape, lambda b, i: (0,) * len(shape))
    return pl.pallas_call(
        functools.partial(_dsa_kernel, topk=topk),
        grid=(bsz, s // Q_TILE),
        in_specs=[
            pl.BlockSpec((None, Q_TILE, att_w), lambda b, i: (b, i, q_blk)),
            pl.BlockSpec((None, Q_TILE, idx_w), lambda b, i: (b, i, qi_blk)),
            const(w_ukt.shape),
            pl.BlockSpec((None, IDX_HEADS, Q_TILE), lambda b, i: (b, 0, i)),
            pl.BlockSpec((None, s, IDX_DIM), lambda b, i: (b, 0, 0)),
            pl.BlockSpec((None, s, KV_LATENT), lambda b, i: (b, 0, 0)),
            pl.BlockSpec((None, KV_LATENT + ONES_ROWS, s), lambda b, i: (b, 0, 0)),
            pl.BlockSpec((None, Q_TILE, att_w), lambda b, i: (b, i, gb_blk)),
            const(bias_tiles.shape),
            const(wuvt.shape),
        ],
        out_specs=pl.BlockSpec((None, Q_TILE, att_w), lambda b, i: (b, i, 0)),
        out_shape=jax.ShapeDtypeStruct((bsz, s, att_w), BF16),
        scratch_shapes=[
            pltpu.VMEM((ATT_HEADS, KV_LATENT, Q_TILE), BF16),
            pltpu.VMEM((IDX_HEADS, IDX_DIM, Q_TILE), BF16),
            pltpu.VMEM((s, Q_TILE), F32),
            pltpu.VMEM((s, Q_TILE), BF16),
            pltpu.VMEM((8, Q_TILE), F32),
            pltpu.VMEM((ATT_HEADS, s, Q_TILE), F32),
            pltpu.VMEM((ATT_HEADS, KV_LATENT + ONES_ROWS, Q_TILE), F32),
        ],
        compiler_params=_cparams(("parallel", "arbitrary")),
        name="dsa",
    )(pb3, pb3, w_ukt, wit, kn, c, ct, pa3, bias_tiles, wuvt)


def _outp_kernel(ya_ref, yb_ref, w_ref, x_ref, g_ref, o_ref, wbf_s, *, final_norm):
    @pl.when(pl.program_id(0) == 0)
    def _():
        wbf_s[...] = w_ref[...].astype(BF16)

    ka = ya_ref.shape[1]
    acc = jnp.dot(ya_ref[...], wbf_s[0:ka, :], preferred_element_type=F32)
    acc = acc + jnp.dot(yb_ref[...], wbf_s[ka:, :], preferred_element_type=F32)
    x = x_ref[...] + acc
    if final_norm:
        x = x * lax.rsqrt(jnp.mean(x * x, axis=-1, keepdims=True) + EPS) * g_ref[...]
    o_ref[...] = x


def _outp(ya, yb, w_out, x2, g, final_norm, tm=512):
    m, d = x2.shape
    ka, kb = ya.shape[1], yb.shape[1]
    assert w_out.shape == (ka + kb, d)
    return pl.pallas_call(
        functools.partial(_outp_kernel, final_norm=final_norm),
        grid=(m // tm,),
        in_specs=[
            pl.BlockSpec((tm, ka), lambda i: (i, 0)),
            pl.BlockSpec((tm, kb), lambda i: (i, 0)),
            pl.BlockSpec((ka + kb, d), lambda i: (0, 0), pipeline_mode=pl.Buffered(1)),
            pl.BlockSpec((tm, d), lambda i: (i, 0)),
            pl.BlockSpec((1, d), lambda i: (0, 0)),
        ],
        out_specs=pl.BlockSpec((tm, d), lambda i: (i, 0)),
        out_shape=jax.ShapeDtypeStruct((m, d), F32),
        scratch_shapes=[pltpu.VMEM((ka + kb, d), BF16)],
        compiler_params=_cparams(("arbitrary",)),
        name="outp",
    )(ya, yb, w_out, x2, g)


def _t5_bucket(dist):
    n = jnp.maximum(dist, 0)
    max_exact = REL_BUCKETS // 2
    nf = jnp.maximum(n, 1).astype(F32)
    large = max_exact + (jnp.log(nf / max_exact) / np.log(REL_MAX_DIST / max_exact)
                         * (REL_BUCKETS - max_exact)).astype(I32)
    large = jnp.minimum(large, REL_BUCKETS - 1)
    return jnp.where(n < max_exact, n, large)


def _bias_tiles(rel_bias):
    qw = K_CHUNK
    span = K_CHUNK + qw
    table = rel_bias[_t5_bucket(jnp.arange(span + 1, dtype=I32))].astype(F32)
    table = ((table[:span] - table[span:]) * LOG2E).T
    n = span + qw - 1
    a = jnp.concatenate([jnp.zeros((ATT_HEADS, qw - 1), F32), table], axis=1)
    shifted = jnp.tile(a, (1, span + 1))[:, :span * (n + 1)].reshape(ATT_HEADS, span, n + 1)
    tiles = shifted[:, ::-1, :qw].reshape(ATT_HEADS, 2, K_CHUNK, qw)
    return jnp.concatenate([jnp.zeros_like(tiles[:, :1]), tiles], axis=1)


def kernel(x, norm_g, w_in, conv_w, conv_b, lru_wa, lru_ba, lru_wx, lru_bx, lru_lambda, ckv_norm_g, idx_k_norm_g, idx_k_norm_b, w_uk, w_uv, w_out, rel_bias, final_norm_g):
    bsz, s, d = x.shape
    depth = w_in.shape[0]
    lru_w = lru_wa.shape[1] * lru_wa.shape[2]
    att_w = ATT_HEADS * HEAD_DIM
    idx_w = IDX_HEADS * IDX_DIM
    assert REL_MAX_DIST <= K_CHUNK
    assert lru_w == att_w == idx_w and att_w % KV_LATENT == 0
    topk = min(INDEX_TOPK, s // 4)

    o_q = 2 * lru_w
    o_ckv = o_q + att_w
    o_gb = o_ckv + KV_LATENT
    o_qi = o_gb + att_w
    o_ki = o_qi + idx_w
    tn = 512
    cols_a = {"xa": 0, "ga": lru_w, "gb": 2 * lru_w}
    cols_b = {"q": 0, "qi": att_w}
    n_f32, n_bf16 = 3 * lru_w, att_w + idx_w

    bias_tiles = _bias_tiles(rel_bias)
    x2 = x.reshape(bsz * s, d)
    for l in range(depth):
        order = [(0, o_q), (o_gb, o_qi), (o_q, o_ckv), (o_qi, o_ki), (o_ckv, o_gb), (o_ki, w_in.shape[2])]
        pa, pb, c, ct, kn, wit = _proj(x2, norm_g[l][None, :], w_in[l].T, order, ckv_norm_g[l][None, :],
                                       idx_k_norm_g[l][None, :], idx_k_norm_b[l][None, :], n_f32, n_bf16, s, tn=tn)
        pa3 = pa.reshape(bsz, s, -1)
        pb3 = pb.reshape(bsz, s, -1)

        ya = _rglru(pa3, cols_a, conv_w[l], conv_b[l][None, :], (0.5 * lru_wa[l]).astype(BF16),
                    0.5 * lru_ba[l][None, :], (0.5 * lru_wx[l]).astype(BF16), 0.5 * lru_bx[l][None, :],
                    lru_lambda[l][None, :])

        wukt = jnp.transpose(w_uk[l], (0, 2, 1)).astype(BF16)
        wuvt = jnp.transpose(w_uv[l], (0, 2, 1)).astype(BF16)
        yb = _dsa(pb3, cols_b, wukt, wit, kn.reshape(bsz, s, -1), c.reshape(bsz, s, -1), ct, pa3,
                  _col_block(cols_a["gb"], att_w), bias_tiles, wuvt, topk)

        x2 = _outp(ya.reshape(bsz * s, lru_w), yb.reshape(bsz * s, att_w), w_out[l], x2,
                   final_norm_g[None, :], final_norm=(l == depth - 1))
    return x2.reshape(bsz, s, d)
```

```python
import functools

import numpy as np
import jax
import jax.numpy as jnp
from jax import lax
from jax.experimental import pallas as pl
from jax.experimental.pallas import tpu as pltpu

F32 = jnp.float32
BF16 = jnp.bfloat16
I32 = jnp.int32

CONV_WIDTH = 4
LRU_C = 8.0
ATT_HEADS = 8
HEAD_DIM = 128
KV_LATENT = 256
IDX_HEADS = 16
IDX_DIM = 64
INDEX_TOPK = 256
REL_BUCKETS = 32
REL_MAX_DIST = 128
EPS = 1e-6
LOG2E = float(np.log2(np.e))
ONES_ROWS = 16

Q_TILE = 256
K_CHUNK = 128
ATT_CHUNK = 256
NEG = float(np.finfo(np.float32).min)
INT_MIN = -(2 ** 31)
VMEM_LIMIT = 56 * 1024 * 1024


def _cparams(sem):
    return pltpu.CompilerParams(dimension_semantics=sem, vmem_limit_bytes=VMEM_LIMIT)


def _col_block(offset, width):
    assert offset % width == 0
    return offset // width


def _proj_kernel(t_ref, x_ref, g_ref, wlo_ref, whi_ref, tail_ref, cg_ref, kg_ref, kb_ref,
                 oa_ref, ob_ref, c_ref, ct_ref, kn_ref, wit_ref, h_ref, *, na, nb):
    j = pl.program_id(1)
    r = pl.program_id(2)
    nt = (((1,), (1,)), ((), ()))

    def w_tile():
        lo = jnp.where(t_ref[2 * j] < 0, tail_ref[...], wlo_ref[...])
        hi = jnp.where(t_ref[2 * j + 1] < 0, tail_ref[...], whi_ref[...])
        return jnp.concatenate([lo, hi], axis=0).astype(BF16)

    @pl.when(j == 0)
    def _():
        x = x_ref[...]
        y = x * lax.rsqrt(jnp.mean(x * x, axis=-1, keepdims=True) + EPS)
        h = (y * g_ref[...]).astype(BF16)
        h_ref[r] = h
        oa_ref[...] = lax.dot_general(h, w_tile(), nt, preferred_element_type=F32)

    @pl.when((j > 0) & (j < na))
    def _():
        oa_ref[...] = lax.dot_general(h_ref[r], w_tile(), nt, preferred_element_type=F32)

    @pl.when((j >= na) & (j < na + nb))
    def _():
        ob_ref[...] = lax.dot_general(h_ref[r], w_tile(), nt, preferred_element_type=F32).T.astype(BF16)

    @pl.when(j >= na + nb)
    def _():
        used = KV_LATENT + 128
        tail = lax.dot_general(h_ref[r], w_tile()[:used], nt, preferred_element_type=F32)
        ckv = tail[:, :KV_LATENT]
        c = ckv * lax.rsqrt(jnp.mean(ckv * ckv, axis=-1, keepdims=True) + EPS) * cg_ref[...]
        c_ref[...] = c.astype(BF16)
        ct_ref[0:KV_LATENT, :] = c.T.astype(BF16)
        ct_ref[KV_LATENT:, :] = jnp.ones((ONES_ROWS, ct_ref.shape[1]), BF16)
        sm = tail[:, KV_LATENT:KV_LATENT + 128]
        ki = sm[:, :IDX_DIM]
        mu = jnp.mean(ki, axis=-1, keepdims=True)
        var = jnp.mean(jnp.square(ki - mu), axis=-1, keepdims=True)
        kn = (ki - mu) * lax.rsqrt(var + EPS) * kg_ref[...] + kb_ref[...]
        kn_ref[...] = kn.astype(BF16)
        wit_ref[...] = sm.T[IDX_DIM:IDX_DIM + IDX_HEADS, :] * (IDX_HEADS ** -0.5 * IDX_DIM ** -0.5)


def _proj(x2, g, w_t, order, ckv_g, k_g, k_b, n_f32, n_bf16, seq_len, tm=1024, tn=512, group=2):
    m, d = x2.shape
    n = w_t.shape[0]
    th = tn // 2
    n_whole = n // th
    src = []
    for start, stop in order:
        assert start % th == 0 and (stop % th == 0 or stop == n)
        src += list(range(start // th, -(-stop // th)))
    na, nb = n_f32 // tn, n_bf16 // tn
    assert n_f32 % tn == 0 and n_bf16 % tn == 0 and len(src) * th == n_f32 + n_bf16 + tn
    tail = jnp.pad(w_t[n_whole * th:], ((0, (n_whole + 1) * th - n), (0, 0)))
    table = jnp.asarray([blk if blk < n_whole else -1 for blk in src], I32)
    tps = seq_len // tm
    assert seq_len % tm == 0 and tn >= KV_LATENT + 128
    assert (m // tm) % group == 0
    nj = na + nb + 1
    const = lambda shape: pl.BlockSpec(shape, lambda i, j, r, t: (0,) * len(shape))
    last = group - 1
    row = lambda i, r_eff: i * group + r_eff
    per_seq = lambda g: (g // tps, 0, g % tps)
    tail_row = lambda i, j, r: row(i, jnp.where(j < nj - 1, 0, r))
    grid_spec = pltpu.PrefetchScalarGridSpec(
        num_scalar_prefetch=1,
        grid=(m // tm // group, nj, group),
        in_specs=[
            pl.BlockSpec((tm, d), lambda i, j, r, t: (row(i, jnp.where(j == 0, r, last)), 0)),
            const((1, d)),
            pl.BlockSpec((th, d), lambda i, j, r, t: (jnp.maximum(t[2 * j], 0), 0)),
            pl.BlockSpec((th, d), lambda i, j, r, t: (jnp.maximum(t[2 * j + 1], 0), 0)),
            const((th, d)),
            const((1, KV_LATENT)),
            const((1, IDX_DIM)),
            const((1, IDX_DIM)),
        ],
        out_specs=[
            pl.BlockSpec((tm, tn), lambda i, j, r, t: (row(i, jnp.where(j < na, r, last)), jnp.minimum(j, na - 1))),
            pl.BlockSpec((tn, tm), lambda i, j, r, t: (
                jnp.clip(j - na, 0, nb - 1), row(i, jnp.where(j < na, 0, jnp.where(j < na + nb, r, last))))),
            pl.BlockSpec((tm, KV_LATENT), lambda i, j, r, t: (tail_row(i, j, r), 0)),
            pl.BlockSpec((None, KV_LATENT + ONES_ROWS, tm), lambda i, j, r, t: per_seq(tail_row(i, j, r))),
            pl.BlockSpec((tm, IDX_DIM), lambda i, j, r, t: (tail_row(i, j, r), 0)),
            pl.BlockSpec((None, IDX_HEADS, tm), lambda i, j, r, t: per_seq(tail_row(i, j, r))),
        ],
        scratch_shapes=[pltpu.VMEM((group, tm, d), BF16)],
    )
    return pl.pallas_call(
        functools.partial(_proj_kernel, na=na, nb=nb),
        grid_spec=grid_spec,
        out_shape=[
            jax.ShapeDtypeStruct((m, n_f32), F32),
            jax.ShapeDtypeStruct((n_bf16, m), BF16),
            jax.ShapeDtypeStruct((m, KV_LATENT), BF16),
            jax.ShapeDtypeStruct((m // seq_len, KV_LATENT + ONES_ROWS, seq_len), BF16),
            jax.ShapeDtypeStruct((m, IDX_DIM), BF16),
            jax.ShapeDtypeStruct((m // seq_len, IDX_HEADS, seq_len), F32),
        ],
        compiler_params=_cparams(("arbitrary", "arbitrary", "arbitrary")),
        name="proj",
    )(table, x2, g, w_t, w_t, tail, ckv_g, k_g, k_b)


def _sigmoid(v):
    return 0.5 * jnp.tanh(0.5 * v) + 0.5


def _scan_step(a, b, k, axis, idx):
    keep = idx >= k
    a_prev = jnp.where(keep, pltpu.roll(a, k, axis=axis), 1.0)
    b_prev = jnp.where(keep, pltpu.roll(b, k, axis=axis), 0.0)
    return a * a_prev, a * b_prev + b


def _rglru_kernel(xa_ref, ga_ref, cw_ref, cb_ref, wa_ref, ba_ref, wx_ref, bx_ref, lam_ref,
                  o_ref, pad_s, a_s, b_s, c_s):
    s, w = xa_ref.shape
    tile = 8
    n_tiles = s // tile

    pad_s[0:tile, :] = jnp.zeros((tile, w), F32)
    pad_s[tile:tile + s, :] = xa_ref[...]
    acc = pad_s[tile:tile + s, :] * cw_ref[CONV_WIDTH - 1:CONV_WIDTH, :]
    for j in range(CONV_WIDTH - 1):
        back = CONV_WIDTH - 1 - j
        acc = acc + pad_s[tile - back:tile - back + s, :] * cw_ref[j:j + 1, :]
    xc = cb_ref[...] + acc

    xcb = xc.astype(BF16)
    tr = jnp.tanh(jnp.dot(xcb, wa_ref[...], preferred_element_type=F32) + ba_ref[...])
    ti = jnp.tanh(jnp.dot(xcb, wx_ref[...], preferred_element_type=F32) + bx_ref[...])
    i = 0.5 * ti + 0.5
    z = -lam_ref[...]
    softplus = jnp.maximum(z, 0.0) + jnp.log1p(jnp.exp(-jnp.abs(z)))
    half = (-0.5 * LRU_C) * softplus
    log_a = half * tr + half
    a = jnp.exp(log_a)
    m2 = (1.0 + a * a) * jnp.tanh(-log_a)
    mult = jnp.where(m2 > 0.0, m2 * lax.rsqrt(m2), 0.0)
    gated = i * xc
    b_s[...] = mult * gated
    b_s[0:1, :] = gated[0:1, :]

    a3 = a.reshape(n_tiles, tile, w)
    b3 = b_s[...].reshape(n_tiles, tile, w)
    sub = lax.broadcasted_iota(I32, (n_tiles, tile, w), 1)
    for k in (1, 2, 4):
        a3, b3 = _scan_step(a3, b3, k, 1, sub)
    a_s[...] = a3.reshape(s, w)
    b_s[...] = b3.reshape(s, w)

    at = a_s[pl.ds(tile - 1, n_tiles, stride=tile), :]
    bt = b_s[pl.ds(tile - 1, n_tiles, stride=tile), :]
    trow = lax.broadcasted_iota(I32, (n_tiles, w), 0)
    k = 1
    while k < n_tiles:
        at, bt = _scan_step(at, bt, k, 0, trow)
        k *= 2
    c_s[0:tile, :] = jnp.zeros((tile, w), F32)
    c_s[tile:tile + n_tiles, :] = bt

    for t in range(n_tiles):
        rows = slice(t * tile, (t + 1) * tile)
        before = c_s[pl.ds(tile - 1 + t, tile, stride=0), :]
        h = a_s[rows, :] * before + b_s[rows, :]
        gh = 0.5 * ga_ref[rows, :]
        o_ref[rows, :] = (h * (gh * (jnp.tanh(gh) + 1.0))).astype(o_ref.dtype)


def _rglru(pa3, cols, conv_w, conv_b, wa, ba, wx, bx, lam):
    bsz, s, _ = pa3.shape
    g, w = wa.shape[0], wa.shape[-1]
    xa_blk = _col_block(cols["xa"], w)
    ga_blk = _col_block(cols["ga"], w)
    vec = lambda: pl.BlockSpec((1, w), lambda b, j: (0, j))
    return pl.pallas_call(
        _rglru_kernel,
        grid=(bsz, g),
        in_specs=[
            pl.BlockSpec((None, s, w), lambda b, j: (b, 0, xa_blk + j)),
            pl.BlockSpec((None, s, w), lambda b, j: (b, 0, ga_blk + j)),
            pl.BlockSpec((CONV_WIDTH, w), lambda b, j: (0, j)),
            vec(),
            pl.BlockSpec((None, w, w), lambda b, j: (j, 0, 0)),
            vec(),
            pl.BlockSpec((None, w, w), lambda b, j: (j, 0, 0)),
            vec(),
            vec(),
        ],
        out_specs=pl.BlockSpec((None, s, w), lambda b, j: (b, 0, j)),
        out_shape=jax.ShapeDtypeStruct((bsz, s, g * w), BF16),
        scratch_shapes=[pltpu.VMEM((s + 8, w), F32), pltpu.VMEM((s, w), F32), pltpu.VMEM((s, w), F32),
                        pltpu.VMEM((s // 8 + 8, w), F32)],
        compiler_params=_cparams(("parallel", "parallel")),
        name="rglru",
    )(pa3, pa3, conv_w, conv_b, wa, ba, wx, bx, lam)


def _tree_sum(parts):
    while len(parts) > 1:
        paired = [parts[i] + parts[i + 1] for i in range(0, len(parts) - 1, 2)]
        parts = paired + ([parts[-1]] if len(parts) % 2 else [])
    return parts[0]


def _sortable_to_f32(u):
    key = u ^ INT_MIN
    return lax.bitcast_convert_type(key ^ ((key >> 31) & 0x7FFFFFFF), F32)


def _count_ge(ref, rows, cand, pack):
    chains = 4
    one, zero = jnp.ones((), ref.dtype), jnp.zeros((), ref.dtype)
    accs = [None] * chains
    for r in range(rows // pack):
        hit = jnp.where(ref[r * pack:(r + 1) * pack, :] >= cand, one, zero)
        accs[r % chains] = hit if accs[r % chains] is None else accs[r % chains] + hit
    parts = [a.astype(F32) for a in accs if a is not None]
    return jnp.sum(_tree_sum(parts), axis=0, keepdims=True)


def _kth_largest(score_ref, score16_ref, rows, k):
    def step16(i, u):
        cand = u | (jnp.int32(1) << (15 - i))
        cand_f = _sortable_to_f32(cand << 16).astype(BF16)
        return jnp.where(_count_ge(score16_ref, rows, cand_f, 16) >= k, cand, u)

    hi = lax.fori_loop(0, 16, step16, jnp.zeros((1, Q_TILE), I32))
    at_hi = _count_ge(score_ref, rows, _sortable_to_f32(hi << 16), 8)
    keeps = at_hi >= k
    hi = jnp.where(keeps, hi, jnp.maximum(hi - 1, 0))

    per_trip = 4

    def trip(state):
        i, u, at_u, _ = state
        for t in range(per_trip):
            cand = u | (jnp.int32(1) << (15 - (i + t)))
            cnt = _count_ge(score_ref, rows, _sortable_to_f32(cand), 8)
            take = cnt >= k
            u = jnp.where(take, cand, u)
            at_u = jnp.where(take, cnt, at_u)
        open_lanes = jnp.sum((at_u != k).astype(I32))
        return i + per_trip, u, at_u, open_lanes

    unknown = jnp.full((1, Q_TILE), -1.0, F32)
    state = (jnp.int32(0), hi << 16, jnp.where(keeps, at_hi, unknown), jnp.int32(1))
    _, u, _, _ = lax.while_loop(lambda st: (st[0] < 16) & (st[3] > 0), trip, state)
    return _sortable_to_f32(u)


def _loop_in_trips(n, body, init):
    def trips(start, count, width, carry):
        def group(i, c):
            for t in range(width):
                c = body(start + width * i + t, c)
            return c
        return lax.fori_loop(0, count, group, carry)

    carry = trips(0, n // 4, 4, init)
    carry = trips(4 * (n // 4), (n % 4) // 2, 2, carry)
    return trips(2 * (n // 2), n % 2, 1, carry)


def _dsa_kernel(q_ref, qi_ref, wukt_ref, wit_ref, kn_ref, c_ref, ct_ref, gb_ref, bias_ref, wuvt_ref,
                o_ref, qat_ref, qit_ref, score_s, score16_s, thr_s, lg_s, acc_s, topk):
    qb = pl.program_id(1)

    qt = q_ref[...]
    scale = HEAD_DIM ** -0.5 * LOG2E
    for h in range(ATT_HEADS):
        qa = jnp.dot(wukt_ref[h], qt[h * HEAD_DIM:(h + 1) * HEAD_DIM], preferred_element_type=F32)
        qat_ref[h] = (qa * scale).astype(BF16)
    qit = qi_ref[...]
    for h in range(IDX_HEADS):
        qit_ref[h] = qit[h * IDX_DIM:(h + 1) * IDX_DIM, :]
    q_tiles = Q_TILE // K_CHUNK
    nkc = (qb + 1) * q_tiles
    nac = (nkc * K_CHUNK + ATT_CHUNK - 1) // ATT_CHUNK
    tiles = ATT_CHUNK // K_CHUNK

    kiota = lax.broadcasted_iota(I32, (K_CHUNK, Q_TILE), 0)
    qpos = qb * Q_TILE + lax.broadcasted_iota(I32, (K_CHUNK, Q_TILE), 1)

    def score_chunk(ac, carry):
        for t in range(tiles):
            k0 = pl.multiple_of(ac * ATT_CHUNK + t * K_CHUNK, K_CHUNK)
            kn = kn_ref[pl.ds(k0, K_CHUNK), :]
            acc = jnp.zeros((K_CHUNK, Q_TILE), F32)
            for h in range(IDX_HEADS):
                sc = jnp.dot(kn, qit_ref[h], preferred_element_type=F32)
                acc = acc + jnp.maximum(sc, 0.0) * wit_ref[h:h + 1, :]
            masked = jnp.where(kiota + k0 <= qpos, acc, -jnp.inf)
            score_s[pl.ds(k0, K_CHUNK), :] = masked
            score16_s[pl.ds(k0, K_CHUNK), :] = masked.astype(BF16)
        return carry

    _loop_in_trips(nac, score_chunk, 0)

    for v in range(1, score_s.shape[0] // ATT_CHUNK + 1):
        @pl.when(nac == v)
        def _(rows=v * ATT_CHUNK):
            if rows <= topk:
                thr = jnp.full((1, Q_TILE), NEG, F32)
            else:
                thr = _kth_largest(score_s, score16_s, rows, float(topk))
                thr = jnp.where(thr >= NEG, thr, NEG)
            thr_s[...] = jnp.broadcast_to(thr, thr_s.shape)

    thr = thr_s[0:1, :]

    def logit_chunk(ac, m8s):
        r0 = pl.multiple_of(ac * ATT_CHUNK, ATT_CHUNK)
        c_chunk = c_ref[pl.ds(r0, ATT_CHUNK), :]
        mbias = jnp.where(score_s[pl.ds(r0, ATT_CHUNK), :] >= thr, 0.0, NEG)
        near = [[jnp.clip(ac * tiles + t - (qb * q_tiles + j) + 2, 0, 2) for j in range(q_tiles)]
                for t in range(tiles)]
        out = []
        for h in range(ATT_HEADS):
            lg = jnp.dot(c_chunk, qat_ref[h], preferred_element_type=F32) + mbias
            lg = jnp.concatenate(
                [lg[t * K_CHUNK:(t + 1) * K_CHUNK]
                 + jnp.concatenate([bias_ref[h, near[t][j]] for j in range(q_tiles)], axis=1)
                 for t in range(tiles)], axis=0)
            lg_s[h, pl.ds(r0, ATT_CHUNK), :] = lg
            out.append(jnp.maximum(m8s[h], jnp.max(lg.reshape(ATT_CHUNK // 8, 8, Q_TILE), axis=0)))
        return tuple(out)

    m8s = _loop_in_trips(nac, logit_chunk, tuple(jnp.full((8, Q_TILE), NEG, F32) for _ in range(ATT_HEADS)))
    ms = [jnp.max(m8, axis=0, keepdims=True) for m8 in m8s]

    acc_s[...] = jnp.zeros(acc_s.shape, F32)

    def pv_chunk(ac, carry):
        r0 = pl.multiple_of(ac * ATT_CHUNK, ATT_CHUNK)
        ct_chunk = ct_ref[:, pl.ds(r0, ATT_CHUNK)]
        for h in range(ATT_HEADS):
            pr = jnp.exp2(lg_s[h, pl.ds(r0, ATT_CHUNK), :] - ms[h])
            acc_s[h] += jnp.dot(ct_chunk, pr.astype(BF16), preferred_element_type=F32)
        return carry

    _loop_in_trips(nac, pv_chunk, 0)

    for h in range(ATT_HEADS):
        denom = acc_s[h, KV_LATENT:KV_LATENT + 1, :]
        o_t = acc_s[h, 0:KV_LATENT, :] * (1.0 / denom)
        y_t = jnp.dot(wuvt_ref[h], o_t.astype(BF16), preferred_element_type=F32)
        gb = gb_ref[:, h * HEAD_DIM:(h + 1) * HEAD_DIM]
        o_ref[:, h * HEAD_DIM:(h + 1) * HEAD_DIM] = (y_t.T * (gb * _sigmoid(gb))).astype(o_ref.dtype)


def _dsa(pb_t, cols_b, w_ukt, wit, kn, c, ct, pa3, gb_blk, bias_tiles, wuvt, topk):
    bsz, s, _ = c.shape
    att_w = ATT_HEADS * HEAD_DIM
    idx_w = IDX_HEADS * IDX_DIM
    assert s % ATT_CHUNK == 0 and s % Q_TILE == 0 and Q_TILE % K_CHUNK == 0 and ATT_CHUNK % K_CHUNK == 0
    assert s // 16 // 4 < 256
    q_blk = _col_block(cols_b["q"], att_w)
    qi_blk = _col_block(cols_b["qi"], idx_w)
    n_q = s // Q_TILE
    const = lambda shape: pl.BlockSpec(shape, lambda b, i: (0,) * len(shape))
    return pl.pallas_call(
        functools.partial(_dsa_kernel, topk=topk),
        grid=(bsz, s // Q_TILE),
        in_specs=[
            pl.BlockSpec((att_w, Q_TILE), lambda b, i: (q_blk, b * n_q + i)),
            pl.BlockSpec((idx_w, Q_TILE), lambda b, i: (qi_blk, b * n_q + i)),
            const(w_ukt.shape),
            pl.BlockSpec((None, IDX_HEADS, Q_TILE), lambda b, i: (b, 0, i)),
            pl.BlockSpec((None, s, IDX_DIM), lambda b, i: (b, 0, 0)),
            pl.BlockSpec((None, s, KV_LATENT), lambda b, i: (b, 0, 0)),
            pl.BlockSpec((None, KV_LATENT + ONES_ROWS, s), lambda b, i: (b, 0, 0)),
            pl.BlockSpec((None, Q_TILE, att_w), lambda b, i: (b, i, gb_blk)),
            const(bias_tiles.shape),
            const(wuvt.shape),
        ],
        out_specs=pl.BlockSpec((None, Q_TILE, att_w), lambda b, i: (b, i, 0)),
        out_shape=jax.ShapeDtypeStruct((bsz, s, att_w), BF16),
        scratch_shapes=[
            pltpu.VMEM((ATT_HEADS, KV_LATENT, Q_TILE), BF16),
            pltpu.VMEM((IDX_HEADS, IDX_DIM, Q_TILE), BF16),
            pltpu.VMEM((s, Q_TILE), F32),
            pltpu.VMEM((s, Q_TILE), BF16),
            pltpu.VMEM((8, Q_TILE), F32),
            pltpu.VMEM((ATT_HEADS, s, Q_TILE), F32),
            pltpu.VMEM((ATT_HEADS, KV_LATENT + ONES_ROWS, Q_TILE), F32),
        ],
        compiler_params=_cparams(("parallel", "arbitrary")),
        name="dsa",
    )(pb_t, pb_t, w_ukt, wit, kn, c, ct, pa3, bias_tiles, wuvt)


def _outp_kernel(ya_ref, yb_ref, w_ref, x_ref, g_ref, o_ref, wbf_s, *, final_norm):
    @pl.when(pl.program_id(0) == 0)
    def _():
        wbf_s[...] = w_ref[...].astype(BF16)

    ka = ya_ref.shape[1]
    acc = jnp.dot(ya_ref[...], wbf_s[0:ka, :], preferred_element_type=F32)
    acc = acc + jnp.dot(yb_ref[...], wbf_s[ka:, :], preferred_element_type=F32)
    x = x_ref[...] + acc
    if final_norm:
        x = x * lax.rsqrt(jnp.mean(x * x, axis=-1, keepdims=True) + EPS) * g_ref[...]
    o_ref[...] = x


def _outp(ya, yb, w_out, x2, g, final_norm, tm=512):
    m, d = x2.shape
    ka, kb = ya.shape[1], yb.shape[1]
    assert w_out.shape == (ka + kb, d)
    return pl.pallas_call(
        functools.partial(_outp_kernel, final_norm=final_norm),
        grid=(m // tm,),
        in_specs=[
            pl.BlockSpec((tm, ka), lambda i: (i, 0)),
            pl.BlockSpec((tm, kb), lambda i: (i, 0)),
            pl.BlockSpec((ka + kb, d), lambda i: (0, 0), pipeline_mode=pl.Buffered(1)),
            pl.BlockSpec((tm, d), lambda i: (i, 0)),
            pl.BlockSpec((1, d), lambda i: (0, 0)),
        ],
        out_specs=pl.BlockSpec((tm, d), lambda i: (i, 0)),
        out_shape=jax.ShapeDtypeStruct((m, d), F32),
        scratch_shapes=[pltpu.VMEM((ka + kb, d), BF16)],
        compiler_params=_cparams(("arbitrary",)),
        name="outp",
    )(ya, yb, w_out, x2, g)


def _t5_bucket(dist):
    n = jnp.maximum(dist, 0)
    max_exact = REL_BUCKETS // 2
    nf = jnp.maximum(n, 1).astype(F32)
    large = max_exact + (jnp.log(nf / max_exact) / np.log(REL_MAX_DIST / max_exact)
                         * (REL_BUCKETS - max_exact)).astype(I32)
    large = jnp.minimum(large, REL_BUCKETS - 1)
    return jnp.where(n < max_exact, n, large)


def _bias_tiles(rel_bias):
    qw = K_CHUNK
    span = K_CHUNK + qw
    table = rel_bias[_t5_bucket(jnp.arange(span + 1, dtype=I32))].astype(F32)
    table = ((table[:span] - table[span:]) * LOG2E).T
    n = span + qw - 1
    a = jnp.concatenate([jnp.zeros((ATT_HEADS, qw - 1), F32), table], axis=1)
    shifted = jnp.tile(a, (1, span + 1))[:, :span * (n + 1)].reshape(ATT_HEADS, span, n + 1)
    tiles = shifted[:, ::-1, :qw].reshape(ATT_HEADS, 2, K_CHUNK, qw)
    return jnp.concatenate([jnp.zeros_like(tiles[:, :1]), tiles], axis=1)


def kernel(x, norm_g, w_in, conv_w, conv_b, lru_wa, lru_ba, lru_wx, lru_bx, lru_lambda, ckv_norm_g, idx_k_norm_g, idx_k_norm_b, w_uk, w_uv, w_out, rel_bias, final_norm_g):
    bsz, s, d = x.shape
    depth = w_in.shape[0]
    lru_w = lru_wa.shape[1] * lru_wa.shape[2]
    att_w = ATT_HEADS * HEAD_DIM
    idx_w = IDX_HEADS * IDX_DIM
    assert REL_MAX_DIST <= K_CHUNK
    assert lru_w == att_w == idx_w and att_w % KV_LATENT == 0
    topk = min(INDEX_TOPK, s // 4)

    o_q = 2 * lru_w
    o_ckv = o_q + att_w
    o_gb = o_ckv + KV_LATENT
    o_qi = o_gb + att_w
    o_ki = o_qi + idx_w
    tn = 512
    cols_a = {"xa": 0, "ga": lru_w, "gb": 2 * lru_w}
    cols_b = {"q": 0, "qi": att_w}
    n_f32, n_bf16 = 3 * lru_w, att_w + idx_w

    bias_tiles = _bias_tiles(rel_bias)
    x2 = x.reshape(bsz * s, d)
    for l in range(depth):
        order = [(0, o_q), (o_gb, o_qi), (o_q, o_ckv), (o_qi, o_ki), (o_ckv, o_gb), (o_ki, w_in.shape[2])]
        pa, pb, c, ct, kn, wit = _proj(x2, norm_g[l][None, :], w_in[l].T, order, ckv_norm_g[l][None, :],
                                       idx_k_norm_g[l][None, :], idx_k_norm_b[l][None, :], n_f32, n_bf16, s, tn=tn)
        pa3 = pa.reshape(bsz, s, -1)

        ya = _rglru(pa3, cols_a, conv_w[l], conv_b[l][None, :], (0.5 * lru_wa[l]).astype(BF16),
                    0.5 * lru_ba[l][None, :], (0.5 * lru_wx[l]).astype(BF16), 0.5 * lru_bx[l][None, :],
                    lru_lambda[l][None, :])

        wukt = jnp.transpose(w_uk[l], (0, 2, 1)).astype(BF16)
        wuvt = jnp.transpose(w_uv[l], (0, 2, 1)).astype(BF16)
        yb = _dsa(pb, cols_b, wukt, wit, kn.reshape(bsz, s, -1), c.reshape(bsz, s, -1), ct, pa3,
                  _col_block(cols_a["gb"], att_w), bias_tiles, wuvt, topk)

        x2 = _outp(ya.reshape(bsz * s, lru_w), yb.reshape(bsz * s, att_w), w_out[l], x2,
                   final_norm_g[None, :], final_norm=(l == depth - 1))
    return x2.reshape(bsz, s, d)
```

```python
import functools

import numpy as np
import jax
import jax.numpy as jnp
from jax import lax
from jax.experimental import pallas as pl
from jax.experimental.pallas import tpu as pltpu

F32 = jnp.float32
BF16 = jnp.bfloat16
I32 = jnp.int32

CONV_WIDTH = 4
LRU_C = 8.0
ATT_HEADS = 8
HEAD_DIM = 128
KV_LATENT = 256
IDX_HEADS = 16
IDX_DIM = 64
INDEX_TOPK = 256
REL_BUCKETS = 32
REL_MAX_DIST = 128
EPS = 1e-6
LOG2E = float(np.log2(np.e))
ONES_ROWS = 16

Q_TILE = 256
K_CHUNK = 128
ATT_CHUNK = 256
NEG = float(np.finfo(np.float32).min)
INT_MIN = -(2 ** 31)
VMEM_LIMIT = 56 * 1024 * 1024


def _cparams(sem):
    return pltpu.CompilerParams(dimension_semantics=sem, vmem_limit_bytes=VMEM_LIMIT)


def _col_block(offset, width):
    assert offset % width == 0
    return offset // width


def _proj_kernel(t_ref, x_ref, g_ref, wlo_ref, whi_ref, tail_ref, cg_ref, kg_ref, kb_ref,
                 oa_ref, ob_ref, c_ref, ct_ref, kn_ref, wit_ref, h_ref, *, na, nb):
    j = pl.program_id(1)
    r = pl.program_id(2)
    nt = (((1,), (1,)), ((), ()))

    def w_tile():
        lo = jnp.where(t_ref[2 * j] < 0, tail_ref[...], wlo_ref[...])
        hi = jnp.where(t_ref[2 * j + 1] < 0, tail_ref[...], whi_ref[...])
        return jnp.concatenate([lo, hi], axis=0).astype(BF16)

    @pl.when(j == 0)
    def _():
        x = x_ref[...]
        y = x * lax.rsqrt(jnp.mean(x * x, axis=-1, keepdims=True) + EPS)
        h = (y * g_ref[...]).astype(BF16)
        h_ref[r] = h
        oa_ref[...] = lax.dot_general(h, w_tile(), nt, preferred_element_type=F32)

    @pl.when((j > 0) & (j < na))
    def _():
        oa_ref[...] = lax.dot_general(h_ref[r], w_tile(), nt, preferred_element_type=F32)

    @pl.when((j >= na) & (j < na + nb))
    def _():
        ob_ref[...] = lax.dot_general(h_ref[r], w_tile(), nt, preferred_element_type=F32).astype(BF16)

    @pl.when(j >= na + nb)
    def _():
        used = KV_LATENT + 128
        tail = lax.dot_general(h_ref[r], w_tile()[:used], nt, preferred_element_type=F32)
        ckv = tail[:, :KV_LATENT]
        c = ckv * lax.rsqrt(jnp.mean(ckv * ckv, axis=-1, keepdims=True) + EPS) * cg_ref[...]
        c_ref[...] = c.astype(BF16)
        ct_ref[0:KV_LATENT, :] = c.T.astype(BF16)
        ct_ref[KV_LATENT:, :] = jnp.ones((ONES_ROWS, ct_ref.shape[1]), BF16)
        sm = tail[:, KV_LATENT:KV_LATENT + 128]
        ki = sm[:, :IDX_DIM]
        mu = jnp.mean(ki, axis=-1, keepdims=True)
        var = jnp.mean(jnp.square(ki - mu), axis=-1, keepdims=True)
        kn = (ki - mu) * lax.rsqrt(var + EPS) * kg_ref[...] + kb_ref[...]
        kn_ref[...] = kn.astype(BF16)
        wit_ref[...] = sm.T[IDX_DIM:IDX_DIM + IDX_HEADS, :] * (IDX_HEADS ** -0.5 * IDX_DIM ** -0.5)


def _proj(x2, g, w_t, order, ckv_g, k_g, k_b, n_f32, n_bf16, seq_len, tm=1024, tn=512, group=2):
    m, d = x2.shape
    n = w_t.shape[0]
    th = tn // 2
    n_whole = n // th
    src = []
    for start, stop in order:
        assert start % th == 0 and (stop % th == 0 or stop == n)
        src += list(range(start // th, -(-stop // th)))
    na, nb = n_f32 // tn, n_bf16 // tn
    assert n_f32 % tn == 0 and n_bf16 % tn == 0 and len(src) * th == n_f32 + n_bf16 + tn
    tail = jnp.pad(w_t[n_whole * th:], ((0, (n_whole + 1) * th - n), (0, 0)))
    table = jnp.asarray([blk if blk < n_whole else -1 for blk in src], I32)
    tps = seq_len // tm
    assert seq_len % tm == 0 and tn >= KV_LATENT + 128
    assert (m // tm) % group == 0
    nj = na + nb + 1
    const = lambda shape: pl.BlockSpec(shape, lambda i, j, r, t: (0,) * len(shape))
    last = group - 1
    row = lambda i, r_eff: i * group + r_eff
    per_seq = lambda g: (g // tps, 0, g % tps)
    tail_row = lambda i, j, r: row(i, jnp.where(j < nj - 1, 0, r))
    grid_spec = pltpu.PrefetchScalarGridSpec(
        num_scalar_prefetch=1,
        grid=(m // tm // group, nj, group),
        in_specs=[
            pl.BlockSpec((tm, d), lambda i, j, r, t: (row(i, jnp.where(j == 0, r, last)), 0)),
            const((1, d)),
            pl.BlockSpec((th, d), lambda i, j, r, t: (jnp.maximum(t[2 * j], 0), 0)),
            pl.BlockSpec((th, d), lambda i, j, r, t: (jnp.maximum(t[2 * j + 1], 0), 0)),
            const((th, d)),
            const((1, KV_LATENT)),
            const((1, IDX_DIM)),
            const((1, IDX_DIM)),
        ],
        out_specs=[
            pl.BlockSpec((tm, tn), lambda i, j, r, t: (row(i, jnp.where(j < na, r, last)), jnp.minimum(j, na - 1))),
            pl.BlockSpec((tm, tn), lambda i, j, r, t: (
                row(i, jnp.where(j < na, 0, jnp.where(j < na + nb, r, last))), jnp.clip(j - na, 0, nb - 1))),
            pl.BlockSpec((tm, KV_LATENT), lambda i, j, r, t: (tail_row(i, j, r), 0)),
            pl.BlockSpec((None, KV_LATENT + ONES_ROWS, tm), lambda i, j, r, t: per_seq(tail_row(i, j, r))),
            pl.BlockSpec((tm, IDX_DIM), lambda i, j, r, t: (tail_row(i, j, r), 0)),
            pl.BlockSpec((None, IDX_HEADS, tm), lambda i, j, r, t: per_seq(tail_row(i, j, r))),
        ],
        scratch_shapes=[pltpu.VMEM((group, tm, d), BF16)],
    )
    return pl.pallas_call(
        functools.partial(_proj_kernel, na=na, nb=nb),
        grid_spec=grid_spec,
        out_shape=[
            jax.ShapeDtypeStruct((m, n_f32), F32),
            jax.ShapeDtypeStruct((m, n_bf16), BF16),
            jax.ShapeDtypeStruct((m, KV_LATENT), BF16),
            jax.ShapeDtypeStruct((m // seq_len, KV_LATENT + ONES_ROWS, seq_len), BF16),
            jax.ShapeDtypeStruct((m, IDX_DIM), BF16),
            jax.ShapeDtypeStruct((m // seq_len, IDX_HEADS, seq_len), F32),
        ],
        compiler_params=_cparams(("arbitrary", "arbitrary", "arbitrary")),
        name="proj",
    )(table, x2, g, w_t, w_t, tail, ckv_g, k_g, k_b)


def _sigmoid(v):
    return 0.5 * jnp.tanh(0.5 * v) + 0.5


def _scan_step(a, b, k, axis, idx):
    keep = idx >= k
    a_prev = jnp.where(keep, pltpu.roll(a, k, axis=axis), 1.0)
    b_prev = jnp.where(keep, pltpu.roll(b, k, axis=axis), 0.0)
    return a * a_prev, a * b_prev + b


def _rglru_kernel(xa_ref, ga_ref, cw_ref, cb_ref, wa_ref, ba_ref, wx_ref, bx_ref, lam_ref,
                  o_ref, pad_s, a_s, b_s, c_s):
    s, w = xa_ref.shape
    tile = 8
    n_tiles = s // tile

    pad_s[0:tile, :] = jnp.zeros((tile, w), F32)
    pad_s[tile:tile + s, :] = xa_ref[...]
    acc = pad_s[tile:tile + s, :] * cw_ref[CONV_WIDTH - 1:CONV_WIDTH, :]
    for j in range(CONV_WIDTH - 1):
        back = CONV_WIDTH - 1 - j
        acc = acc + pad_s[tile - back:tile - back + s, :] * cw_ref[j:j + 1, :]
    xc = cb_ref[...] + acc

    xcb = xc.astype(BF16)
    tr = jnp.tanh(jnp.dot(xcb, wa_ref[...], preferred_element_type=F32) + ba_ref[...])
    ti = jnp.tanh(jnp.dot(xcb, wx_ref[...], preferred_element_type=F32) + bx_ref[...])
    i = 0.5 * ti + 0.5
    z = -lam_ref[...]
    softplus = jnp.maximum(z, 0.0) + jnp.log1p(jnp.exp(-jnp.abs(z)))
    half = (-0.5 * LRU_C) * softplus
    log_a = half * tr + half
    a = jnp.exp(log_a)
    m2 = (1.0 + a * a) * jnp.tanh(-log_a)
    mult = jnp.where(m2 > 0.0, m2 * lax.rsqrt(m2), 0.0)
    gated = i * xc
    b_s[...] = mult * gated
    b_s[0:1, :] = gated[0:1, :]

    a3 = a.reshape(n_tiles, tile, w)
    b3 = b_s[...].reshape(n_tiles, tile, w)
    sub = lax.broadcasted_iota(I32, (n_tiles, tile, w), 1)
    for k in (1, 2, 4):
        a3, b3 = _scan_step(a3, b3, k, 1, sub)
    a_s[...] = a3.reshape(s, w)
    b_s[...] = b3.reshape(s, w)

    at = a_s[pl.ds(tile - 1, n_tiles, stride=tile), :]
    bt = b_s[pl.ds(tile - 1, n_tiles, stride=tile), :]
    trow = lax.broadcasted_iota(I32, (n_tiles, w), 0)
    k = 1
    while k < n_tiles:
        at, bt = _scan_step(at, bt, k, 0, trow)
        k *= 2
    c_s[0:tile, :] = jnp.zeros((tile, w), F32)
    c_s[tile:tile + n_tiles, :] = bt

    for t in range(n_tiles):
        rows = slice(t * tile, (t + 1) * tile)
        before = c_s[pl.ds(tile - 1 + t, tile, stride=0), :]
        h = a_s[rows, :] * before + b_s[rows, :]
        gh = 0.5 * ga_ref[rows, :]
        o_ref[rows, :] = (h * (gh * (jnp.tanh(gh) + 1.0))).astype(o_ref.dtype)


def _rglru(pa3, cols, conv_w, conv_b, wa, ba, wx, bx, lam):
    bsz, s, _ = pa3.shape
    g, w = wa.shape[0], wa.shape[-1]
    xa_blk = _col_block(cols["xa"], w)
    ga_blk = _col_block(cols["ga"], w)
    vec = lambda: pl.BlockSpec((1, w), lambda b, j: (0, j))
    return pl.pallas_call(
        _rglru_kernel,
        grid=(bsz, g),
        in_specs=[
            pl.BlockSpec((None, s, w), lambda b, j: (b, 0, xa_blk + j)),
            pl.BlockSpec((None, s, w), lambda b, j: (b, 0, ga_blk + j)),
            pl.BlockSpec((CONV_WIDTH, w), lambda b, j: (0, j)),
            vec(),
            pl.BlockSpec((None, w, w), lambda b, j: (j, 0, 0)),
            vec(),
            pl.BlockSpec((None, w, w), lambda b, j: (j, 0, 0)),
            vec(),
            vec(),
        ],
        out_specs=pl.BlockSpec((None, s, w), lambda b, j: (b, 0, j)),
        out_shape=jax.ShapeDtypeStruct((bsz, s, g * w), BF16),
        scratch_shapes=[pltpu.VMEM((s + 8, w), F32), pltpu.VMEM((s, w), F32), pltpu.VMEM((s, w), F32),
                        pltpu.VMEM((s // 8 + 8, w), F32)],
        compiler_params=_cparams(("parallel", "parallel")),
        name="rglru",
    )(pa3, pa3, conv_w, conv_b, wa, ba, wx, bx, lam)


def _tree_sum(parts):
    while len(parts) > 1:
        paired = [parts[i] + parts[i + 1] for i in range(0, len(parts) - 1, 2)]
        parts = paired + ([parts[-1]] if len(parts) % 2 else [])
    return parts[0]


def _sortable_to_f32(u):
    key = u ^ INT_MIN
    return lax.bitcast_convert_type(key ^ ((key >> 31) & 0x7FFFFFFF), F32)


def _count_ge(ref, rows, cand, pack, strict=False):
    chains = 4
    one, zero = jnp.ones((), ref.dtype), jnp.zeros((), ref.dtype)
    accs = [None] * chains
    for r in range(rows // pack):
        x = ref[r * pack:(r + 1) * pack, :]
        hit = jnp.where(x > cand if strict else x >= cand, one, zero)
        accs[r % chains] = hit if accs[r % chains] is None else accs[r % chains] + hit
    parts = [a.astype(F32) for a in accs if a is not None]
    return jnp.sum(_tree_sum(parts), axis=0, keepdims=True)


def _kth_largest(score_ref, score16_ref, rows, k):
    def step16(i, u):
        cand = u | (jnp.int32(1) << (15 - i))
        cand_f = _sortable_to_f32(cand << 16).astype(BF16)
        return jnp.where(_count_ge(score16_ref, rows, cand_f, 16) >= k, cand, u)

    hi = lax.fori_loop(0, 16, step16, jnp.zeros((1, Q_TILE), I32))
    at_hi = _count_ge(score_ref, rows, _sortable_to_f32(hi << 16), 8)
    keeps = at_hi >= k
    hi = jnp.where(keeps, hi, jnp.maximum(hi - 1, 0))

    per_trip = 4

    def trip(state):
        i, u, at_u, _ = state
        for t in range(per_trip):
            cand = u | (jnp.int32(1) << (15 - (i + t)))
            cnt = _count_ge(score_ref, rows, _sortable_to_f32(cand), 8)
            take = cnt >= k
            u = jnp.where(take, cand, u)
            at_u = jnp.where(take, cnt, at_u)
        open_lanes = jnp.sum((at_u != k).astype(I32))
        return i + per_trip, u, at_u, open_lanes

    unknown = jnp.full((1, Q_TILE), -1.0, F32)
    state = (jnp.int32(0), hi << 16, jnp.where(keeps, at_hi, unknown), jnp.int32(1))
    _, u, _, open_lanes = lax.while_loop(lambda st: (st[0] < 16) & (st[3] > 0), trip, state)
    return _sortable_to_f32(u), open_lanes


def _drop_extra_ties(score_ref, rows, thr, k):
    need = k - _count_ge(score_ref, rows, thr, 8, strict=True)
    sub = lax.broadcasted_iota(I32, (8, Q_TILE), 0)

    def group(g, seen):
        r0 = pl.multiple_of(g * 8, 8)
        x = score_ref[pl.ds(r0, 8), :]
        tie = jnp.where(x == thr, 1.0, 0.0)
        upto = tie
        for k8 in (1, 2, 4):
            upto = upto + jnp.where(sub >= k8, pltpu.roll(upto, k8, axis=0), 0.0)
        before = seen + upto - tie
        score_ref[pl.ds(r0, 8), :] = jnp.where((tie > 0.0) & (before >= need), -jnp.inf, x)
        return seen + upto[7:8, :]

    lax.fori_loop(0, rows // 8, group, jnp.zeros((1, Q_TILE), F32))


def _loop_in_trips(n, body, init):
    def trips(start, count, width, carry):
        def group(i, c):
            for t in range(width):
                c = body(start + width * i + t, c)
            return c
        return lax.fori_loop(0, count, group, carry)

    carry = trips(0, n // 4, 4, init)
    carry = trips(4 * (n // 4), (n % 4) // 2, 2, carry)
    return trips(2 * (n // 2), n % 2, 1, carry)


def _dsa_kernel(q_ref, qi_ref, wukt_ref, wit_ref, kn_ref, c_ref, ct_ref, gb_ref, bias_ref, wuvt_ref,
                o_ref, qat_ref, qit_ref, score_s, score16_s, thr_s, lg_s, acc_s, topk):
    qb = pl.program_id(1)

    qt = q_ref[...].T
    scale = HEAD_DIM ** -0.5 * LOG2E
    for h in range(ATT_HEADS):
        qa = jnp.dot(wukt_ref[h], qt[h * HEAD_DIM:(h + 1) * HEAD_DIM], preferred_element_type=F32)
        qat_ref[h] = (qa * scale).astype(BF16)
    qit = qi_ref[...].T
    for h in range(IDX_HEADS):
        qit_ref[h] = qit[h * IDX_DIM:(h + 1) * IDX_DIM, :]
    q_tiles = Q_TILE // K_CHUNK
    nkc = (qb + 1) * q_tiles
    nac = (nkc * K_CHUNK + ATT_CHUNK - 1) // ATT_CHUNK
    tiles = ATT_CHUNK // K_CHUNK

    kiota = lax.broadcasted_iota(I32, (K_CHUNK, Q_TILE), 0)
    qpos = qb * Q_TILE + lax.broadcasted_iota(I32, (K_CHUNK, Q_TILE), 1)

    def score_chunk(ac, carry):
        for t in range(tiles):
            k0 = pl.multiple_of(ac * ATT_CHUNK + t * K_CHUNK, K_CHUNK)
            kn = kn_ref[pl.ds(k0, K_CHUNK), :]
            acc = jnp.zeros((K_CHUNK, Q_TILE), F32)
            for h in range(IDX_HEADS):
                sc = jnp.dot(kn, qit_ref[h], preferred_element_type=F32)
                acc = acc + jnp.maximum(sc, 0.0) * wit_ref[h:h + 1, :]
            masked = jnp.where(kiota + k0 <= qpos, acc, -jnp.inf)
            score_s[pl.ds(k0, K_CHUNK), :] = masked
            score16_s[pl.ds(k0, K_CHUNK), :] = masked.astype(BF16)
        return carry

    _loop_in_trips(nac, score_chunk, 0)

    for v in range(1, score_s.shape[0] // ATT_CHUNK + 1):
        @pl.when(nac == v)
        def _(rows=v * ATT_CHUNK):
            if rows <= topk:
                thr = jnp.full((1, Q_TILE), NEG, F32)
            else:
                thr, open_lanes = _kth_largest(score_s, score16_s, rows, float(topk))
                thr = jnp.where(thr >= NEG, thr, NEG)

                @pl.when(open_lanes > 0)
                def _():
                    _drop_extra_ties(score_s, rows, thr, float(topk))
            thr_s[...] = jnp.broadcast_to(thr, thr_s.shape)

    thr = thr_s[0:1, :]

    def logit_chunk(ac, m8s):
        r0 = pl.multiple_of(ac * ATT_CHUNK, ATT_CHUNK)
        c_chunk = c_ref[pl.ds(r0, ATT_CHUNK), :]
        mbias = jnp.where(score_s[pl.ds(r0, ATT_CHUNK), :] >= thr, 0.0, NEG)
        near = [[jnp.clip(ac * tiles + t - (qb * q_tiles + j) + 2, 0, 2) for j in range(q_tiles)]
                for t in range(tiles)]
        out = []
        for h in range(ATT_HEADS):
            lg = jnp.dot(c_chunk, qat_ref[h], preferred_element_type=F32) + mbias
            lg = jnp.concatenate(
                [lg[t * K_CHUNK:(t + 1) * K_CHUNK]
                 + jnp.concatenate([bias_ref[h, near[t][j]] for j in range(q_tiles)], axis=1)
                 for t in range(tiles)], axis=0)
            lg_s[h, pl.ds(r0, ATT_CHUNK), :] = lg
            out.append(jnp.maximum(m8s[h], jnp.max(lg.reshape(ATT_CHUNK // 8, 8, Q_TILE), axis=0)))
        return tuple(out)

    m8s = _loop_in_trips(nac, logit_chunk, tuple(jnp.full((8, Q_TILE), NEG, F32) for _ in range(ATT_HEADS)))
    ms = [jnp.max(m8, axis=0, keepdims=True) for m8 in m8s]

    acc_s[...] = jnp.zeros(acc_s.shape, F32)

    def pv_chunk(ac, carry):
        r0 = pl.multiple_of(ac * ATT_CHUNK, ATT_CHUNK)
        ct_chunk = ct_ref[:, pl.ds(r0, ATT_CHUNK)]
        for h in range(ATT_HEADS):
            pr = jnp.exp2(lg_s[h, pl.ds(r0, ATT_CHUNK), :] - ms[h])
            acc_s[h] += jnp.dot(ct_chunk, pr.astype(BF16), preferred_element_type=F32)
        return carry

    _loop_in_trips(nac, pv_chunk, 0)

    for h in range(ATT_HEADS):
        denom = acc_s[h, KV_LATENT:KV_LATENT + 1, :]
        o_t = acc_s[h, 0:KV_LATENT, :] * (1.0 / denom)
        y_t = jnp.dot(wuvt_ref[h], o_t.astype(BF16), preferred_element_type=F32)
        gb = gb_ref[:, h * HEAD_DIM:(h + 1) * HEAD_DIM]
        o_ref[:, h * HEAD_DIM:(h + 1) * HEAD_DIM] = (y_t.T * (gb * _sigmoid(gb))).astype(o_ref.dtype)


def _dsa(pb3, cols_b, w_ukt, wit, kn, c, ct, pa3, gb_blk, bias_tiles, wuvt, topk):
    bsz, s, _ = c.shape
    att_w = ATT_HEADS * HEAD_DIM
    idx_w = IDX_HEADS * IDX_DIM
    assert s % ATT_CHUNK == 0 and s % Q_TILE == 0 and Q_TILE % K_CHUNK == 0 and ATT_CHUNK % K_CHUNK == 0
    assert s // 16 // 4 < 256
    q_blk = _col_block(cols_b["q"], att_w)
    qi_blk = _col_block(cols_b["qi"], idx_w)
    const = lambda shape: pl.BlockSpec(shape, lambda b, i: (0,) * len(shape))
    return pl.pallas_call(
        functools.partial(_dsa_kernel, topk=topk),
        grid=(bsz, s // Q_TILE),
        in_specs=[
            pl.BlockSpec((None, Q_TILE, att_w), lambda b, i: (b, i, q_blk)),
            pl.BlockSpec((None, Q_TILE, idx_w), lambda b, i: (b, i, qi_blk)),
            const(w_ukt.shape),
            pl.BlockSpec((None, IDX_HEADS, Q_TILE), lambda b, i: (b, 0, i)),
            pl.BlockSpec((None, s, IDX_DIM), lambda b, i: (b, 0, 0)),
            pl.BlockSpec((None, s, KV_LATENT), lambda b, i: (b, 0, 0)),
            pl.BlockSpec((None, KV_LATENT + ONES_ROWS, s), lambda b, i: (b, 0, 0)),
            pl.BlockSpec((None, Q_TILE, att_w), lambda b, i: (b, i, gb_blk)),
            const(bias_tiles.shape),
            const(wuvt.shape),
        ],
        out_specs=pl.BlockSpec((None, Q_TILE, att_w), lambda b, i: (b, i, 0)),
        out_shape=jax.ShapeDtypeStruct((bsz, s, att_w), BF16),
        scratch_shapes=[
            pltpu.VMEM((ATT_HEADS, KV_LATENT, Q_TILE), BF16),
            pltpu.VMEM((IDX_HEADS, IDX_DIM, Q_TILE), BF16),
            pltpu.VMEM((s, Q_TILE), F32),
            pltpu.VMEM((s, Q_TILE), BF16),
            pltpu.VMEM((8, Q_TILE), F32),
            pltpu.VMEM((ATT_HEADS, s, Q_TILE), F32),
            pltpu.VMEM((ATT_HEADS, KV_LATENT + ONES_ROWS, Q_TILE), F32),
        ],
        compiler_params=_cparams(("parallel", "arbitrary")),
        name="dsa",
    )(pb3, pb3, w_ukt, wit, kn, c, ct, pa3, bias_tiles, wuvt)


def _outp_kernel(ya_ref, yb_ref, w_ref, x_ref, g_ref, o_ref, wbf_s, *, final_norm):
    @pl.when(pl.program_id(0) == 0)
    def _():
        wbf_s[...] = w_ref[...].astype(BF16)

    ka = ya_ref.shape[1]
    acc = jnp.dot(ya_ref[...], wbf_s[0:ka, :], preferred_element_type=F32)
    acc = acc + jnp.dot(yb_ref[...], wbf_s[ka:, :], preferred_element_type=F32)
    x = x_ref[...] + acc
    if final_norm:
        x = x * lax.rsqrt(jnp.mean(x * x, axis=-1, keepdims=True) + EPS) * g_ref[...]
    o_ref[...] = x


def _outp(ya, yb, w_out, x2, g, final_norm, tm=512):
    m, d = x2.shape
    ka, kb = ya.shape[1], yb.shape[1]
    assert w_out.shape == (ka + kb, d)
    return pl.pallas_call(
        functools.partial(_outp_kernel, final_norm=final_norm),
        grid=(m // tm,),
        in_specs=[
            pl.BlockSpec((tm, ka), lambda i: (i, 0)),
            pl.BlockSpec((tm, kb), lambda i: (i, 0)),
            pl.BlockSpec((ka + kb, d), lambda i: (0, 0), pipeline_mode=pl.Buffered(1)),
            pl.BlockSpec((tm, d), lambda i: (i, 0)),
            pl.BlockSpec((1, d), lambda i: (0, 0)),
        ],
        out_specs=pl.BlockSpec((tm, d), lambda i: (i, 0)),
        out_shape=jax.ShapeDtypeStruct((m, d), F32),
        scratch_shapes=[pltpu.VMEM((ka + kb, d), BF16)],
        compiler_params=_cparams(("arbitrary",)),
        name="outp",
    )(ya, yb, w_out, x2, g)


def _t5_bucket(dist):
    n = jnp.maximum(dist, 0)
    max_exact = REL_BUCKETS // 2
    nf = jnp.maximum(n, 1).astype(F32)
    large = max_exact + (jnp.log(nf / max_exact) / np.log(REL_MAX_DIST / max_exact)
                         * (REL_BUCKETS - max_exact)).astype(I32)
    large = jnp.minimum(large, REL_BUCKETS - 1)
    return jnp.where(n < max_exact, n, large)


def _bias_tiles(rel_bias):
    qw = K_CHUNK
    span = K_CHUNK + qw
    table = rel_bias[_t5_bucket(jnp.arange(span + 1, dtype=I32))].astype(F32)
    table = ((table[:span] - table[span:]) * LOG2E).T
    n = span + qw - 1
    a = jnp.concatenate([jnp.zeros((ATT_HEADS, qw - 1), F32), table], axis=1)
    shifted = jnp.tile(a, (1, span + 1))[:, :span * (n + 1)].reshape(ATT_HEADS, span, n + 1)
    tiles = shifted[:, ::-1, :qw].reshape(ATT_HEADS, 2, K_CHUNK, qw)
    return jnp.concatenate([jnp.zeros_like(tiles[:, :1]), tiles], axis=1)


def kernel(x, norm_g, w_in, conv_w, conv_b, lru_wa, lru_ba, lru_wx, lru_bx, lru_lambda, ckv_norm_g, idx_k_norm_g, idx_k_norm_b, w_uk, w_uv, w_out, rel_bias, final_norm_g):
    bsz, s, d = x.shape
    depth = w_in.shape[0]
    lru_w = lru_wa.shape[1] * lru_wa.shape[2]
    att_w = ATT_HEADS * HEAD_DIM
    idx_w = IDX_HEADS * IDX_DIM
    assert REL_MAX_DIST <= K_CHUNK
    assert lru_w == att_w == idx_w and att_w % KV_LATENT == 0
    topk = min(INDEX_TOPK, s // 4)

    o_q = 2 * lru_w
    o_ckv = o_q + att_w
    o_gb = o_ckv + KV_LATENT
    o_qi = o_gb + att_w
    o_ki = o_qi + idx_w
    tn = 512
    cols_a = {"xa": 0, "ga": lru_w, "gb": 2 * lru_w}
    cols_b = {"q": 0, "qi": att_w}
    n_f32, n_bf16 = 3 * lru_w, att_w + idx_w

    bias_tiles = _bias_tiles(rel_bias)
    x2 = x.reshape(bsz * s, d)
    for l in range(depth):
        order = [(0, o_q), (o_gb, o_qi), (o_q, o_ckv), (o_qi, o_ki), (o_ckv, o_gb), (o_ki, w_in.shape[2])]
        pa, pb, c, ct, kn, wit = _proj(x2, norm_g[l][None, :], w_in[l].T, order, ckv_norm_g[l][None, :],
                                       idx_k_norm_g[l][None, :], idx_k_norm_b[l][None, :], n_f32, n_bf16, s, tn=tn)
        pa3 = pa.reshape(bsz, s, -1)
        pb3 = pb.reshape(bsz, s, -1)

        ya = _rglru(pa3, cols_a, conv_w[l], conv_b[l][None, :], (0.5 * lru_wa[l]).astype(BF16),
                    0.5 * lru_ba[l][None, :], (0.5 * lru_wx[l]).astype(BF16), 0.5 * lru_bx[l][None, :],
                    lru_lambda[l][None, :])

        wukt = jnp.transpose(w_uk[l], (0, 2, 1)).astype(BF16)
        wuvt = jnp.transpose(w_uv[l], (0, 2, 1)).astype(BF16)
        yb = _dsa(pb3, cols_b, wukt, wit, kn.reshape(bsz, s, -1), c.reshape(bsz, s, -1), ct, pa3,
                  _col_block(cols_a["gb"], att_w), bias_tiles, wuvt, topk)

        x2 = _outp(ya.reshape(bsz * s, lru_w), yb.reshape(bsz * s, att_w), w_out[l], x2,
                   final_norm_g[None, :], final_norm=(l == depth - 1))
    return x2.reshape(bsz, s, d)
```

```python
import functools

import numpy as np
import jax
import jax.numpy as jnp
from jax import lax
from jax.experimental import pallas as pl
from jax.experimental.pallas import tpu as pltpu

F32 = jnp.float32
BF16 = jnp.bfloat16
I32 = jnp.int32

CONV_WIDTH = 4
LRU_C = 8.0
ATT_HEADS = 8
HEAD_DIM = 128
KV_LATENT = 256
IDX_HEADS = 16
IDX_DIM = 64
INDEX_TOPK = 256
REL_BUCKETS = 32
REL_MAX_DIST = 128
EPS = 1e-6
LOG2E = float(np.log2(np.e))
ONES_ROWS = 16

Q_TILE = 256
K_CHUNK = 128
ATT_CHUNK = 256
NEG = float(np.finfo(np.float32).min)
INT_MIN = -(2 ** 31)
VMEM_LIMIT = 56 * 1024 * 1024


def _cparams(sem):
    return pltpu.CompilerParams(dimension_semantics=sem, vmem_limit_bytes=VMEM_LIMIT)


def _col_block(offset, width):
    assert offset % width == 0
    return offset // width


def _proj_kernel(t_ref, x_ref, g_ref, wlo_ref, whi_ref, tail_ref, cg_ref, kg_ref, kb_ref,
                 oa_ref, ob_ref, c_ref, ct_ref, kn_ref, wit_ref, h_ref, *, na, nb):
    j = pl.program_id(1)
    r = pl.program_id(2)
    nt = (((1,), (1,)), ((), ()))

    def w_tile():
        lo = jnp.where(t_ref[2 * j] < 0, tail_ref[...], wlo_ref[...])
        hi = jnp.where(t_ref[2 * j + 1] < 0, tail_ref[...], whi_ref[...])
        return jnp.concatenate([lo, hi], axis=0).astype(BF16)

    @pl.when(j == 0)
    def _():
        x = x_ref[...]
        y = x * lax.rsqrt(jnp.mean(x * x, axis=-1, keepdims=True) + EPS)
        h = (y * g_ref[...]).astype(BF16)
        h_ref[r] = h
        oa_ref[...] = lax.dot_general(h, w_tile(), nt, preferred_element_type=F32)

    @pl.when((j > 0) & (j < na))
    def _():
        oa_ref[...] = lax.dot_general(h_ref[r], w_tile(), nt, preferred_element_type=F32)

    @pl.when((j >= na) & (j < na + nb))
    def _():
        ob_ref[...] = lax.dot_general(h_ref[r], w_tile(), nt, preferred_element_type=F32).astype(BF16)

    @pl.when(j >= na + nb)
    def _():
        used = KV_LATENT + 128
        tail = lax.dot_general(h_ref[r], w_tile()[:used], nt, preferred_element_type=F32)
        ckv = tail[:, :KV_LATENT]
        c = ckv * lax.rsqrt(jnp.mean(ckv * ckv, axis=-1, keepdims=True) + EPS) * cg_ref[...]
        c_ref[...] = c.astype(BF16)
        ct_ref[0:KV_LATENT, :] = c.T.astype(BF16)
        ct_ref[KV_LATENT:, :] = jnp.ones((ONES_ROWS, ct_ref.shape[1]), BF16)
        sm = tail[:, KV_LATENT:KV_LATENT + 128]
        ki = sm[:, :IDX_DIM]
        mu = jnp.mean(ki, axis=-1, keepdims=True)
        var = jnp.mean(jnp.square(ki - mu), axis=-1, keepdims=True)
        kn = (ki - mu) * lax.rsqrt(var + EPS) * kg_ref[...] + kb_ref[...]
        kn_ref[...] = kn.astype(BF16)
        wit_ref[...] = sm.T[IDX_DIM:IDX_DIM + IDX_HEADS, :] * (IDX_HEADS ** -0.5 * IDX_DIM ** -0.5)


def _proj(x2, g, w_t, order, ckv_g, k_g, k_b, n_f32, n_bf16, seq_len, tm=1024, tn=512, group=2):
    m, d = x2.shape
    n = w_t.shape[0]
    th = tn // 2
    n_whole = n // th
    src = []
    for start, stop in order:
        assert start % th == 0 and (stop % th == 0 or stop == n)
        src += list(range(start // th, -(-stop // th)))
    na, nb = n_f32 // tn, n_bf16 // tn
    assert n_f32 % tn == 0 and n_bf16 % tn == 0 and len(src) * th == n_f32 + n_bf16 + tn
    tail = jnp.pad(w_t[n_whole * th:], ((0, (n_whole + 1) * th - n), (0, 0)))
    table = jnp.asarray([blk if blk < n_whole else -1 for blk in src], I32)
    tps = seq_len // tm
    assert seq_len % tm == 0 and tn >= KV_LATENT + 128
    assert (m // tm) % group == 0
    nj = na + nb + 1
    const = lambda shape: pl.BlockSpec(shape, lambda i, j, r, t: (0,) * len(shape))
    last = group - 1
    row = lambda i, r_eff: i * group + r_eff
    per_seq = lambda g: (g // tps, 0, g % tps)
    tail_row = lambda i, j, r: row(i, jnp.where(j < nj - 1, 0, r))
    grid_spec = pltpu.PrefetchScalarGridSpec(
        num_scalar_prefetch=1,
        grid=(m // tm // group, nj, group),
        in_specs=[
            pl.BlockSpec((tm, d), lambda i, j, r, t: (row(i, jnp.where(j == 0, r, last)), 0)),
            const((1, d)),
            pl.BlockSpec((th, d), lambda i, j, r, t: (jnp.maximum(t[2 * j], 0), 0)),
            pl.BlockSpec((th, d), lambda i, j, r, t: (jnp.maximum(t[2 * j + 1], 0), 0)),
            const((th, d)),
            const((1, KV_LATENT)),
            const((1, IDX_DIM)),
            const((1, IDX_DIM)),
        ],
        out_specs=[
            pl.BlockSpec((tm, tn), lambda i, j, r, t: (row(i, jnp.where(j < na, r, last)), jnp.minimum(j, na - 1))),
            pl.BlockSpec((tm, tn), lambda i, j, r, t: (
                row(i, jnp.where(j < na, 0, jnp.where(j < na + nb, r, last))), jnp.clip(j - na, 0, nb - 1))),
            pl.BlockSpec((tm, KV_LATENT), lambda i, j, r, t: (tail_row(i, j, r), 0)),
            pl.BlockSpec((None, KV_LATENT + ONES_ROWS, tm), lambda i, j, r, t: per_seq(tail_row(i, j, r))),
            pl.BlockSpec((tm, IDX_DIM), lambda i, j, r, t: (tail_row(i, j, r), 0)),
            pl.BlockSpec((None, IDX_HEADS, tm), lambda i, j, r, t: per_seq(tail_row(i, j, r))),
        ],
        scratch_shapes=[pltpu.VMEM((group, tm, d), BF16)],
    )
    return pl.pallas_call(
        functools.partial(_proj_kernel, na=na, nb=nb),
        grid_spec=grid_spec,
        out_shape=[
            jax.ShapeDtypeStruct((m, n_f32), F32),
            jax.ShapeDtypeStruct((m, n_bf16), BF16),
            jax.ShapeDtypeStruct((m, KV_LATENT), BF16),
            jax.ShapeDtypeStruct((m // seq_len, KV_LATENT + ONES_ROWS, seq_len), BF16),
            jax.ShapeDtypeStruct((m, IDX_DIM), BF16),
            jax.ShapeDtypeStruct((m // seq_len, IDX_HEADS, seq_len), F32),
        ],
        compiler_params=_cparams(("arbitrary", "arbitrary", "arbitrary")),
        name="proj",
    )(table, x2, g, w_t, w_t, tail, ckv_g, k_g, k_b)


def _sigmoid(v):
    return 0.5 * jnp.tanh(0.5 * v) + 0.5


def _scan_step(a, b, k, axis, idx):
    keep = idx >= k
    a_prev = jnp.where(keep, pltpu.roll(a, k, axis=axis), 1.0)
    b_prev = jnp.where(keep, pltpu.roll(b, k, axis=axis), 0.0)
    return a * a_prev, a * b_prev + b


def _rglru_kernel(xa_ref, ga_ref, cw_ref, cb_ref, wa_ref, ba_ref, wx_ref, bx_ref, lam_ref,
                  o_ref, pad_s, a_s, b_s, c_s):
    s, w = xa_ref.shape
    tile = 8
    n_tiles = s // tile

    pad_s[0:tile, :] = jnp.zeros((tile, w), F32)
    pad_s[tile:tile + s, :] = xa_ref[...]
    acc = pad_s[tile:tile + s, :] * cw_ref[CONV_WIDTH - 1:CONV_WIDTH, :]
    for j in range(CONV_WIDTH - 1):
        back = CONV_WIDTH - 1 - j
        acc = acc + pad_s[tile - back:tile - back + s, :] * cw_ref[j:j + 1, :]
    xc = cb_ref[...] + acc

    xcb = xc.astype(BF16)
    tr = jnp.tanh(jnp.dot(xcb, wa_ref[...], preferred_element_type=F32) + ba_ref[...])
    ti = jnp.tanh(jnp.dot(xcb, wx_ref[...], preferred_element_type=F32) + bx_ref[...])
    i = 0.5 * ti + 0.5
    z = -lam_ref[...]
    softplus = jnp.maximum(z, 0.0) + jnp.log1p(jnp.exp(-jnp.abs(z)))
    half = (-0.5 * LRU_C) * softplus
    log_a = half * tr + half
    a = jnp.exp(log_a)
    m2 = (1.0 + a * a) * jnp.tanh(-log_a)
    mult = jnp.where(m2 > 0.0, m2 * lax.rsqrt(m2), 0.0)
    gated = i * xc
    b_s[...] = mult * gated
    b_s[0:1, :] = gated[0:1, :]

    a3 = a.reshape(n_tiles, tile, w)
    b3 = b_s[...].reshape(n_tiles, tile, w)
    sub = lax.broadcasted_iota(I32, (n_tiles, tile, w), 1)
    for k in (1, 2, 4):
        a3, b3 = _scan_step(a3, b3, k, 1, sub)
    a_s[...] = a3.reshape(s, w)
    b_s[...] = b3.reshape(s, w)

    at = a_s[pl.ds(tile - 1, n_tiles, stride=tile), :]
    bt = b_s[pl.ds(tile - 1, n_tiles, stride=tile), :]
    trow = lax.broadcasted_iota(I32, (n_tiles, w), 0)
    k = 1
    while k < n_tiles:
        at, bt = _scan_step(at, bt, k, 0, trow)
        k *= 2
    c_s[0:tile, :] = jnp.zeros((tile, w), F32)
    c_s[tile:tile + n_tiles, :] = bt

    for t in range(n_tiles):
        rows = slice(t * tile, (t + 1) * tile)
        before = c_s[pl.ds(tile - 1 + t, tile, stride=0), :]
        h = a_s[rows, :] * before + b_s[rows, :]
        gh = 0.5 * ga_ref[rows, :]
        o_ref[rows, :] = (h * (gh * (jnp.tanh(gh) + 1.0))).astype(o_ref.dtype)


def _rglru(pa3, cols, conv_w, conv_b, wa, ba, wx, bx, lam):
    bsz, s, _ = pa3.shape
    g, w = wa.shape[0], wa.shape[-1]
    xa_blk = _col_block(cols["xa"], w)
    ga_blk = _col_block(cols["ga"], w)
    vec = lambda: pl.BlockSpec((1, w), lambda b, j: (0, j))
    return pl.pallas_call(
        _rglru_kernel,
        grid=(bsz, g),
        in_specs=[
            pl.BlockSpec((None, s, w), lambda b, j: (b, 0, xa_blk + j)),
            pl.BlockSpec((None, s, w), lambda b, j: (b, 0, ga_blk + j)),
            pl.BlockSpec((CONV_WIDTH, w), lambda b, j: (0, j)),
            vec(),
            pl.BlockSpec((None, w, w), lambda b, j: (j, 0, 0)),
            vec(),
            pl.BlockSpec((None, w, w), lambda b, j: (j, 0, 0)),
            vec(),
            vec(),
        ],
        out_specs=pl.BlockSpec((None, s, w), lambda b, j: (b, 0, j)),
        out_shape=jax.ShapeDtypeStruct((bsz, s, g * w), BF16),
        scratch_shapes=[pltpu.VMEM((s + 8, w), F32), pltpu.VMEM((s, w), F32), pltpu.VMEM((s, w), F32),
                        pltpu.VMEM((s // 8 + 8, w), F32)],
        compiler_params=_cparams(("parallel", "parallel")),
        name="rglru",
    )(pa3, pa3, conv_w, conv_b, wa, ba, wx, bx, lam)


def _tree_sum(parts):
    while len(parts) > 1:
        paired = [parts[i] + parts[i + 1] for i in range(0, len(parts) - 1, 2)]
        parts = paired + ([parts[-1]] if len(parts) % 2 else [])
    return parts[0]


def _sortable_to_f32(u):
    key = u ^ INT_MIN
    return lax.bitcast_convert_type(key ^ ((key >> 31) & 0x7FFFFFFF), F32)


def _count_ge(ref, rows, cand, pack):
    chains = 4
    one, zero = jnp.ones((), ref.dtype), jnp.zeros((), ref.dtype)
    accs = [None] * chains
    for r in range(rows // pack):
        hit = jnp.where(ref[r * pack:(r + 1) * pack, :] >= cand, one, zero)
        accs[r % chains] = hit if accs[r % chains] is None else accs[r % chains] + hit
    parts = [a.astype(F32) for a in accs if a is not None]
    return jnp.sum(_tree_sum(parts), axis=0, keepdims=True)


def _kth_largest(score_ref, score16_ref, rows, k):
    def step16(i, u):
        cand = u | (jnp.int32(1) << (15 - i))
        cand_f = _sortable_to_f32(cand << 16).astype(BF16)
        return jnp.where(_count_ge(score16_ref, rows, cand_f, 16) >= k, cand, u)

    hi = lax.fori_loop(0, 16, step16, jnp.zeros((1, Q_TILE), I32))
    at_hi = _count_ge(score_ref, rows, _sortable_to_f32(hi << 16), 8)
    keeps = at_hi >= k
    hi = jnp.where(keeps, hi, jnp.maximum(hi - 1, 0))

    per_trip = 4

    def trip(state):
        i, u, at_u, _ = state
        for t in range(per_trip):
            cand = u | (jnp.int32(1) << (15 - (i + t)))
            cnt = _count_ge(score_ref, rows, _sortable_to_f32(cand), 8)
            take = cnt >= k
            u = jnp.where(take, cand, u)
            at_u = jnp.where(take, cnt, at_u)
        open_lanes = jnp.sum((at_u != k).astype(I32))
        return i + per_trip, u, at_u, open_lanes

    unknown = jnp.full((1, Q_TILE), -1.0, F32)
    state = (jnp.int32(0), hi << 16, jnp.where(keeps, at_hi, unknown), jnp.int32(1))
    _, u, _, open_lanes = lax.while_loop(lambda st: (st[0] < 16) & (st[3] > 0), trip, state)
    return _sortable_to_f32(u), open_lanes


def _drop_extra_ties(score_ref, n_groups, thr, k):
    def count_above(g, acc):
        x = score_ref[pl.ds(pl.multiple_of(g * 8, 8), 8), :]
        return acc + jnp.where(x > thr, 1.0, 0.0)

    above = lax.fori_loop(0, n_groups, count_above, jnp.zeros((8, Q_TILE), F32))
    need = k - jnp.sum(above, axis=0, keepdims=True)
    sub = lax.broadcasted_iota(I32, (8, Q_TILE), 0)

    def demote(g, seen):
        r0 = pl.multiple_of(g * 8, 8)
        x = score_ref[pl.ds(r0, 8), :]
        tie = jnp.where(x == thr, 1.0, 0.0)
        upto = tie
        for k8 in (1, 2, 4):
            upto = upto + jnp.where(sub >= k8, pltpu.roll(upto, k8, axis=0), 0.0)
        before = seen + upto - tie
        score_ref[pl.ds(r0, 8), :] = jnp.where((tie > 0.0) & (before >= need), -jnp.inf, x)
        return seen + upto[7:8, :]

    lax.fori_loop(0, n_groups, demote, jnp.zeros((1, Q_TILE), F32))


def _loop_in_trips(n, body, init):
    def trips(start, count, width, carry):
        def group(i, c):
            for t in range(width):
                c = body(start + width * i + t, c)
            return c
        return lax.fori_loop(0, count, group, carry)

    carry = trips(0, n // 4, 4, init)
    carry = trips(4 * (n // 4), (n % 4) // 2, 2, carry)
    return trips(2 * (n // 2), n % 2, 1, carry)


def _dsa_kernel(q_ref, qi_ref, wukt_ref, wit_ref, kn_ref, c_ref, ct_ref, gb_ref, bias_ref, wuvt_ref,
                o_ref, qat_ref, qit_ref, score_s, score16_s, thr_s, open_s, lg_s, acc_s, topk):
    qb = pl.program_id(1)

    qt = q_ref[...].T
    scale = HEAD_DIM ** -0.5 * LOG2E
    for h in range(ATT_HEADS):
        qa = jnp.dot(wukt_ref[h], qt[h * HEAD_DIM:(h + 1) * HEAD_DIM], preferred_element_type=F32)
        qat_ref[h] = (qa * scale).astype(BF16)
    qit = qi_ref[...].T
    for h in range(IDX_HEADS):
        qit_ref[h] = qit[h * IDX_DIM:(h + 1) * IDX_DIM, :]
    q_tiles = Q_TILE // K_CHUNK
    nkc = (qb + 1) * q_tiles
    nac = (nkc * K_CHUNK + ATT_CHUNK - 1) // ATT_CHUNK
    tiles = ATT_CHUNK // K_CHUNK

    kiota = lax.broadcasted_iota(I32, (K_CHUNK, Q_TILE), 0)
    qpos = qb * Q_TILE + lax.broadcasted_iota(I32, (K_CHUNK, Q_TILE), 1)

    def score_chunk(ac, carry):
        for t in range(tiles):
            k0 = pl.multiple_of(ac * ATT_CHUNK + t * K_CHUNK, K_CHUNK)
            kn = kn_ref[pl.ds(k0, K_CHUNK), :]
            acc = jnp.zeros((K_CHUNK, Q_TILE), F32)
            for h in range(IDX_HEADS):
                sc = jnp.dot(kn, qit_ref[h], preferred_element_type=F32)
                acc = acc + jnp.maximum(sc, 0.0) * wit_ref[h:h + 1, :]
            masked = jnp.where(kiota + k0 <= qpos, acc, -jnp.inf)
            score_s[pl.ds(k0, K_CHUNK), :] = masked
            score16_s[pl.ds(k0, K_CHUNK), :] = masked.astype(BF16)
        return carry

    _loop_in_trips(nac, score_chunk, 0)

    for v in range(1, score_s.shape[0] // ATT_CHUNK + 1):
        @pl.when(nac == v)
        def _(rows=v * ATT_CHUNK):
            if rows <= topk:
                thr, open_lanes = jnp.full((1, Q_TILE), NEG, F32), jnp.int32(0)
            else:
                thr, open_lanes = _kth_largest(score_s, score16_s, rows, float(topk))
                thr = jnp.where(thr >= NEG, thr, NEG)
            thr_s[...] = jnp.broadcast_to(thr, thr_s.shape)
            open_s[0] = open_lanes

    thr = thr_s[0:1, :]

    @pl.when(open_s[0] > 0)
    def _():
        _drop_extra_ties(score_s, nac * (ATT_CHUNK // 8), thr, float(topk))

    def logit_chunk(ac, m8s):
        r0 = pl.multiple_of(ac * ATT_CHUNK, ATT_CHUNK)
        c_chunk = c_ref[pl.ds(r0, ATT_CHUNK), :]
        mbias = jnp.where(score_s[pl.ds(r0, ATT_CHUNK), :] >= thr, 0.0, NEG)
        near = [[jnp.clip(ac * tiles + t - (qb * q_tiles + j) + 2, 0, 2) for j in range(q_tiles)]
                for t in range(tiles)]
        out = []
        for h in range(ATT_HEADS):
            lg = jnp.dot(c_chunk, qat_ref[h], preferred_element_type=F32) + mbias
            lg = jnp.concatenate(
                [lg[t * K_CHUNK:(t + 1) * K_CHUNK]
                 + jnp.concatenate([bias_ref[h, near[t][j]] for j in range(q_tiles)], axis=1)
                 for t in range(tiles)], axis=0)
            lg_s[h, pl.ds(r0, ATT_CHUNK), :] = lg
            out.append(jnp.maximum(m8s[h], jnp.max(lg.reshape(ATT_CHUNK // 8, 8, Q_TILE), axis=0)))
        return tuple(out)

    m8s = _loop_in_trips(nac, logit_chunk, tuple(jnp.full((8, Q_TILE), NEG, F32) for _ in range(ATT_HEADS)))
    ms = [jnp.max(m8, axis=0, keepdims=True) for m8 in m8s]

    acc_s[...] = jnp.zeros(acc_s.shape, F32)

    def pv_chunk(ac, carry):
        r0 = pl.multiple_of(ac * ATT_CHUNK, ATT_CHUNK)
        ct_chunk = ct_ref[:, pl.ds(r0, ATT_CHUNK)]
        for h in range(ATT_HEADS):
            pr = jnp.exp2(lg_s[h, pl.ds(r0, ATT_CHUNK), :] - ms[h])
            acc_s[h] += jnp.dot(ct_chunk, pr.astype(BF16), preferred_element_type=F32)
        return carry

    _loop_in_trips(nac, pv_chunk, 0)

    for h in range(ATT_HEADS):
        denom = acc_s[h, KV_LATENT:KV_LATENT + 1, :]
        o_t = acc_s[h, 0:KV_LATENT, :] * (1.0 / denom)
        y_t = jnp.dot(wuvt_ref[h], o_t.astype(BF16), preferred_element_type=F32)
        gb = gb_ref[:, h * HEAD_DIM:(h + 1) * HEAD_DIM]
        o_ref[:, h * HEAD_DIM:(h + 1) * HEAD_DIM] = (y_t.T * (gb * _sigmoid(gb))).astype(o_ref.dtype)


def _dsa(pb3, cols_b, w_ukt, wit, kn, c, ct, pa3, gb_blk, bias_tiles, wuvt, topk):
    bsz, s, _ = c.shape
    att_w = ATT_HEADS * HEAD_DIM
    idx_w = IDX_HEADS * IDX_DIM
    assert s % ATT_CHUNK == 0 and s % Q_TILE == 0 and Q_TILE % K_CHUNK == 0 and ATT_CHUNK % K_CHUNK == 0
    assert s // 16 // 4 < 256
    q_blk = _col_block(cols_b["q"], att_w)
    qi_blk = _col_block(cols_b["qi"], idx_w)
    const = lambda shape: pl.BlockSpec(shape, lambda b, i: (0,) * len(shape))
    return pl.pallas_call(
        functools.partial(_dsa_kernel, topk=topk),
        grid=(bsz, s // Q_TILE),
        in_specs=[
            pl.BlockSpec((None, Q_TILE, att_w), lambda b, i: (b, i, q_blk)),
            pl.BlockSpec((None, Q_TILE, idx_w), lambda b, i: (b, i, qi_blk)),
            const(w_ukt.shape),
            pl.BlockSpec((None, IDX_HEADS, Q_TILE), lambda b, i: (b, 0, i)),
            pl.BlockSpec((None, s, IDX_DIM), lambda b, i: (b, 0, 0)),
            pl.BlockSpec((None, s, KV_LATENT), lambda b, i: (b, 0, 0)),
            pl.BlockSpec((None, KV_LATENT + ONES_ROWS, s), lambda b, i: (b, 0, 0)),
            pl.BlockSpec((None, Q_TILE, att_w), lambda b, i: (b, i, gb_blk)),
            const(bias_tiles.shape),
            const(wuvt.shape),
        ],
        out_specs=pl.BlockSpec((None, Q_TILE, att_w), lambda b, i: (b, i, 0)),
        out_shape=jax.ShapeDtypeStruct((bsz, s, att_w), BF16),
        scratch_shapes=[
            pltpu.VMEM((ATT_HEADS, KV_LATENT, Q_TILE), BF16),
            pltpu.VMEM((IDX_HEADS, IDX_DIM, Q_TILE), BF16),
            pltpu.VMEM((s, Q_TILE), F32),
            pltpu.VMEM((s, Q_TILE), BF16),
            pltpu.VMEM((8, Q_TILE), F32),
            pltpu.SMEM((1,), I32),
            pltpu.VMEM((ATT_HEADS, s, Q_TILE), F32),
            pltpu.VMEM((ATT_HEADS, KV_LATENT + ONES_ROWS, Q_TILE), F32),
        ],
        compiler_params=_cparams(("parallel", "arbitrary")),
        name="dsa",
    )(pb3, pb3, w_ukt, wit, kn, c, ct, pa3, bias_tiles, wuvt)


def _outp_kernel(ya_ref, yb_ref, w_ref, x_ref, g_ref, o_ref, wbf_s, *, final_norm):
    @pl.when(pl.program_id(0) == 0)
    def _():
        wbf_s[...] = w_ref[...].astype(BF16)

    ka = ya_ref.shape[1]
    acc = jnp.dot(ya_ref[...], wbf_s[0:ka, :], preferred_element_type=F32)
    acc = acc + jnp.dot(yb_ref[...], wbf_s[ka:, :], preferred_element_type=F32)
    x = x_ref[...] + acc
    if final_norm:
        x = x * lax.rsqrt(jnp.mean(x * x, axis=-1, keepdims=True) + EPS) * g_ref[...]
    o_ref[...] = x


def _outp(ya, yb, w_out, x2, g, final_norm, tm=512):
    m, d = x2.shape
    ka, kb = ya.shape[1], yb.shape[1]
    assert w_out.shape == (ka + kb, d)
    return pl.pallas_call(
        functools.partial(_outp_kernel, final_norm=final_norm),
        grid=(m // tm,),
        in_specs=[
            pl.BlockSpec((tm, ka), lambda i: (i, 0)),
            pl.BlockSpec((tm, kb), lambda i: (i, 0)),
            pl.BlockSpec((ka + kb, d), lambda i: (0, 0), pipeline_mode=pl.Buffered(1)),
            pl.BlockSpec((tm, d), lambda i: (i, 0)),
            pl.BlockSpec((1, d), lambda i: (0, 0)),
        ],
        out_specs=pl.BlockSpec((tm, d), lambda i: (i, 0)),
        out_shape=jax.ShapeDtypeStruct((m, d), F32),
        scratch_shapes=[pltpu.VMEM((ka + kb, d), BF16)],
        compiler_params=_cparams(("arbitrary",)),
        name="outp",
    )(ya, yb, w_out, x2, g)


def _t5_bucket(dist):
    n = jnp.maximum(dist, 0)
    max_exact = REL_BUCKETS // 2
    nf = jnp.maximum(n, 1).astype(F32)
    large = max_exact + (jnp.log(nf / max_exact) / np.log(REL_MAX_DIST / max_exact)
                         * (REL_BUCKETS - max_exact)).astype(I32)
    large = jnp.minimum(large, REL_BUCKETS - 1)
    return jnp.where(n < max_exact, n, large)


def _bias_tiles(rel_bias):
    qw = K_CHUNK
    span = K_CHUNK + qw
    table = rel_bias[_t5_bucket(jnp.arange(span + 1, dtype=I32))].astype(F32)
    table = ((table[:span] - table[span:]) * LOG2E).T
    n = span + qw - 1
    a = jnp.concatenate([jnp.zeros((ATT_HEADS, qw - 1), F32), table], axis=1)
    shifted = jnp.tile(a, (1, span + 1))[:, :span * (n + 1)].reshape(ATT_HEADS, span, n + 1)
    tiles = shifted[:, ::-1, :qw].reshape(ATT_HEADS, 2, K_CHUNK, qw)
    return jnp.concatenate([jnp.zeros_like(tiles[:, :1]), tiles], axis=1)


def kernel(x, norm_g, w_in, conv_w, conv_b, lru_wa, lru_ba, lru_wx, lru_bx, lru_lambda, ckv_norm_g, idx_k_norm_g, idx_k_norm_b, w_uk, w_uv, w_out, rel_bias, final_norm_g):
    bsz, s, d = x.shape
    depth = w_in.shape[0]
    lru_w = lru_wa.shape[1] * lru_wa.shape[2]
    att_w = ATT_HEADS * HEAD_DIM
    idx_w = IDX_HEADS * IDX_DIM
    assert REL_MAX_DIST <= K_CHUNK
    assert lru_w == att_w == idx_w and att_w % KV_LATENT == 0
    topk = min(INDEX_TOPK, s // 4)

    o_q = 2 * lru_w
    o_ckv = o_q + att_w
    o_gb = o_ckv + KV_LATENT
    o_qi = o_gb + att_w
    o_ki = o_qi + idx_w
    tn = 512
    cols_a = {"xa": 0, "ga": lru_w, "gb": 2 * lru_w}
    cols_b = {"q": 0, "qi": att_w}
    n_f32, n_bf16 = 3 * lru_w, att_w + idx_w

    bias_tiles = _bias_tiles(rel_bias)
    x2 = x.reshape(bsz * s, d)
    for l in range(depth):
        order = [(0, o_q), (o_gb, o_qi), (o_q, o_ckv), (o_qi, o_ki), (o_ckv, o_gb), (o_ki, w_in.shape[2])]
        pa, pb, c, ct, kn, wit = _proj(x2, norm_g[l][None, :], w_in[l].T, order, ckv_norm_g[l][None, :],
                                       idx_k_norm_g[l][None, :], idx_k_norm_b[l][None, :], n_f32, n_bf16, s, tn=tn)
        pa3 = pa.reshape(bsz, s, -1)
        pb3 = pb.reshape(bsz, s, -1)

        ya = _rglru(pa3, cols_a, conv_w[l], conv_b[l][None, :], (0.5 * lru_wa[l]).astype(BF16),
                    0.5 * lru_ba[l][None, :], (0.5 * lru_wx[l]).astype(BF16), 0.5 * lru_bx[l][None, :],
                    lru_lambda[l][None, :])

        wukt = jnp.transpose(w_uk[l], (0, 2, 1)).astype(BF16)
        wuvt = jnp.transpose(w_uv[l], (0, 2, 1)).astype(BF16)
        yb = _dsa(pb3, cols_b, wukt, wit, kn.reshape(bsz, s, -1), c.reshape(bsz, s, -1), ct, pa3,
                  _col_block(cols_a["gb"], att_w), bias_tiles, wuvt, topk)

        x2 = _outp(ya.reshape(bsz * s, lru_w), yb.reshape(bsz * s, att_w), w_out[l], x2,
                   final_norm_g[None, :], final_norm=(l == depth - 1))
    return x2.reshape(bsz, s, d)
```

```python
import functools

import numpy as np
import jax
import jax.numpy as jnp
from jax import lax
from jax.experimental import pallas as pl
from jax.experimental.pallas import tpu as pltpu

F32 = jnp.float32
BF16 = jnp.bfloat16
I32 = jnp.int32

CONV_WIDTH = 4
LRU_C = 8.0
ATT_HEADS = 8
HEAD_DIM = 128
KV_LATENT = 256
IDX_HEADS = 16
IDX_DIM = 64
INDEX_TOPK = 256
REL_BUCKETS = 32
REL_MAX_DIST = 128
EPS = 1e-6
LOG2E = float(np.log2(np.e))
ONES_ROWS = 16

Q_TILE = 256
K_CHUNK = 128
ATT_CHUNK = 256
NEG = float(np.finfo(np.float32).min)
INT_MIN = -(2 ** 31)
VMEM_LIMIT = 56 * 1024 * 1024


def _cparams(sem):
    return pltpu.CompilerParams(dimension_semantics=sem, vmem_limit_bytes=VMEM_LIMIT)


def _col_block(offset, width):
    assert offset % width == 0
    return offset // width


def _proj_kernel(t_ref, x_ref, g_ref, wlo_ref, whi_ref, tail_ref, cg_ref, kg_ref, kb_ref,
                 oa_ref, ob_ref, c_ref, ct_ref, kn_ref, wit_ref, h_ref, *, na, nb):
    j = pl.program_id(1)
    r = pl.program_id(2)
    nt = (((1,), (1,)), ((), ()))

    def w_tile():
        lo = jnp.where(t_ref[2 * j] < 0, tail_ref[...], wlo_ref[...])
        hi = jnp.where(t_ref[2 * j + 1] < 0, tail_ref[...], whi_ref[...])
        return jnp.concatenate([lo, hi], axis=0).astype(BF16)

    @pl.when(j == 0)
    def _():
        x = x_ref[...]
        y = x * lax.rsqrt(jnp.mean(x * x, axis=-1, keepdims=True) + EPS)
        h = (y * g_ref[...]).astype(BF16)
        h_ref[r] = h
        oa_ref[...] = lax.dot_general(h, w_tile(), nt, preferred_element_type=F32)

    @pl.when((j > 0) & (j < na))
    def _():
        oa_ref[...] = lax.dot_general(h_ref[r], w_tile(), nt, preferred_element_type=F32)

    @pl.when((j >= na) & (j < na + nb))
    def _():
        ob_ref[...] = lax.dot_general(h_ref[r], w_tile(), nt, preferred_element_type=F32).astype(BF16)

    @pl.when(j >= na + nb)
    def _():
        used = KV_LATENT + 128
        tail = lax.dot_general(h_ref[r], w_tile()[:used], nt, preferred_element_type=F32)
        ckv = tail[:, :KV_LATENT]
        c = ckv * lax.rsqrt(jnp.mean(ckv * ckv, axis=-1, keepdims=True) + EPS) * cg_ref[...]
        c_ref[...] = c.astype(BF16)
        ct_ref[0:KV_LATENT, :] = c.T.astype(BF16)
        ct_ref[KV_LATENT:, :] = jnp.ones((ONES_ROWS, ct_ref.shape[1]), BF16)
        sm = tail[:, KV_LATENT:KV_LATENT + 128]
        ki = sm[:, :IDX_DIM]
        mu = jnp.mean(ki, axis=-1, keepdims=True)
        var = jnp.mean(jnp.square(ki - mu), axis=-1, keepdims=True)
        kn = (ki - mu) * lax.rsqrt(var + EPS) * kg_ref[...] + kb_ref[...]
        kn_ref[...] = kn.astype(BF16)
        wit_ref[...] = sm.T[IDX_DIM:IDX_DIM + IDX_HEADS, :] * (IDX_HEADS ** -0.5 * IDX_DIM ** -0.5)


def _proj(x2, g, w_t, order, ckv_g, k_g, k_b, n_f32, n_bf16, seq_len, tm=1024, tn=512, group=2):
    m, d = x2.shape
    n = w_t.shape[0]
    th = tn // 2
    n_whole = n // th
    src = []
    for start, stop in order:
        assert start % th == 0 and (stop % th == 0 or stop == n)
        src += list(range(start // th, -(-stop // th)))
    na, nb = n_f32 // tn, n_bf16 // tn
    assert n_f32 % tn == 0 and n_bf16 % tn == 0 and len(src) * th == n_f32 + n_bf16 + tn
    tail = jnp.pad(w_t[n_whole * th:], ((0, (n_whole + 1) * th - n), (0, 0)))
    table = jnp.asarray([blk if blk < n_whole else -1 for blk in src], I32)
    tps = seq_len // tm
    assert seq_len % tm == 0 and tn >= KV_LATENT + 128
    assert (m // tm) % group == 0
    nj = na + nb + 1
    const = lambda shape: pl.BlockSpec(shape, lambda i, j, r, t: (0,) * len(shape))
    last = group - 1
    row = lambda i, r_eff: i * group + r_eff
    per_seq = lambda g: (g // tps, 0, g % tps)
    tail_row = lambda i, j, r: row(i, jnp.where(j < nj - 1, 0, r))
    n_groups = m // tm // group

    def rest(i, last_col):
        nxt = i + 1 < n_groups
        return jnp.where(nxt, row(i + 1, 0), row(i, last)), jnp.where(nxt, 0, last_col)

    def oa_index(i, j, r, t):
        rest_row, rest_col = rest(i, na - 1)
        return jnp.where(j < na, row(i, r), rest_row), jnp.where(j < na, j, rest_col)

    def ob_index(i, j, r, t):
        rest_row, rest_col = rest(i, nb - 1)
        own = (j >= na) & (j < na + nb)
        return (jnp.where(j < na, row(i, 0), jnp.where(own, row(i, r), rest_row)),
                jnp.where(j < na, 0, jnp.where(own, j - na, rest_col)))

    grid_spec = pltpu.PrefetchScalarGridSpec(
        num_scalar_prefetch=1,
        grid=(m // tm // group, nj, group),
        in_specs=[
            pl.BlockSpec((tm, d), lambda i, j, r, t: (row(i, jnp.where(j == 0, r, last)), 0)),
            const((1, d)),
            pl.BlockSpec((th, d), lambda i, j, r, t: (jnp.maximum(t[2 * j], 0), 0)),
            pl.BlockSpec((th, d), lambda i, j, r, t: (jnp.maximum(t[2 * j + 1], 0), 0)),
            const((th, d)),
            const((1, KV_LATENT)),
            const((1, IDX_DIM)),
            const((1, IDX_DIM)),
        ],
        out_specs=[
            pl.BlockSpec((tm, tn), oa_index),
            pl.BlockSpec((tm, tn), ob_index),
            pl.BlockSpec((tm, KV_LATENT), lambda i, j, r, t: (tail_row(i, j, r), 0)),
            pl.BlockSpec((None, KV_LATENT + ONES_ROWS, tm), lambda i, j, r, t: per_seq(tail_row(i, j, r))),
            pl.BlockSpec((tm, IDX_DIM), lambda i, j, r, t: (tail_row(i, j, r), 0)),
            pl.BlockSpec((None, IDX_HEADS, tm), lambda i, j, r, t: per_seq(tail_row(i, j, r))),
        ],
        scratch_shapes=[pltpu.VMEM((group, tm, d), BF16)],
    )
    return pl.pallas_call(
        functools.partial(_proj_kernel, na=na, nb=nb),
        grid_spec=grid_spec,
        out_shape=[
            jax.ShapeDtypeStruct((m, n_f32), F32),
            jax.ShapeDtypeStruct((m, n_bf16), BF16),
            jax.ShapeDtypeStruct((m, KV_LATENT), BF16),
            jax.ShapeDtypeStruct((m // seq_len, KV_LATENT + ONES_ROWS, seq_len), BF16),
            jax.ShapeDtypeStruct((m, IDX_DIM), BF16),
            jax.ShapeDtypeStruct((m // seq_len, IDX_HEADS, seq_len), F32),
        ],
        compiler_params=_cparams(("arbitrary", "arbitrary", "arbitrary")),
        name="proj",
    )(table, x2, g, w_t, w_t, tail, ckv_g, k_g, k_b)


def _sigmoid(v):
    return 0.5 * jnp.tanh(0.5 * v) + 0.5


def _scan_step(a, b, k, axis, idx):
    keep = idx >= k
    a_prev = jnp.where(keep, pltpu.roll(a, k, axis=axis), 1.0)
    b_prev = jnp.where(keep, pltpu.roll(b, k, axis=axis), 0.0)
    return a * a_prev, a * b_prev + b


def _rglru_kernel(xa_ref, ga_ref, cw_ref, cb_ref, wa_ref, ba_ref, wx_ref, bx_ref, lam_ref,
                  o_ref, pad_s, a_s, b_s, c_s):
    s, w = xa_ref.shape
    tile = 8
    n_tiles = s // tile

    pad_s[0:tile, :] = jnp.zeros((tile, w), F32)
    pad_s[tile:tile + s, :] = xa_ref[...]
    acc = pad_s[tile:tile + s, :] * cw_ref[CONV_WIDTH - 1:CONV_WIDTH, :]
    for j in range(CONV_WIDTH - 1):
        back = CONV_WIDTH - 1 - j
        acc = acc + pad_s[tile - back:tile - back + s, :] * cw_ref[j:j + 1, :]
    xc = cb_ref[...] + acc

    xcb = xc.astype(BF16)
    tr = jnp.tanh(jnp.dot(xcb, wa_ref[...], preferred_element_type=F32) + ba_ref[...])
    ti = jnp.tanh(jnp.dot(xcb, wx_ref[...], preferred_element_type=F32) + bx_ref[...])
    i = 0.5 * ti + 0.5
    z = -lam_ref[...]
    softplus = jnp.maximum(z, 0.0) + jnp.log1p(jnp.exp(-jnp.abs(z)))
    half = (-0.5 * LRU_C) * softplus
    log_a = half * tr + half
    a = jnp.exp(log_a)
    m2 = (1.0 + a * a) * jnp.tanh(-log_a)
    mult = jnp.where(m2 > 0.0, m2 * lax.rsqrt(m2), 0.0)
    gated = i * xc
    b_s[...] = mult * gated
    b_s[0:1, :] = gated[0:1, :]

    a3 = a.reshape(n_tiles, tile, w)
    b3 = b_s[...].reshape(n_tiles, tile, w)
    sub = lax.broadcasted_iota(I32, (n_tiles, tile, w), 1)
    for k in (1, 2, 4):
        a3, b3 = _scan_step(a3, b3, k, 1, sub)
    a_s[...] = a3.reshape(s, w)
    b_s[...] = b3.reshape(s, w)

    at = a_s[pl.ds(tile - 1, n_tiles, stride=tile), :]
    bt = b_s[pl.ds(tile - 1, n_tiles, stride=tile), :]
    trow = lax.broadcasted_iota(I32, (n_tiles, w), 0)
    k = 1
    while k < n_tiles:
        at, bt = _scan_step(at, bt, k, 0, trow)
        k *= 2
    c_s[0:tile, :] = jnp.zeros((tile, w), F32)
    c_s[tile:tile + n_tiles, :] = bt

    for t in range(n_tiles):
        rows = slice(t * tile, (t + 1) * tile)
        before = c_s[pl.ds(tile - 1 + t, tile, stride=0), :]
        h = a_s[rows, :] * before + b_s[rows, :]
        gh = 0.5 * ga_ref[rows, :]
        o_ref[rows, :] = (h * (gh * (jnp.tanh(gh) + 1.0))).astype(o_ref.dtype)


def _rglru(pa3, cols, conv_w, conv_b, wa, ba, wx, bx, lam):
    bsz, s, _ = pa3.shape
    g, w = wa.shape[0], wa.shape[-1]
    xa_blk = _col_block(cols["xa"], w)
    ga_blk = _col_block(cols["ga"], w)
    vec = lambda: pl.BlockSpec((1, w), lambda b, j: (0, j))
    return pl.pallas_call(
        _rglru_kernel,
        grid=(bsz, g),
        in_specs=[
            pl.BlockSpec((None, s, w), lambda b, j: (b, 0, xa_blk + j)),
            pl.BlockSpec((None, s, w), lambda b, j: (b, 0, ga_blk + j)),
            pl.BlockSpec((CONV_WIDTH, w), lambda b, j: (0, j)),
            vec(),
            pl.BlockSpec((None, w, w), lambda b, j: (j, 0, 0)),
            vec(),
            pl.BlockSpec((None, w, w), lambda b, j: (j, 0, 0)),
            vec(),
            vec(),
        ],
        out_specs=pl.BlockSpec((None, s, w), lambda b, j: (b, 0, j)),
        out_shape=jax.ShapeDtypeStruct((bsz, s, g * w), BF16),
        scratch_shapes=[pltpu.VMEM((s + 8, w), F32), pltpu.VMEM((s, w), F32), pltpu.VMEM((s, w), F32),
                        pltpu.VMEM((s // 8 + 8, w), F32)],
        compiler_params=_cparams(("parallel", "parallel")),
        name="rglru",
    )(pa3, pa3, conv_w, conv_b, wa, ba, wx, bx, lam)


def _tree_sum(parts):
    while len(parts) > 1:
        paired = [parts[i] + parts[i + 1] for i in range(0, len(parts) - 1, 2)]
        parts = paired + ([parts[-1]] if len(parts) % 2 else [])
    return parts[0]


def _sortable_to_f32(u):
    key = u ^ INT_MIN
    return lax.bitcast_convert_type(key ^ ((key >> 31) & 0x7FFFFFFF), F32)


def _count_ge(ref, rows, cand, pack):
    chains = 4
    one, zero = jnp.ones((), ref.dtype), jnp.zeros((), ref.dtype)
    accs = [None] * chains
    for r in range(rows // pack):
        hit = jnp.where(ref[r * pack:(r + 1) * pack, :] >= cand, one, zero)
        accs[r % chains] = hit if accs[r % chains] is None else accs[r % chains] + hit
    parts = [a.astype(F32) for a in accs if a is not None]
    return jnp.sum(_tree_sum(parts), axis=0, keepdims=True)


def _kth_largest(score_ref, score16_ref, rows, k):
    def step16(i, u):
        cand = u | (jnp.int32(1) << (15 - i))
        cand_f = _sortable_to_f32(cand << 16).astype(BF16)
        return jnp.where(_count_ge(score16_ref, rows, cand_f, 16) >= k, cand, u)

    hi = lax.fori_loop(0, 16, step16, jnp.zeros((1, Q_TILE), I32))
    at_hi = _count_ge(score_ref, rows, _sortable_to_f32(hi << 16), 8)
    keeps = at_hi >= k
    hi = jnp.where(keeps, hi, jnp.maximum(hi - 1, 0))

    per_trip = 4

    def trip(state):
        i, u, at_u, _ = state
        for t in range(per_trip):
            cand = u | (jnp.int32(1) << (15 - (i + t)))
            cnt = _count_ge(score_ref, rows, _sortable_to_f32(cand), 8)
            take = cnt >= k
            u = jnp.where(take, cand, u)
            at_u = jnp.where(take, cnt, at_u)
        open_lanes = jnp.sum((at_u != k).astype(I32))
        return i + per_trip, u, at_u, open_lanes

    unknown = jnp.full((1, Q_TILE), -1.0, F32)
    state = (jnp.int32(0), hi << 16, jnp.where(keeps, at_hi, unknown), jnp.int32(1))
    _, u, _, open_lanes = lax.while_loop(lambda st: (st[0] < 16) & (st[3] > 0), trip, state)
    return _sortable_to_f32(u), open_lanes


def _drop_extra_ties(score_ref, n_groups, thr, k):
    def count_above(g, acc):
        x = score_ref[pl.ds(pl.multiple_of(g * 8, 8), 8), :]
        return acc + jnp.where(x > thr, 1.0, 0.0)

    above = lax.fori_loop(0, n_groups, count_above, jnp.zeros((8, Q_TILE), F32))
    need = k - jnp.sum(above, axis=0, keepdims=True)
    sub = lax.broadcasted_iota(I32, (8, Q_TILE), 0)

    def demote(g, seen):
        r0 = pl.multiple_of(g * 8, 8)
        x = score_ref[pl.ds(r0, 8), :]
        tie = jnp.where(x == thr, 1.0, 0.0)
        upto = tie
        for k8 in (1, 2, 4):
            upto = upto + jnp.where(sub >= k8, pltpu.roll(upto, k8, axis=0), 0.0)
        before = seen + upto - tie
        score_ref[pl.ds(r0, 8), :] = jnp.where((tie > 0.0) & (before >= need), -jnp.inf, x)
        return seen + upto[7:8, :]

    lax.fori_loop(0, n_groups, demote, jnp.zeros((1, Q_TILE), F32))


def _loop_in_trips(n, body, init):
    def trips(start, count, width, carry):
        def group(i, c):
            for t in range(width):
                c = body(start + width * i + t, c)
            return c
        return lax.fori_loop(0, count, group, carry)

    carry = trips(0, n // 4, 4, init)
    carry = trips(4 * (n // 4), (n % 4) // 2, 2, carry)
    return trips(2 * (n // 2), n % 2, 1, carry)


def _dsa_kernel(q_ref, qi_ref, wukt_ref, wit_ref, kn_ref, c_ref, ct_ref, gb_ref, bias_ref, wuvt_ref,
                o_ref, qat_ref, qit_ref, score_s, score16_s, thr_s, open_s, lg_s, acc_s, topk):
    qb = pl.program_id(1)

    qt = q_ref[...].T
    scale = HEAD_DIM ** -0.5 * LOG2E
    for h in range(ATT_HEADS):
        qa = jnp.dot(wukt_ref[h], qt[h * HEAD_DIM:(h + 1) * HEAD_DIM], preferred_element_type=F32)
        qat_ref[h] = (qa * scale).astype(BF16)
    qit = qi_ref[...].T
    for h in range(IDX_HEADS):
        qit_ref[h] = qit[h * IDX_DIM:(h + 1) * IDX_DIM, :]
    q_tiles = Q_TILE // K_CHUNK
    nkc = (qb + 1) * q_tiles
    nac = (nkc * K_CHUNK + ATT_CHUNK - 1) // ATT_CHUNK
    tiles = ATT_CHUNK // K_CHUNK

    kiota = lax.broadcasted_iota(I32, (K_CHUNK, Q_TILE), 0)
    qpos = qb * Q_TILE + lax.broadcasted_iota(I32, (K_CHUNK, Q_TILE), 1)

    def score_chunk(ac, carry):
        for t in range(tiles):
            k0 = pl.multiple_of(ac * ATT_CHUNK + t * K_CHUNK, K_CHUNK)
            kn = kn_ref[pl.ds(k0, K_CHUNK), :]
            acc = jnp.zeros((K_CHUNK, Q_TILE), F32)
            for h in range(IDX_HEADS):
                sc = jnp.dot(kn, qit_ref[h], preferred_element_type=F32)
                acc = acc + jnp.maximum(sc, 0.0) * wit_ref[h:h + 1, :]
            masked = jnp.where(kiota + k0 <= qpos, acc, -jnp.inf)
            score_s[pl.ds(k0, K_CHUNK), :] = masked
            score16_s[pl.ds(k0, K_CHUNK), :] = masked.astype(BF16)
        return carry

    _loop_in_trips(nac, score_chunk, 0)

    for v in range(1, score_s.shape[0] // ATT_CHUNK + 1):
        @pl.when(nac == v)
        def _(rows=v * ATT_CHUNK):
            if rows <= topk:
                thr, open_lanes = jnp.full((1, Q_TILE), NEG, F32), jnp.int32(0)
            else:
                thr, open_lanes = _kth_largest(score_s, score16_s, rows, float(topk))
                thr = jnp.where(thr >= NEG, thr, NEG)
            thr_s[...] = jnp.broadcast_to(thr, thr_s.shape)
            open_s[0] = open_lanes

    thr = thr_s[0:1, :]

    @pl.when(open_s[0] > 0)
    def _():
        _drop_extra_ties(score_s, nac * (ATT_CHUNK // 8), thr, float(topk))

    def logit_chunk(ac, m8s):
        r0 = pl.multiple_of(ac * ATT_CHUNK, ATT_CHUNK)
        c_chunk = c_ref[pl.ds(r0, ATT_CHUNK), :]
        mbias = jnp.where(score_s[pl.ds(r0, ATT_CHUNK), :] >= thr, 0.0, NEG)
        near = [[jnp.clip(ac * tiles + t - (qb * q_tiles + j) + 2, 0, 2) for j in range(q_tiles)]
                for t in range(tiles)]
        out = []
        for h in range(ATT_HEADS):
            lg = jnp.dot(c_chunk, qat_ref[h], preferred_element_type=F32) + mbias
            lg = jnp.concatenate(
                [lg[t * K_CHUNK:(t + 1) * K_CHUNK]
                 + jnp.concatenate([bias_ref[h, near[t][j]] for j in range(q_tiles)], axis=1)
                 for t in range(tiles)], axis=0)
            lg_s[h, pl.ds(r0, ATT_CHUNK), :] = lg
            out.append(jnp.maximum(m8s[h], jnp.max(lg.reshape(ATT_CHUNK // 8, 8, Q_TILE), axis=0)))
        return tuple(out)

    m8s = _loop_in_trips(nac, logit_chunk, tuple(jnp.full((8, Q_TILE), NEG, F32) for _ in range(ATT_HEADS)))
    ms = [jnp.max(m8, axis=0, keepdims=True) for m8 in m8s]

    acc_s[...] = jnp.zeros(acc_s.shape, F32)

    def pv_chunk(ac, carry):
        r0 = pl.multiple_of(ac * ATT_CHUNK, ATT_CHUNK)
        ct_chunk = ct_ref[:, pl.ds(r0, ATT_CHUNK)]
        for h in range(ATT_HEADS):
            pr = jnp.exp2(lg_s[h, pl.ds(r0, ATT_CHUNK), :] - ms[h])
            acc_s[h] += jnp.dot(ct_chunk, pr.astype(BF16), preferred_element_type=F32)
        return carry

    _loop_in_trips(nac, pv_chunk, 0)

    for h in range(ATT_HEADS):
        denom = acc_s[h, KV_LATENT:KV_LATENT + 1, :]
        o_t = acc_s[h, 0:KV_LATENT, :] * (1.0 / denom)
        y_t = jnp.dot(wuvt_ref[h], o_t.astype(BF16), preferred_element_type=F32)
        gb = gb_ref[:, h * HEAD_DIM:(h + 1) * HEAD_DIM]
        o_ref[:, h * HEAD_DIM:(h + 1) * HEAD_DIM] = (y_t.T * (gb * _sigmoid(gb))).astype(o_ref.dtype)


def _dsa(pb3, cols_b, w_ukt, wit, kn, c, ct, pa3, gb_blk, bias_tiles, wuvt, topk):
    bsz, s, _ = c.shape
    att_w = ATT_HEADS * HEAD_DIM
    idx_w = IDX_HEADS * IDX_DIM
    assert s % ATT_CHUNK == 0 and s % Q_TILE == 0 and Q_TILE % K_CHUNK == 0 and ATT_CHUNK % K_CHUNK == 0
    assert s // 16 // 4 < 256
    q_blk = _col_block(cols_b["q"], att_w)
    qi_blk = _col_block(cols_b["qi"], idx_w)
    const = lambda shape: pl.BlockSpec(shape, lambda b, i: (0,) * len(shape))
    return pl.pallas_call(
        functools.partial(_dsa_kernel, topk=topk),
        grid=(bsz, s // Q_TILE),
        in_specs=[
            pl.BlockSpec((None, Q_TILE, att_w), lambda b, i: (b, i, q_blk)),
            pl.BlockSpec((None, Q_TILE, idx_w), lambda b, i: (b, i, qi_blk)),
            const(w_ukt.shape),
            pl.BlockSpec((None, IDX_HEADS, Q_TILE), lambda b, i: (b, 0, i)),
            pl.BlockSpec((None, s, IDX_DIM), lambda b, i: (b, 0, 0)),
            pl.BlockSpec((None, s, KV_LATENT), lambda b, i: (b, 0, 0)),
            pl.BlockSpec((None, KV_LATENT + ONES_ROWS, s), lambda b, i: (b, 0, 0)),
            pl.BlockSpec((None, Q_TILE, att_w), lambda b, i: (b, i, gb_blk)),
            const(bias_tiles.shape),
            const(wuvt.shape),
        ],
        out_specs=pl.BlockSpec((None, Q_TILE, att_w), lambda b, i: (b, i, 0)),
        out_shape=jax.ShapeDtypeStruct((bsz, s, att_w), BF16),
        scratch_shapes=[
            pltpu.VMEM((ATT_HEADS, KV_LATENT, Q_TILE), BF16),
            pltpu.VMEM((IDX_HEADS, IDX_DIM, Q_TILE), BF16),
            pltpu.VMEM((s, Q_TILE), F32),
            pltpu.VMEM((s, Q_TILE), BF16),
            pltpu.VMEM((8, Q_TILE), F32),
            pltpu.SMEM((1,), I32),
            pltpu.VMEM((ATT_HEADS, s, Q_TILE), F32),
            pltpu.VMEM((ATT_HEADS, KV_LATENT + ONES_ROWS, Q_TILE), F32),
        ],
        compiler_params=_cparams(("parallel", "arbitrary")),
        name="dsa",
    )(pb3, pb3, w_ukt, wit, kn, c, ct, pa3, bias_tiles, wuvt)


def _outp_kernel(ya_ref, yb_ref, w_ref, x_ref, g_ref, o_ref, wbf_s, *, final_norm):
    @pl.when(pl.program_id(0) == 0)
    def _():
        wbf_s[...] = w_ref[...].astype(BF16)

    ka = ya_ref.shape[1]
    acc = jnp.dot(ya_ref[...], wbf_s[0:ka, :], preferred_element_type=F32)
    acc = acc + jnp.dot(yb_ref[...], wbf_s[ka:, :], preferred_element_type=F32)
    x = x_ref[...] + acc
    if final_norm:
        x = x * lax.rsqrt(jnp.mean(x * x, axis=-1, keepdims=True) + EPS) * g_ref[...]
    o_ref[...] = x


def _outp(ya, yb, w_out, x2, g, final_norm, tm=512):
    m, d = x2.shape
    ka, kb = ya.shape[1], yb.shape[1]
    assert w_out.shape == (ka + kb, d)
    return pl.pallas_call(
        functools.partial(_outp_kernel, final_norm=final_norm),
        grid=(m // tm,),
        in_specs=[
            pl.BlockSpec((tm, ka), lambda i: (i, 0)),
            pl.BlockSpec((tm, kb), lambda i: (i, 0)),
            pl.BlockSpec((ka + kb, d), lambda i: (0, 0), pipeline_mode=pl.Buffered(1)),
            pl.BlockSpec((tm, d), lambda i: (i, 0)),
            pl.BlockSpec((1, d), lambda i: (0, 0)),
        ],
        out_specs=pl.BlockSpec((tm, d), lambda i: (i, 0)),
        out_shape=jax.ShapeDtypeStruct((m, d), F32),
        scratch_shapes=[pltpu.VMEM((ka + kb, d), BF16)],
        compiler_params=_cparams(("arbitrary",)),
        name="outp",
    )(ya, yb, w_out, x2, g)


def _t5_bucket(dist):
    n = jnp.maximum(dist, 0)
    max_exact = REL_BUCKETS // 2
    nf = jnp.maximum(n, 1).astype(F32)
    large = max_exact + (jnp.log(nf / max_exact) / np.log(REL_MAX_DIST / max_exact)
                         * (REL_BUCKETS - max_exact)).astype(I32)
    large = jnp.minimum(large, REL_BUCKETS - 1)
    return jnp.where(n < max_exact, n, large)


def _bias_tiles(rel_bias):
    qw = K_CHUNK
    span = K_CHUNK + qw
    table = rel_bias[_t5_bucket(jnp.arange(span + 1, dtype=I32))].astype(F32)
    table = ((table[:span] - table[span:]) * LOG2E).T
    n = span + qw - 1
    a = jnp.concatenate([jnp.zeros((ATT_HEADS, qw - 1), F32), table], axis=1)
    shifted = jnp.tile(a, (1, span + 1))[:, :span * (n + 1)].reshape(ATT_HEADS, span, n + 1)
    tiles = shifted[:, ::-1, :qw].reshape(ATT_HEADS, 2, K_CHUNK, qw)
    return jnp.concatenate([jnp.zeros_like(tiles[:, :1]), tiles], axis=1)


def kernel(x, norm_g, w_in, conv_w, conv_b, lru_wa, lru_ba, lru_wx, lru_bx, lru_lambda, ckv_norm_g, idx_k_norm_g, idx_k_norm_b, w_uk, w_uv, w_out, rel_bias, final_norm_g):
    bsz, s, d = x.shape
    depth = w_in.shape[0]
    lru_w = lru_wa.shape[1] * lru_wa.shape[2]
    att_w = ATT_HEADS * HEAD_DIM
    idx_w = IDX_HEADS * IDX_DIM
    assert REL_MAX_DIST <= K_CHUNK
    assert lru_w == att_w == idx_w and att_w % KV_LATENT == 0
    topk = min(INDEX_TOPK, s // 4)

    o_q = 2 * lru_w
    o_ckv = o_q + att_w
    o_gb = o_ckv + KV_LATENT
    o_qi = o_gb + att_w
    o_ki = o_qi + idx_w
    tn = 512
    cols_a = {"xa": 0, "ga": lru_w, "gb": 2 * lru_w}
    cols_b = {"q": 0, "qi": att_w}
    n_f32, n_bf16 = 3 * lru_w, att_w + idx_w

    bias_tiles = _bias_tiles(rel_bias)
    x2 = x.reshape(bsz * s, d)
    for l in range(depth):
        order = [(0, o_q), (o_gb, o_qi), (o_q, o_ckv), (o_qi, o_ki), (o_ckv, o_gb), (o_ki, w_in.shape[2])]
        pa, pb, c, ct, kn, wit = _proj(x2, norm_g[l][None, :], w_in[l].T, order, ckv_norm_g[l][None, :],
                                       idx_k_norm_g[l][None, :], idx_k_norm_b[l][None, :], n_f32, n_bf16, s, tn=tn)
        pa3 = pa.reshape(bsz, s, -1)
        pb3 = pb.reshape(bsz, s, -1)

        ya = _rglru(pa3, cols_a, conv_w[l], conv_b[l][None, :], (0.5 * lru_wa[l]).astype(BF16),
                    0.5 * lru_ba[l][None, :], (0.5 * lru_wx[l]).astype(BF16), 0.5 * lru_bx[l][None, :],
                    lru_lambda[l][None, :])

        wukt = jnp.transpose(w_uk[l], (0, 2, 1)).astype(BF16)
        wuvt = jnp.transpose(w_uv[l], (0, 2, 1)).astype(BF16)
        yb = _dsa(pb3, cols_b, wukt, wit, kn.reshape(bsz, s, -1), c.reshape(bsz, s, -1), ct, pa3,
                  _col_block(cols_a["gb"], att_w), bias_tiles, wuvt, topk)

        x2 = _outp(ya.reshape(bsz * s, lru_w), yb.reshape(bsz * s, att_w), w_out[l], x2,
                   final_norm_g[None, :], final_norm=(l == depth - 1))
    return x2.reshape(bsz, s, d)
```

```python
import functools

import numpy as np
import jax
import jax.numpy as jnp
from jax import lax
from jax.experimental import pallas as pl
from jax.experimental.pallas import tpu as pltpu

F32 = jnp.float32
BF16 = jnp.bfloat16
I32 = jnp.int32

CONV_WIDTH = 4
LRU_C = 8.0
ATT_HEADS = 8
HEAD_DIM = 128
KV_LATENT = 256
IDX_HEADS = 16
IDX_DIM = 64
INDEX_TOPK = 256
REL_BUCKETS = 32
REL_MAX_DIST = 128
EPS = 1e-6
LOG2E = float(np.log2(np.e))
ONES_ROWS = 16

Q_TILE = 256
K_CHUNK = 128
ATT_CHUNK = 256
NEG = float(np.finfo(np.float32).min)
INT_MIN = -(2 ** 31)
VMEM_LIMIT = 56 * 1024 * 1024


def _cparams(sem):
    return pltpu.CompilerParams(dimension_semantics=sem, vmem_limit_bytes=VMEM_LIMIT)


def _col_block(offset, width):
    assert offset % width == 0
    return offset // width


def _proj_kernel(t_ref, x_ref, g_ref, wlo_ref, whi_ref, tail_ref, cg_ref, kg_ref, kb_ref,
                 oa_ref, ob_ref, c_ref, ct_ref, kn_ref, wit_ref, h_ref, *, na, nb):
    j = pl.program_id(1)
    r = pl.program_id(2)
    nt = (((1,), (1,)), ((), ()))

    def w_tile():
        lo = jnp.where(t_ref[2 * j] < 0, tail_ref[...], wlo_ref[...])
        hi = jnp.where(t_ref[2 * j + 1] < 0, tail_ref[...], whi_ref[...])
        return jnp.concatenate([lo, hi], axis=0).astype(BF16)

    @pl.when(j == 0)
    def _():
        x = x_ref[...]
        y = x * lax.rsqrt(jnp.mean(x * x, axis=-1, keepdims=True) + EPS)
        h = (y * g_ref[...]).astype(BF16)
        h_ref[r] = h
        oa_ref[...] = lax.dot_general(h, w_tile(), nt, preferred_element_type=F32)

    @pl.when((j > 0) & (j < na))
    def _():
        oa_ref[...] = lax.dot_general(h_ref[r], w_tile(), nt, preferred_element_type=F32)

    @pl.when((j >= na) & (j < na + nb))
    def _():
        ob_ref[...] = lax.dot_general(h_ref[r], w_tile(), nt, preferred_element_type=F32).astype(BF16)

    @pl.when(j >= na + nb)
    def _():
        used = KV_LATENT + 128
        tail = lax.dot_general(h_ref[r], w_tile()[:used], nt, preferred_element_type=F32)
        ckv = tail[:, :KV_LATENT]
        c = ckv * lax.rsqrt(jnp.mean(ckv * ckv, axis=-1, keepdims=True) + EPS) * cg_ref[...]
        c_ref[...] = c.astype(BF16)
        ct_ref[0:KV_LATENT, :] = c.T.astype(BF16)
        ct_ref[KV_LATENT:, :] = jnp.ones((ONES_ROWS, ct_ref.shape[1]), BF16)
        sm = tail[:, KV_LATENT:KV_LATENT + 128]
        ki = sm[:, :IDX_DIM]
        mu = jnp.mean(ki, axis=-1, keepdims=True)
        var = jnp.mean(jnp.square(ki - mu), axis=-1, keepdims=True)
        kn = (ki - mu) * lax.rsqrt(var + EPS) * kg_ref[...] + kb_ref[...]
        kn_ref[...] = kn.astype(BF16)
        wit_ref[...] = sm.T[IDX_DIM:IDX_DIM + IDX_HEADS, :] * (IDX_HEADS ** -0.5 * IDX_DIM ** -0.5)


def _proj(x2, g, w_t, order, ckv_g, k_g, k_b, n_f32, n_bf16, seq_len, tm=1024, tn=512, group=2):
    m, d = x2.shape
    n = w_t.shape[0]
    th = tn // 2
    n_whole = n // th
    src = []
    for start, stop in order:
        assert start % th == 0 and (stop % th == 0 or stop == n)
        src += list(range(start // th, -(-stop // th)))
    na, nb = n_f32 // tn, n_bf16 // tn
    assert n_f32 % tn == 0 and n_bf16 % tn == 0 and len(src) * th == n_f32 + n_bf16 + tn
    tail = jnp.pad(w_t[n_whole * th:], ((0, (n_whole + 1) * th - n), (0, 0)))
    table = jnp.asarray([blk if blk < n_whole else -1 for blk in src], I32)
    tps = seq_len // tm
    assert seq_len % tm == 0 and tn >= KV_LATENT + 128
    assert (m // tm) % group == 0
    nj = na + nb + 1
    const = lambda shape: pl.BlockSpec(shape, lambda i, j, r, t: (0,) * len(shape))
    last = group - 1
    row = lambda i, r_eff: i * group + r_eff
    per_seq = lambda g: (g // tps, 0, g % tps)
    tail_row = lambda i, j, r: row(i, jnp.where(j < nj - 1, 0, r))
    n_groups = m // tm // group

    def rest(i, last_col):
        nxt = i + 1 < n_groups
        return jnp.where(nxt, row(i + 1, 0), row(i, last)), jnp.where(nxt, 0, last_col)

    def oa_index(i, j, r, t):
        rest_row, rest_col = rest(i, na - 1)
        return jnp.where(j < na, row(i, r), rest_row), jnp.where(j < na, j, rest_col)

    def ob_index(i, j, r, t):
        rest_row, rest_col = rest(i, nb - 1)
        own = (j >= na) & (j < na + nb)
        return (jnp.where(j < na, row(i, 0), jnp.where(own, row(i, r), rest_row)),
                jnp.where(j < na, 0, jnp.where(own, j - na, rest_col)))

    grid_spec = pltpu.PrefetchScalarGridSpec(
        num_scalar_prefetch=1,
        grid=(m // tm // group, nj, group),
        in_specs=[
            pl.BlockSpec((tm, d), lambda i, j, r, t: (jnp.where(j == 0, row(i, r), rest(i, 0)[0]), 0)),
            const((1, d)),
            pl.BlockSpec((th, d), lambda i, j, r, t: (jnp.maximum(t[2 * j], 0), 0)),
            pl.BlockSpec((th, d), lambda i, j, r, t: (jnp.maximum(t[2 * j + 1], 0), 0)),
            const((th, d)),
            const((1, KV_LATENT)),
            const((1, IDX_DIM)),
            const((1, IDX_DIM)),
        ],
        out_specs=[
            pl.BlockSpec((tm, tn), oa_index),
            pl.BlockSpec((tm, tn), ob_index),
            pl.BlockSpec((tm, KV_LATENT), lambda i, j, r, t: (tail_row(i, j, r), 0)),
            pl.BlockSpec((None, KV_LATENT + ONES_ROWS, tm), lambda i, j, r, t: per_seq(tail_row(i, j, r))),
            pl.BlockSpec((tm, IDX_DIM), lambda i, j, r, t: (tail_row(i, j, r), 0)),
            pl.BlockSpec((None, IDX_HEADS, tm), lambda i, j, r, t: per_seq(tail_row(i, j, r))),
        ],
        scratch_shapes=[pltpu.VMEM((group, tm, d), BF16)],
    )
    return pl.pallas_call(
        functools.partial(_proj_kernel, na=na, nb=nb),
        grid_spec=grid_spec,
        out_shape=[
            jax.ShapeDtypeStruct((m, n_f32), F32),
            jax.ShapeDtypeStruct((m, n_bf16), BF16),
            jax.ShapeDtypeStruct((m, KV_LATENT), BF16),
            jax.ShapeDtypeStruct((m // seq_len, KV_LATENT + ONES_ROWS, seq_len), BF16),
            jax.ShapeDtypeStruct((m, IDX_DIM), BF16),
            jax.ShapeDtypeStruct((m // seq_len, IDX_HEADS, seq_len), F32),
        ],
        compiler_params=_cparams(("arbitrary", "arbitrary", "arbitrary")),
        name="proj",
    )(table, x2, g, w_t, w_t, tail, ckv_g, k_g, k_b)


def _sigmoid(v):
    return 0.5 * jnp.tanh(0.5 * v) + 0.5


def _scan_step(a, b, k, axis, idx):
    keep = idx >= k
    a_prev = jnp.where(keep, pltpu.roll(a, k, axis=axis), 1.0)
    b_prev = jnp.where(keep, pltpu.roll(b, k, axis=axis), 0.0)
    return a * a_prev, a * b_prev + b


def _rglru_kernel(xa_ref, ga_ref, cw_ref, cb_ref, wa_ref, ba_ref, wx_ref, bx_ref, lam_ref,
                  o_ref, pad_s, a_s, b_s, c_s):
    s, w = xa_ref.shape
    tile = 8
    n_tiles = s // tile

    pad_s[0:tile, :] = jnp.zeros((tile, w), F32)
    pad_s[tile:tile + s, :] = xa_ref[...]
    acc = pad_s[tile:tile + s, :] * cw_ref[CONV_WIDTH - 1:CONV_WIDTH, :]
    for j in range(CONV_WIDTH - 1):
        back = CONV_WIDTH - 1 - j
        acc = acc + pad_s[tile - back:tile - back + s, :] * cw_ref[j:j + 1, :]
    xc = cb_ref[...] + acc

    xcb = xc.astype(BF16)
    tr = jnp.tanh(jnp.dot(xcb, wa_ref[...], preferred_element_type=F32) + ba_ref[...])
    ti = jnp.tanh(jnp.dot(xcb, wx_ref[...], preferred_element_type=F32) + bx_ref[...])
    i = 0.5 * ti + 0.5
    z = -lam_ref[...]
    softplus = jnp.maximum(z, 0.0) + jnp.log1p(jnp.exp(-jnp.abs(z)))
    half = (-0.5 * LRU_C) * softplus
    log_a = half * tr + half
    a = jnp.exp(log_a)
    m2 = (1.0 + a * a) * jnp.tanh(-log_a)
    mult = jnp.where(m2 > 0.0, m2 * lax.rsqrt(m2), 0.0)
    gated = i * xc
    b_s[...] = mult * gated
    b_s[0:1, :] = gated[0:1, :]

    a3 = a.reshape(n_tiles, tile, w)
    b3 = b_s[...].reshape(n_tiles, tile, w)
    sub = lax.broadcasted_iota(I32, (n_tiles, tile, w), 1)
    for k in (1, 2, 4):
        a3, b3 = _scan_step(a3, b3, k, 1, sub)
    a_s[...] = a3.reshape(s, w)
    b_s[...] = b3.reshape(s, w)

    at = a_s[pl.ds(tile - 1, n_tiles, stride=tile), :]
    bt = b_s[pl.ds(tile - 1, n_tiles, stride=tile), :]
    trow = lax.broadcasted_iota(I32, (n_tiles, w), 0)
    k = 1
    while k < n_tiles:
        at, bt = _scan_step(at, bt, k, 0, trow)
        k *= 2
    c_s[0:tile, :] = jnp.zeros((tile, w), F32)
    c_s[tile:tile + n_tiles, :] = bt

    for t in range(n_tiles):
        rows = slice(t * tile, (t + 1) * tile)
        before = c_s[pl.ds(tile - 1 + t, tile, stride=0), :]
        h = a_s[rows, :] * before + b_s[rows, :]
        gh = 0.5 * ga_ref[rows, :]
        o_ref[rows, :] = (h * (gh * (jnp.tanh(gh) + 1.0))).astype(o_ref.dtype)


def _rglru(pa3, cols, conv_w, conv_b, wa, ba, wx, bx, lam):
    bsz, s, _ = pa3.shape
    g, w = wa.shape[0], wa.shape[-1]
    xa_blk = _col_block(cols["xa"], w)
    ga_blk = _col_block(cols["ga"], w)
    vec = lambda: pl.BlockSpec((1, w), lambda b, j: (0, j))
    return pl.pallas_call(
        _rglru_kernel,
        grid=(bsz, g),
        in_specs=[
            pl.BlockSpec((None, s, w), lambda b, j: (b, 0, xa_blk + j)),
            pl.BlockSpec((None, s, w), lambda b, j: (b, 0, ga_blk + j)),
            pl.BlockSpec((CONV_WIDTH, w), lambda b, j: (0, j)),
            vec(),
            pl.BlockSpec((None, w, w), lambda b, j: (j, 0, 0)),
            vec(),
            pl.BlockSpec((None, w, w), lambda b, j: (j, 0, 0)),
            vec(),
            vec(),
        ],
        out_specs=pl.BlockSpec((None, s, w), lambda b, j: (b, 0, j)),
        out_shape=jax.ShapeDtypeStruct((bsz, s, g * w), BF16),
        scratch_shapes=[pltpu.VMEM((s + 8, w), F32), pltpu.VMEM((s, w), F32), pltpu.VMEM((s, w), F32),
                        pltpu.VMEM((s // 8 + 8, w), F32)],
        compiler_params=_cparams(("parallel", "parallel")),
        name="rglru",
    )(pa3, pa3, conv_w, conv_b, wa, ba, wx, bx, lam)


def _tree_sum(parts):
    while len(parts) > 1:
        paired = [parts[i] + parts[i + 1] for i in range(0, len(parts) - 1, 2)]
        parts = paired + ([parts[-1]] if len(parts) % 2 else [])
    return parts[0]


def _sortable_to_f32(u):
    key = u ^ INT_MIN
    return lax.bitcast_convert_type(key ^ ((key >> 31) & 0x7FFFFFFF), F32)


def _count_ge(ref, rows, cand, pack):
    chains = 4
    one, zero = jnp.ones((), ref.dtype), jnp.zeros((), ref.dtype)
    accs = [None] * chains
    for r in range(rows // pack):
        hit = jnp.where(ref[r * pack:(r + 1) * pack, :] >= cand, one, zero)
        accs[r % chains] = hit if accs[r % chains] is None else accs[r % chains] + hit
    parts = [a.astype(F32) for a in accs if a is not None]
    return jnp.sum(_tree_sum(parts), axis=0, keepdims=True)


def _kth_largest(score_ref, score16_ref, rows, k):
    def step16(i, u):
        cand = u | (jnp.int32(1) << (15 - i))
        cand_f = _sortable_to_f32(cand << 16).astype(BF16)
        return jnp.where(_count_ge(score16_ref, rows, cand_f, 16) >= k, cand, u)

    hi = lax.fori_loop(0, 16, step16, jnp.zeros((1, Q_TILE), I32))
    at_hi = _count_ge(score_ref, rows, _sortable_to_f32(hi << 16), 8)
    keeps = at_hi >= k
    hi = jnp.where(keeps, hi, jnp.maximum(hi - 1, 0))

    per_trip = 4

    def trip(state):
        i, u, at_u, _ = state
        for t in range(per_trip):
            cand = u | (jnp.int32(1) << (15 - (i + t)))
            cnt = _count_ge(score_ref, rows, _sortable_to_f32(cand), 8)
            take = cnt >= k
            u = jnp.where(take, cand, u)
            at_u = jnp.where(take, cnt, at_u)
        open_lanes = jnp.sum((at_u != k).astype(I32))
        return i + per_trip, u, at_u, open_lanes

    unknown = jnp.full((1, Q_TILE), -1.0, F32)
    state = (jnp.int32(0), hi << 16, jnp.where(keeps, at_hi, unknown), jnp.int32(1))
    _, u, _, open_lanes = lax.while_loop(lambda st: (st[0] < 16) & (st[3] > 0), trip, state)
    return _sortable_to_f32(u), open_lanes


def _drop_extra_ties(score_ref, n_groups, thr, k):
    def count_above(g, acc):
        x = score_ref[pl.ds(pl.multiple_of(g * 8, 8), 8), :]
        return acc + jnp.where(x > thr, 1.0, 0.0)

    above = lax.fori_loop(0, n_groups, count_above, jnp.zeros((8, Q_TILE), F32))
    need = k - jnp.sum(above, axis=0, keepdims=True)
    sub = lax.broadcasted_iota(I32, (8, Q_TILE), 0)

    def demote(g, seen):
        r0 = pl.multiple_of(g * 8, 8)
        x = score_ref[pl.ds(r0, 8), :]
        tie = jnp.where(x == thr, 1.0, 0.0)
        upto = tie
        for k8 in (1, 2, 4):
            upto = upto + jnp.where(sub >= k8, pltpu.roll(upto, k8, axis=0), 0.0)
        before = seen + upto - tie
        score_ref[pl.ds(r0, 8), :] = jnp.where((tie > 0.0) & (before >= need), -jnp.inf, x)
        return seen + upto[7:8, :]

    lax.fori_loop(0, n_groups, demote, jnp.zeros((1, Q_TILE), F32))


def _loop_in_trips(n, body, init):
    def trips(start, count, width, carry):
        def group(i, c):
            for t in range(width):
                c = body(start + width * i + t, c)
            return c
        return lax.fori_loop(0, count, group, carry)

    carry = trips(0, n // 4, 4, init)
    carry = trips(4 * (n // 4), (n % 4) // 2, 2, carry)
    return trips(2 * (n // 2), n % 2, 1, carry)


def _dsa_kernel(q_ref, qi_ref, wukt_ref, wit_ref, kn_ref, c_ref, ct_ref, gb_ref, bias_ref, wuvt_ref,
                o_ref, qat_ref, qit_ref, score_s, score16_s, thr_s, open_s, lg_s, acc_s, topk):
    qb = pl.program_id(1)

    qt = q_ref[...].T
    scale = HEAD_DIM ** -0.5 * LOG2E
    for h in range(ATT_HEADS):
        qa = jnp.dot(wukt_ref[h], qt[h * HEAD_DIM:(h + 1) * HEAD_DIM], preferred_element_type=F32)
        qat_ref[h] = (qa * scale).astype(BF16)
    qit = qi_ref[...].T
    for h in range(IDX_HEADS):
        qit_ref[h] = qit[h * IDX_DIM:(h + 1) * IDX_DIM, :]
    q_tiles = Q_TILE // K_CHUNK
    nkc = (qb + 1) * q_tiles
    nac = (nkc * K_CHUNK + ATT_CHUNK - 1) // ATT_CHUNK
    tiles = ATT_CHUNK // K_CHUNK

    kiota = lax.broadcasted_iota(I32, (K_CHUNK, Q_TILE), 0)
    qpos = qb * Q_TILE + lax.broadcasted_iota(I32, (K_CHUNK, Q_TILE), 1)

    def score_chunk(ac, carry):
        for t in range(tiles):
            k0 = pl.multiple_of(ac * ATT_CHUNK + t * K_CHUNK, K_CHUNK)
            kn = kn_ref[pl.ds(k0, K_CHUNK), :]
            acc = jnp.zeros((K_CHUNK, Q_TILE), F32)
            for h in range(IDX_HEADS):
                sc = jnp.dot(kn, qit_ref[h], preferred_element_type=F32)
                acc = acc + jnp.maximum(sc, 0.0) * wit_ref[h:h + 1, :]
            masked = jnp.where(kiota + k0 <= qpos, acc, -jnp.inf)
            score_s[pl.ds(k0, K_CHUNK), :] = masked
            score16_s[pl.ds(k0, K_CHUNK), :] = masked.astype(BF16)
        return carry

    _loop_in_trips(nac, score_chunk, 0)

    for v in range(1, score_s.shape[0] // ATT_CHUNK + 1):
        @pl.when(nac == v)
        def _(rows=v * ATT_CHUNK):
            if rows <= topk:
                thr, open_lanes = jnp.full((1, Q_TILE), NEG, F32), jnp.int32(0)
            else:
                thr, open_lanes = _kth_largest(score_s, score16_s, rows, float(topk))
                thr = jnp.where(thr >= NEG, thr, NEG)
            thr_s[...] = jnp.broadcast_to(thr, thr_s.shape)
            open_s[0] = open_lanes

    thr = thr_s[0:1, :]

    @pl.when(open_s[0] > 0)
    def _():
        _drop_extra_ties(score_s, nac * (ATT_CHUNK // 8), thr, float(topk))

    def logit_chunk(ac, m8s):
        r0 = pl.multiple_of(ac * ATT_CHUNK, ATT_CHUNK)
        c_chunk = c_ref[pl.ds(r0, ATT_CHUNK), :]
        mbias = jnp.where(score_s[pl.ds(r0, ATT_CHUNK), :] >= thr, 0.0, NEG)
        near = [[jnp.clip(ac * tiles + t - (qb * q_tiles + j) + 2, 0, 2) for j in range(q_tiles)]
                for t in range(tiles)]
        out = []
        for h in range(ATT_HEADS):
            lg = jnp.dot(c_chunk, qat_ref[h], preferred_element_type=F32) + mbias
            lg = jnp.concatenate(
                [lg[t * K_CHUNK:(t + 1) * K_CHUNK]
                 + jnp.concatenate([bias_ref[h, near[t][j]] for j in range(q_tiles)], axis=1)
                 for t in range(tiles)], axis=0)
            lg_s[h, pl.ds(r0, ATT_CHUNK), :] = lg
            out.append(jnp.maximum(m8s[h], jnp.max(lg.reshape(ATT_CHUNK // 8, 8, Q_TILE), axis=0)))
        return tuple(out)

    m8s = _loop_in_trips(nac, logit_chunk, tuple(jnp.full((8, Q_TILE), NEG, F32) for _ in range(ATT_HEADS)))
    ms = [jnp.max(m8, axis=0, keepdims=True) for m8 in m8s]

    acc_s[...] = jnp.zeros(acc_s.shape, F32)

    def pv_chunk(ac, carry):
        r0 = pl.multiple_of(ac * ATT_CHUNK, ATT_CHUNK)
        ct_chunk = ct_ref[:, pl.ds(r0, ATT_CHUNK)]
        for h in range(ATT_HEADS):
            pr = jnp.exp2(lg_s[h, pl.ds(r0, ATT_CHUNK), :] - ms[h])
            acc_s[h] += jnp.dot(ct_chunk, pr.astype(BF16), preferred_element_type=F32)
        return carry

    _loop_in_trips(nac, pv_chunk, 0)

    for h in range(ATT_HEADS):
        denom = acc_s[h, KV_LATENT:KV_LATENT + 1, :]
        o_t = acc_s[h, 0:KV_LATENT, :] * (1.0 / denom)
        y_t = jnp.dot(wuvt_ref[h], o_t.astype(BF16), preferred_element_type=F32)
        gb = gb_ref[:, h * HEAD_DIM:(h + 1) * HEAD_DIM]
        o_ref[:, h * HEAD_DIM:(h + 1) * HEAD_DIM] = (y_t.T * (gb * _sigmoid(gb))).astype(o_ref.dtype)


def _dsa(pb3, cols_b, w_ukt, wit, kn, c, ct, pa3, gb_blk, bias_tiles, wuvt, topk):
    bsz, s, _ = c.shape
    att_w = ATT_HEADS * HEAD_DIM
    idx_w = IDX_HEADS * IDX_DIM
    assert s % ATT_CHUNK == 0 and s % Q_TILE == 0 and Q_TILE % K_CHUNK == 0 and ATT_CHUNK % K_CHUNK == 0
    assert s // 16 // 4 < 256
    q_blk = _col_block(cols_b["q"], att_w)
    qi_blk = _col_block(cols_b["qi"], idx_w)
    const = lambda shape: pl.BlockSpec(shape, lambda b, i: (0,) * len(shape))
    return pl.pallas_call(
        functools.partial(_dsa_kernel, topk=topk),
        grid=(bsz, s // Q_TILE),
        in_specs=[
            pl.BlockSpec((None, Q_TILE, att_w), lambda b, i: (b, i, q_blk)),
            pl.BlockSpec((None, Q_TILE, idx_w), lambda b, i: (b, i, qi_blk)),
            const(w_ukt.shape),
            pl.BlockSpec((None, IDX_HEADS, Q_TILE), lambda b, i: (b, 0, i)),
            pl.BlockSpec((None, s, IDX_DIM), lambda b, i: (b, 0, 0)),
            pl.BlockSpec((None, s, KV_LATENT), lambda b, i: (b, 0, 0)),
            pl.BlockSpec((None, KV_LATENT + ONES_ROWS, s), lambda b, i: (b, 0, 0)),
            pl.BlockSpec((None, Q_TILE, att_w), lambda b, i: (b, i, gb_blk)),
            const(bias_tiles.shape),
            const(wuvt.shape),
        ],
        out_specs=pl.BlockSpec((None, Q_TILE, att_w), lambda b, i: (b, i, 0)),
        out_shape=jax.ShapeDtypeStruct((bsz, s, att_w), BF16),
        scratch_shapes=[
            pltpu.VMEM((ATT_HEADS, KV_LATENT, Q_TILE), BF16),
            pltpu.VMEM((IDX_HEADS, IDX_DIM, Q_TILE), BF16),
            pltpu.VMEM((s, Q_TILE), F32),
            pltpu.VMEM((s, Q_TILE), BF16),
            pltpu.VMEM((8, Q_TILE), F32),
            pltpu.SMEM((1,), I32),
            pltpu.VMEM((ATT_HEADS, s, Q_TILE), F32),
            pltpu.VMEM((ATT_HEADS, KV_LATENT + ONES_ROWS, Q_TILE), F32),
        ],
        compiler_params=_cparams(("parallel", "arbitrary")),
        name="dsa",
    )(pb3, pb3, w_ukt, wit, kn, c, ct, pa3, bias_tiles, wuvt)


def _outp_kernel(ya_ref, yb_ref, w_ref, x_ref, g_ref, o_ref, wbf_s, *, final_norm):
    @pl.when(pl.program_id(0) == 0)
    def _():
        wbf_s[...] = w_ref[...].astype(BF16)

    ka = ya_ref.shape[1]
    acc = jnp.dot(ya_ref[...], wbf_s[0:ka, :], preferred_element_type=F32)
    acc = acc + jnp.dot(yb_ref[...], wbf_s[ka:, :], preferred_element_type=F32)
    x = x_ref[...] + acc
    if final_norm:
        x = x * lax.rsqrt(jnp.mean(x * x, axis=-1, keepdims=True) + EPS) * g_ref[...]
    o_ref[...] = x


def _outp(ya, yb, w_out, x2, g, final_norm, tm=512):
    m, d = x2.shape
    ka, kb = ya.shape[1], yb.shape[1]
    assert w_out.shape == (ka + kb, d)
    return pl.pallas_call(
        functools.partial(_outp_kernel, final_norm=final_norm),
        grid=(m // tm,),
        in_specs=[
            pl.BlockSpec((tm, ka), lambda i: (i, 0)),
            pl.BlockSpec((tm, kb), lambda i: (i, 0)),
            pl.BlockSpec((ka + kb, d), lambda i: (0, 0), pipeline_mode=pl.Buffered(1)),
            pl.BlockSpec((tm, d), lambda i: (i, 0)),
            pl.BlockSpec((1, d), lambda i: (0, 0)),
        ],
        out_specs=pl.BlockSpec((tm, d), lambda i: (i, 0)),
        out_shape=jax.ShapeDtypeStruct((m, d), F32),
        scratch_shapes=[pltpu.VMEM((ka + kb, d), BF16)],
        compiler_params=_cparams(("arbitrary",)),
        name="outp",
    )(ya, yb, w_out, x2, g)


def _t5_bucket(dist):
    n = jnp.maximum(dist, 0)
    max_exact = REL_BUCKETS // 2
    nf = jnp.maximum(n, 1).astype(F32)
    large = max_exact + (jnp.log(nf / max_exact) / np.log(REL_MAX_DIST / max_exact)
                         * (REL_BUCKETS - max_exact)).astype(I32)
    large = jnp.minimum(large, REL_BUCKETS - 1)
    return jnp.where(n < max_exact, n, large)


def _bias_tiles(rel_bias):
    qw = K_CHUNK
    span = K_CHUNK + qw
    table = rel_bias[_t5_bucket(jnp.arange(span + 1, dtype=I32))].astype(F32)
    table = ((table[:span] - table[span:]) * LOG2E).T
    n = span + qw - 1
    a = jnp.concatenate([jnp.zeros((ATT_HEADS, qw - 1), F32), table], axis=1)
    shifted = jnp.tile(a, (1, span + 1))[:, :span * (n + 1)].reshape(ATT_HEADS, span, n + 1)
    tiles = shifted[:, ::-1, :qw].reshape(ATT_HEADS, 2, K_CHUNK, qw)
    return jnp.concatenate([jnp.zeros_like(tiles[:, :1]), tiles], axis=1)


def kernel(x, norm_g, w_in, conv_w, conv_b, lru_wa, lru_ba, lru_wx, lru_bx, lru_lambda, ckv_norm_g, idx_k_norm_g, idx_k_norm_b, w_uk, w_uv, w_out, rel_bias, final_norm_g):
    bsz, s, d = x.shape
    depth = w_in.shape[0]
    lru_w = lru_wa.shape[1] * lru_wa.shape[2]
    att_w = ATT_HEADS * HEAD_DIM
    idx_w = IDX_HEADS * IDX_DIM
    assert REL_MAX_DIST <= K_CHUNK
    assert lru_w == att_w == idx_w and att_w % KV_LATENT == 0
    topk = min(INDEX_TOPK, s // 4)

    o_q = 2 * lru_w
    o_ckv = o_q + att_w
    o_gb = o_ckv + KV_LATENT
    o_qi = o_gb + att_w
    o_ki = o_qi + idx_w
    tn = 512
    cols_a = {"xa": 0, "ga": lru_w, "gb": 2 * lru_w}
    cols_b = {"q": 0, "qi": att_w}
    n_f32, n_bf16 = 3 * lru_w, att_w + idx_w

    bias_tiles = _bias_tiles(rel_bias)
    x2 = x.reshape(bsz * s, d)
    for l in range(depth):
        order = [(0, o_q), (o_gb, o_qi), (o_q, o_ckv), (o_qi, o_ki), (o_ckv, o_gb), (o_ki, w_in.shape[2])]
        pa, pb, c, ct, kn, wit = _proj(x2, norm_g[l][None, :], w_in[l].T, order, ckv_norm_g[l][None, :],
                                       idx_k_norm_g[l][None, :], idx_k_norm_b[l][None, :], n_f32, n_bf16, s, tn=tn)
        pa3 = pa.reshape(bsz, s, -1)
        pb3 = pb.reshape(bsz, s, -1)

        ya = _rglru(pa3, cols_a, conv_w[l], conv_b[l][None, :], (0.5 * lru_wa[l]).astype(BF16),
                    0.5 * lru_ba[l][None, :], (0.5 * lru_wx[l]).astype(BF16), 0.5 * lru_bx[l][None, :],
                    lru_lambda[l][None, :])

        wukt = jnp.transpose(w_uk[l], (0, 2, 1)).astype(BF16)
        wuvt = jnp.transpose(w_uv[l], (0, 2, 1)).astype(BF16)
        yb = _dsa(pb3, cols_b, wukt, wit, kn.reshape(bsz, s, -1), c.reshape(bsz, s, -1), ct, pa3,
                  _col_block(cols_a["gb"], att_w), bias_tiles, wuvt, topk)

        x2 = _outp(ya.reshape(bsz * s, lru_w), yb.reshape(bsz * s, att_w), w_out[l], x2,
                   final_norm_g[None, :], final_norm=(l == depth - 1))
    return x2.reshape(bsz, s, d)
```

```python
import functools

import numpy as np
import jax
import jax.numpy as jnp
from jax import lax
from jax.experimental import pallas as pl
from jax.experimental.pallas import tpu as pltpu

F32 = jnp.float32
BF16 = jnp.bfloat16
I32 = jnp.int32

CONV_WIDTH = 4
LRU_C = 8.0
ATT_HEADS = 8
HEAD_DIM = 128
KV_LATENT = 256
IDX_HEADS = 16
IDX_DIM = 64
INDEX_TOPK = 256
REL_BUCKETS = 32
REL_MAX_DIST = 128
EPS = 1e-6
LOG2E = float(np.log2(np.e))
ONES_ROWS = 16

Q_TILE = 256
K_CHUNK = 128
ATT_CHUNK = 256
NEG = float(np.finfo(np.float32).min)
INT_MIN = -(2 ** 31)
VMEM_LIMIT = 56 * 1024 * 1024


def _cparams(sem):
    return pltpu.CompilerParams(dimension_semantics=sem, vmem_limit_bytes=VMEM_LIMIT)


def _col_block(offset, width):
    assert offset % width == 0
    return offset // width


def _proj_kernel(t_ref, x_ref, g_ref, wlo_ref, whi_ref, tail_ref, cg_ref, kg_ref, kb_ref,
                 oa_ref, ob_ref, c_ref, ct_ref, kn_ref, wit_ref, h_ref, *, na, nb):
    j = pl.program_id(1)
    r = pl.program_id(2)
    nt = (((1,), (1,)), ((), ()))

    def w_tile():
        lo = jnp.where(t_ref[2 * j] < 0, tail_ref[...], wlo_ref[...])
        hi = jnp.where(t_ref[2 * j + 1] < 0, tail_ref[...], whi_ref[...])
        return jnp.concatenate([lo, hi], axis=0).astype(BF16)

    @pl.when(j == 0)
    def _():
        x = x_ref[...]
        y = x * lax.rsqrt(jnp.mean(x * x, axis=-1, keepdims=True) + EPS)
        h = (y * g_ref[...]).astype(BF16)
        h_ref[r] = h
        oa_ref[...] = lax.dot_general(h, w_tile(), nt, preferred_element_type=F32)

    @pl.when((j > 0) & (j < na))
    def _():
        oa_ref[...] = lax.dot_general(h_ref[r], w_tile(), nt, preferred_element_type=F32)

    @pl.when((j >= na) & (j < na + nb))
    def _():
        ob_ref[...] = lax.dot_general(h_ref[r], w_tile(), nt, preferred_element_type=F32).astype(BF16)

    @pl.when(j >= na + nb)
    def _():
        used = KV_LATENT + 128
        tail = lax.dot_general(h_ref[r], w_tile()[:used], nt, preferred_element_type=F32)
        ckv = tail[:, :KV_LATENT]
        c = ckv * lax.rsqrt(jnp.mean(ckv * ckv, axis=-1, keepdims=True) + EPS) * cg_ref[...]
        c_ref[...] = c.astype(BF16)
        ct_ref[0:KV_LATENT, :] = c.T.astype(BF16)
        ct_ref[KV_LATENT:, :] = jnp.ones((ONES_ROWS, ct_ref.shape[1]), BF16)
        sm = tail[:, KV_LATENT:KV_LATENT + 128]
        ki = sm[:, :IDX_DIM]
        mu = jnp.mean(ki, axis=-1, keepdims=True)
        var = jnp.mean(jnp.square(ki - mu), axis=-1, keepdims=True)
        kn = (ki - mu) * lax.rsqrt(var + EPS) * kg_ref[...] + kb_ref[...]
        kn_ref[...] = kn.astype(BF16)
        wit_ref[...] = sm.T[IDX_DIM:IDX_DIM + IDX_HEADS, :] * (IDX_HEADS ** -0.5 * IDX_DIM ** -0.5)


def _proj(x2, g, w_t, order, ckv_g, k_g, k_b, n_f32, n_bf16, seq_len, tm=1024, tn=512, group=2):
    m, d = x2.shape
    n = w_t.shape[0]
    th = tn // 2
    n_whole = n // th
    src = []
    for start, stop in order:
        assert start % th == 0 and (stop % th == 0 or stop == n)
        src += list(range(start // th, -(-stop // th)))
    na, nb = n_f32 // tn, n_bf16 // tn
    assert n_f32 % tn == 0 and n_bf16 % tn == 0 and len(src) * th == n_f32 + n_bf16 + tn
    tail = jnp.pad(w_t[n_whole * th:], ((0, (n_whole + 1) * th - n), (0, 0)))
    table = jnp.asarray([blk if blk < n_whole else -1 for blk in src], I32)
    tps = seq_len // tm
    assert seq_len % tm == 0 and tn >= KV_LATENT + 128
    assert (m // tm) % group == 0
    nj = na + nb + 1
    const = lambda shape: pl.BlockSpec(shape, lambda i, j, r, t: (0,) * len(shape))
    last = group - 1
    row = lambda i, r_eff: i * group + r_eff
    per_seq = lambda g: (g // tps, 0, g % tps)
    tail_row = lambda i, j, r: row(i, jnp.where(j < nj - 1, 0, r))
    n_groups = m // tm // group

    def rest(i, last_col):
        nxt = i + 1 < n_groups
        return jnp.where(nxt, row(i + 1, 0), row(i, last)), jnp.where(nxt, 0, last_col)

    def oa_index(i, j, r, t):
        rest_row, rest_col = rest(i, na - 1)
        return jnp.where(j < na, row(i, r), rest_row), jnp.where(j < na, j, rest_col)

    def ob_index(i, j, r, t):
        rest_row, rest_col = rest(i, nb - 1)
        own = (j >= na) & (j < na + nb)
        return (jnp.where(j < na, row(i, 0), jnp.where(own, row(i, r), rest_row)),
                jnp.where(j < na, 0, jnp.where(own, j - na, rest_col)))

    grid_spec = pltpu.PrefetchScalarGridSpec(
        num_scalar_prefetch=1,
        grid=(m // tm // group, nj, group),
        in_specs=[
            pl.BlockSpec((tm, d), lambda i, j, r, t: (row(i, jnp.where(j == 0, r, last)), 0)),
            const((1, d)),
            pl.BlockSpec((th, d), lambda i, j, r, t: (jnp.maximum(t[2 * j], 0), 0)),
            pl.BlockSpec((th, d), lambda i, j, r, t: (jnp.maximum(t[2 * j + 1], 0), 0)),
            const((th, d)),
            const((1, KV_LATENT)),
            const((1, IDX_DIM)),
            const((1, IDX_DIM)),
        ],
        out_specs=[
            pl.BlockSpec((tm, tn), oa_index),
            pl.BlockSpec((tm, tn), ob_index),
            pl.BlockSpec((tm, KV_LATENT), lambda i, j, r, t: (tail_row(i, j, r), 0)),
            pl.BlockSpec((None, KV_LATENT + ONES_ROWS, tm), lambda i, j, r, t: per_seq(tail_row(i, j, r))),
            pl.BlockSpec((tm, IDX_DIM), lambda i, j, r, t: (tail_row(i, j, r), 0)),
            pl.BlockSpec((None, IDX_HEADS, tm), lambda i, j, r, t: per_seq(tail_row(i, j, r))),
        ],
        scratch_shapes=[pltpu.VMEM((group, tm, d), BF16)],
    )
    return pl.pallas_call(
        functools.partial(_proj_kernel, na=na, nb=nb),
        grid_spec=grid_spec,
        out_shape=[
            jax.ShapeDtypeStruct((m, n_f32), F32),
            jax.ShapeDtypeStruct((m, n_bf16), BF16),
            jax.ShapeDtypeStruct((m, KV_LATENT), BF16),
            jax.ShapeDtypeStruct((m // seq_len, KV_LATENT + ONES_ROWS, seq_len), BF16),
            jax.ShapeDtypeStruct((m, IDX_DIM), BF16),
            jax.ShapeDtypeStruct((m // seq_len, IDX_HEADS, seq_len), F32),
        ],
        compiler_params=_cparams(("arbitrary", "arbitrary", "arbitrary")),
        name="proj",
    )(table, x2, g, w_t, w_t, tail, ckv_g, k_g, k_b)


def _sigmoid(v):
    return 0.5 * jnp.tanh(0.5 * v) + 0.5


def _scan_step(a, b, k, axis, idx):
    keep = idx >= k
    a_prev = jnp.where(keep, pltpu.roll(a, k, axis=axis), 1.0)
    b_prev = jnp.where(keep, pltpu.roll(b, k, axis=axis), 0.0)
    return a * a_prev, a * b_prev + b


def _rglru_kernel(xa_ref, ga_ref, cw_ref, cb_ref, wa_ref, ba_ref, wx_ref, bx_ref, lam_ref,
                  o_ref, pad_s, a_s, b_s, c_s):
    s, w = xa_ref.shape
    tile = 8
    n_tiles = s // tile

    pad_s[0:tile, :] = jnp.zeros((tile, w), F32)
    pad_s[tile:tile + s, :] = xa_ref[...]
    acc = pad_s[tile:tile + s, :] * cw_ref[CONV_WIDTH - 1:CONV_WIDTH, :]
    for j in range(CONV_WIDTH - 1):
        back = CONV_WIDTH - 1 - j
        acc = acc + pad_s[tile - back:tile - back + s, :] * cw_ref[j:j + 1, :]
    xc = cb_ref[...] + acc

    xcb = xc.astype(BF16)
    tr = jnp.tanh(jnp.dot(xcb, wa_ref[...], preferred_element_type=F32) + ba_ref[...])
    ti = jnp.tanh(jnp.dot(xcb, wx_ref[...], preferred_element_type=F32) + bx_ref[...])
    i = 0.5 * ti + 0.5
    z = -lam_ref[...]
    softplus = jnp.maximum(z, 0.0) + jnp.log1p(jnp.exp(-jnp.abs(z)))
    half = (-0.5 * LRU_C) * softplus
    log_a = half * tr + half
    a = jnp.exp(log_a)
    m2 = (1.0 + a * a) * jnp.tanh(-log_a)
    mult = jnp.where(m2 > 0.0, m2 * lax.rsqrt(m2), 0.0)
    gated = i * xc
    b_s[...] = mult * gated
    b_s[0:1, :] = gated[0:1, :]

    a3 = a.reshape(n_tiles, tile, w)
    b3 = b_s[...].reshape(n_tiles, tile, w)
    sub = lax.broadcasted_iota(I32, (n_tiles, tile, w), 1)
    for k in (1, 2, 4):
        a3, b3 = _scan_step(a3, b3, k, 1, sub)
    a_s[...] = a3.reshape(s, w)
    b_s[...] = b3.reshape(s, w)

    at = a_s[pl.ds(tile - 1, n_tiles, stride=tile), :]
    bt = b_s[pl.ds(tile - 1, n_tiles, stride=tile), :]
    trow = lax.broadcasted_iota(I32, (n_tiles, w), 0)
    k = 1
    while k < n_tiles:
        at, bt = _scan_step(at, bt, k, 0, trow)
        k *= 2
    c_s[0:tile, :] = jnp.zeros((tile, w), F32)
    c_s[tile:tile + n_tiles, :] = bt

    for t in range(n_tiles):
        rows = slice(t * tile, (t + 1) * tile)
        before = c_s[pl.ds(tile - 1 + t, tile, stride=0), :]
        h = a_s[rows, :] * before + b_s[rows, :]
        gh = 0.5 * ga_ref[rows, :]
        o_ref[rows, :] = (h * (gh * (jnp.tanh(gh) + 1.0))).astype(o_ref.dtype)


def _rglru(pa3, cols, conv_w, conv_b, wa, ba, wx, bx, lam):
    bsz, s, _ = pa3.shape
    g, w = wa.shape[0], wa.shape[-1]
    xa_blk = _col_block(cols["xa"], w)
    ga_blk = _col_block(cols["ga"], w)
    vec = lambda: pl.BlockSpec((1, w), lambda b, j: (0, j))
    return pl.pallas_call(
        _rglru_kernel,
        grid=(bsz, g),
        in_specs=[
            pl.BlockSpec((None, s, w), lambda b, j: (b, 0, xa_blk + j)),
            pl.BlockSpec((None, s, w), lambda b, j: (b, 0, ga_blk + j)),
            pl.BlockSpec((CONV_WIDTH, w), lambda b, j: (0, j)),
            vec(),
            pl.BlockSpec((None, w, w), lambda b, j: (j, 0, 0)),
            vec(),
            pl.BlockSpec((None, w, w), lambda b, j: (j, 0, 0)),
            vec(),
            vec(),
        ],
        out_specs=pl.BlockSpec((None, s, w), lambda b, j: (b, 0, j)),
        out_shape=jax.ShapeDtypeStruct((bsz, s, g * w), BF16),
        scratch_shapes=[pltpu.VMEM((s + 8, w), F32), pltpu.VMEM((s, w), F32), pltpu.VMEM((s, w), F32),
                        pltpu.VMEM((s // 8 + 8, w), F32)],
        compiler_params=_cparams(("parallel", "parallel")),
        name="rglru",
    )(pa3, pa3, conv_w, conv_b, wa, ba, wx, bx, lam)


def _tree_sum(parts):
    while len(parts) > 1:
        paired = [parts[i] + parts[i + 1] for i in range(0, len(parts) - 1, 2)]
        parts = paired + ([parts[-1]] if len(parts) % 2 else [])
    return parts[0]


def _sortable_to_f32(u):
    key = u ^ INT_MIN
    return lax.bitcast_convert_type(key ^ ((key >> 31) & 0x7FFFFFFF), F32)


def _count_ge(ref, rows, cand, pack):
    chains = 4
    one, zero = jnp.ones((), ref.dtype), jnp.zeros((), ref.dtype)
    accs = [None] * chains
    for r in range(rows // pack):
        hit = jnp.where(ref[r * pack:(r + 1) * pack, :] >= cand, one, zero)
        accs[r % chains] = hit if accs[r % chains] is None else accs[r % chains] + hit
    parts = [a.astype(F32) for a in accs if a is not None]
    return jnp.sum(_tree_sum(parts), axis=0, keepdims=True)


def _kth_largest(score_ref, score16_ref, rows, k):
    def step16(i, u):
        cand = u | (jnp.int32(1) << (15 - i))
        cand_f = _sortable_to_f32(cand << 16).astype(BF16)
        return jnp.where(_count_ge(score16_ref, rows, cand_f, 16) >= k, cand, u)

    hi = lax.fori_loop(0, 16, step16, jnp.zeros((1, Q_TILE), I32))
    at_hi = _count_ge(score_ref, rows, _sortable_to_f32(hi << 16), 8)
    keeps = at_hi >= k
    hi = jnp.where(keeps, hi, jnp.maximum(hi - 1, 0))

    per_trip = 4

    def trip(state):
        i, u, at_u, _ = state
        for t in range(per_trip):
            cand = u | (jnp.int32(1) << (15 - (i + t)))
            cnt = _count_ge(score_ref, rows, _sortable_to_f32(cand), 8)
            take = cnt >= k
            u = jnp.where(take, cand, u)
            at_u = jnp.where(take, cnt, at_u)
        open_lanes = jnp.sum((at_u != k).astype(I32))
        return i + per_trip, u, at_u, open_lanes

    unknown = jnp.full((1, Q_TILE), -1.0, F32)
    state = (jnp.int32(0), hi << 16, jnp.where(keeps, at_hi, unknown), jnp.int32(1))
    _, u, _, open_lanes = lax.while_loop(lambda st: (st[0] < 16) & (st[3] > 0), trip, state)
    return _sortable_to_f32(u), open_lanes


def _drop_extra_ties(score_ref, n_groups, thr, k):
    def count_above(g, acc):
        x = score_ref[pl.ds(pl.multiple_of(g * 8, 8), 8), :]
        return acc + jnp.where(x > thr, 1.0, 0.0)

    above = lax.fori_loop(0, n_groups, count_above, jnp.zeros((8, Q_TILE), F32))
    need = k - jnp.sum(above, axis=0, keepdims=True)
    sub = lax.broadcasted_iota(I32, (8, Q_TILE), 0)

    def demote(g, seen):
        r0 = pl.multiple_of(g * 8, 8)
        x = score_ref[pl.ds(r0, 8), :]
        tie = jnp.where(x == thr, 1.0, 0.0)
        upto = tie
        for k8 in (1, 2, 4):
            upto = upto + jnp.where(sub >= k8, pltpu.roll(upto, k8, axis=0), 0.0)
        before = seen + upto - tie
        score_ref[pl.ds(r0, 8), :] = jnp.where((tie > 0.0) & (before >= need), -jnp.inf, x)
        return seen + upto[7:8, :]

    lax.fori_loop(0, n_groups, demote, jnp.zeros((1, Q_TILE), F32))


def _loop_in_trips(n, body, init):
    def trips(start, count, width, carry):
        def group(i, c):
            for t in range(width):
                c = body(start + width * i + t, c)
            return c
        return lax.fori_loop(0, count, group, carry)

    carry = trips(0, n // 4, 4, init)
    carry = trips(4 * (n // 4), (n % 4) // 2, 2, carry)
    return trips(2 * (n // 2), n % 2, 1, carry)


def _dsa_kernel(q_ref, qi_ref, wukt_ref, wit_ref, kn_ref, c_ref, ct_ref, gb_ref, bias_ref, wuvt_ref,
                o_ref, qat_ref, qit_ref, score_s, score16_s, thr_s, open_s, lg_s, acc_s, topk):
    qb = pl.program_id(1)

    qt = q_ref[...].T
    scale = HEAD_DIM ** -0.5 * LOG2E
    for h in range(ATT_HEADS):
        qa = jnp.dot(wukt_ref[h], qt[h * HEAD_DIM:(h + 1) * HEAD_DIM], preferred_element_type=F32)
        qat_ref[h] = (qa * scale).astype(BF16)
    qit = qi_ref[...].T
    for h in range(IDX_HEADS):
        qit_ref[h] = qit[h * IDX_DIM:(h + 1) * IDX_DIM, :]
    q_tiles = Q_TILE // K_CHUNK
    nkc = (qb + 1) * q_tiles
    nac = (nkc * K_CHUNK + ATT_CHUNK - 1) // ATT_CHUNK
    tiles = ATT_CHUNK // K_CHUNK

    kiota = lax.broadcasted_iota(I32, (K_CHUNK, Q_TILE), 0)
    qpos = qb * Q_TILE + lax.broadcasted_iota(I32, (K_CHUNK, Q_TILE), 1)

    def score_chunk(ac, carry):
        for t in range(tiles):
            k0 = pl.multiple_of(ac * ATT_CHUNK + t * K_CHUNK, K_CHUNK)
            kn = kn_ref[pl.ds(k0, K_CHUNK), :]
            acc = jnp.zeros((K_CHUNK, Q_TILE), F32)
            for h in range(IDX_HEADS):
                sc = jnp.dot(kn, qit_ref[h], preferred_element_type=F32)
                acc = acc + jnp.maximum(sc, 0.0) * wit_ref[h:h + 1, :]
            masked = jnp.where(kiota + k0 <= qpos, acc, -jnp.inf)
            score_s[pl.ds(k0, K_CHUNK), :] = masked
            score16_s[pl.ds(k0, K_CHUNK), :] = masked.astype(BF16)
        return carry

    _loop_in_trips(nac, score_chunk, 0)

    for v in range(1, score_s.shape[0] // ATT_CHUNK + 1):
        @pl.when(nac == v)
        def _(rows=v * ATT_CHUNK):
            if rows <= topk:
                thr, open_lanes = jnp.full((1, Q_TILE), NEG, F32), jnp.int32(0)
            else:
                thr, open_lanes = _kth_largest(score_s, score16_s, rows, float(topk))
                thr = jnp.where(thr >= NEG, thr, NEG)
            thr_s[...] = jnp.broadcast_to(thr, thr_s.shape)
            open_s[0] = open_lanes

    thr = thr_s[0:1, :]

    @pl.when(open_s[0] > 0)
    def _():
        _drop_extra_ties(score_s, nac * (ATT_CHUNK // 8), thr, float(topk))

    def logit_chunk(ac, m8s):
        r0 = pl.multiple_of(ac * ATT_CHUNK, ATT_CHUNK)
        c_chunk = c_ref[pl.ds(r0, ATT_CHUNK), :]
        mbias = jnp.where(score_s[pl.ds(r0, ATT_CHUNK), :] >= thr, 0.0, NEG)
        near = [[jnp.clip(ac * tiles + t - (qb * q_tiles + j) + 2, 0, 2) for j in range(q_tiles)]
                for t in range(tiles)]
        out = []
        for h in range(ATT_HEADS):
            lg = jnp.dot(c_chunk, qat_ref[h], preferred_element_type=F32) + mbias
            lg = jnp.concatenate(
                [lg[t * K_CHUNK:(t + 1) * K_CHUNK]
                 + jnp.concatenate([bias_ref[h, near[t][j]] for j in range(q_tiles)], axis=1)
                 for t in range(tiles)], axis=0)
            lg_s[h, pl.ds(r0, ATT_CHUNK), :] = lg
            out.append(jnp.maximum(m8s[h], jnp.max(lg.reshape(ATT_CHUNK // 8, 8, Q_TILE), axis=0)))
        return tuple(out)

    m8s = _loop_in_trips(nac, logit_chunk, tuple(jnp.full((8, Q_TILE), NEG, F32) for _ in range(ATT_HEADS)))
    ms = [jnp.max(m8, axis=0, keepdims=True) for m8 in m8s]

    acc_s[...] = jnp.zeros(acc_s.shape, F32)

    def pv_chunk(ac, carry):
        r0 = pl.multiple_of(ac * ATT_CHUNK, ATT_CHUNK)
        ct_chunk = ct_ref[:, pl.ds(r0, ATT_CHUNK)]
        for h in range(ATT_HEADS):
            pr = jnp.exp2(lg_s[h, pl.ds(r0, ATT_CHUNK), :] - ms[h])
            acc_s[h] += jnp.dot(ct_chunk, pr.astype(BF16), preferred_element_type=F32)
        return carry

    _loop_in_trips(nac, pv_chunk, 0)

    for h in range(ATT_HEADS):
        denom = acc_s[h, KV_LATENT:KV_LATENT + 1, :]
        o_t = acc_s[h, 0:KV_LATENT, :] * (1.0 / denom)
        y_t = jnp.dot(wuvt_ref[h], o_t.astype(BF16), preferred_element_type=F32)
        gb = gb_ref[:, h * HEAD_DIM:(h + 1) * HEAD_DIM]
        o_ref[:, h * HEAD_DIM:(h + 1) * HEAD_DIM] = (y_t.T * (gb * _sigmoid(gb))).astype(o_ref.dtype)


def _dsa(pb3, cols_b, w_ukt, wit, kn, c, ct, pa3, gb_blk, bias_tiles, wuvt, topk):
    bsz, s, _ = c.shape
    att_w = ATT_HEADS * HEAD_DIM
    idx_w = IDX_HEADS * IDX_DIM
    assert s % ATT_CHUNK == 0 and s % Q_TILE == 0 and Q_TILE % K_CHUNK == 0 and ATT_CHUNK % K_CHUNK == 0
    assert s // 16 // 4 < 256
    q_blk = _col_block(cols_b["q"], att_w)
    qi_blk = _col_block(cols_b["qi"], idx_w)
    const = lambda shape: pl.BlockSpec(shape, lambda b, i: (0,) * len(shape))
    return pl.pallas_call(
        functools.partial(_dsa_kernel, topk=topk),
        grid=(bsz, s // Q_TILE),
        in_specs=[
            pl.BlockSpec((None, Q_TILE, att_w), lambda b, i: (b, i, q_blk)),
            pl.BlockSpec((None, Q_TILE, idx_w), lambda b, i: (b, i, qi_blk)),
            const(w_ukt.shape),
            pl.BlockSpec((None, IDX_HEADS, Q_TILE), lambda b, i: (b, 0, i)),
            pl.BlockSpec((None, s, IDX_DIM), lambda b, i: (b, 0, 0)),
            pl.BlockSpec((None, s, KV_LATENT), lambda b, i: (b, 0, 0)),
            pl.BlockSpec((None, KV_LATENT + ONES_ROWS, s), lambda b, i: (b, 0, 0)),
            pl.BlockSpec((None, Q_TILE, att_w), lambda b, i: (b, i, gb_blk)),
            const(bias_tiles.shape),
            const(wuvt.shape),
        ],
        out_specs=pl.BlockSpec((None, Q_TILE, att_w), lambda b, i: (b, i, 0)),
        out_shape=jax.ShapeDtypeStruct((bsz, s, att_w), BF16),
        scratch_shapes=[
            pltpu.VMEM((ATT_HEADS, KV_LATENT, Q_TILE), BF16),
            pltpu.VMEM((IDX_HEADS, IDX_DIM, Q_TILE), BF16),
            pltpu.VMEM((s, Q_TILE), F32),
            pltpu.VMEM((s, Q_TILE), BF16),
            pltpu.VMEM((8, Q_TILE), F32),
            pltpu.SMEM((1,), I32),
            pltpu.VMEM((ATT_HEADS, s, Q_TILE), F32),
            pltpu.VMEM((ATT_HEADS, KV_LATENT + ONES_ROWS, Q_TILE), F32),
        ],
        compiler_params=_cparams(("parallel", "arbitrary")),
        name="dsa",
    )(pb3, pb3, w_ukt, wit, kn, c, ct, pa3, bias_tiles, wuvt)


W_COPIES = 4


def _outp_kernel(ya_ref, yb_ref, w_hbm, x_ref, g_ref, o_ref, w32_s, wbf_s, sem, *, final_norm):
    first = pl.program_id(0) == 0
    rows = w32_s.shape[0] // W_COPIES

    def copy(c):
        chunk = pl.ds(c * rows, rows)
        return pltpu.make_async_copy(w_hbm.at[chunk], w32_s.at[chunk], sem.at[c])

    @pl.when(first)
    def _():
        for c in range(W_COPIES):
            copy(c).start()

    def part(y_ref, lo):
        k = y_ref.shape[1]

        @pl.when(first)
        def _():
            for c in range(lo // rows, (lo + k) // rows):
                copy(c).wait()
                wbf_s[c * rows:(c + 1) * rows, :] = w32_s[c * rows:(c + 1) * rows, :].astype(BF16)

        return jnp.dot(y_ref[...], wbf_s[lo:lo + k, :], preferred_element_type=F32)

    acc = part(ya_ref, 0)
    acc = acc + part(yb_ref, ya_ref.shape[1])
    x = x_ref[...] + acc
    if final_norm:
        x = x * lax.rsqrt(jnp.mean(x * x, axis=-1, keepdims=True) + EPS) * g_ref[...]
    o_ref[...] = x


def _outp(ya, yb, w_out, x2, g, final_norm, tm=512):
    m, d = x2.shape
    ka, kb = ya.shape[1], yb.shape[1]
    assert w_out.shape == (ka + kb, d)
    rows = (ka + kb) // W_COPIES
    assert rows * W_COPIES == ka + kb and ka % rows == 0
    return pl.pallas_call(
        functools.partial(_outp_kernel, final_norm=final_norm),
        grid=(m // tm,),
        in_specs=[
            pl.BlockSpec((tm, ka), lambda i: (i, 0)),
            pl.BlockSpec((tm, kb), lambda i: (i, 0)),
            pl.BlockSpec(memory_space=pl.ANY),
            pl.BlockSpec((tm, d), lambda i: (i, 0)),
            pl.BlockSpec((1, d), lambda i: (0, 0)),
        ],
        out_specs=pl.BlockSpec((tm, d), lambda i: (i, 0)),
        out_shape=jax.ShapeDtypeStruct((m, d), F32),
        scratch_shapes=[pltpu.VMEM((ka + kb, d), F32), pltpu.VMEM((ka + kb, d), BF16),
                        pltpu.SemaphoreType.DMA((W_COPIES,))],
        compiler_params=_cparams(("arbitrary",)),
        name="outp",
    )(ya, yb, w_out, x2, g)


def _t5_bucket(dist):
    n = jnp.maximum(dist, 0)
    max_exact = REL_BUCKETS // 2
    nf = jnp.maximum(n, 1).astype(F32)
    large = max_exact + (jnp.log(nf / max_exact) / np.log(REL_MAX_DIST / max_exact)
                         * (REL_BUCKETS - max_exact)).astype(I32)
    large = jnp.minimum(large, REL_BUCKETS - 1)
    return jnp.where(n < max_exact, n, large)


def _bias_tiles(rel_bias):
    qw = K_CHUNK
    span = K_CHUNK + qw
    table = rel_bias[_t5_bucket(jnp.arange(span + 1, dtype=I32))].astype(F32)
    table = ((table[:span] - table[span:]) * LOG2E).T
    n = span + qw - 1
    a = jnp.concatenate([jnp.zeros((ATT_HEADS, qw - 1), F32), table], axis=1)
    shifted = jnp.tile(a, (1, span + 1))[:, :span * (n + 1)].reshape(ATT_HEADS, span, n + 1)
    tiles = shifted[:, ::-1, :qw].reshape(ATT_HEADS, 2, K_CHUNK, qw)
    return jnp.concatenate([jnp.zeros_like(tiles[:, :1]), tiles], axis=1)


def kernel(x, norm_g, w_in, conv_w, conv_b, lru_wa, lru_ba, lru_wx, lru_bx, lru_lambda, ckv_norm_g, idx_k_norm_g, idx_k_norm_b, w_uk, w_uv, w_out, rel_bias, final_norm_g):
    bsz, s, d = x.shape
    depth = w_in.shape[0]
    lru_w = lru_wa.shape[1] * lru_wa.shape[2]
    att_w = ATT_HEADS * HEAD_DIM
    idx_w = IDX_HEADS * IDX_DIM
    assert REL_MAX_DIST <= K_CHUNK
    assert lru_w == att_w == idx_w and att_w % KV_LATENT == 0
    topk = min(INDEX_TOPK, s // 4)

    o_q = 2 * lru_w
    o_ckv = o_q + att_w
    o_gb = o_ckv + KV_LATENT
    o_qi = o_gb + att_w
    o_ki = o_qi + idx_w
    tn = 512
    cols_a = {"xa": 0, "ga": lru_w, "gb": 2 * lru_w}
    cols_b = {"q": 0, "qi": att_w}
    n_f32, n_bf16 = 3 * lru_w, att_w + idx_w

    bias_tiles = _bias_tiles(rel_bias)
    x2 = x.reshape(bsz * s, d)
    for l in range(depth):
        order = [(0, o_q), (o_gb, o_qi), (o_q, o_ckv), (o_qi, o_ki), (o_ckv, o_gb), (o_ki, w_in.shape[2])]
        pa, pb, c, ct, kn, wit = _proj(x2, norm_g[l][None, :], w_in[l].T, order, ckv_norm_g[l][None, :],
                                       idx_k_norm_g[l][None, :], idx_k_norm_b[l][None, :], n_f32, n_bf16, s, tn=tn)
        pa3 = pa.reshape(bsz, s, -1)
        pb3 = pb.reshape(bsz, s, -1)

        ya = _rglru(pa3, cols_a, conv_w[l], conv_b[l][None, :], (0.5 * lru_wa[l]).astype(BF16),
                    0.5 * lru_ba[l][None, :], (0.5 * lru_wx[l]).astype(BF16), 0.5 * lru_bx[l][None, :],
                    lru_lambda[l][None, :])

        wukt = jnp.transpose(w_uk[l], (0, 2, 1)).astype(BF16)
        wuvt = jnp.transpose(w_uv[l], (0, 2, 1)).astype(BF16)
        yb = _dsa(pb3, cols_b, wukt, wit, kn.reshape(bsz, s, -1), c.reshape(bsz, s, -1), ct, pa3,
                  _col_block(cols_a["gb"], att_w), bias_tiles, wuvt, topk)

        x2 = _outp(ya.reshape(bsz * s, lru_w), yb.reshape(bsz * s, att_w), w_out[l], x2,
                   final_norm_g[None, :], final_norm=(l == depth - 1))
    return x2.reshape(bsz, s, d)
```

```python
import functools

import numpy as np
import jax
import jax.numpy as jnp
from jax import lax
from jax.experimental import pallas as pl
from jax.experimental.pallas import tpu as pltpu

F32 = jnp.float32
BF16 = jnp.bfloat16
I32 = jnp.int32

CONV_WIDTH = 4
LRU_C = 8.0
ATT_HEADS = 8
HEAD_DIM = 128
KV_LATENT = 256
IDX_HEADS = 16
IDX_DIM = 64
INDEX_TOPK = 256
REL_BUCKETS = 32
REL_MAX_DIST = 128
EPS = 1e-6
LOG2E = float(np.log2(np.e))
ONES_ROWS = 16

Q_TILE = 256
K_CHUNK = 128
ATT_CHUNK = 256
NEG = float(np.finfo(np.float32).min)
INT_MIN = -(2 ** 31)
VMEM_LIMIT = 56 * 1024 * 1024


def _cparams(sem):
    return pltpu.CompilerParams(dimension_semantics=sem, vmem_limit_bytes=VMEM_LIMIT)


def _col_block(offset, width):
    assert offset % width == 0
    return offset // width


def _proj_kernel(t_ref, x_ref, g_ref, wlo_ref, whi_ref, tail_ref, cg_ref, kg_ref, kb_ref,
                 oa_ref, ob_ref, c_ref, ct_ref, kn_ref, wit_ref, h_ref, *, na, nb):
    j = pl.program_id(1)
    r = pl.program_id(2)
    nt = (((1,), (1,)), ((), ()))

    def w_tile():
        lo = jnp.where(t_ref[2 * j] < 0, tail_ref[...], wlo_ref[...])
        hi = jnp.where(t_ref[2 * j + 1] < 0, tail_ref[...], whi_ref[...])
        return jnp.concatenate([lo, hi], axis=0).astype(BF16)

    @pl.when(j == 0)
    def _():
        x = x_ref[...]
        y = x * lax.rsqrt(jnp.mean(x * x, axis=-1, keepdims=True) + EPS)
        h = (y * g_ref[...]).astype(BF16)
        h_ref[r] = h
        oa_ref[...] = lax.dot_general(h, w_tile(), nt, preferred_element_type=F32)

    @pl.when((j > 0) & (j < na))
    def _():
        oa_ref[...] = lax.dot_general(h_ref[r], w_tile(), nt, preferred_element_type=F32)

    @pl.when((j >= na) & (j < na + nb))
    def _():
        ob_ref[...] = lax.dot_general(h_ref[r], w_tile(), nt, preferred_element_type=F32).astype(BF16)

    @pl.when(j >= na + nb)
    def _():
        used = KV_LATENT + 128
        tail = lax.dot_general(h_ref[r], w_tile()[:used], nt, preferred_element_type=F32)
        ckv = tail[:, :KV_LATENT]
        c = ckv * lax.rsqrt(jnp.mean(ckv * ckv, axis=-1, keepdims=True) + EPS) * cg_ref[...]
        c_ref[...] = c.astype(BF16)
        ct_ref[0:KV_LATENT, :] = c.T.astype(BF16)
        ct_ref[KV_LATENT:, :] = jnp.ones((ONES_ROWS, ct_ref.shape[1]), BF16)
        sm = tail[:, KV_LATENT:KV_LATENT + 128]
        ki = sm[:, :IDX_DIM]
        mu = jnp.mean(ki, axis=-1, keepdims=True)
        var = jnp.mean(jnp.square(ki - mu), axis=-1, keepdims=True)
        kn = (ki - mu) * lax.rsqrt(var + EPS) * kg_ref[...] + kb_ref[...]
        kn_ref[...] = kn.astype(BF16)
        wit_ref[...] = sm.T[IDX_DIM:IDX_DIM + IDX_HEADS, :] * (IDX_HEADS ** -0.5 * IDX_DIM ** -0.5)


def _proj(x2, g, w_t, order, ckv_g, k_g, k_b, n_f32, n_bf16, seq_len, tm=1024, tn=512, group=2):
    m, d = x2.shape
    n = w_t.shape[0]
    th = tn // 2
    n_whole = n // th
    src = []
    for start, stop in order:
        assert start % th == 0 and (stop % th == 0 or stop == n)
        src += list(range(start // th, -(-stop // th)))
    na, nb = n_f32 // tn, n_bf16 // tn
    assert n_f32 % tn == 0 and n_bf16 % tn == 0 and len(src) * th == n_f32 + n_bf16 + tn
    tail = jnp.pad(w_t[n_whole * th:], ((0, (n_whole + 1) * th - n), (0, 0)))
    table = jnp.asarray([blk if blk < n_whole else -1 for blk in src], I32)
    tps = seq_len // tm
    assert seq_len % tm == 0 and tn >= KV_LATENT + 128
    assert (m // tm) % group == 0
    nj = na + nb + 1
    const = lambda shape: pl.BlockSpec(shape, lambda i, j, r, t: (0,) * len(shape))
    last = group - 1
    row = lambda i, r_eff: i * group + r_eff
    per_seq = lambda g: (g // tps, 0, g % tps)
    tail_row = lambda i, j, r: row(i, jnp.where(j < nj - 1, 0, r))
    n_groups = m // tm // group

    def rest(i, last_col):
        nxt = i + 1 < n_groups
        return jnp.where(nxt, row(i + 1, 0), row(i, last)), jnp.where(nxt, 0, last_col)

    def oa_index(i, j, r, t):
        rest_row, rest_col = rest(i, na - 1)
        return jnp.where(j < na, row(i, r), rest_row), jnp.where(j < na, j, rest_col)

    def ob_index(i, j, r, t):
        rest_row, rest_col = rest(i, nb - 1)
        own = (j >= na) & (j < na + nb)
        return (jnp.where(j < na, row(i, 0), jnp.where(own, row(i, r), rest_row)),
                jnp.where(j < na, 0, jnp.where(own, j - na, rest_col)))

    grid_spec = pltpu.PrefetchScalarGridSpec(
        num_scalar_prefetch=1,
        grid=(m // tm // group, nj, group),
        in_specs=[
            pl.BlockSpec((tm, d), lambda i, j, r, t: (row(i, jnp.where(j == 0, r, last)), 0)),
            const((1, d)),
            pl.BlockSpec((th, d), lambda i, j, r, t: (jnp.maximum(t[2 * j], 0), 0)),
            pl.BlockSpec((th, d), lambda i, j, r, t: (jnp.maximum(t[2 * j + 1], 0), 0)),
            const((th, d)),
            const((1, KV_LATENT)),
            const((1, IDX_DIM)),
            const((1, IDX_DIM)),
        ],
        out_specs=[
            pl.BlockSpec((tm, tn), oa_index),
            pl.BlockSpec((tm, tn), ob_index),
            pl.BlockSpec((tm, KV_LATENT), lambda i, j, r, t: (tail_row(i, j, r), 0)),
            pl.BlockSpec((None, KV_LATENT + ONES_ROWS, tm), lambda i, j, r, t: per_seq(tail_row(i, j, r))),
            pl.BlockSpec((tm, IDX_DIM), lambda i, j, r, t: (tail_row(i, j, r), 0)),
            pl.BlockSpec((None, IDX_HEADS, tm), lambda i, j, r, t: per_seq(tail_row(i, j, r))),
        ],
        scratch_shapes=[pltpu.VMEM((group, tm, d), BF16)],
    )
    return pl.pallas_call(
        functools.partial(_proj_kernel, na=na, nb=nb),
        grid_spec=grid_spec,
        out_shape=[
            jax.ShapeDtypeStruct((m, n_f32), F32),
            jax.ShapeDtypeStruct((m, n_bf16), BF16),
            jax.ShapeDtypeStruct((m, KV_LATENT), BF16),
            jax.ShapeDtypeStruct((m // seq_len, KV_LATENT + ONES_ROWS, seq_len), BF16),
            jax.ShapeDtypeStruct((m, IDX_DIM), BF16),
            jax.ShapeDtypeStruct((m // seq_len, IDX_HEADS, seq_len), F32),
        ],
        compiler_params=_cparams(("arbitrary", "arbitrary", "arbitrary")),
        name="proj",
    )(table, x2, g, w_t, w_t, tail, ckv_g, k_g, k_b)


def _sigmoid(v):
    return 0.5 * jnp.tanh(0.5 * v) + 0.5


def _scan_step(a, b, k, axis, idx):
    keep = idx >= k
    a_prev = jnp.where(keep, pltpu.roll(a, k, axis=axis), 1.0)
    b_prev = jnp.where(keep, pltpu.roll(b, k, axis=axis), 0.0)
    return a * a_prev, a * b_prev + b


def _rglru_kernel(xa_ref, ga_ref, cw_ref, cb_ref, wa_ref, ba_ref, wx_ref, bx_ref, lam_ref,
                  o_ref, pad_s, a_s, b_s, c_s):
    s, w = xa_ref.shape
    tile = 8
    n_tiles = s // tile

    pad_s[0:tile, :] = jnp.zeros((tile, w), F32)
    pad_s[tile:tile + s, :] = xa_ref[...]
    acc = pad_s[tile:tile + s, :] * cw_ref[CONV_WIDTH - 1:CONV_WIDTH, :]
    for j in range(CONV_WIDTH - 1):
        back = CONV_WIDTH - 1 - j
        acc = acc + pad_s[tile - back:tile - back + s, :] * cw_ref[j:j + 1, :]
    xc = cb_ref[...] + acc

    xcb = xc.astype(BF16)
    tr = jnp.tanh(jnp.dot(xcb, wa_ref[...], preferred_element_type=F32) + ba_ref[...])
    ti = jnp.tanh(jnp.dot(xcb, wx_ref[...], preferred_element_type=F32) + bx_ref[...])
    i = 0.5 * ti + 0.5
    z = -lam_ref[...]
    softplus = jnp.maximum(z, 0.0) + jnp.log1p(jnp.exp(-jnp.abs(z)))
    half = (-0.5 * LRU_C) * softplus
    log_a = half * tr + half
    a = jnp.exp(log_a)
    m2 = (1.0 + a * a) * jnp.tanh(-log_a)
    mult = jnp.where(m2 > 0.0, m2 * lax.rsqrt(m2), 0.0)
    gated = i * xc
    b_s[...] = mult * gated
    b_s[0:1, :] = gated[0:1, :]

    a3 = a.reshape(n_tiles, tile, w)
    b3 = b_s[...].reshape(n_tiles, tile, w)
    sub = lax.broadcasted_iota(I32, (n_tiles, tile, w), 1)
    for k in (1, 2, 4):
        a3, b3 = _scan_step(a3, b3, k, 1, sub)
    a_s[...] = a3.reshape(s, w)
    b_s[...] = b3.reshape(s, w)

    at = a_s[pl.ds(tile - 1, n_tiles, stride=tile), :]
    bt = b_s[pl.ds(tile - 1, n_tiles, stride=tile), :]
    trow = lax.broadcasted_iota(I32, (n_tiles, w), 0)
    k = 1
    while k < n_tiles:
        at, bt = _scan_step(at, bt, k, 0, trow)
        k *= 2
    c_s[0:tile, :] = jnp.zeros((tile, w), F32)
    c_s[tile:tile + n_tiles, :] = bt

    for t in range(n_tiles):
        rows = slice(t * tile, (t + 1) * tile)
        before = c_s[pl.ds(tile - 1 + t, tile, stride=0), :]
        h = a_s[rows, :] * before + b_s[rows, :]
        gh = 0.5 * ga_ref[rows, :]
        o_ref[rows, :] = (h * (gh * (jnp.tanh(gh) + 1.0))).astype(o_ref.dtype)


def _rglru(pa3, cols, conv_w, conv_b, wa, ba, wx, bx, lam):
    bsz, s, _ = pa3.shape
    g, w = wa.shape[0], wa.shape[-1]
    xa_blk = _col_block(cols["xa"], w)
    ga_blk = _col_block(cols["ga"], w)
    vec = lambda: pl.BlockSpec((1, w), lambda b, j: (0, j))
    return pl.pallas_call(
        _rglru_kernel,
        grid=(bsz, g),
        in_specs=[
            pl.BlockSpec((None, s, w), lambda b, j: (b, 0, xa_blk + j)),
            pl.BlockSpec((None, s, w), lambda b, j: (b, 0, ga_blk + j)),
            pl.BlockSpec((CONV_WIDTH, w), lambda b, j: (0, j)),
            vec(),
            pl.BlockSpec((None, w, w), lambda b, j: (j, 0, 0)),
            vec(),
            pl.BlockSpec((None, w, w), lambda b, j: (j, 0, 0)),
            vec(),
            vec(),
        ],
        out_specs=pl.BlockSpec((None, s, w), lambda b, j: (b, 0, j)),
        out_shape=jax.ShapeDtypeStruct((bsz, s, g * w), BF16),
        scratch_shapes=[pltpu.VMEM((s + 8, w), F32), pltpu.VMEM((s, w), F32), pltpu.VMEM((s, w), F32),
                        pltpu.VMEM((s // 8 + 8, w), F32)],
        compiler_params=_cparams(("parallel", "parallel")),
        name="rglru",
    )(pa3, pa3, conv_w, conv_b, wa, ba, wx, bx, lam)


def _tree_sum(parts):
    while len(parts) > 1:
        paired = [parts[i] + parts[i + 1] for i in range(0, len(parts) - 1, 2)]
        parts = paired + ([parts[-1]] if len(parts) % 2 else [])
    return parts[0]


def _sortable_to_f32(u):
    key = u ^ INT_MIN
    return lax.bitcast_convert_type(key ^ ((key >> 31) & 0x7FFFFFFF), F32)


def _count_ge(ref, rows, cand, pack):
    chains = 4
    one, zero = jnp.ones((), ref.dtype), jnp.zeros((), ref.dtype)
    accs = [None] * chains
    for r in range(rows // pack):
        hit = jnp.where(ref[r * pack:(r + 1) * pack, :] >= cand, one, zero)
        accs[r % chains] = hit if accs[r % chains] is None else accs[r % chains] + hit
    parts = [a.astype(F32) for a in accs if a is not None]
    return jnp.sum(_tree_sum(parts), axis=0, keepdims=True)


def _kth_largest(score_ref, score16_ref, rows, k):
    def step16(i, u):
        cand = u | (jnp.int32(1) << (15 - i))
        cand_f = _sortable_to_f32(cand << 16).astype(BF16)
        return jnp.where(_count_ge(score16_ref, rows, cand_f, 16) >= k, cand, u)

    hi = lax.fori_loop(0, 16, step16, jnp.zeros((1, Q_TILE), I32))
    at_hi = _count_ge(score_ref, rows, _sortable_to_f32(hi << 16), 8)
    keeps = at_hi >= k
    hi = jnp.where(keeps, hi, jnp.maximum(hi - 1, 0))

    per_trip = 4

    def trip(state):
        i, u, at_u, _ = state
        for t in range(per_trip):
            cand = u | (jnp.int32(1) << (15 - (i + t)))
            cnt = _count_ge(score_ref, rows, _sortable_to_f32(cand), 8)
            take = cnt >= k
            u = jnp.where(take, cand, u)
            at_u = jnp.where(take, cnt, at_u)
        open_lanes = jnp.sum((at_u != k).astype(I32))
        return i + per_trip, u, at_u, open_lanes

    unknown = jnp.full((1, Q_TILE), -1.0, F32)
    state = (jnp.int32(0), hi << 16, jnp.where(keeps, at_hi, unknown), jnp.int32(1))
    _, u, _, open_lanes = lax.while_loop(lambda st: (st[0] < 16) & (st[3] > 0), trip, state)
    return _sortable_to_f32(u), open_lanes


def _drop_extra_ties(score_ref, n_groups, thr, k):
    def count_above(g, acc):
        x = score_ref[pl.ds(pl.multiple_of(g * 8, 8), 8), :]
        return acc + jnp.where(x > thr, 1.0, 0.0)

    above = lax.fori_loop(0, n_groups, count_above, jnp.zeros((8, Q_TILE), F32))
    need = k - jnp.sum(above, axis=0, keepdims=True)
    sub = lax.broadcasted_iota(I32, (8, Q_TILE), 0)

    def demote(g, seen):
        r0 = pl.multiple_of(g * 8, 8)
        x = score_ref[pl.ds(r0, 8), :]
        tie = jnp.where(x == thr, 1.0, 0.0)
        upto = tie
        for k8 in (1, 2, 4):
            upto = upto + jnp.where(sub >= k8, pltpu.roll(upto, k8, axis=0), 0.0)
        before = seen + upto - tie
        score_ref[pl.ds(r0, 8), :] = jnp.where((tie > 0.0) & (before >= need), -jnp.inf, x)
        return seen + upto[7:8, :]

    lax.fori_loop(0, n_groups, demote, jnp.zeros((1, Q_TILE), F32))


def _loop_in_trips(n, body, init):
    def trips(start, count, width, carry):
        def group(i, c):
            for t in range(width):
                c = body(start + width * i + t, c)
            return c
        return lax.fori_loop(0, count, group, carry)

    carry = trips(0, n // 4, 4, init)
    carry = trips(4 * (n // 4), (n % 4) // 2, 2, carry)
    return trips(2 * (n // 2), n % 2, 1, carry)


def _dsa_kernel(q_ref, qi_ref, wukt_ref, wit_ref, kn_ref, c_ref, ct_ref, gb_ref, bias_ref, wuvt_ref,
                wout_ref, o_ref, woutb_ref, qat_ref, qit_ref, score_s, score16_s, thr_s, open_s, lg_s,
                acc_s, topk):
    qb = pl.program_id(1)
    woutb_ref[...] = wout_ref[...].astype(BF16)

    qt = q_ref[...].T
    scale = HEAD_DIM ** -0.5 * LOG2E
    for h in range(ATT_HEADS):
        qa = jnp.dot(wukt_ref[h], qt[h * HEAD_DIM:(h + 1) * HEAD_DIM], preferred_element_type=F32)
        qat_ref[h] = (qa * scale).astype(BF16)
    qit = qi_ref[...].T
    for h in range(IDX_HEADS):
        qit_ref[h] = qit[h * IDX_DIM:(h + 1) * IDX_DIM, :]
    q_tiles = Q_TILE // K_CHUNK
    nkc = (qb + 1) * q_tiles
    nac = (nkc * K_CHUNK + ATT_CHUNK - 1) // ATT_CHUNK
    tiles = ATT_CHUNK // K_CHUNK

    kiota = lax.broadcasted_iota(I32, (K_CHUNK, Q_TILE), 0)
    qpos = qb * Q_TILE + lax.broadcasted_iota(I32, (K_CHUNK, Q_TILE), 1)

    def score_chunk(ac, carry):
        for t in range(tiles):
            k0 = pl.multiple_of(ac * ATT_CHUNK + t * K_CHUNK, K_CHUNK)
            kn = kn_ref[pl.ds(k0, K_CHUNK), :]
            acc = jnp.zeros((K_CHUNK, Q_TILE), F32)
            for h in range(IDX_HEADS):
                sc = jnp.dot(kn, qit_ref[h], preferred_element_type=F32)
                acc = acc + jnp.maximum(sc, 0.0) * wit_ref[h:h + 1, :]
            masked = jnp.where(kiota + k0 <= qpos, acc, -jnp.inf)
            score_s[pl.ds(k0, K_CHUNK), :] = masked
            score16_s[pl.ds(k0, K_CHUNK), :] = masked.astype(BF16)
        return carry

    _loop_in_trips(nac, score_chunk, 0)

    for v in range(1, score_s.shape[0] // ATT_CHUNK + 1):
        @pl.when(nac == v)
        def _(rows=v * ATT_CHUNK):
            if rows <= topk:
                thr, open_lanes = jnp.full((1, Q_TILE), NEG, F32), jnp.int32(0)
            else:
                thr, open_lanes = _kth_largest(score_s, score16_s, rows, float(topk))
                thr = jnp.where(thr >= NEG, thr, NEG)
            thr_s[...] = jnp.broadcast_to(thr, thr_s.shape)
            open_s[0] = open_lanes

    thr = thr_s[0:1, :]

    @pl.when(open_s[0] > 0)
    def _():
        _drop_extra_ties(score_s, nac * (ATT_CHUNK // 8), thr, float(topk))

    def logit_chunk(ac, m8s):
        r0 = pl.multiple_of(ac * ATT_CHUNK, ATT_CHUNK)
        c_chunk = c_ref[pl.ds(r0, ATT_CHUNK), :]
        mbias = jnp.where(score_s[pl.ds(r0, ATT_CHUNK), :] >= thr, 0.0, NEG)
        near = [[jnp.clip(ac * tiles + t - (qb * q_tiles + j) + 2, 0, 2) for j in range(q_tiles)]
                for t in range(tiles)]
        out = []
        for h in range(ATT_HEADS):
            lg = jnp.dot(c_chunk, qat_ref[h], preferred_element_type=F32) + mbias
            lg = jnp.concatenate(
                [lg[t * K_CHUNK:(t + 1) * K_CHUNK]
                 + jnp.concatenate([bias_ref[h, near[t][j]] for j in range(q_tiles)], axis=1)
                 for t in range(tiles)], axis=0)
            lg_s[h, pl.ds(r0, ATT_CHUNK), :] = lg
            out.append(jnp.maximum(m8s[h], jnp.max(lg.reshape(ATT_CHUNK // 8, 8, Q_TILE), axis=0)))
        return tuple(out)

    m8s = _loop_in_trips(nac, logit_chunk, tuple(jnp.full((8, Q_TILE), NEG, F32) for _ in range(ATT_HEADS)))
    ms = [jnp.max(m8, axis=0, keepdims=True) for m8 in m8s]

    acc_s[...] = jnp.zeros(acc_s.shape, F32)

    def pv_chunk(ac, carry):
        r0 = pl.multiple_of(ac * ATT_CHUNK, ATT_CHUNK)
        ct_chunk = ct_ref[:, pl.ds(r0, ATT_CHUNK)]
        for h in range(ATT_HEADS):
            pr = jnp.exp2(lg_s[h, pl.ds(r0, ATT_CHUNK), :] - ms[h])
            acc_s[h] += jnp.dot(ct_chunk, pr.astype(BF16), preferred_element_type=F32)
        return carry

    _loop_in_trips(nac, pv_chunk, 0)

    for h in range(ATT_HEADS):
        denom = acc_s[h, KV_LATENT:KV_LATENT + 1, :]
        o_t = acc_s[h, 0:KV_LATENT, :] * (1.0 / denom)
        y_t = jnp.dot(wuvt_ref[h], o_t.astype(BF16), preferred_element_type=F32)
        gb = gb_ref[:, h * HEAD_DIM:(h + 1) * HEAD_DIM]
        o_ref[:, h * HEAD_DIM:(h + 1) * HEAD_DIM] = (y_t.T * (gb * _sigmoid(gb))).astype(o_ref.dtype)


def _dsa(pb3, cols_b, w_ukt, wit, kn, c, ct, pa3, gb_blk, bias_tiles, wuvt, w_out, topk):
    bsz, s, _ = c.shape
    n_tiles = s // Q_TILE
    wo_rows = w_out.shape[0] // (bsz * n_tiles)
    assert wo_rows * bsz * n_tiles == w_out.shape[0] and wo_rows % 16 == 0
    wo_spec = pl.BlockSpec((wo_rows, w_out.shape[1]), lambda b, i: (b * n_tiles + i, 0))
    att_w = ATT_HEADS * HEAD_DIM
    idx_w = IDX_HEADS * IDX_DIM
    assert s % ATT_CHUNK == 0 and s % Q_TILE == 0 and Q_TILE % K_CHUNK == 0 and ATT_CHUNK % K_CHUNK == 0
    assert s // 16 // 4 < 256
    q_blk = _col_block(cols_b["q"], att_w)
    qi_blk = _col_block(cols_b["qi"], idx_w)
    const = lambda shape: pl.BlockSpec(shape, lambda b, i: (0,) * len(shape))
    return pl.pallas_call(
        functools.partial(_dsa_kernel, topk=topk),
        grid=(bsz, s // Q_TILE),
        in_specs=[
            pl.BlockSpec((None, Q_TILE, att_w), lambda b, i: (b, i, q_blk)),
            pl.BlockSpec((None, Q_TILE, idx_w), lambda b, i: (b, i, qi_blk)),
            const(w_ukt.shape),
            pl.BlockSpec((None, IDX_HEADS, Q_TILE), lambda b, i: (b, 0, i)),
            pl.BlockSpec((None, s, IDX_DIM), lambda b, i: (b, 0, 0)),
            pl.BlockSpec((None, s, KV_LATENT), lambda b, i: (b, 0, 0)),
            pl.BlockSpec((None, KV_LATENT + ONES_ROWS, s), lambda b, i: (b, 0, 0)),
            pl.BlockSpec((None, Q_TILE, att_w), lambda b, i: (b, i, gb_blk)),
            const(bias_tiles.shape),
            const(wuvt.shape),
            wo_spec,
        ],
        out_specs=[pl.BlockSpec((None, Q_TILE, att_w), lambda b, i: (b, i, 0)), wo_spec],
        out_shape=[jax.ShapeDtypeStruct((bsz, s, att_w), BF16),
                   jax.ShapeDtypeStruct(w_out.shape, BF16)],
        scratch_shapes=[
            pltpu.VMEM((ATT_HEADS, KV_LATENT, Q_TILE), BF16),
            pltpu.VMEM((IDX_HEADS, IDX_DIM, Q_TILE), BF16),
            pltpu.VMEM((s, Q_TILE), F32),
            pltpu.VMEM((s, Q_TILE), BF16),
            pltpu.VMEM((8, Q_TILE), F32),
            pltpu.SMEM((1,), I32),
            pltpu.VMEM((ATT_HEADS, s, Q_TILE), F32),
            pltpu.VMEM((ATT_HEADS, KV_LATENT + ONES_ROWS, Q_TILE), F32),
        ],
        compiler_params=_cparams(("parallel", "arbitrary")),
        name="dsa",
    )(pb3, pb3, w_ukt, wit, kn, c, ct, pa3, bias_tiles, wuvt, w_out)


def _outp_kernel(ya_ref, yb_ref, w_ref, x_ref, g_ref, o_ref, *, final_norm):
    ka = ya_ref.shape[1]
    acc = jnp.dot(ya_ref[...], w_ref[0:ka, :], preferred_element_type=F32)
    acc = acc + jnp.dot(yb_ref[...], w_ref[ka:, :], preferred_element_type=F32)
    x = x_ref[...] + acc
    if final_norm:
        x = x * lax.rsqrt(jnp.mean(x * x, axis=-1, keepdims=True) + EPS) * g_ref[...]
    o_ref[...] = x


def _outp(ya, yb, w_out, x2, g, final_norm, tm=512):
    m, d = x2.shape
    ka, kb = ya.shape[1], yb.shape[1]
    assert w_out.shape == (ka + kb, d)
    return pl.pallas_call(
        functools.partial(_outp_kernel, final_norm=final_norm),
        grid=(m // tm,),
        in_specs=[
            pl.BlockSpec((tm, ka), lambda i: (i, 0)),
            pl.BlockSpec((tm, kb), lambda i: (i, 0)),
            pl.BlockSpec((ka + kb, d), lambda i: (0, 0), pipeline_mode=pl.Buffered(1)),
            pl.BlockSpec((tm, d), lambda i: (i, 0)),
            pl.BlockSpec((1, d), lambda i: (0, 0)),
        ],
        out_specs=pl.BlockSpec((tm, d), lambda i: (i, 0)),
        out_shape=jax.ShapeDtypeStruct((m, d), F32),
        compiler_params=_cparams(("arbitrary",)),
        name="outp",
    )(ya, yb, w_out, x2, g)


def _t5_bucket(dist):
    n = jnp.maximum(dist, 0)
    max_exact = REL_BUCKETS // 2
    nf = jnp.maximum(n, 1).astype(F32)
    large = max_exact + (jnp.log(nf / max_exact) / np.log(REL_MAX_DIST / max_exact)
                         * (REL_BUCKETS - max_exact)).astype(I32)
    large = jnp.minimum(large, REL_BUCKETS - 1)
    return jnp.where(n < max_exact, n, large)


def _bias_tiles(rel_bias):
    qw = K_CHUNK
    span = K_CHUNK + qw
    table = rel_bias[_t5_bucket(jnp.arange(span + 1, dtype=I32))].astype(F32)
    table = ((table[:span] - table[span:]) * LOG2E).T
    n = span + qw - 1
    a = jnp.concatenate([jnp.zeros((ATT_HEADS, qw - 1), F32), table], axis=1)
    shifted = jnp.tile(a, (1, span + 1))[:, :span * (n + 1)].reshape(ATT_HEADS, span, n + 1)
    tiles = shifted[:, ::-1, :qw].reshape(ATT_HEADS, 2, K_CHUNK, qw)
    return jnp.concatenate([jnp.zeros_like(tiles[:, :1]), tiles], axis=1)


def kernel(x, norm_g, w_in, conv_w, conv_b, lru_wa, lru_ba, lru_wx, lru_bx, lru_lambda, ckv_norm_g, idx_k_norm_g, idx_k_norm_b, w_uk, w_uv, w_out, rel_bias, final_norm_g):
    bsz, s, d = x.shape
    depth = w_in.shape[0]
    lru_w = lru_wa.shape[1] * lru_wa.shape[2]
    att_w = ATT_HEADS * HEAD_DIM
    idx_w = IDX_HEADS * IDX_DIM
    assert REL_MAX_DIST <= K_CHUNK
    assert lru_w == att_w == idx_w and att_w % KV_LATENT == 0
    topk = min(INDEX_TOPK, s // 4)

    o_q = 2 * lru_w
    o_ckv = o_q + att_w
    o_gb = o_ckv + KV_LATENT
    o_qi = o_gb + att_w
    o_ki = o_qi + idx_w
    tn = 512
    cols_a = {"xa": 0, "ga": lru_w, "gb": 2 * lru_w}
    cols_b = {"q": 0, "qi": att_w}
    n_f32, n_bf16 = 3 * lru_w, att_w + idx_w

    bias_tiles = _bias_tiles(rel_bias)
    x2 = x.reshape(bsz * s, d)
    for l in range(depth):
        order = [(0, o_q), (o_gb, o_qi), (o_q, o_ckv), (o_qi, o_ki), (o_ckv, o_gb), (o_ki, w_in.shape[2])]
        pa, pb, c, ct, kn, wit = _proj(x2, norm_g[l][None, :], w_in[l].T, order, ckv_norm_g[l][None, :],
                                       idx_k_norm_g[l][None, :], idx_k_norm_b[l][None, :], n_f32, n_bf16, s, tn=tn)
        pa3 = pa.reshape(bsz, s, -1)
        pb3 = pb.reshape(bsz, s, -1)

        ya = _rglru(pa3, cols_a, conv_w[l], conv_b[l][None, :], (0.5 * lru_wa[l]).astype(BF16),
                    0.5 * lru_ba[l][None, :], (0.5 * lru_wx[l]).astype(BF16), 0.5 * lru_bx[l][None, :],
                    lru_lambda[l][None, :])

        wukt = jnp.transpose(w_uk[l], (0, 2, 1)).astype(BF16)
        wuvt = jnp.transpose(w_uv[l], (0, 2, 1)).astype(BF16)
        yb, w_out_bf = _dsa(pb3, cols_b, wukt, wit, kn.reshape(bsz, s, -1), c.reshape(bsz, s, -1), ct, pa3,
                            _col_block(cols_a["gb"], att_w), bias_tiles, wuvt, w_out[l], topk)

        x2 = _outp(ya.reshape(bsz * s, lru_w), yb.reshape(bsz * s, att_w), w_out_bf, x2,
                   final_norm_g[None, :], final_norm=(l == depth - 1))
    return x2.reshape(bsz, s, d)
```
